```python
import math
import jax, jax.numpy as jnp
from jax import lax
import numpy as np

D_MODEL = 1024
BATCH = 2
SEQ = 8192
DEPTH = 2

GRID_W = 64
HEAD_DIM = 64
D_MIX = D_MODEL
C_WIDTH = D_MIX // 4
A_WIDTH = ((D_MIX - C_WIDTH) // 2 // HEAD_DIM) * HEAD_DIM
B_WIDTH = D_MIX - C_WIDTH - A_WIDTH
A_HEADS = A_WIDTH // HEAD_DIM
B_HEADS = B_WIDTH // HEAD_DIM
NA_KH = 8
NA_KW = 16
POOL_WINDOWS = (2, 4, 8, 16)
N_POOL_GROUPS = 4
POOL_CH = C_WIDTH // N_POOL_GROUPS
R_W = 32
R_A = 32
R_G = 64
DECAY_SCALE = math.exp(-0.5)
GN_EPS = 64e-5
IN_WIDTH = 3 * A_WIDTH + 3 * B_WIDTH + R_W + R_A + R_G + C_WIDTH
N_GROUPS = 4
EXPERTS_PER_GROUP = 8
N_EXPERTS = N_GROUPS * EXPERTS_PER_GROUP
TOP_K = 2
D_EXPERT = 512
MOE_BLOCK = 128
ALPHA = (2 * DEPTH) ** 0.25
BETA = (8 * DEPTH) ** -0.25
LN_EPS = 1e-5
NEG_INF = -1e30

kernel_name = "hybrid_natten_rwkv7_pool_hmoe_encoder"


def layer_norm(x, eps=LN_EPS):
    xf = x.astype(jnp.float32)
    mu = jnp.mean(xf, axis=-1, keepdims=True)
    var = jnp.mean(jnp.square(xf - mu), axis=-1, keepdims=True)
    return ((xf - mu) * lax.rsqrt(var + eps)).astype(x.dtype)


def neighbourhood_attention(q, k, v, rpb):
    bn, s, _ = q.shape
    rows = s // GRID_W
    kh = min(NA_KH, rows)
    shp = (bn, rows, GRID_W, A_HEADS, HEAD_DIM)
    q, k, v = q.reshape(shp), k.reshape(shp), v.reshape(shp)
    ridx = jnp.arange(rows)
    rstart = jnp.clip(ridx - kh // 2, 0, rows - kh)
    key_rows = rstart[:, None] + jnp.arange(kh)[None, :]
    k_rows = k[:, key_rows]
    v_rows = v[:, key_rows]
    scores = jnp.einsum('brqhd,brkchd->brhqkc', q, k_rows).astype(jnp.float32) * (HEAD_DIM ** -0.5)
    col = jnp.arange(GRID_W)
    cstart = jnp.clip(col - NA_KW // 2, 0, GRID_W - NA_KW)
    in_win = (col[None, :] >= cstart[:, None]) & (col[None, :] < cstart[:, None] + NA_KW)
    dr = key_rows - ridx[:, None] + (NA_KH - 1)
    dc = jnp.clip(col[None, :] - col[:, None], -(NA_KW - 1), NA_KW - 1) + (NA_KW - 1)
    bias = rpb.astype(jnp.float32)[:, dr][..., dc]
    scores = scores + jnp.transpose(bias, (1, 0, 3, 2, 4))[None]
    scores = jnp.where(in_win[:, None, :], scores, NEG_INF)
    p = jax.nn.softmax(scores, axis=(-2, -1)).astype(v.dtype)
    out = jnp.einsum('brhqkc,brkchd->brqhd', p, v_rows)
    return out.reshape(bn, s, A_WIDTH)


def centred_conv3(z, w):
    zp = jnp.pad(z, ((0, 0), (1, 1), (0, 0)))
    return zp[:, :-2] * w[0] + zp[:, 1:-1] * w[1] + zp[:, 2:] * w[2]


def _heads(z):
    return z.reshape(z.shape[:-1] + (B_HEADS, HEAD_DIM))


def _rwkv_step(state, inp):
    r, w, k, v, kk, a = inp
    sa = jnp.einsum('dbhij,dbhj->dbhi', state, -kk)
    state = (state * w[..., None, :] + sa[..., :, None] * (kk * a)[..., None, :]
             + v[..., :, None] * k[..., None, :])
    y = jnp.einsum('dbhij,dbhj->dbhi', state, r)
    return state, y


def rwkv7_bidirectional(rkv, wl, al, gl, conv_w, w0, w_up, a0, a_up, g_up, k_k, k_a, r_k, gn_gain, gn_bias):
    bn, s, _ = rkv.shape
    dt = rkv.dtype
    rkv = centred_conv3(rkv, conv_w)
    r, k, v = jnp.split(rkv, 3, axis=-1)
    w = jnp.exp(-DECAY_SCALE * jax.nn.sigmoid(
        (w0[:, None, None, :] + jnp.einsum('bsr,nrc->nbsc', jnp.tanh(wl), w_up)).astype(jnp.float32)))
    a = jax.nn.sigmoid((a0[:, None, None, :] + jnp.einsum('bsr,nrc->nbsc', al, a_up)).astype(jnp.float32))
    g = (jax.nn.sigmoid(gl) @ g_up).astype(jnp.float32)
    rf = _heads(r.astype(jnp.float32))
    kf = _heads(k.astype(jnp.float32))
    vf = _heads(v.astype(jnp.float32))
    kk = kf * _heads(k_k.astype(jnp.float32))
    kk = kk * lax.rsqrt(jnp.maximum(jnp.sum(kk * kk, axis=-1, keepdims=True), 1e-24))
    a_h = _heads(a)
    k_dir = kf[None] * (1.0 + (a_h - 1.0) * _heads(k_a.astype(jnp.float32)))

    def to_scan(z):
        z = jnp.broadcast_to(z, (2,) + z.shape[-4:])
        z = jnp.stack([z[0], z[1][:, ::-1]])
        return jnp.moveaxis(z, 2, 0)

    xs = (to_scan(rf), to_scan(_heads(w)), to_scan(k_dir), to_scan(vf), to_scan(kk), to_scan(a_h))
    state0 = jnp.zeros((2, bn, B_HEADS, HEAD_DIM, HEAD_DIM), jnp.float32)
    _, ys = lax.scan(_rwkv_step, state0, xs)
    ys = jnp.moveaxis(ys, 0, 2)
    y = ys[0] + ys[1][:, ::-1]
    mu = jnp.mean(y, axis=-1, keepdims=True)
    var = jnp.mean(jnp.square(y - mu), axis=-1, keepdims=True)
    yn = ((y - mu) * lax.rsqrt(var + GN_EPS)).reshape(bn, s, B_WIDTH) * gn_gain + gn_bias
    bonus = jnp.sum(jnp.sum(rf[None] * k_dir * r_k.astype(jnp.float32), axis=-1, keepdims=True) * vf[None], axis=0)
    return ((yn + bonus.reshape(bn, s, B_WIDTH)) * g).astype(dt)


def multiscale_pool(p, pool_w, pool_scale):
    bn, s, cdim = p.shape
    pf = p.astype(jnp.float32)
    cs = jnp.concatenate([jnp.zeros((bn, 1, cdim), jnp.float32), jnp.cumsum(pf, axis=1)], axis=1)
    t = jnp.arange(s)
    outs = []
    for gi, win in enumerate(POOL_WINDOWS):
        lo = jnp.clip(t - win // 2, 0, s - 1)
        hi = jnp.clip(t + win // 2 - 1, 0, s - 1)
        sl = slice(gi * POOL_CH, (gi + 1) * POOL_CH)
        tot = cs[:, hi + 1, sl] - cs[:, lo, sl]
        cnt = (hi - lo + 1).astype(jnp.float32)
        outs.append(tot / cnt[None, :, None] - pf[:, :, sl])
    pooled = jnp.stack(outs, axis=2).astype(p.dtype)
    y = jnp.einsum('bsgc,gce->bsge', pooled, pool_w).reshape(bn, s, cdim)
    return y * pool_scale


def hierarchical_moe(u, w_group, b_group, w_expert, b_expert, w_gate, w_up, w_down):
    bn, s, d = u.shape
    t = bn * s
    xt = u.reshape(t, d)
    gl = (xt @ w_group + b_group).astype(jnp.float32)
    pg = jax.nn.softmax(gl, axis=-1)
    g_idx = jnp.argmax(gl, axis=-1)
    pg_sel = jnp.take_along_axis(pg, g_idx[:, None], axis=-1)[:, 0]
    el = (xt @ w_expert + b_expert).astype(jnp.float32).reshape(t, N_GROUPS, EXPERTS_PER_GROUP)
    el = jnp.take_along_axis(el, g_idx[:, None, None], axis=1)[:, 0]
    top_l, top_i = lax.top_k(el, TOP_K)
    gate = pg_sel[:, None] * jax.nn.softmax(top_l, axis=-1)
    e_flat = (g_idx[:, None] * EXPERTS_PER_GROUP + top_i).reshape(-1)
    w_flat = gate.reshape(-1)
    tok = jnp.repeat(jnp.arange(t), TOP_K)
    n_assign = t * TOP_K
    order = jnp.argsort(e_flat)
    se, stok, sw = e_flat[order], tok[order], w_flat[order]
    counts = jnp.bincount(e_flat, length=N_EXPERTS)
    starts = jnp.cumsum(counts) - counts
    padded = ((counts + MOE_BLOCK - 1) // MOE_BLOCK) * MOE_BLOCK
    pends = jnp.cumsum(padded)
    pstarts = pends - padded
    dest = pstarts[se] + (jnp.arange(n_assign) - starts[se])
    n_blocks = -(-n_assign // MOE_BLOCK) + N_EXPERTS
    total = n_blocks * MOE_BLOCK
    buf_tok = jnp.full((total,), t, jnp.int32).at[dest].set(stok.astype(jnp.int32))
    buf_w = jnp.zeros((total,), jnp.float32).at[dest].set(sw)
    block_e = jnp.clip(jnp.searchsorted(pends, jnp.arange(n_blocks) * MOE_BLOCK, side='right'), 0, N_EXPERTS - 1)
    x_pad = jnp.concatenate([xt, jnp.zeros((1, d), xt.dtype)], axis=0)

    def expert_block(args):
        ti, e, wt = args
        xb = x_pad[ti]
        hb = jax.nn.silu(xb @ w_gate[e]) * (xb @ w_up[e])
        return (hb @ w_down[e]) * wt[:, None]

    y = lax.map(expert_block, (buf_tok.reshape(n_blocks, MOE_BLOCK), block_e,
                               buf_w.reshape(n_blocks, MOE_BLOCK).astype(u.dtype)))
    out = jax.ops.segment_sum(y.reshape(total, d), buf_tok, num_segments=t + 1)[:t]
    return out.reshape(bn, s, d)


def setup_inputs(seed: int = 0) -> dict:
    key = jax.random.key(seed)
    ks = jax.random.split(key, 32)
    f32 = jnp.float32
    L = DEPTH

    def nrm(k, shape, scale):
        return jax.random.normal(k, shape, f32) * scale

    col_scale = (jnp.ones((IN_WIDTH,), f32)
                 .at[2 * A_WIDTH:3 * A_WIDTH].set(BETA)
                 .at[3 * A_WIDTH + 2 * B_WIDTH:3 * A_WIDTH + 3 * B_WIDTH].set(BETA))
    return {
        "x": nrm(ks[0], (BATCH, SEQ, D_MODEL), 1.0),
        "c": nrm(ks[1], (BATCH, D_MODEL), 1.0),
        "w_mod": nrm(ks[2], (L, D_MODEL, 6 * D_MODEL), 0.5 * D_MODEL ** -0.5),
        "b_mod": nrm(ks[3], (L, 6 * D_MODEL), 0.02),
        "w_in": nrm(ks[4], (L, D_MODEL, IN_WIDTH), D_MODEL ** -0.5) * col_scale,
        "na_rpb": nrm(ks[5], (L, A_HEADS, 2 * NA_KH - 1, 2 * NA_KW - 1), 0.1),
        "rw_conv": jnp.array([0.25, 0.5, 0.25], f32)[:, None] + nrm(ks[6], (L, 3, 3 * B_WIDTH), 0.05),
        "rw_w0": jax.random.uniform(ks[7], (L, 2, B_WIDTH), f32, -3.0, 1.0),
        "rw_w_up": nrm(ks[8], (L, 2, R_W, B_WIDTH), 0.5 * R_W ** -0.5),
        "rw_a0": nrm(ks[9], (L, 2, B_WIDTH), 0.5),
        "rw_a_up": nrm(ks[10], (L, 2, R_A, B_WIDTH), 0.5 * R_A ** -0.5),
        "rw_g_up": nrm(ks[11], (L, R_G, B_WIDTH), R_G ** -0.5),
        "rw_k_k": 0.85 + nrm(ks[12], (L, B_WIDTH), 0.05),
        "rw_k_a": 1.0 + nrm(ks[13], (L, B_WIDTH), 0.05),
        "rw_r_k": nrm(ks[14], (L, B_HEADS, HEAD_DIM), 0.1),
        "rw_gn_gain": 1.0 + nrm(ks[15], (L, B_WIDTH), 0.05),
        "rw_gn_bias": nrm(ks[16], (L, B_WIDTH), 0.01),
        "pool_w": nrm(ks[17], (L, N_POOL_GROUPS, POOL_CH, POOL_CH), POOL_CH ** -0.5),
        "pool_scale": 1.0 + nrm(ks[18], (L, C_WIDTH), 0.05),
        "w_out": nrm(ks[19], (L, D_MIX, D_MODEL), BETA * D_MIX ** -0.5),
        "ln1_gain": 1.0 + nrm(ks[20], (L, D_MODEL), 0.05),
        "ln1_bias": nrm(ks[21], (L, D_MODEL), 0.01),
        "ln2_gain": 1.0 + nrm(ks[22], (L, D_MODEL), 0.05),
        "ln2_bias": nrm(ks[23], (L, D_MODEL), 0.01),
        "moe_w_group": nrm(ks[24], (L, D_MODEL, N_GROUPS), D_MODEL ** -0.5),
        "moe_b_group": nrm(ks[25], (L, N_GROUPS), 0.01),
        "moe_w_expert": nrm(ks[26], (L, D_MODEL, N_EXPERTS), D_MODEL ** -0.5),
        "moe_b_expert": nrm(ks[27], (L, N_EXPERTS), 0.01),
        "moe_w_gate": nrm(ks[28], (L, N_EXPERTS, D_MODEL, D_EXPERT), D_MODEL ** -0.5),
        "moe_w_up": nrm(ks[29], (L, N_EXPERTS, D_MODEL, D_EXPERT), BETA * D_MODEL ** -0.5),
        "moe_w_down": nrm(ks[30], (L, N_EXPERTS, D_EXPERT, D_MODEL), BETA * D_EXPERT ** -0.5),
    }


def reference(x, c, w_mod, b_mod, w_in, na_rpb, rw_conv, rw_w0, rw_w_up, rw_a0, rw_a_up, rw_g_up,
              rw_k_k, rw_k_a, rw_r_k, rw_gn_gain, rw_gn_bias, pool_w, pool_scale, w_out,
              ln1_gain, ln1_bias, ln2_gain, ln2_bias, moe_w_group, moe_b_group, moe_w_expert,
              moe_b_expert, moe_w_gate, moe_w_up, moe_w_down):
    o_a = 3 * A_WIDTH
    o_b = o_a + 3 * B_WIDTH
    o_w = o_b + R_W
    o_al = o_w + R_A
    o_g = o_al + R_G
    for l in range(DEPTH):
        mod = jax.nn.silu(c) @ w_mod[l] + b_mod[l]
        sh1, sc1, g1, sh2, sc2, g2 = [m[:, None, :] for m in jnp.split(mod, 6, axis=-1)]
        u = layer_norm(x) * (1.0 + sc1) + sh1
        h = u @ w_in[l]
        qa, ka, va = jnp.split(h[..., :o_a], 3, axis=-1)
        y_a = neighbourhood_attention(qa, ka, va, na_rpb[l])
        y_b = rwkv7_bidirectional(h[..., o_a:o_b], h[..., o_b:o_w], h[..., o_w:o_al], h[..., o_al:o_g],
                                  rw_conv[l], rw_w0[l], rw_w_up[l], rw_a0[l], rw_a_up[l], rw_g_up[l],
                                  rw_k_k[l], rw_k_a[l], rw_r_k[l], rw_gn_gain[l], rw_gn_bias[l])
        y_c = multiscale_pool(h[..., o_g:], pool_w[l], pool_scale[l])
        mix = jnp.concatenate([y_a, y_b.astype(y_a.dtype), y_c.astype(y_a.dtype)], axis=-1) @ w_out[l]
        x = layer_norm(ALPHA * x + g1 * mix) * ln1_gain[l] + ln1_bias[l]
        u2 = layer_norm(x) * (1.0 + sc2) + sh2
        f = hierarchical_moe(u2, moe_w_group[l], moe_b_group[l], moe_w_expert[l], moe_b_expert[l],
                             moe_w_gate[l], moe_w_up[l], moe_w_down[l])
        x = layer_norm(ALPHA * x + g2 * f) * ln2_gain[l] + ln2_bias[l]
    return x
```

```python
import functools
import math

import jax
import jax.numpy as jnp
from jax import lax
from jax.experimental import pallas as pl
from jax.experimental.pallas import tpu as pltpu

F32 = jnp.float32
BF16 = jnp.bfloat16
HI = lax.Precision.HIGHEST

GRID_W = 64
HEAD_DIM = 64
NA_KH = 8
NA_KW = 16
POOL_WINDOWS = (2, 4, 8, 16)
R_W = 32
R_A = 32
R_G = 64
DECAY_SCALE = math.exp(-0.5)
GN_EPS = 64e-5
N_GROUPS = 4
EXPERTS_PER_GROUP = 8
N_EXPERTS = N_GROUPS * EXPERTS_PER_GROUP
TOP_K = 2
MOE_BLOCK = 128
LN_EPS = 1e-5
NEG_INF = -1e30

SCAN_CHUNK = 64
HALO = 8
LANES = 128
VMEM_LIMIT = 52 * 1024 * 1024


def _ln(x):
    mu = jnp.mean(x, axis=-1, keepdims=True)
    xc = x - mu
    var = jnp.mean(xc * xc, axis=-1, keepdims=True)
    return xc * lax.rsqrt(var + LN_EPS)


def _sigmoid(x):
    return 1.0 / (1.0 + jnp.exp(-x))


def _cparams(n_axes, semantics="parallel"):
    return pltpu.CompilerParams(dimension_semantics=(semantics,) * n_axes, vmem_limit_bytes=VMEM_LIMIT)


def _mod_kernel(c_ref, w_ref, b_ref, o_ref):
    c = c_ref[...]
    s = c * _sigmoid(c)
    o_ref[0] = jnp.dot(s, w_ref[0], precision=HI, preferred_element_type=F32) + b_ref[0]


def _modulation(c, w_mod, b_mod):
    n_layers, d, d6 = w_mod.shape
    b = c.shape[0]
    bp = -(-b // 8) * 8
    cp = jnp.zeros((bp, d), F32).at[:b].set(c)
    out = pl.pallas_call(
        _mod_kernel,
        grid=(n_layers, d6 // d),
        in_specs=[pl.BlockSpec((bp, d), lambda l, j: (0, 0)),
                  pl.BlockSpec((1, d, d), lambda l, j: (l, 0, j)),
                  pl.BlockSpec((1, 1, d), lambda l, j: (l, 0, j))],
        out_specs=pl.BlockSpec((1, bp, d), lambda l, j: (l, 0, j)),
        out_shape=jax.ShapeDtypeStruct((n_layers, bp, d6), F32),
        compiler_params=_cparams(2),
        name="modulation",
    )(cp, w_mod, b_mod.reshape(n_layers, 1, d6))
    return out[:, :b].reshape(n_layers, b, d6 // d, d)


def _inproj_kernel(x_ref, mod_ref, w_ref, qkv_ref, rkv_ref, lr_ref, pool_ref, *, a3, b3, lr_w):
    m = mod_ref[0]
    u = _ln(x_ref[...]) * (1.0 + m[1:2]) + m[0:1]
    h = jnp.dot(u.astype(BF16), w_ref[...], preferred_element_type=F32)
    qkv_ref[...] = h[:, :a3].astype(BF16)
    rkv_ref[...] = h[:, a3:a3 + b3]
    lr_ref[...] = h[:, a3 + b3:a3 + b3 + lr_w]
    pool_ref[...] = h[:, a3 + b3 + lr_w:]


def _inproj(x2, modl, w_in_bf, seq, tm, a3, b3, lr_w, c_w):
    t, d = x2.shape
    tpb = seq // tm
    kern = functools.partial(_inproj_kernel, a3=a3, b3=b3, lr_w=lr_w)
    return pl.pallas_call(
        kern,
        grid=(t // tm,),
        in_specs=[pl.BlockSpec((tm, d), lambda i: (i, 0)),
                  pl.BlockSpec((1,) + modl.shape[1:], lambda i: (i // tpb, 0, 0)),
                  pl.BlockSpec(w_in_bf.shape, lambda i: (0, 0))],
        out_specs=[pl.BlockSpec((tm, a3), lambda i: (i, 0)),
                   pl.BlockSpec((tm, b3), lambda i: (i, 0)),
                   pl.BlockSpec((tm, lr_w), lambda i: (i, 0)),
                   pl.BlockSpec((tm, c_w), lambda i: (i, 0))],
        out_shape=[jax.ShapeDtypeStruct((t, a3), BF16),
                   jax.ShapeDtypeStruct((t, b3), F32),
                   jax.ShapeDtypeStruct((t, lr_w), F32),
                   jax.ShapeDtypeStruct((t, c_w), F32)],
        compiler_params=_cparams(1),
        name="inproj",
    )(x2, modl, w_in_bf)


def _na_bias_table(rpb):
    col = jnp.arange(GRID_W)
    cstart = jnp.clip(col - NA_KW // 2, 0, GRID_W - NA_KW)
    in_win = (col[None, :] >= cstart[:, None]) & (col[None, :] < cstart[:, None] + NA_KW)
    dc = jnp.clip(col[None, :] - col[:, None], -(NA_KW - 1), NA_KW - 1) + (NA_KW - 1)
    dr = jnp.arange(NA_KH)[None, :] - jnp.arange(NA_KH)[:, None] + (NA_KH - 1)
    b = rpb.astype(F32)[:, dr][..., dc]
    b = jnp.where(in_win, b, NEG_INF)
    h = rpb.shape[0]
    return jnp.transpose(b, (0, 1, 3, 2, 4)).reshape(h, NA_KH, GRID_W, NA_KH * GRID_W)


def _natten_kernel(q_ref, k_ref, v_ref, bias_ref, o_ref, *, rows, heads):
    r = pl.program_id(1)
    rstart = jnp.clip(r - NA_KH // 2, 0, rows - NA_KH)
    off = r - rstart
    start = pl.multiple_of(rstart * GRID_W, GRID_W)
    nk = NA_KH * GRID_W
    kw = k_ref[pl.ds(start, nk), :]
    vw = v_ref[pl.ds(start, nk), :]
    q = q_ref[...]
    width = q.shape[1]
    lane = lax.broadcasted_iota(jnp.int32, (GRID_W, width), 1)
    scale = HEAD_DIM ** -0.5
    acc = jnp.zeros((GRID_W, width), F32)
    for h in range(heads):
        hm = (lane >= h * HEAD_DIM) & (lane < (h + 1) * HEAD_DIM)
        qh = jnp.where(hm, q, jnp.zeros_like(q))
        s = lax.dot_general(qh, kw, (((1,), (1,)), ((), ())), preferred_element_type=F32) * scale
        s = s + bias_ref[h, off]
        mx = jnp.max(s, axis=-1, keepdims=True)
        p = jnp.exp(s - mx)
        den = jnp.sum(p, axis=-1, keepdims=True)
        oh = jnp.dot(p.astype(BF16), vw, preferred_element_type=F32) / den
        acc = jnp.where(hm, oh, acc)
    o_ref[...] = acc


def _natten(qkv, bias_tab, batch, seq, width):
    rows = seq // GRID_W
    assert rows >= NA_KH
    heads = width // HEAD_DIM
    kern = functools.partial(_natten_kernel, rows=rows, heads=heads)
    return pl.pallas_call(
        kern,
        grid=(batch, rows),
        in_specs=[pl.BlockSpec((GRID_W, width), lambda b, r: (b * rows + r, 0)),
                  pl.BlockSpec((seq, width), lambda b, r: (b, 1)),
                  pl.BlockSpec((seq, width), lambda b, r: (b, 2)),
                  pl.BlockSpec(bias_tab.shape, lambda b, r: (0, 0, 0, 0))],
        out_specs=pl.BlockSpec((GRID_W, width), lambda b, r: (b * rows + r, 0)),
        out_shape=jax.ShapeDtypeStruct((batch * seq, width), F32),
        compiler_params=_cparams(2),
        name="natten",
    )(qkv, qkv, qkv, bias_tab)


def _rwkv_prep_kernel(z_ref, zp_ref, zn_ref, lr_ref, cw_ref, w0_ref, wup_ref, a0_ref, aup_ref, gup_ref,
                      kk_ref, ka_ref, rk_ref, ones_ref,
                      r_o, v_o, nkk_o, lw_o, b_o, kd_o, bonus_o, g_o, *, tiles_per_batch, width):
    i = pl.program_id(0)
    tb = i % tiles_per_batch
    z = z_ref[...]
    tm = z.shape[0]
    prev = jnp.where(tb == 0, 0.0, zp_ref[HALO - 1:HALO, :])
    nxt = jnp.where(tb == tiles_per_batch - 1, 0.0, zn_ref[0:1, :])
    row = lax.broadcasted_iota(jnp.int32, z.shape, 0)
    zm1 = jnp.where(row == 0, prev, pltpu.roll(z, 1, 0))
    zp1 = jnp.where(row == tm - 1, nxt, pltpu.roll(z, tm - 1, 0))
    rkv = zm1 * cw_ref[0:1, :] + z * cw_ref[1:2, :] + zp1 * cw_ref[2:3, :]
    r = rkv[:, :width]
    k = rkv[:, width:2 * width]
    v = rkv[:, 2 * width:]
    lr = lr_ref[...]
    th = jnp.tanh(lr)
    sg = _sigmoid(lr)
    ones = ones_ref[...]

    def headsum(x):
        return jnp.dot(x, ones, precision=HI, preferred_element_type=F32)

    kk = k * kk_ref[...]
    kk = kk * lax.rsqrt(jnp.maximum(headsum(kk * kk), 1e-24))
    g_o[...] = jnp.dot(sg, gup_ref[...], precision=HI, preferred_element_type=F32)
    r_o[...] = r
    v_o[...] = v
    nkk_o[...] = -kk
    bonus = jnp.zeros_like(r)
    for d in range(2):
        wl = jnp.dot(th, wup_ref[d], precision=HI, preferred_element_type=F32) + w0_ref[d:d + 1, :]
        lw_o[d] = -DECAY_SCALE * _sigmoid(wl)
        a = _sigmoid(jnp.dot(lr, aup_ref[d], precision=HI, preferred_element_type=F32) + a0_ref[d:d + 1, :])
        kd = k * (1.0 + (a - 1.0) * ka_ref[...])
        kd_o[d] = kd
        b_o[d] = kk * a
        bonus = bonus + headsum(r * kd * rk_ref[...]) * v
    bonus_o[...] = bonus


def _rwkv_prep(rkv_raw, lr, p, seq, tm):
    t, w3 = rkv_raw.shape
    width = w3 // 3
    tpb = seq // tm
    hb = tm // HALO
    nhb = t // HALO
    kern = functools.partial(_rwkv_prep_kernel, tiles_per_batch=tpb, width=width)
    tok = lambda i: (i, 0)
    dtok = lambda i: (0, i, 0)
    full2 = lambda i: (0, 0)
    full3 = lambda i: (0, 0, 0)
    tw = jax.ShapeDtypeStruct((t, width), F32)
    dtw = jax.ShapeDtypeStruct((2, t, width), F32)
    return pl.pallas_call(
        kern,
        grid=(t // tm,),
        in_specs=[pl.BlockSpec((tm, w3), tok),
                  pl.BlockSpec((HALO, w3), lambda i: (jnp.maximum(i * hb - 1, 0), 0)),
                  pl.BlockSpec((HALO, w3), lambda i: (jnp.minimum((i + 1) * hb, nhb - 1), 0)),
                  pl.BlockSpec((tm, lr.shape[1]), tok),
                  pl.BlockSpec(p["conv"].shape, full2),
                  pl.BlockSpec(p["w0"].shape, full2),
                  pl.BlockSpec(p["w_up"].shape, full3),
                  pl.BlockSpec(p["a0"].shape, full2),
                  pl.BlockSpec(p["a_up"].shape, full3),
                  pl.BlockSpec(p["g_up"].shape, full2),
                  pl.BlockSpec(p["k_k"].shape, full2),
                  pl.BlockSpec(p["k_a"].shape, full2),
                  pl.BlockSpec(p["r_k"].shape, full2),
                  pl.BlockSpec(p["ones"].shape, full2)],
        out_specs=[pl.BlockSpec((tm, width), tok), pl.BlockSpec((tm, width), tok), pl.BlockSpec((tm, width), tok),
                   pl.BlockSpec((2, tm, width), dtok), pl.BlockSpec((2, tm, width), dtok),
                   pl.BlockSpec((2, tm, width), dtok),
                   pl.BlockSpec((tm, width), tok), pl.BlockSpec((tm, width), tok)],
        out_shape=[tw, tw, tw, dtw, dtw, dtw, tw, tw],
        compiler_params=_cparams(1),
        name="rwkv_prep",
    )(rkv_raw, rkv_raw, rkv_raw, lr, p["conv"], p["w0"], p["w_up"], p["a0"], p["a_up"], p["g_up"],
      p["k_k"], p["k_a"], p["r_k"], p["ones"])


def _tri_inverse(l_mat, same):
    t = same[0].astype(F32) + jnp.where(same[1], l_mat, 0.0)
    for sh in range(1, len(same) - 1):
        cs = jnp.where(same[sh + 1] & jnp.logical_not(same[sh]), l_mat, 0.0)
        t = t + jnp.dot(jnp.dot(t, cs, preferred_element_type=F32), t, preferred_element_type=F32)
    return t


def _dot_nt(a, b):
    return lax.dot_general(a, b, (((1,), (1,)), ((), ())), preferred_element_type=F32)


def _dot_tn(a, b):
    return lax.dot_general(a, b, (((0,), (0,)), ((), ())), preferred_element_type=F32)


def _rwkv_scan_kernel(r_ref, v_ref, nkk_ref, lw_ref, b_ref, kd_ref, y_ref, s_ref, *, heads):
    d = pl.program_id(1)
    c = pl.program_id(2)

    @pl.when(c == 0)
    def _():
        s_ref[...] = jnp.zeros_like(s_ref)

    n = SCAN_CHUNK
    row = lax.broadcasted_iota(jnp.int32, (n, n), 0)
    col = lax.broadcasted_iota(jnp.int32, (n, n), 1)
    order = jnp.where(d == 0, row - col, col - row)
    strict = order > 0
    incl = order >= 0
    levels = n.bit_length()
    same = [(row >> k) == (col >> k) for k in range(levels)]

    lw = lw_ref[0]
    g_inc = jnp.dot(incl.astype(F32), lw, precision=HI, preferred_element_type=F32)
    g_exc = g_inc - lw
    g_tot = jnp.sum(lw, axis=0, keepdims=True)
    e_inc = jnp.exp(g_inc)
    e_neg = jnp.exp(-g_inc)
    e_end = jnp.exp(g_tot - g_inc)
    decay = jnp.exp(g_tot)
    a_t = nkk_ref[...] * jnp.exp(g_exc)
    r_t = r_ref[...] * e_inc
    bb = b_ref[0]
    kd = kd_ref[0]
    b_t = bb * e_neg
    k_t = kd * e_neg
    b_h = bb * e_end
    k_h = kd * e_end
    v = v_ref[...]

    for h in range(heads):
        sl = slice(h * HEAD_DIM, (h + 1) * HEAD_DIM)
        ah, rh, bh, kh, vh = a_t[:, sl], r_t[:, sl], b_t[:, sl], k_t[:, sl], v[:, sl]
        l_ab = jnp.where(strict, _dot_nt(ah, bh), 0.0)
        l_ak = jnp.where(strict, _dot_nt(ah, kh), 0.0)
        m_rb = jnp.where(incl, _dot_nt(rh, bh), 0.0)
        m_rk = jnp.where(incl, _dot_nt(rh, kh), 0.0)
        t_inv = _tri_inverse(l_ab, same)
        s0 = s_ref[h]
        rhs = _dot_nt(ah, s0) + jnp.dot(l_ak, vh, preferred_element_type=F32)
        u = jnp.dot(t_inv, rhs, preferred_element_type=F32)
        y = (_dot_nt(rh, s0) + jnp.dot(m_rb, u, preferred_element_type=F32)
             + jnp.dot(m_rk, vh, preferred_element_type=F32))
        y_ref[0, :, sl] = y
        s_ref[h] = s0 * decay[:, sl] + _dot_tn(u, b_h[:, sl]) + _dot_tn(vh, k_h[:, sl])


def _rwkv_scan(r, v, nkk, lw, b, kd, batch, seq):
    t, width = r.shape
    heads = width // HEAD_DIM
    n = SCAN_CHUNK
    nc = seq // n

    def cidx(bi, d, c):
        return bi * nc + jnp.where(d == 0, c, nc - 1 - c)

    tok = lambda bi, d, c: (cidx(bi, d, c), 0)
    dtok = lambda bi, d, c: (d, cidx(bi, d, c), 0)
    kern = functools.partial(_rwkv_scan_kernel, heads=heads)
    return pl.pallas_call(
        kern,
        grid=(batch, 2, nc),
        in_specs=[pl.BlockSpec((n, width), tok), pl.BlockSpec((n, width), tok), pl.BlockSpec((n, width), tok),
                  pl.BlockSpec((1, n, width), dtok), pl.BlockSpec((1, n, width), dtok),
                  pl.BlockSpec((1, n, width), dtok)],
        out_specs=pl.BlockSpec((1, n, width), dtok),
        out_shape=jax.ShapeDtypeStruct((2, t, width), F32),
        scratch_shapes=[pltpu.VMEM((heads, HEAD_DIM, HEAD_DIM), F32)],
        compiler_params=_cparams(3, "arbitrary"),
        name="rwkv_scan",
    )(r, v, nkk, lw, b, kd)


def _pool_kernel(p_ref, pp_ref, pn_ref, w_ref, sc_ref, o_ref, ext_ref, *, tiles_per_batch, seq):
    i = pl.program_id(0)
    tb = i % tiles_per_batch
    p = p_ref[...]
    tm, width = p.shape
    ext_ref[0:HALO, :] = jnp.where(tb == 0, 0.0, pp_ref[...])
    ext_ref[HALO:HALO + tm, :] = p
    ext_ref[HALO + tm:2 * HALO + tm, :] = jnp.where(tb == tiles_per_batch - 1, 0.0, pn_ref[...])

    def shifted(o):
        return ext_ref[HALO + o:HALO + o + tm, :]

    t = tb * tm + lax.broadcasted_iota(jnp.int32, (tm, width), 0)
    grp = lax.broadcasted_iota(jnp.int32, (tm, width), 1) // (width // len(POOL_WINDOWS))
    tot = p
    prev_half = 0
    pooled = jnp.zeros_like(p)
    for gi, win in enumerate(POOL_WINDOWS):
        half = win // 2
        for o in range(prev_half, half):
            tot = tot + shifted(-o - 1)
            if o > 0:
                tot = tot + shifted(o)
        prev_half = half
        lo = jnp.clip(t - half, 0, seq - 1)
        hi = jnp.clip(t + half - 1, 0, seq - 1)
        cnt = (hi - lo + 1).astype(F32)
        pooled = jnp.where(grp == gi, tot / cnt, pooled)
    pooled = pooled - p
    o_ref[...] = jnp.dot(pooled, w_ref[...], preferred_element_type=F32) * sc_ref[...]


def _pool(praw, w_blk, scale, seq, tm):
    t, width = praw.shape
    tpb = seq // tm
    hb = tm // HALO
    nhb = t // HALO
    kern = functools.partial(_pool_kernel, tiles_per_batch=tpb, seq=seq)
    return pl.pallas_call(
        kern,
        grid=(t // tm,),
        in_specs=[pl.BlockSpec((tm, width), lambda i: (i, 0)),
                  pl.BlockSpec((HALO, width), lambda i: (jnp.maximum(i * hb - 1, 0), 0)),
                  pl.BlockSpec((HALO, width), lambda i: (jnp.minimum((i + 1) * hb, nhb - 1), 0)),
                  pl.BlockSpec(w_blk.shape, lambda i: (0, 0)),
                  pl.BlockSpec(scale.shape, lambda i: (0, 0))],
        out_specs=pl.BlockSpec((tm, width), lambda i: (i, 0)),
        out_shape=jax.ShapeDtypeStruct((t, width), F32),
        scratch_shapes=[pltpu.VMEM((tm + 2 * HALO, width), F32)],
        compiler_params=_cparams(1),
        name="pool",
    )(praw, praw, praw, w_blk, scale)


def _outproj_kernel(ya_ref, ys_ref, bonus_ref, g_ref, yc_ref, x_ref, mod_ref, wa_ref, wb_ref, wc_ref,
                    gng_ref, gnb_ref, ones_ref, l1g_ref, l1b_ref, wr_ref, br_ref,
                    x1_o, u2_o, ri_o, rw_o, *, alpha):
    m = mod_ref[0]
    ones = ones_ref[...]

    def headmean(x):
        return jnp.dot(x, ones, precision=HI, preferred_element_type=F32) * (1.0 / HEAD_DIM)

    ysum = ys_ref[0] + ys_ref[1]
    yc0 = ysum - headmean(ysum)
    yn = yc0 * lax.rsqrt(headmean(yc0 * yc0) + GN_EPS) * gng_ref[...] + gnb_ref[...]
    yb = (yn + bonus_ref[...]) * g_ref[...]
    mix = (jnp.dot(ya_ref[...].astype(BF16), wa_ref[...], preferred_element_type=F32)
           + jnp.dot(yb.astype(BF16), wb_ref[...], preferred_element_type=F32)
           + jnp.dot(yc_ref[...].astype(BF16), wc_ref[...], preferred_element_type=F32))
    x1 = _ln(alpha * x_ref[...] + m[2:3] * mix) * l1g_ref[...] + l1b_ref[...]
    x1_o[...] = x1
    u2 = _ln(x1) * (1.0 + m[4:5]) + m[3:4]
    u2_o[...] = u2

    lg = jnp.dot(u2, wr_ref[...], precision=HI, preferred_element_type=F32) + br_ref[...]
    lane = lax.broadcasted_iota(jnp.int32, lg.shape, 1)
    big = jnp.int32(1 << 20)
    gl = jnp.where(lane < N_GROUPS, lg, -jnp.inf)
    gmax = jnp.max(gl, axis=-1, keepdims=True)
    gidx = jnp.min(jnp.where(gl == gmax, lane, big), axis=-1, keepdims=True)
    pg_sel = 1.0 / jnp.sum(jnp.exp(gl - gmax), axis=-1, keepdims=True)
    e_lo = N_GROUPS + gidx * EXPERTS_PER_GROUP
    el = jnp.where((lane >= e_lo) & (lane < e_lo + EXPERTS_PER_GROUP), lg, -jnp.inf)
    m1 = jnp.max(el, axis=-1, keepdims=True)
    i1 = jnp.min(jnp.where(el == m1, lane, big), axis=-1, keepdims=True)
    el2 = jnp.where(lane == i1, -jnp.inf, el)
    m2 = jnp.max(el2, axis=-1, keepdims=True)
    i2 = jnp.min(jnp.where(el2 == m2, lane, big), axis=-1, keepdims=True)
    e21 = jnp.exp(m2 - m1)
    p1 = 1.0 / (1.0 + e21)
    p2 = e21 / (1.0 + e21)
    ri_o[...] = jnp.where(lane == 0, i1 - N_GROUPS, jnp.where(lane == 1, i2 - N_GROUPS, 0))
    rw_o[...] = jnp.where(lane == 0, pg_sel * p1, jnp.where(lane == 1, pg_sel * p2, 0.0))


def _outproj(ya, ys, bonus, g, yc, x2, modl, p, seq, tm, alpha):
    t, d = x2.shape
    tpb = seq // tm
    aw, bw, cw = ya.shape[1], bonus.shape[1], yc.shape[1]
    tok = lambda i: (i, 0)
    full2 = lambda i: (0, 0)
    kern = functools.partial(_outproj_kernel, alpha=alpha)
    small = ["gn_gain", "gn_bias", "ones", "ln1_gain", "ln1_bias", "w_router", "b_router"]
    return pl.pallas_call(
        kern,
        grid=(t // tm,),
        in_specs=[pl.BlockSpec((tm, aw), tok),
                  pl.BlockSpec((2, tm, bw), lambda i: (0, i, 0)),
                  pl.BlockSpec((tm, bw), tok), pl.BlockSpec((tm, bw), tok),
                  pl.BlockSpec((tm, cw), tok),
                  pl.BlockSpec((tm, d), tok),
                  pl.BlockSpec((1,) + modl.shape[1:], lambda i: (i // tpb, 0, 0)),
                  pl.BlockSpec(p["w_out_a"].shape, full2),
                  pl.BlockSpec(p["w_out_b"].shape, full2),
                  pl.BlockSpec(p["w_out_c"].shape, full2)]
                 + [pl.BlockSpec(p[k].shape, full2) for k in small],
        out_specs=[pl.BlockSpec((tm, d), tok), pl.BlockSpec((tm, d), tok),
                   pl.BlockSpec((tm, LANES), tok), pl.BlockSpec((tm, LANES), tok)],
        out_shape=[jax.ShapeDtypeStruct((t, d), F32), jax.ShapeDtypeStruct((t, d), F32),
                   jax.ShapeDtypeStruct((t, LANES), jnp.int32), jax.ShapeDtypeStruct((t, LANES), F32)],
        compiler_params=_cparams(1),
        name="outproj",
    )(ya, ys, bonus, g, yc, x2, modl, p["w_out_a"], p["w_out_b"], p["w_out_c"], *[p[k] for k in small])


def _experts_kernel(be_ref, xs_ref, wg_ref, wu_ref, wd_ref, o_ref):
    del be_ref
    xb = xs_ref[...].astype(BF16)
    gate = jnp.dot(xb, wg_ref[0], preferred_element_type=F32)
    up = jnp.dot(xb, wu_ref[0], preferred_element_type=F32)
    hb = gate * _sigmoid(gate) * up
    o_ref[...] = jnp.dot(hb.astype(BF16), wd_ref[0], preferred_element_type=F32)


def _experts(block_e, xs, wg, wu, wd):
    total, d = xs.shape
    nb = total // MOE_BLOCK
    de = wg.shape[2]
    grid_spec = pltpu.PrefetchScalarGridSpec(
        num_scalar_prefetch=1,
        grid=(nb,),
        in_specs=[pl.BlockSpec((MOE_BLOCK, d), lambda i, be: (i, 0)),
                  pl.BlockSpec((1, d, de), lambda i, be: (be[i], 0, 0)),
                  pl.BlockSpec((1, d, de), lambda i, be: (be[i], 0, 0)),
                  pl.BlockSpec((1, de, d), lambda i, be: (be[i], 0, 0))],
        out_specs=pl.BlockSpec((MOE_BLOCK, d), lambda i, be: (i, 0)),
    )
    return pl.pallas_call(
        _experts_kernel,
        grid_spec=grid_spec,
        out_shape=jax.ShapeDtypeStruct((total, d), F32),
        compiler_params=_cparams(1, "arbitrary"),
        name="experts",
    )(block_e, xs, wg, wu, wd)


def _dispatch(route_i, n_tokens):
    e_flat = route_i[:, :TOP_K].reshape(-1)
    n_assign = e_flat.shape[0]
    onehot = (e_flat[:, None] == jnp.arange(N_EXPERTS, dtype=jnp.int32)[None, :]).astype(jnp.int32)
    csum = jnp.cumsum(onehot, axis=0)
    rank = jnp.sum((csum - onehot) * onehot, axis=1)
    counts = csum[-1]
    padded = ((counts + MOE_BLOCK - 1) // MOE_BLOCK) * MOE_BLOCK
    pends = jnp.cumsum(padded)
    pstarts = pends - padded
    dest = pstarts[e_flat] + rank
    n_blocks = -(-n_assign // MOE_BLOCK) + N_EXPERTS
    total = n_blocks * MOE_BLOCK
    tok = jnp.arange(n_assign, dtype=jnp.int32) // TOP_K
    buf_tok = jnp.zeros((total,), jnp.int32).at[dest].set(tok)
    block_e = jnp.clip(jnp.searchsorted(pends, jnp.arange(n_blocks, dtype=jnp.int32) * MOE_BLOCK, side="right"),
                       0, N_EXPERTS - 1).astype(jnp.int32)
    del n_tokens
    return dest, buf_tok, block_e


def _final_kernel(x1_ref, yp_ref, rw_ref, mod_ref, g_ref, b_ref, o_ref, *, alpha):
    m = mod_ref[0]
    d = x1_ref.shape[1]
    rw = rw_ref[...]
    f = rw[:, 0:1] * yp_ref[:, :d] + rw[:, 1:2] * yp_ref[:, d:]
    o_ref[...] = _ln(alpha * x1_ref[...] + m[5:6] * f) * g_ref[...] + b_ref[...]


def _final(x1, ypair, rw, modl, gain, bias, seq, tm, alpha):
    t, d = x1.shape
    tpb = seq // tm
    tok = lambda i: (i, 0)
    kern = functools.partial(_final_kernel, alpha=alpha)
    return pl.pallas_call(
        kern,
        grid=(t // tm,),
        in_specs=[pl.BlockSpec((tm, d), tok), pl.BlockSpec((tm, 2 * d), tok), pl.BlockSpec((tm, LANES), tok),
                  pl.BlockSpec((1,) + modl.shape[1:], lambda i: (i // tpb, 0, 0)),
                  pl.BlockSpec(gain.shape, lambda i: (0, 0)), pl.BlockSpec(bias.shape, lambda i: (0, 0))],
        out_specs=pl.BlockSpec((tm, d), tok),
        out_shape=jax.ShapeDtypeStruct((t, d), F32),
        compiler_params=_cparams(1),
        name="final_ln",
    )(x1, ypair, rw, modl, gain, bias)


def _block_diag(blocks):
    n, a, b = blocks.shape
    out = jnp.zeros((n * a, n * b), blocks.dtype)
    for i in range(n):
        out = out.at[i * a:(i + 1) * a, i * b:(i + 1) * b].set(blocks[i])
    return out


def _pad_rows(w, lo, total):
    return jnp.zeros((total, w.shape[-1]), w.dtype).at[lo:lo + w.shape[0]].set(w)


def kernel(x, c, w_mod, b_mod, w_in, na_rpb, rw_conv, rw_w0, rw_w_up, rw_a0, rw_a_up, rw_g_up, rw_k_k, rw_k_a, rw_r_k, rw_gn_gain, rw_gn_bias, pool_w, pool_scale, w_out, ln1_gain, ln1_bias, ln2_gain, ln2_bias, moe_w_group, moe_b_group, moe_w_expert, moe_b_expert, moe_w_gate, moe_w_up, moe_w_down):
    batch, seq, d = x.shape
    depth = w_mod.shape[0]
    t = batch * seq
    a_w = na_rpb.shape[1] * HEAD_DIM
    b_w = rw_w0.shape[-1]
    c_w = pool_scale.shape[-1]
    lr_w = R_W + R_A + R_G
    alpha = (2 * depth) ** 0.25
    tm = min(512, seq)
    tm_prep = min(256, seq)
    assert seq % tm == 0 and seq % SCAN_CHUNK == 0 and seq % GRID_W == 0 and lr_w == LANES

    mod = _modulation(c, w_mod, b_mod)
    ones_blk = _block_diag(jnp.ones((b_w // HEAD_DIM, HEAD_DIM, HEAD_DIM), F32))
    row = lambda v: v.reshape(1, -1)

    x2 = x.reshape(t, d)
    for l in range(depth):
        modl = mod[l]
        qkv, rkv_raw, lr, praw = _inproj(x2, modl, w_in[l].astype(BF16), seq, tm, 3 * a_w, 3 * b_w, lr_w, c_w)
        ya = _natten(qkv, _na_bias_table(na_rpb[l]), batch, seq, a_w)
        prep_params = {
            "conv": rw_conv[l], "w0": rw_w0[l], "a0": rw_a0[l],
            "w_up": jnp.stack([_pad_rows(rw_w_up[l, dd], 0, lr_w) for dd in range(2)]),
            "a_up": jnp.stack([_pad_rows(rw_a_up[l, dd], R_W, lr_w) for dd in range(2)]),
            "g_up": _pad_rows(rw_g_up[l], R_W + R_A, lr_w),
            "k_k": row(rw_k_k[l]), "k_a": row(rw_k_a[l]), "r_k": row(rw_r_k[l]), "ones": ones_blk,
        }
        r, v, nkk, lw, bb, kd, bonus, g = _rwkv_prep(rkv_raw, lr, prep_params, seq, tm_prep)
        ys = _rwkv_scan(r, v, nkk, lw, bb, kd, batch, seq)
        yc = _pool(praw, _block_diag(pool_w[l]), row(pool_scale[l]), seq, tm)
        w_router = jnp.zeros((d, LANES), F32).at[:, :N_GROUPS].set(moe_w_group[l])
        w_router = w_router.at[:, N_GROUPS:N_GROUPS + N_EXPERTS].set(moe_w_expert[l])
        b_router = jnp.zeros((1, LANES), F32).at[0, :N_GROUPS].set(moe_b_group[l])
        b_router = b_router.at[0, N_GROUPS:N_GROUPS + N_EXPERTS].set(moe_b_expert[l])
        wo = w_out[l].astype(BF16)
        out_params = {
            "w_out_a": wo[:a_w], "w_out_b": wo[a_w:a_w + b_w], "w_out_c": wo[a_w + b_w:],
            "gn_gain": row(rw_gn_gain[l]), "gn_bias": row(rw_gn_bias[l]), "ones": ones_blk,
            "ln1_gain": row(ln1_gain[l]), "ln1_bias": row(ln1_bias[l]),
            "w_router": w_router, "b_router": b_router,
        }
        x1, u2, route_i, route_w = _outproj(ya, ys, bonus, g, yc, x2, modl, out_params, seq, tm, alpha)
        dest, buf_tok, block_e = _dispatch(route_i, t)
        xs = jnp.take(u2, buf_tok, axis=0)
        ysorted = _experts(block_e, xs, moe_w_gate[l].astype(BF16), moe_w_up[l].astype(BF16),
                           moe_w_down[l].astype(BF16))
        ypair = jnp.take(ysorted, dest, axis=0).reshape(t, TOP_K * d)
        x2 = _final(x1, ypair, route_w, modl, row(ln2_gain[l]), row(ln2_bias[l]), seq, tm, alpha)
    return x2.reshape(batch, seq, d)
```

```python
import functools
import math

import jax
import jax.numpy as jnp
from jax import lax
from jax.experimental import pallas as pl
from jax.experimental.pallas import tpu as pltpu

F32 = jnp.float32
BF16 = jnp.bfloat16
HI = lax.Precision.HIGHEST

GRID_W = 64
HEAD_DIM = 64
NA_KH = 8
NA_KW = 16
POOL_WINDOWS = (2, 4, 8, 16)
R_W = 32
R_A = 32
R_G = 64
DECAY_SCALE = math.exp(-0.5)
GN_EPS = 64e-5
N_GROUPS = 4
EXPERTS_PER_GROUP = 8
N_EXPERTS = N_GROUPS * EXPERTS_PER_GROUP
TOP_K = 2
MOE_BLOCK = 128
LN_EPS = 1e-5
NEG_INF = -1e30

SCAN_CHUNK = 64
HALO = 8
LANES = 128
VMEM_LIMIT = 52 * 1024 * 1024


def _ln(x):
    mu = jnp.mean(x, axis=-1, keepdims=True)
    xc = x - mu
    var = jnp.mean(xc * xc, axis=-1, keepdims=True)
    return xc * lax.rsqrt(var + LN_EPS)


def _sigmoid(x):
    return 1.0 / (1.0 + jnp.exp(-x))


def _cparams(n_axes, semantics="parallel"):
    return pltpu.CompilerParams(dimension_semantics=(semantics,) * n_axes, vmem_limit_bytes=VMEM_LIMIT)


def _mod_kernel(c_ref, w_ref, b_ref, o_ref):
    c = c_ref[...]
    s = c * _sigmoid(c)
    o_ref[0] = jnp.dot(s, w_ref[0], precision=HI, preferred_element_type=F32) + b_ref[0]


def _modulation(c, w_mod, b_mod):
    n_layers, d, d6 = w_mod.shape
    b = c.shape[0]
    bp = -(-b // 8) * 8
    cp = jnp.zeros((bp, d), F32).at[:b].set(c)
    out = pl.pallas_call(
        _mod_kernel,
        grid=(n_layers, d6 // d),
        in_specs=[pl.BlockSpec((bp, d), lambda l, j: (0, 0)),
                  pl.BlockSpec((1, d, d), lambda l, j: (l, 0, j)),
                  pl.BlockSpec((1, 1, d), lambda l, j: (l, 0, j))],
        out_specs=pl.BlockSpec((1, bp, d), lambda l, j: (l, 0, j)),
        out_shape=jax.ShapeDtypeStruct((n_layers, bp, d6), F32),
        compiler_params=_cparams(2),
        name="modulation",
    )(cp, w_mod, b_mod.reshape(n_layers, 1, d6))
    return out[:, :b].reshape(n_layers, b, d6 // d, d)


def _inproj_kernel(x_ref, mod_ref, w_ref, qkv_ref, rkv_ref, lr_ref, pool_ref, *, a3, b3, lr_w):
    m = mod_ref[0]
    u = _ln(x_ref[...]) * (1.0 + m[1:2]) + m[0:1]
    h = jnp.dot(u.astype(BF16), w_ref[...], preferred_element_type=F32)
    qkv_ref[...] = h[:, :a3].astype(BF16)
    rkv_ref[...] = h[:, a3:a3 + b3]
    lr_ref[...] = h[:, a3 + b3:a3 + b3 + lr_w]
    pool_ref[...] = h[:, a3 + b3 + lr_w:]


def _inproj(x2, modl, w_in_bf, seq, tm, a3, b3, lr_w, c_w):
    t, d = x2.shape
    tpb = seq // tm
    kern = functools.partial(_inproj_kernel, a3=a3, b3=b3, lr_w=lr_w)
    return pl.pallas_call(
        kern,
        grid=(t // tm,),
        in_specs=[pl.BlockSpec((tm, d), lambda i: (i, 0)),
                  pl.BlockSpec((1,) + modl.shape[1:], lambda i: (i // tpb, 0, 0)),
                  pl.BlockSpec(w_in_bf.shape, lambda i: (0, 0))],
        out_specs=[pl.BlockSpec((tm, a3), lambda i: (i, 0)),
                   pl.BlockSpec((tm, b3), lambda i: (i, 0)),
                   pl.BlockSpec((tm, lr_w), lambda i: (i, 0)),
                   pl.BlockSpec((tm, c_w), lambda i: (i, 0))],
        out_shape=[jax.ShapeDtypeStruct((t, a3), BF16),
                   jax.ShapeDtypeStruct((t, b3), F32),
                   jax.ShapeDtypeStruct((t, lr_w), F32),
                   jax.ShapeDtypeStruct((t, c_w), F32)],
        compiler_params=_cparams(1),
        name="inproj",
    )(x2, modl, w_in_bf)


def _na_bias_table(rpb):
    col = jnp.arange(GRID_W)
    cstart = jnp.clip(col - NA_KW // 2, 0, GRID_W - NA_KW)
    in_win = (col[None, :] >= cstart[:, None]) & (col[None, :] < cstart[:, None] + NA_KW)
    dc = jnp.clip(col[None, :] - col[:, None], -(NA_KW - 1), NA_KW - 1) + (NA_KW - 1)
    dr = jnp.arange(NA_KH)[None, :] - jnp.arange(NA_KH)[:, None] + (NA_KH - 1)
    b = rpb.astype(F32)[:, dr][..., dc]
    b = jnp.where(in_win, b, NEG_INF)
    h = rpb.shape[0]
    return jnp.transpose(b, (0, 1, 3, 2, 4)).reshape(h, NA_KH, GRID_W, NA_KH * GRID_W)


def _natten_kernel(q_ref, k_ref, v_ref, bias_ref, o_ref, *, rows, heads):
    r = pl.program_id(1)
    rstart = jnp.clip(r - NA_KH // 2, 0, rows - NA_KH)
    off = r - rstart
    start = pl.multiple_of(rstart * GRID_W, GRID_W)
    nk = NA_KH * GRID_W
    kw = k_ref[pl.ds(start, nk), :]
    vw = v_ref[pl.ds(start, nk), :]
    q = q_ref[...]
    width = q.shape[1]
    lane = lax.broadcasted_iota(jnp.int32, (GRID_W, width), 1)
    scale = HEAD_DIM ** -0.5
    acc = jnp.zeros((GRID_W, width), F32)
    for h in range(heads):
        hm = (lane >= h * HEAD_DIM) & (lane < (h + 1) * HEAD_DIM)
        qh = jnp.where(hm, q, jnp.zeros_like(q))
        s = lax.dot_general(qh, kw, (((1,), (1,)), ((), ())), preferred_element_type=F32) * scale
        s = s + bias_ref[h, off]
        mx = jnp.max(s, axis=-1, keepdims=True)
        p = jnp.exp(s - mx)
        den = jnp.sum(p, axis=-1, keepdims=True)
        oh = jnp.dot(p.astype(BF16), vw, preferred_element_type=F32) / den
        acc = jnp.where(hm, oh, acc)
    o_ref[...] = acc


def _natten(qkv, bias_tab, batch, seq, width):
    rows = seq // GRID_W
    assert rows >= NA_KH
    heads = width // HEAD_DIM
    kern = functools.partial(_natten_kernel, rows=rows, heads=heads)
    return pl.pallas_call(
        kern,
        grid=(batch, rows),
        in_specs=[pl.BlockSpec((GRID_W, width), lambda b, r: (b * rows + r, 0)),
                  pl.BlockSpec((seq, width), lambda b, r: (b, 1)),
                  pl.BlockSpec((seq, width), lambda b, r: (b, 2)),
                  pl.BlockSpec(bias_tab.shape, lambda b, r: (0, 0, 0, 0))],
        out_specs=pl.BlockSpec((GRID_W, width), lambda b, r: (b * rows + r, 0)),
        out_shape=jax.ShapeDtypeStruct((batch * seq, width), F32),
        compiler_params=_cparams(2),
        name="natten",
    )(qkv, qkv, qkv, bias_tab)


def _rwkv_prep_kernel(z_ref, zp_ref, zn_ref, lr_ref, cw_ref, w0_ref, wup_ref, a0_ref, aup_ref, gup_ref,
                      kk_ref, ka_ref, rk_ref, ones_ref,
                      r_o, v_o, nkk_o, lw_o, b_o, kd_o, bonus_o, g_o, *, tiles_per_batch, width):
    i = pl.program_id(0)
    tb = i % tiles_per_batch
    z = z_ref[...]
    tm = z.shape[0]
    prev = jnp.where(tb == 0, 0.0, zp_ref[HALO - 1:HALO, :])
    nxt = jnp.where(tb == tiles_per_batch - 1, 0.0, zn_ref[0:1, :])
    row = lax.broadcasted_iota(jnp.int32, z.shape, 0)
    zm1 = jnp.where(row == 0, prev, pltpu.roll(z, 1, 0))
    zp1 = jnp.where(row == tm - 1, nxt, pltpu.roll(z, tm - 1, 0))
    rkv = zm1 * cw_ref[0:1, :] + z * cw_ref[1:2, :] + zp1 * cw_ref[2:3, :]
    r = rkv[:, :width]
    k = rkv[:, width:2 * width]
    v = rkv[:, 2 * width:]
    lr = lr_ref[...]
    th = jnp.tanh(lr)
    sg = _sigmoid(lr)
    ones = ones_ref[...]

    def headsum(x):
        return jnp.dot(x, ones, precision=HI, preferred_element_type=F32)

    kk = k * kk_ref[...]
    kk = kk * lax.rsqrt(jnp.maximum(headsum(kk * kk), 1e-24))
    g_o[...] = jnp.dot(sg, gup_ref[...], precision=HI, preferred_element_type=F32)
    r_o[...] = r
    v_o[...] = v
    nkk_o[...] = -kk
    bonus = jnp.zeros_like(r)
    for d in range(2):
        wl = jnp.dot(th, wup_ref[d], precision=HI, preferred_element_type=F32) + w0_ref[d:d + 1, :]
        lw_o[d] = -DECAY_SCALE * _sigmoid(wl)
        a = _sigmoid(jnp.dot(lr, aup_ref[d], precision=HI, preferred_element_type=F32) + a0_ref[d:d + 1, :])
        kd = k * (1.0 + (a - 1.0) * ka_ref[...])
        kd_o[d] = kd
        b_o[d] = kk * a
        bonus = bonus + headsum(r * kd * rk_ref[...]) * v
    bonus_o[...] = bonus


def _rwkv_prep(rkv_raw, lr, p, seq, tm):
    t, w3 = rkv_raw.shape
    width = w3 // 3
    tpb = seq // tm
    hb = tm // HALO
    nhb = t // HALO
    kern = functools.partial(_rwkv_prep_kernel, tiles_per_batch=tpb, width=width)
    tok = lambda i: (i, 0)
    dtok = lambda i: (0, i, 0)
    full2 = lambda i: (0, 0)
    full3 = lambda i: (0, 0, 0)
    tw = jax.ShapeDtypeStruct((t, width), F32)
    dtw = jax.ShapeDtypeStruct((2, t, width), F32)
    return pl.pallas_call(
        kern,
        grid=(t // tm,),
        in_specs=[pl.BlockSpec((tm, w3), tok),
                  pl.BlockSpec((HALO, w3), lambda i: (jnp.maximum(i * hb - 1, 0), 0)),
                  pl.BlockSpec((HALO, w3), lambda i: (jnp.minimum((i + 1) * hb, nhb - 1), 0)),
                  pl.BlockSpec((tm, lr.shape[1]), tok),
                  pl.BlockSpec(p["conv"].shape, full2),
                  pl.BlockSpec(p["w0"].shape, full2),
                  pl.BlockSpec(p["w_up"].shape, full3),
                  pl.BlockSpec(p["a0"].shape, full2),
                  pl.BlockSpec(p["a_up"].shape, full3),
                  pl.BlockSpec(p["g_up"].shape, full2),
                  pl.BlockSpec(p["k_k"].shape, full2),
                  pl.BlockSpec(p["k_a"].shape, full2),
                  pl.BlockSpec(p["r_k"].shape, full2),
                  pl.BlockSpec(p["ones"].shape, full2)],
        out_specs=[pl.BlockSpec((tm, width), tok), pl.BlockSpec((tm, width), tok), pl.BlockSpec((tm, width), tok),
                   pl.BlockSpec((2, tm, width), dtok), pl.BlockSpec((2, tm, width), dtok),
                   pl.BlockSpec((2, tm, width), dtok),
                   pl.BlockSpec((tm, width), tok), pl.BlockSpec((tm, width), tok)],
        out_shape=[tw, tw, tw, dtw, dtw, dtw, tw, tw],
        compiler_params=_cparams(1),
        name="rwkv_prep",
    )(rkv_raw, rkv_raw, rkv_raw, lr, p["conv"], p["w0"], p["w_up"], p["a0"], p["a_up"], p["g_up"],
      p["k_k"], p["k_a"], p["r_k"], p["ones"])


def _dot_nt(a, b):
    return lax.dot_general(a, b, (((1,), (1,)), ((), ())), preferred_element_type=F32)


def _dot_tn(a, b):
    return lax.dot_general(a, b, (((0,), (0,)), ((), ())), preferred_element_type=F32)


def _mm(a, b):
    return jnp.dot(a.astype(BF16), b.astype(BF16), preferred_element_type=F32)


def _rwkv_scan_kernel(rf_ref, vf_ref, nf_ref, rb_ref, vb_ref, nb_ref, lwf_ref, bf_ref, kf_ref, lwb_ref, bb_ref, kb_ref,
                      yf_ref, yb_ref, s_ref, *, heads, batch):
    @pl.when(pl.program_id(0) == 0)
    def _():
        s_ref[...] = jnp.zeros_like(s_ref)

    n = SCAN_CHUNK
    row = lax.broadcasted_iota(jnp.int32, (n, n), 0)
    col = lax.broadcasted_iota(jnp.int32, (n, n), 1)
    levels = n.bit_length()
    same = [(row >> k) == (col >> k) for k in range(levels)]
    eye = same[0].astype(F32)
    level_masks = [same[sh + 1] & jnp.logical_not(same[sh]) for sh in range(1, levels - 1)]

    dirs = ((rf_ref, vf_ref, nf_ref, lwf_ref, bf_ref, kf_ref, yf_ref),
            (rb_ref, vb_ref, nb_ref, lwb_ref, bb_ref, kb_ref, yb_ref))
    chains = []
    for d, (r_ref, v_ref, n_ref, lw_ref, b_ref, k_ref, y_ref) in enumerate(dirs):
        order = row - col if d == 0 else col - row
        strict = order > 0
        incl = order >= 0
        incl_f = incl.astype(F32)
        for bi in range(batch):
            lw = lw_ref[0, bi]
            g_inc = jnp.dot(incl_f, lw, precision=HI, preferred_element_type=F32)
            g_tot = jnp.sum(lw, axis=0, keepdims=True)
            e_neg = jnp.exp(-g_inc)
            e_end = jnp.exp(g_tot - g_inc)
            decay = jnp.exp(g_tot)
            a_t = n_ref[bi] * jnp.exp(g_inc - lw)
            r_t = r_ref[bi] * jnp.exp(g_inc)
            bb = b_ref[0, bi]
            kd = k_ref[0, bi]
            b_t = (bb * e_neg).astype(BF16)
            k_t = (kd * e_neg).astype(BF16)
            ar_t = jnp.concatenate([a_t, r_t], axis=0).astype(BF16)
            bk_h = jnp.concatenate([bb * e_end, kd * e_end], axis=0).astype(BF16)
            v = v_ref[bi]
            for h in range(heads):
                sl = slice(h * HEAD_DIM, (h + 1) * HEAD_DIM)
                chains.append(dict(strict=strict, incl=incl, sl=sl, bi=bi, y_ref=y_ref,
                                   si=(d * batch + bi) * heads + h, decay=decay[:, sl],
                                   ar=ar_t[:, sl], b=b_t[:, sl], k=k_t[:, sl], bk_h=bk_h[:, sl], v=v[:, sl]))

    for ch in chains:
        pb = _dot_nt(ch["ar"], ch["b"])
        pk = _dot_nt(ch["ar"], ch["k"])
        ch["l_ab"] = jnp.where(ch["strict"], pb[:n], 0.0)
        ch["m_rb"] = jnp.where(ch["incl"], pb[n:], 0.0)
        ch["l_ak"] = jnp.where(ch["strict"], pk[:n], 0.0)
        ch["m_rk"] = jnp.where(ch["incl"], pk[n:], 0.0)
        ch["t"] = eye + jnp.where(same[1], ch["l_ab"], 0.0)
    for mask in level_masks:
        for ch in chains:
            ch["tc"] = _mm(ch["t"], jnp.where(mask, ch["l_ab"], 0.0))
        for ch in chains:
            ch["t"] = ch["t"] + _mm(ch["tc"], ch["t"])
    for ch in chains:
        ch["s0"] = s_ref[ch["si"]]
        ch["x"] = _dot_nt(ch["ar"], ch["s0"].astype(BF16))
    for ch in chains:
        ch["rhs"] = ch["x"][:n] + _mm(ch["l_ak"], ch["v"])
    for ch in chains:
        ch["u"] = _mm(ch["t"], ch["rhs"])
    for ch in chains:
        y = ch["x"][n:] + _mm(ch["m_rb"], ch["u"]) + _mm(ch["m_rk"], ch["v"])
        ch["y_ref"][ch["bi"], :, ch["sl"]] = y
    for ch in chains:
        uv = jnp.concatenate([ch["u"], ch["v"]], axis=0)
        s_ref[ch["si"]] = ch["s0"] * ch["decay"] + _dot_tn(uv.astype(BF16), ch["bk_h"])


def _rwkv_scan(r, v, nkk, lw, b, kd, batch, seq):
    t, width = r.shape
    heads = width // HEAD_DIM
    n = SCAN_CHUNK
    nc = seq // n
    r3, v3, n3 = (z.reshape(batch, seq, width) for z in (r, v, nkk))
    lw4, b4, k4 = (z.reshape(2, batch, seq, width) for z in (lw, b, kd))
    fwd = pl.BlockSpec((batch, n, width), lambda c: (0, c, 0))
    bwd = pl.BlockSpec((batch, n, width), lambda c: (0, nc - 1 - c, 0))
    fwd_d = pl.BlockSpec((1, batch, n, width), lambda c: (0, 0, c, 0))
    bwd_d = pl.BlockSpec((1, batch, n, width), lambda c: (1, 0, nc - 1 - c, 0))
    kern = functools.partial(_rwkv_scan_kernel, heads=heads, batch=batch)
    yf, yb = pl.pallas_call(
        kern,
        grid=(nc,),
        in_specs=[fwd, fwd, fwd, bwd, bwd, bwd, fwd_d, fwd_d, fwd_d, bwd_d, bwd_d, bwd_d],
        out_specs=[fwd, bwd],
        out_shape=[jax.ShapeDtypeStruct((batch, seq, width), F32)] * 2,
        scratch_shapes=[pltpu.VMEM((2 * batch * heads, HEAD_DIM, HEAD_DIM), F32)],
        compiler_params=_cparams(1, "arbitrary"),
        name="rwkv_scan",
    )(r3, v3, n3, r3, v3, n3, lw4, b4, k4, lw4, b4, k4)
    return yf.reshape(t, width), yb.reshape(t, width)


def _pool_kernel(p_ref, pp_ref, pn_ref, w_ref, sc_ref, o_ref, ext_ref, *, tiles_per_batch, seq):
    i = pl.program_id(0)
    tb = i % tiles_per_batch
    p = p_ref[...]
    tm, width = p.shape
    ext_ref[0:HALO, :] = jnp.where(tb == 0, 0.0, pp_ref[...])
    ext_ref[HALO:HALO + tm, :] = p
    ext_ref[HALO + tm:2 * HALO + tm, :] = jnp.where(tb == tiles_per_batch - 1, 0.0, pn_ref[...])

    def shifted(o):
        return ext_ref[HALO + o:HALO + o + tm, :]

    t = tb * tm + lax.broadcasted_iota(jnp.int32, (tm, width), 0)
    grp = lax.broadcasted_iota(jnp.int32, (tm, width), 1) // (width // len(POOL_WINDOWS))
    tot = p
    prev_half = 0
    pooled = jnp.zeros_like(p)
    for gi, win in enumerate(POOL_WINDOWS):
        half = win // 2
        for o in range(prev_half, half):
            tot = tot + shifted(-o - 1)
            if o > 0:
                tot = tot + shifted(o)
        prev_half = half
        lo = jnp.clip(t - half, 0, seq - 1)
        hi = jnp.clip(t + half - 1, 0, seq - 1)
        cnt = (hi - lo + 1).astype(F32)
        pooled = jnp.where(grp == gi, tot / cnt, pooled)
    pooled = pooled - p
    o_ref[...] = jnp.dot(pooled, w_ref[...], preferred_element_type=F32) * sc_ref[...]


def _pool(praw, w_blk, scale, seq, tm):
    t, width = praw.shape
    tpb = seq // tm
    hb = tm // HALO
    nhb = t // HALO
    kern = functools.partial(_pool_kernel, tiles_per_batch=tpb, seq=seq)
    return pl.pallas_call(
        kern,
        grid=(t // tm,),
        in_specs=[pl.BlockSpec((tm, width), lambda i: (i, 0)),
                  pl.BlockSpec((HALO, width), lambda i: (jnp.maximum(i * hb - 1, 0), 0)),
                  pl.BlockSpec((HALO, width), lambda i: (jnp.minimum((i + 1) * hb, nhb - 1), 0)),
                  pl.BlockSpec(w_blk.shape, lambda i: (0, 0)),
                  pl.BlockSpec(scale.shape, lambda i: (0, 0))],
        out_specs=pl.BlockSpec((tm, width), lambda i: (i, 0)),
        out_shape=jax.ShapeDtypeStruct((t, width), F32),
        scratch_shapes=[pltpu.VMEM((tm + 2 * HALO, width), F32)],
        compiler_params=_cparams(1),
        name="pool",
    )(praw, praw, praw, w_blk, scale)


def _outproj_kernel(ya_ref, yf_ref, yb_ref, bonus_ref, g_ref, yc_ref, x_ref, mod_ref, wa_ref, wb_ref, wc_ref,
                    gng_ref, gnb_ref, ones_ref, l1g_ref, l1b_ref, wr_ref, br_ref,
                    x1_o, u2_o, ri_o, rw_o, *, alpha):
    m = mod_ref[0]
    ones = ones_ref[...]

    def headmean(x):
        return jnp.dot(x, ones, precision=HI, preferred_element_type=F32) * (1.0 / HEAD_DIM)

    ysum = yf_ref[...] + yb_ref[...]
    yc0 = ysum - headmean(ysum)
    yn = yc0 * lax.rsqrt(headmean(yc0 * yc0) + GN_EPS) * gng_ref[...] + gnb_ref[...]
    yb = (yn + bonus_ref[...]) * g_ref[...]
    mix = (jnp.dot(ya_ref[...].astype(BF16), wa_ref[...], preferred_element_type=F32)
           + jnp.dot(yb.astype(BF16), wb_ref[...], preferred_element_type=F32)
           + jnp.dot(yc_ref[...].astype(BF16), wc_ref[...], preferred_element_type=F32))
    x1 = _ln(alpha * x_ref[...] + m[2:3] * mix) * l1g_ref[...] + l1b_ref[...]
    x1_o[...] = x1
    u2 = _ln(x1) * (1.0 + m[4:5]) + m[3:4]
    u2_o[...] = u2

    lg = jnp.dot(u2, wr_ref[...], precision=HI, preferred_element_type=F32) + br_ref[...]
    lane = lax.broadcasted_iota(jnp.int32, lg.shape, 1)
    big = jnp.int32(1 << 20)
    gl = jnp.where(lane < N_GROUPS, lg, -jnp.inf)
    gmax = jnp.max(gl, axis=-1, keepdims=True)
    gidx = jnp.min(jnp.where(gl == gmax, lane, big), axis=-1, keepdims=True)
    pg_sel = 1.0 / jnp.sum(jnp.exp(gl - gmax), axis=-1, keepdims=True)
    e_lo = N_GROUPS + gidx * EXPERTS_PER_GROUP
    el = jnp.where((lane >= e_lo) & (lane < e_lo + EXPERTS_PER_GROUP), lg, -jnp.inf)
    m1 = jnp.max(el, axis=-1, keepdims=True)
    i1 = jnp.min(jnp.where(el == m1, lane, big), axis=-1, keepdims=True)
    el2 = jnp.where(lane == i1, -jnp.inf, el)
    m2 = jnp.max(el2, axis=-1, keepdims=True)
    i2 = jnp.min(jnp.where(el2 == m2, lane, big), axis=-1, keepdims=True)
    e21 = jnp.exp(m2 - m1)
    p1 = 1.0 / (1.0 + e21)
    p2 = e21 / (1.0 + e21)
    ri_o[...] = jnp.where(lane == 0, i1 - N_GROUPS, jnp.where(lane == 1, i2 - N_GROUPS, 0))
    rw_o[...] = jnp.where(lane == 0, pg_sel * p1, jnp.where(lane == 1, pg_sel * p2, 0.0))


def _outproj(ya, yf, yb, bonus, g, yc, x2, modl, p, seq, tm, alpha):
    t, d = x2.shape
    tpb = seq // tm
    aw, bw, cw = ya.shape[1], bonus.shape[1], yc.shape[1]
    tok = lambda i: (i, 0)
    full2 = lambda i: (0, 0)
    kern = functools.partial(_outproj_kernel, alpha=alpha)
    small = ["gn_gain", "gn_bias", "ones", "ln1_gain", "ln1_bias", "w_router", "b_router"]
    return pl.pallas_call(
        kern,
        grid=(t // tm,),
        in_specs=[pl.BlockSpec((tm, aw), tok),
                  pl.BlockSpec((tm, bw), tok), pl.BlockSpec((tm, bw), tok),
                  pl.BlockSpec((tm, bw), tok), pl.BlockSpec((tm, bw), tok),
                  pl.BlockSpec((tm, cw), tok),
                  pl.BlockSpec((tm, d), tok),
                  pl.BlockSpec((1,) + modl.shape[1:], lambda i: (i // tpb, 0, 0)),
                  pl.BlockSpec(p["w_out_a"].shape, full2),
                  pl.BlockSpec(p["w_out_b"].shape, full2),
                  pl.BlockSpec(p["w_out_c"].shape, full2)]
                 + [pl.BlockSpec(p[k].shape, full2) for k in small],
        out_specs=[pl.BlockSpec((tm, d), tok), pl.BlockSpec((tm, d), tok),
                   pl.BlockSpec((tm, LANES), tok), pl.BlockSpec((tm, LANES), tok)],
        out_shape=[jax.ShapeDtypeStruct((t, d), F32), jax.ShapeDtypeStruct((t, d), F32),
                   jax.ShapeDtypeStruct((t, LANES), jnp.int32), jax.ShapeDtypeStruct((t, LANES), F32)],
        compiler_params=_cparams(1),
        name="outproj",
    )(ya, yf, yb, bonus, g, yc, x2, modl, p["w_out_a"], p["w_out_b"], p["w_out_c"], *[p[k] for k in small])


def _experts_kernel(be_ref, xs_ref, wg_ref, wu_ref, wd_ref, o_ref):
    del be_ref
    xb = xs_ref[...].astype(BF16)
    gate = jnp.dot(xb, wg_ref[0], preferred_element_type=F32)
    up = jnp.dot(xb, wu_ref[0], preferred_element_type=F32)
    hb = gate * _sigmoid(gate) * up
    o_ref[...] = jnp.dot(hb.astype(BF16), wd_ref[0], preferred_element_type=F32)


def _experts(block_e, xs, wg, wu, wd):
    total, d = xs.shape
    nb = total // MOE_BLOCK
    de = wg.shape[2]
    grid_spec = pltpu.PrefetchScalarGridSpec(
        num_scalar_prefetch=1,
        grid=(nb,),
        in_specs=[pl.BlockSpec((MOE_BLOCK, d), lambda i, be: (i, 0)),
                  pl.BlockSpec((1, d, de), lambda i, be: (be[i], 0, 0)),
                  pl.BlockSpec((1, d, de), lambda i, be: (be[i], 0, 0)),
                  pl.BlockSpec((1, de, d), lambda i, be: (be[i], 0, 0))],
        out_specs=pl.BlockSpec((MOE_BLOCK, d), lambda i, be: (i, 0)),
    )
    return pl.pallas_call(
        _experts_kernel,
        grid_spec=grid_spec,
        out_shape=jax.ShapeDtypeStruct((total, d), F32),
        compiler_params=_cparams(1, "arbitrary"),
        name="experts",
    )(block_e, xs, wg, wu, wd)


def _dispatch(route_i, n_tokens):
    e_flat = route_i[:, :TOP_K].reshape(-1)
    n_assign = e_flat.shape[0]
    onehot = (e_flat[:, None] == jnp.arange(N_EXPERTS, dtype=jnp.int32)[None, :]).astype(jnp.int32)
    csum = jnp.cumsum(onehot, axis=0)
    rank = jnp.sum((csum - onehot) * onehot, axis=1)
    counts = csum[-1]
    padded = ((counts + MOE_BLOCK - 1) // MOE_BLOCK) * MOE_BLOCK
    pends = jnp.cumsum(padded)
    pstarts = pends - padded
    dest = pstarts[e_flat] + rank
    n_blocks = -(-n_assign // MOE_BLOCK) + N_EXPERTS
    total = n_blocks * MOE_BLOCK
    tok = jnp.arange(n_assign, dtype=jnp.int32) // TOP_K
    buf_tok = jnp.zeros((total,), jnp.int32).at[dest].set(tok)
    block_e = jnp.clip(jnp.searchsorted(pends, jnp.arange(n_blocks, dtype=jnp.int32) * MOE_BLOCK, side="right"),
                       0, N_EXPERTS - 1).astype(jnp.int32)
    del n_tokens
    return dest, buf_tok, block_e


def _final_kernel(x1_ref, yp_ref, rw_ref, mod_ref, g_ref, b_ref, o_ref, *, alpha):
    m = mod_ref[0]
    d = x1_ref.shape[1]
    rw = rw_ref[...]
    f = rw[:, 0:1] * yp_ref[:, :d] + rw[:, 1:2] * yp_ref[:, d:]
    o_ref[...] = _ln(alpha * x1_ref[...] + m[5:6] * f) * g_ref[...] + b_ref[...]


def _final(x1, ypair, rw, modl, gain, bias, seq, tm, alpha):
    t, d = x1.shape
    tpb = seq // tm
    tok = lambda i: (i, 0)
    kern = functools.partial(_final_kernel, alpha=alpha)
    return pl.pallas_call(
        kern,
        grid=(t // tm,),
        in_specs=[pl.BlockSpec((tm, d), tok), pl.BlockSpec((tm, 2 * d), tok), pl.BlockSpec((tm, LANES), tok),
                  pl.BlockSpec((1,) + modl.shape[1:], lambda i: (i // tpb, 0, 0)),
                  pl.BlockSpec(gain.shape, lambda i: (0, 0)), pl.BlockSpec(bias.shape, lambda i: (0, 0))],
        out_specs=pl.BlockSpec((tm, d), tok),
        out_shape=jax.ShapeDtypeStruct((t, d), F32),
        compiler_params=_cparams(1),
        name="final_ln",
    )(x1, ypair, rw, modl, gain, bias)


def _block_diag(blocks):
    n, a, b = blocks.shape
    out = jnp.zeros((n * a, n * b), blocks.dtype)
    for i in range(n):
        out = out.at[i * a:(i + 1) * a, i * b:(i + 1) * b].set(blocks[i])
    return out


def _pad_rows(w, lo, total):
    return jnp.zeros((total, w.shape[-1]), w.dtype).at[lo:lo + w.shape[0]].set(w)


def kernel(x, c, w_mod, b_mod, w_in, na_rpb, rw_conv, rw_w0, rw_w_up, rw_a0, rw_a_up, rw_g_up, rw_k_k, rw_k_a, rw_r_k, rw_gn_gain, rw_gn_bias, pool_w, pool_scale, w_out, ln1_gain, ln1_bias, ln2_gain, ln2_bias, moe_w_group, moe_b_group, moe_w_expert, moe_b_expert, moe_w_gate, moe_w_up, moe_w_down):
    batch, seq, d = x.shape
    depth = w_mod.shape[0]
    t = batch * seq
    a_w = na_rpb.shape[1] * HEAD_DIM
    b_w = rw_w0.shape[-1]
    c_w = pool_scale.shape[-1]
    lr_w = R_W + R_A + R_G
    alpha = (2 * depth) ** 0.25
    tm = min(512, seq)
    tm_prep = min(256, seq)
    assert seq % tm == 0 and seq % SCAN_CHUNK == 0 and seq % GRID_W == 0 and lr_w == LANES

    mod = _modulation(c, w_mod, b_mod)
    ones_blk = _block_diag(jnp.ones((b_w // HEAD_DIM, HEAD_DIM, HEAD_DIM), F32))
    row = lambda v: v.reshape(1, -1)

    x2 = x.reshape(t, d)
    for l in range(depth):
        modl = mod[l]
        qkv, rkv_raw, lr, praw = _inproj(x2, modl, w_in[l].astype(BF16), seq, tm, 3 * a_w, 3 * b_w, lr_w, c_w)
        ya = _natten(qkv, _na_bias_table(na_rpb[l]), batch, seq, a_w)
        prep_params = {
            "conv": rw_conv[l], "w0": rw_w0[l], "a0": rw_a0[l],
            "w_up": jnp.stack([_pad_rows(rw_w_up[l, dd], 0, lr_w) for dd in range(2)]),
            "a_up": jnp.stack([_pad_rows(rw_a_up[l, dd], R_W, lr_w) for dd in range(2)]),
            "g_up": _pad_rows(rw_g_up[l], R_W + R_A, lr_w),
            "k_k": row(rw_k_k[l]), "k_a": row(rw_k_a[l]), "r_k": row(rw_r_k[l]), "ones": ones_blk,
        }
        r, v, nkk, lw, bb, kd, bonus, g = _rwkv_prep(rkv_raw, lr, prep_params, seq, tm_prep)
        yf, yb = _rwkv_scan(r, v, nkk, lw, bb, kd, batch, seq)
        yc = _pool(praw, _block_diag(pool_w[l]), row(pool_scale[l]), seq, tm)
        w_router = jnp.zeros((d, LANES), F32).at[:, :N_GROUPS].set(moe_w_group[l])
        w_router = w_router.at[:, N_GROUPS:N_GROUPS + N_EXPERTS].set(moe_w_expert[l])
        b_router = jnp.zeros((1, LANES), F32).at[0, :N_GROUPS].set(moe_b_group[l])
        b_router = b_router.at[0, N_GROUPS:N_GROUPS + N_EXPERTS].set(moe_b_expert[l])
        wo = w_out[l].astype(BF16)
        out_params = {
            "w_out_a": wo[:a_w], "w_out_b": wo[a_w:a_w + b_w], "w_out_c": wo[a_w + b_w:],
            "gn_gain": row(rw_gn_gain[l]), "gn_bias": row(rw_gn_bias[l]), "ones": ones_blk,
            "ln1_gain": row(ln1_gain[l]), "ln1_bias": row(ln1_bias[l]),
            "w_router": w_router, "b_router": b_router,
        }
        x1, u2, route_i, route_w = _outproj(ya, yf, yb, bonus, g, yc, x2, modl, out_params, seq, tm, alpha)
        dest, buf_tok, block_e = _dispatch(route_i, t)
        xs = jnp.take(u2, buf_tok, axis=0)
        ysorted = _experts(block_e, xs, moe_w_gate[l].astype(BF16), moe_w_up[l].astype(BF16),
                           moe_w_down[l].astype(BF16))
        ypair = jnp.take(ysorted, dest, axis=0).reshape(t, TOP_K * d)
        x2 = _final(x1, ypair, route_w, modl, row(ln2_gain[l]), row(ln2_bias[l]), seq, tm, alpha)
    return x2.reshape(batch, seq, d)
```

```python
import functools
import math

import jax
import jax.numpy as jnp
from jax import lax
from jax.experimental import pallas as pl
from jax.experimental.pallas import tpu as pltpu

F32 = jnp.float32
BF16 = jnp.bfloat16
HI = lax.Precision.HIGHEST

GRID_W = 64
HEAD_DIM = 64
NA_KH = 8
NA_KW = 16
POOL_WINDOWS = (2, 4, 8, 16)
R_W = 32
R_A = 32
R_G = 64
DECAY_SCALE = math.exp(-0.5)
GN_EPS = 64e-5
N_GROUPS = 4
EXPERTS_PER_GROUP = 8
N_EXPERTS = N_GROUPS * EXPERTS_PER_GROUP
TOP_K = 2
EXPERT_BLOCK = 256
LN_EPS = 1e-5
NEG_INF = -1e30

SCAN_CHUNK = 64
HALO = 8
LANES = 128
VMEM_LIMIT = 52 * 1024 * 1024


def _ln(x):
    mu = jnp.mean(x, axis=-1, keepdims=True)
    xc = x - mu
    var = jnp.mean(xc * xc, axis=-1, keepdims=True)
    return xc * lax.rsqrt(var + LN_EPS)


def _sigmoid(x):
    return 1.0 / (1.0 + jnp.exp(-x))


def _cparams(n_axes, semantics="parallel"):
    return pltpu.CompilerParams(dimension_semantics=(semantics,) * n_axes, vmem_limit_bytes=VMEM_LIMIT)


def _mod_kernel(c_ref, w_ref, b_ref, o_ref):
    c = c_ref[...]
    s = c * _sigmoid(c)
    o_ref[0] = jnp.dot(s, w_ref[0], precision=HI, preferred_element_type=F32) + b_ref[0]


def _modulation(c, w_mod, b_mod):
    n_layers, d, d6 = w_mod.shape
    b = c.shape[0]
    bp = -(-b // 8) * 8
    cp = jnp.zeros((bp, d), F32).at[:b].set(c)
    out = pl.pallas_call(
        _mod_kernel,
        grid=(n_layers, d6 // d),
        in_specs=[pl.BlockSpec((bp, d), lambda l, j: (0, 0)),
                  pl.BlockSpec((1, d, d), lambda l, j: (l, 0, j)),
                  pl.BlockSpec((1, 1, d), lambda l, j: (l, 0, j))],
        out_specs=pl.BlockSpec((1, bp, d), lambda l, j: (l, 0, j)),
        out_shape=jax.ShapeDtypeStruct((n_layers, bp, d6), F32),
        compiler_params=_cparams(2),
        name="modulation",
    )(cp, w_mod, b_mod.reshape(n_layers, 1, d6))
    return out[:, :b].reshape(n_layers, b, d6 // d, d)


def _inproj_kernel(x_ref, mod_ref, w_ref, qkv_ref, rkv_ref, lr_ref, pool_ref, *, a3, b3, lr_w):
    m = mod_ref[0]
    u = _ln(x_ref[...]) * (1.0 + m[1:2]) + m[0:1]
    h = jnp.dot(u.astype(BF16), w_ref[...], preferred_element_type=F32)
    qkv_ref[...] = h[:, :a3].astype(BF16)
    rkv_ref[...] = h[:, a3:a3 + b3]
    lr_ref[...] = h[:, a3 + b3:a3 + b3 + lr_w]
    pool_ref[...] = h[:, a3 + b3 + lr_w:]


def _inproj(x2, modl, w_in_bf, seq, tm, a3, b3, lr_w, c_w):
    t, d = x2.shape
    tpb = seq // tm
    kern = functools.partial(_inproj_kernel, a3=a3, b3=b3, lr_w=lr_w)
    return pl.pallas_call(
        kern,
        grid=(t // tm,),
        in_specs=[pl.BlockSpec((tm, d), lambda i: (i, 0)),
                  pl.BlockSpec((1,) + modl.shape[1:], lambda i: (i // tpb, 0, 0)),
                  pl.BlockSpec(w_in_bf.shape, lambda i: (0, 0))],
        out_specs=[pl.BlockSpec((tm, a3), lambda i: (i, 0)),
                   pl.BlockSpec((tm, b3), lambda i: (i, 0)),
                   pl.BlockSpec((tm, lr_w), lambda i: (i, 0)),
                   pl.BlockSpec((tm, c_w), lambda i: (i, 0))],
        out_shape=[jax.ShapeDtypeStruct((t, a3), BF16),
                   jax.ShapeDtypeStruct((t, b3), F32),
                   jax.ShapeDtypeStruct((t, lr_w), F32),
                   jax.ShapeDtypeStruct((t, c_w), F32)],
        compiler_params=_cparams(1),
        name="inproj",
    )(x2, modl, w_in_bf)


def _na_bias_table(rpb):
    col = jnp.arange(GRID_W)
    cstart = jnp.clip(col - NA_KW // 2, 0, GRID_W - NA_KW)
    in_win = (col[None, :] >= cstart[:, None]) & (col[None, :] < cstart[:, None] + NA_KW)
    dc = jnp.clip(col[None, :] - col[:, None], -(NA_KW - 1), NA_KW - 1) + (NA_KW - 1)
    dr = jnp.arange(NA_KH)[None, :] - jnp.arange(NA_KH)[:, None] + (NA_KH - 1)
    b = rpb.astype(F32)[:, dr][..., dc]
    b = jnp.where(in_win, b, NEG_INF)
    h = rpb.shape[0]
    return jnp.transpose(b, (0, 1, 3, 2, 4)).reshape(h, NA_KH, GRID_W, NA_KH * GRID_W)


def _natten_kernel(q_ref, k_ref, v_ref, bias_ref, o_ref, *, rows, heads):
    r = pl.program_id(1)
    rstart = jnp.clip(r - NA_KH // 2, 0, rows - NA_KH)
    off = r - rstart
    start = pl.multiple_of(rstart * GRID_W, GRID_W)
    nk = NA_KH * GRID_W
    kw = k_ref[pl.ds(start, nk), :]
    vw = v_ref[pl.ds(start, nk), :]
    q = q_ref[...]
    width = q.shape[1]
    lane = lax.broadcasted_iota(jnp.int32, (GRID_W, width), 1)
    scale = HEAD_DIM ** -0.5
    acc = jnp.zeros((GRID_W, width), F32)
    for h in range(heads):
        hm = (lane >= h * HEAD_DIM) & (lane < (h + 1) * HEAD_DIM)
        qh = jnp.where(hm, q, jnp.zeros_like(q))
        s = lax.dot_general(qh, kw, (((1,), (1,)), ((), ())), preferred_element_type=F32) * scale
        s = s + bias_ref[h, off]
        mx = jnp.max(s, axis=-1, keepdims=True)
        p = jnp.exp(s - mx)
        den = jnp.sum(p, axis=-1, keepdims=True)
        oh = jnp.dot(p.astype(BF16), vw, preferred_element_type=F32) / den
        acc = jnp.where(hm, oh, acc)
    o_ref[...] = acc


def _natten(qkv, bias_tab, batch, seq, width):
    rows = seq // GRID_W
    assert rows >= NA_KH
    heads = width // HEAD_DIM
    kern = functools.partial(_natten_kernel, rows=rows, heads=heads)
    return pl.pallas_call(
        kern,
        grid=(batch, rows),
        in_specs=[pl.BlockSpec((GRID_W, width), lambda b, r: (b * rows + r, 0)),
                  pl.BlockSpec((seq, width), lambda b, r: (b, 1)),
                  pl.BlockSpec((seq, width), lambda b, r: (b, 2)),
                  pl.BlockSpec(bias_tab.shape, lambda b, r: (0, 0, 0, 0))],
        out_specs=pl.BlockSpec((GRID_W, width), lambda b, r: (b * rows + r, 0)),
        out_shape=jax.ShapeDtypeStruct((batch * seq, width), F32),
        compiler_params=_cparams(2),
        name="natten",
    )(qkv, qkv, qkv, bias_tab)


def _rwkv_prep_kernel(z_ref, zp_ref, zn_ref, lr_ref, cw_ref, w0_ref, wup_ref, a0_ref, aup_ref, gup_ref,
                      kk_ref, ka_ref, rk_ref, ones_ref,
                      r_o, v_o, nkk_o, lw_o, b_o, kd_o, bonus_o, g_o, *, tiles_per_batch, width):
    i = pl.program_id(0)
    tb = i % tiles_per_batch
    z = z_ref[...]
    tm = z.shape[0]
    prev = jnp.where(tb == 0, 0.0, zp_ref[HALO - 1:HALO, :])
    nxt = jnp.where(tb == tiles_per_batch - 1, 0.0, zn_ref[0:1, :])
    row = lax.broadcasted_iota(jnp.int32, z.shape, 0)
    zm1 = jnp.where(row == 0, prev, pltpu.roll(z, 1, 0))
    zp1 = jnp.where(row == tm - 1, nxt, pltpu.roll(z, tm - 1, 0))
    rkv = zm1 * cw_ref[0:1, :] + z * cw_ref[1:2, :] + zp1 * cw_ref[2:3, :]
    r = rkv[:, :width]
    k = rkv[:, width:2 * width]
    v = rkv[:, 2 * width:]
    lr = lr_ref[...]
    th = jnp.tanh(lr)
    sg = _sigmoid(lr)
    ones = ones_ref[...]

    def headsum(x):
        return jnp.dot(x, ones, precision=HI, preferred_element_type=F32)

    kk = k * kk_ref[...]
    kk = kk * lax.rsqrt(jnp.maximum(headsum(kk * kk), 1e-24))
    g_o[...] = jnp.dot(sg, gup_ref[...], precision=HI, preferred_element_type=F32)
    r_o[...] = r
    v_o[...] = v
    nkk_o[...] = -kk
    bonus = jnp.zeros_like(r)
    for d in range(2):
        wl = jnp.dot(th, wup_ref[d], precision=HI, preferred_element_type=F32) + w0_ref[d:d + 1, :]
        lw_o[d] = -DECAY_SCALE * _sigmoid(wl)
        a = _sigmoid(jnp.dot(lr, aup_ref[d], precision=HI, preferred_element_type=F32) + a0_ref[d:d + 1, :])
        kd = k * (1.0 + (a - 1.0) * ka_ref[...])
        kd_o[d] = kd
        b_o[d] = kk * a
        bonus = bonus + headsum(r * kd * rk_ref[...]) * v
    bonus_o[...] = bonus


def _rwkv_prep(rkv_raw, lr, p, seq, tm):
    t, w3 = rkv_raw.shape
    width = w3 // 3
    tpb = seq // tm
    hb = tm // HALO
    nhb = t // HALO
    kern = functools.partial(_rwkv_prep_kernel, tiles_per_batch=tpb, width=width)
    tok = lambda i: (i, 0)
    dtok = lambda i: (0, i, 0)
    full2 = lambda i: (0, 0)
    full3 = lambda i: (0, 0, 0)
    tw = jax.ShapeDtypeStruct((t, width), F32)
    dtw = jax.ShapeDtypeStruct((2, t, width), F32)
    return pl.pallas_call(
        kern,
        grid=(t // tm,),
        in_specs=[pl.BlockSpec((tm, w3), tok),
                  pl.BlockSpec((HALO, w3), lambda i: (jnp.maximum(i * hb - 1, 0), 0)),
                  pl.BlockSpec((HALO, w3), lambda i: (jnp.minimum((i + 1) * hb, nhb - 1), 0)),
                  pl.BlockSpec((tm, lr.shape[1]), tok),
                  pl.BlockSpec(p["conv"].shape, full2),
                  pl.BlockSpec(p["w0"].shape, full2),
                  pl.BlockSpec(p["w_up"].shape, full3),
                  pl.BlockSpec(p["a0"].shape, full2),
                  pl.BlockSpec(p["a_up"].shape, full3),
                  pl.BlockSpec(p["g_up"].shape, full2),
                  pl.BlockSpec(p["k_k"].shape, full2),
                  pl.BlockSpec(p["k_a"].shape, full2),
                  pl.BlockSpec(p["r_k"].shape, full2),
                  pl.BlockSpec(p["ones"].shape, full2)],
        out_specs=[pl.BlockSpec((tm, width), tok), pl.BlockSpec((tm, width), tok), pl.BlockSpec((tm, width), tok),
                   pl.BlockSpec((2, tm, width), dtok), pl.BlockSpec((2, tm, width), dtok),
                   pl.BlockSpec((2, tm, width), dtok),
                   pl.BlockSpec((tm, width), tok), pl.BlockSpec((tm, width), tok)],
        out_shape=[tw, tw, tw, dtw, dtw, dtw, tw, tw],
        compiler_params=_cparams(1),
        name="rwkv_prep",
    )(rkv_raw, rkv_raw, rkv_raw, lr, p["conv"], p["w0"], p["w_up"], p["a0"], p["a_up"], p["g_up"],
      p["k_k"], p["k_a"], p["r_k"], p["ones"])


def _dot_nt(a, b):
    return lax.dot_general(a, b, (((1,), (1,)), ((), ())), preferred_element_type=F32)


def _dot_tn(a, b):
    return lax.dot_general(a, b, (((0,), (0,)), ((), ())), preferred_element_type=F32)


def _mm(a, b):
    return jnp.dot(a.astype(BF16), b.astype(BF16), preferred_element_type=F32)


def _rwkv_scan_kernel(rf_ref, vf_ref, nf_ref, rb_ref, vb_ref, nb_ref, lwf_ref, bf_ref, kf_ref, lwb_ref, bb_ref, kb_ref,
                      yf_ref, yb_ref, s_ref, *, heads, batch):
    @pl.when(pl.program_id(0) == 0)
    def _():
        s_ref[...] = jnp.zeros_like(s_ref)

    n = SCAN_CHUNK
    row = lax.broadcasted_iota(jnp.int32, (n, n), 0)
    col = lax.broadcasted_iota(jnp.int32, (n, n), 1)
    levels = n.bit_length()
    same = [(row >> k) == (col >> k) for k in range(levels)]
    eye = same[0].astype(F32)
    level_masks = [same[sh + 1] & jnp.logical_not(same[sh]) for sh in range(1, levels - 1)]

    dirs = ((rf_ref, vf_ref, nf_ref, lwf_ref, bf_ref, kf_ref, yf_ref),
            (rb_ref, vb_ref, nb_ref, lwb_ref, bb_ref, kb_ref, yb_ref))
    chains = []
    for d, (r_ref, v_ref, n_ref, lw_ref, b_ref, k_ref, y_ref) in enumerate(dirs):
        order = row - col if d == 0 else col - row
        strict = order > 0
        incl = order >= 0
        incl_f = incl.astype(F32)
        for bi in range(batch):
            lw = lw_ref[0, bi]
            g_inc = jnp.dot(incl_f, lw, precision=HI, preferred_element_type=F32)
            g_tot = jnp.sum(lw, axis=0, keepdims=True)
            e_neg = jnp.exp(-g_inc)
            e_end = jnp.exp(g_tot - g_inc)
            decay = jnp.exp(g_tot)
            a_t = n_ref[bi] * jnp.exp(g_inc - lw)
            r_t = r_ref[bi] * jnp.exp(g_inc)
            bb = b_ref[0, bi]
            kd = k_ref[0, bi]
            b_t = (bb * e_neg).astype(BF16)
            k_t = (kd * e_neg).astype(BF16)
            ar_t = jnp.concatenate([a_t, r_t], axis=0).astype(BF16)
            bk_h = jnp.concatenate([bb * e_end, kd * e_end], axis=0).astype(BF16)
            v = v_ref[bi]
            for h in range(heads):
                sl = slice(h * HEAD_DIM, (h + 1) * HEAD_DIM)
                chains.append(dict(strict=strict, incl=incl, sl=sl, bi=bi, y_ref=y_ref,
                                   si=(d * batch + bi) * heads + h, decay=decay[:, sl],
                                   ar=ar_t[:, sl], b=b_t[:, sl], k=k_t[:, sl], bk_h=bk_h[:, sl], v=v[:, sl]))

    for ch in chains:
        pb = _dot_nt(ch["ar"], ch["b"])
        pk = _dot_nt(ch["ar"], ch["k"])
        ch["l_ab"] = jnp.where(ch["strict"], pb[:n], 0.0)
        ch["m_rb"] = jnp.where(ch["incl"], pb[n:], 0.0)
        ch["l_ak"] = jnp.where(ch["strict"], pk[:n], 0.0)
        ch["m_rk"] = jnp.where(ch["incl"], pk[n:], 0.0)
        ch["t"] = eye + jnp.where(same[1], ch["l_ab"], 0.0)
    for mask in level_masks:
        for ch in chains:
            ch["tc"] = _mm(ch["t"], jnp.where(mask, ch["l_ab"], 0.0))
        for ch in chains:
            ch["t"] = ch["t"] + _mm(ch["tc"], ch["t"])
    for ch in chains:
        ch["s0"] = s_ref[ch["si"]]
        ch["x"] = _dot_nt(ch["ar"], ch["s0"].astype(BF16))
    for ch in chains:
        ch["rhs"] = ch["x"][:n] + _mm(ch["l_ak"], ch["v"])
    for ch in chains:
        ch["u"] = _mm(ch["t"], ch["rhs"])
    for ch in chains:
        y = ch["x"][n:] + _mm(ch["m_rb"], ch["u"]) + _mm(ch["m_rk"], ch["v"])
        ch["y_ref"][ch["bi"], :, ch["sl"]] = y
    for ch in chains:
        uv = jnp.concatenate([ch["u"], ch["v"]], axis=0)
        s_ref[ch["si"]] = ch["s0"] * ch["decay"] + _dot_tn(uv.astype(BF16), ch["bk_h"])


def _rwkv_scan(r, v, nkk, lw, b, kd, batch, seq):
    t, width = r.shape
    heads = width // HEAD_DIM
    n = SCAN_CHUNK
    nc = seq // n
    r3, v3, n3 = (z.reshape(batch, seq, width) for z in (r, v, nkk))
    lw4, b4, k4 = (z.reshape(2, batch, seq, width) for z in (lw, b, kd))
    fwd = pl.BlockSpec((batch, n, width), lambda c: (0, c, 0))
    bwd = pl.BlockSpec((batch, n, width), lambda c: (0, nc - 1 - c, 0))
    fwd_d = pl.BlockSpec((1, batch, n, width), lambda c: (0, 0, c, 0))
    bwd_d = pl.BlockSpec((1, batch, n, width), lambda c: (1, 0, nc - 1 - c, 0))
    kern = functools.partial(_rwkv_scan_kernel, heads=heads, batch=batch)
    yf, yb = pl.pallas_call(
        kern,
        grid=(nc,),
        in_specs=[fwd, fwd, fwd, bwd, bwd, bwd, fwd_d, fwd_d, fwd_d, bwd_d, bwd_d, bwd_d],
        out_specs=[fwd, bwd],
        out_shape=[jax.ShapeDtypeStruct((batch, seq, width), F32)] * 2,
        scratch_shapes=[pltpu.VMEM((2 * batch * heads, HEAD_DIM, HEAD_DIM), F32)],
        compiler_params=_cparams(1, "arbitrary"),
        name="rwkv_scan",
    )(r3, v3, n3, r3, v3, n3, lw4, b4, k4, lw4, b4, k4)
    return yf.reshape(t, width), yb.reshape(t, width)


def _pool_kernel(p_ref, pp_ref, pn_ref, w_ref, sc_ref, o_ref, ext_ref, *, tiles_per_batch, seq):
    i = pl.program_id(0)
    tb = i % tiles_per_batch
    p = p_ref[...]
    tm, width = p.shape
    ext_ref[0:HALO, :] = jnp.where(tb == 0, 0.0, pp_ref[...])
    ext_ref[HALO:HALO + tm, :] = p
    ext_ref[HALO + tm:2 * HALO + tm, :] = jnp.where(tb == tiles_per_batch - 1, 0.0, pn_ref[...])

    def shifted(o):
        return ext_ref[HALO + o:HALO + o + tm, :]

    t = tb * tm + lax.broadcasted_iota(jnp.int32, (tm, width), 0)
    grp = lax.broadcasted_iota(jnp.int32, (tm, width), 1) // (width // len(POOL_WINDOWS))
    tot = p
    prev_half = 0
    pooled = jnp.zeros_like(p)
    for gi, win in enumerate(POOL_WINDOWS):
        half = win // 2
        for o in range(prev_half, half):
            tot = tot + shifted(-o - 1)
            if o > 0:
                tot = tot + shifted(o)
        prev_half = half
        lo = jnp.clip(t - half, 0, seq - 1)
        hi = jnp.clip(t + half - 1, 0, seq - 1)
        cnt = (hi - lo + 1).astype(F32)
        pooled = jnp.where(grp == gi, tot / cnt, pooled)
    pooled = pooled - p
    o_ref[...] = jnp.dot(pooled, w_ref[...], preferred_element_type=F32) * sc_ref[...]


def _pool(praw, w_blk, scale, seq, tm):
    t, width = praw.shape
    tpb = seq // tm
    hb = tm // HALO
    nhb = t // HALO
    kern = functools.partial(_pool_kernel, tiles_per_batch=tpb, seq=seq)
    return pl.pallas_call(
        kern,
        grid=(t // tm,),
        in_specs=[pl.BlockSpec((tm, width), lambda i: (i, 0)),
                  pl.BlockSpec((HALO, width), lambda i: (jnp.maximum(i * hb - 1, 0), 0)),
                  pl.BlockSpec((HALO, width), lambda i: (jnp.minimum((i + 1) * hb, nhb - 1), 0)),
                  pl.BlockSpec(w_blk.shape, lambda i: (0, 0)),
                  pl.BlockSpec(scale.shape, lambda i: (0, 0))],
        out_specs=pl.BlockSpec((tm, width), lambda i: (i, 0)),
        out_shape=jax.ShapeDtypeStruct((t, width), F32),
        scratch_shapes=[pltpu.VMEM((tm + 2 * HALO, width), F32)],
        compiler_params=_cparams(1),
        name="pool",
    )(praw, praw, praw, w_blk, scale)


def _outproj_kernel(ya_ref, yf_ref, yb_ref, bonus_ref, g_ref, yc_ref, x_ref, mod_ref, wa_ref, wb_ref, wc_ref,
                    gng_ref, gnb_ref, ones_ref, l1g_ref, l1b_ref, wr_ref, br_ref,
                    x1_o, u2_o, ri_o, rw_o, cnt_o, cnt_ref, *, alpha):
    m = mod_ref[0]
    ones = ones_ref[...]

    def headmean(x):
        return jnp.dot(x, ones, precision=HI, preferred_element_type=F32) * (1.0 / HEAD_DIM)

    ysum = yf_ref[...] + yb_ref[...]
    yc0 = ysum - headmean(ysum)
    yn = yc0 * lax.rsqrt(headmean(yc0 * yc0) + GN_EPS) * gng_ref[...] + gnb_ref[...]
    yb = (yn + bonus_ref[...]) * g_ref[...]
    mix = (jnp.dot(ya_ref[...].astype(BF16), wa_ref[...], preferred_element_type=F32)
           + jnp.dot(yb.astype(BF16), wb_ref[...], preferred_element_type=F32)
           + jnp.dot(yc_ref[...].astype(BF16), wc_ref[...], preferred_element_type=F32))
    x1 = _ln(alpha * x_ref[...] + m[2:3] * mix) * l1g_ref[...] + l1b_ref[...]
    x1_o[...] = x1
    u2 = _ln(x1) * (1.0 + m[4:5]) + m[3:4]
    u2_o[...] = u2

    lg = jnp.dot(u2, wr_ref[...], precision=HI, preferred_element_type=F32) + br_ref[...]
    lane = lax.broadcasted_iota(jnp.int32, lg.shape, 1)
    big = jnp.int32(1 << 20)
    gl = jnp.where(lane < N_GROUPS, lg, -jnp.inf)
    gmax = jnp.max(gl, axis=-1, keepdims=True)
    gidx = jnp.min(jnp.where(gl == gmax, lane, big), axis=-1, keepdims=True)
    pg_sel = 1.0 / jnp.sum(jnp.exp(gl - gmax), axis=-1, keepdims=True)
    e_lo = N_GROUPS + gidx * EXPERTS_PER_GROUP
    el = jnp.where((lane >= e_lo) & (lane < e_lo + EXPERTS_PER_GROUP), lg, -jnp.inf)
    m1 = jnp.max(el, axis=-1, keepdims=True)
    i1 = jnp.min(jnp.where(el == m1, lane, big), axis=-1, keepdims=True)
    el2 = jnp.where(lane == i1, -jnp.inf, el)
    m2 = jnp.max(el2, axis=-1, keepdims=True)
    i2 = jnp.min(jnp.where(el2 == m2, lane, big), axis=-1, keepdims=True)
    e21 = jnp.exp(m2 - m1)
    p1 = 1.0 / (1.0 + e21)
    p2 = e21 / (1.0 + e21)
    rw_o[...] = jnp.where(lane == 0, pg_sel * p1, jnp.where(lane == 1, pg_sel * p2, 0.0))

    @pl.when(pl.program_id(0) == 0)
    def _():
        cnt_ref[...] = jnp.zeros_like(cnt_ref)

    tm = lg.shape[0]
    earlier = (lax.broadcasted_iota(jnp.int32, (tm, tm), 1)
               < lax.broadcasted_iota(jnp.int32, (tm, tm), 0)).astype(BF16)
    oh1 = (lane == i1).astype(F32)
    oh2 = (lane == i2).astype(F32)
    run = cnt_ref[...]
    c1 = jnp.sum(oh1, axis=0, keepdims=True)
    before1 = run + jnp.dot(earlier, oh1.astype(BF16), preferred_element_type=F32)
    before2 = run + c1 + jnp.dot(earlier, oh2.astype(BF16), preferred_element_type=F32)
    rank1 = jnp.sum(oh1 * before1, axis=-1, keepdims=True).astype(jnp.int32)
    rank2 = jnp.sum(oh2 * before2, axis=-1, keepdims=True).astype(jnp.int32)
    total = run + c1 + jnp.sum(oh2, axis=0, keepdims=True)
    cnt_ref[...] = total
    cnt_o[...] = total
    ri_o[...] = jnp.where(lane == 0, i1 - N_GROUPS, jnp.where(lane == 1, i2 - N_GROUPS,
                          jnp.where(lane == 2, rank1, jnp.where(lane == 3, rank2, 0))))


def _outproj(ya, yf, yb, bonus, g, yc, x2, modl, p, seq, tm, alpha):
    t, d = x2.shape
    tpb = seq // tm
    aw, bw, cw = ya.shape[1], bonus.shape[1], yc.shape[1]
    tok = lambda i: (i, 0)
    full2 = lambda i: (0, 0)
    kern = functools.partial(_outproj_kernel, alpha=alpha)
    small = ["gn_gain", "gn_bias", "ones", "ln1_gain", "ln1_bias", "w_router", "b_router"]
    return pl.pallas_call(
        kern,
        grid=(t // tm,),
        in_specs=[pl.BlockSpec((tm, aw), tok),
                  pl.BlockSpec((tm, bw), tok), pl.BlockSpec((tm, bw), tok),
                  pl.BlockSpec((tm, bw), tok), pl.BlockSpec((tm, bw), tok),
                  pl.BlockSpec((tm, cw), tok),
                  pl.BlockSpec((tm, d), tok),
                  pl.BlockSpec((1,) + modl.shape[1:], lambda i: (i // tpb, 0, 0)),
                  pl.BlockSpec(p["w_out_a"].shape, full2),
                  pl.BlockSpec(p["w_out_b"].shape, full2),
                  pl.BlockSpec(p["w_out_c"].shape, full2)]
                 + [pl.BlockSpec(p[k].shape, full2) for k in small],
        out_specs=[pl.BlockSpec((tm, d), tok), pl.BlockSpec((tm, d), tok),
                   pl.BlockSpec((tm, LANES), tok), pl.BlockSpec((tm, LANES), tok),
                   pl.BlockSpec((1, LANES), full2)],
        out_shape=[jax.ShapeDtypeStruct((t, d), F32), jax.ShapeDtypeStruct((t, d), F32),
                   jax.ShapeDtypeStruct((t, LANES), jnp.int32), jax.ShapeDtypeStruct((t, LANES), F32),
                   jax.ShapeDtypeStruct((1, LANES), F32)],
        scratch_shapes=[pltpu.VMEM((1, LANES), F32)],
        compiler_params=_cparams(1, "arbitrary"),
        name="outproj",
    )(ya, yf, yb, bonus, g, yc, x2, modl, p["w_out_a"], p["w_out_b"], p["w_out_c"], *[p[k] for k in small])


def _dispatch(route_i, counts_lanes, n_blocks):
    counts = counts_lanes[0, N_GROUPS:N_GROUPS + N_EXPERTS].astype(jnp.int32)
    padded = ((counts + EXPERT_BLOCK - 1) // EXPERT_BLOCK) * EXPERT_BLOCK
    pends = jnp.cumsum(padded)
    pstarts = pends - padded
    e = route_i[:, :TOP_K]
    rank = route_i[:, TOP_K:2 * TOP_K]
    ids = jnp.arange(N_EXPERTS, dtype=jnp.int32)
    dest = jnp.sum(jnp.where(e[..., None] == ids, pstarts, 0), axis=-1) + rank
    block_start = jnp.arange(n_blocks, dtype=jnp.int32) * EXPERT_BLOCK
    block_e = jnp.minimum(jnp.sum((pends[None, :] <= block_start[:, None]).astype(jnp.int32), axis=1), N_EXPERTS - 1)
    meta = jnp.concatenate([block_e, (pends[-1] // EXPERT_BLOCK)[None]]).astype(jnp.int32)
    return dest, meta


def _scatter_rows_kernel(dest_ref, u_ref, xs_in_ref, xs_ref, sem):
    del xs_in_ref
    tm = u_ref.shape[0]

    def issue(r, carry):
        for k in range(TOP_K):
            dst = dest_ref[0, 0, TOP_K * r + k]
            pltpu.make_async_copy(u_ref.at[pl.ds(r, 1)], xs_ref.at[pl.ds(dst, 1)], sem).start()
        return carry

    lax.fori_loop(0, tm, issue, 0, unroll=8)
    rows = pl.ds(0, TOP_K * tm)
    pltpu.make_async_copy(xs_ref.at[rows], xs_ref.at[rows], sem).wait()


def _scatter_rows(u2, dest3, total, tm):
    t, d = u2.shape
    return pl.pallas_call(
        _scatter_rows_kernel,
        grid=(t // tm,),
        in_specs=[pl.BlockSpec((1, 1, TOP_K * tm), lambda i: (i, 0, 0), memory_space=pltpu.SMEM),
                  pl.BlockSpec((tm, d), lambda i: (i, 0)),
                  pl.BlockSpec(memory_space=pl.ANY)],
        out_specs=pl.BlockSpec(memory_space=pl.ANY),
        out_shape=jax.ShapeDtypeStruct((total, d), F32),
        scratch_shapes=[pltpu.SemaphoreType.DMA(())],
        input_output_aliases={2: 0},
        compiler_params=_cparams(1, "arbitrary"),
        name="scatter_rows",
    )(dest3, u2, jnp.zeros((total, d), F32))


def _experts_kernel(meta_ref, xs_ref, wg_ref, wu_ref, wd_ref, o_ref, wg_b, wu_b, wd_b):
    i = pl.program_id(0)
    n_used = meta_ref[pl.num_programs(0)]

    @pl.when((i == 0) | (meta_ref[i] != meta_ref[jnp.maximum(i - 1, 0)]))
    def _():
        wg_b[...] = wg_ref[0].astype(BF16)
        wu_b[...] = wu_ref[0].astype(BF16)
        wd_b[...] = wd_ref[0].astype(BF16)

    @pl.when(i < n_used)
    def _():
        xb = xs_ref[...].astype(BF16)
        gate = jnp.dot(xb, wg_b[...], preferred_element_type=F32)
        up = jnp.dot(xb, wu_b[...], preferred_element_type=F32)
        hb = gate * _sigmoid(gate) * up
        o_ref[...] = jnp.dot(hb.astype(BF16), wd_b[...], preferred_element_type=F32)

    @pl.when(i >= n_used)
    def _():
        o_ref[...] = jnp.zeros_like(o_ref)


def _experts(meta, xs, wg, wu, wd):
    total, d = xs.shape
    nb = total // EXPERT_BLOCK
    de = wg.shape[2]
    grid_spec = pltpu.PrefetchScalarGridSpec(
        num_scalar_prefetch=1,
        grid=(nb,),
        in_specs=[pl.BlockSpec((EXPERT_BLOCK, d), lambda i, m: (i, 0)),
                  pl.BlockSpec((1, d, de), lambda i, m: (m[i], 0, 0)),
                  pl.BlockSpec((1, d, de), lambda i, m: (m[i], 0, 0)),
                  pl.BlockSpec((1, de, d), lambda i, m: (m[i], 0, 0))],
        out_specs=pl.BlockSpec((EXPERT_BLOCK, d), lambda i, m: (i, 0)),
        scratch_shapes=[pltpu.VMEM((d, de), BF16), pltpu.VMEM((d, de), BF16), pltpu.VMEM((de, d), BF16)],
    )
    return pl.pallas_call(
        _experts_kernel,
        grid_spec=grid_spec,
        out_shape=jax.ShapeDtypeStruct((total, d), F32),
        compiler_params=_cparams(1, "arbitrary"),
        name="experts",
    )(meta, xs, wg, wu, wd)


def _final_kernel(dcur_ref, dnext_ref, x1_ref, rw_ref, mod_ref, g_ref, b_ref, ys_ref, o_ref, ybuf, sem, *, alpha):
    i = pl.program_id(0)
    tm = x1_ref.shape[0]
    slot = i % 2

    def gather(d_ref, s):
        def issue(r, carry):
            for k in range(TOP_K):
                src = d_ref[0, 0, TOP_K * r + k]
                pltpu.make_async_copy(ys_ref.at[pl.ds(src, 1)], ybuf.at[s, k, pl.ds(r, 1)], sem.at[s]).start()
            return carry

        lax.fori_loop(0, tm, issue, 0, unroll=8)

    @pl.when(i == 0)
    def _():
        gather(dcur_ref, 0)

    @pl.when(i + 1 < pl.num_programs(0))
    def _():
        gather(dnext_ref, 1 - slot)

    pltpu.make_async_copy(ybuf.at[slot], ybuf.at[slot], sem.at[slot]).wait()
    m = mod_ref[0]
    rw = rw_ref[...]
    f = rw[:, 0:1] * ybuf[slot, 0] + rw[:, 1:2] * ybuf[slot, 1]
    o_ref[...] = _ln(alpha * x1_ref[...] + m[5:6] * f) * g_ref[...] + b_ref[...]


def _final(x1, ysorted, dest3, rw, modl, gain, bias, seq, tm, alpha):
    t, d = x1.shape
    tpb = seq // tm
    n_tiles = t // tm
    tok = lambda i: (i, 0)
    kern = functools.partial(_final_kernel, alpha=alpha)
    dspec = lambda f: pl.BlockSpec((1, 1, TOP_K * tm), f, memory_space=pltpu.SMEM)
    return pl.pallas_call(
        kern,
        grid=(n_tiles,),
        in_specs=[dspec(lambda i: (i, 0, 0)), dspec(lambda i: (jnp.minimum(i + 1, n_tiles - 1), 0, 0)),
                  pl.BlockSpec((tm, d), tok), pl.BlockSpec((tm, LANES), tok),
                  pl.BlockSpec((1,) + modl.shape[1:], lambda i: (i // tpb, 0, 0)),
                  pl.BlockSpec(gain.shape, lambda i: (0, 0)), pl.BlockSpec(bias.shape, lambda i: (0, 0)),
                  pl.BlockSpec(memory_space=pl.ANY)],
        out_specs=pl.BlockSpec((tm, d), tok),
        out_shape=jax.ShapeDtypeStruct((t, d), F32),
        scratch_shapes=[pltpu.VMEM((2, TOP_K, tm, d), F32), pltpu.SemaphoreType.DMA((2,))],
        compiler_params=_cparams(1, "arbitrary"),
        name="final_ln",
    )(dest3, dest3, x1, rw, modl, gain, bias, ysorted)


def _block_diag(blocks):
    n, a, b = blocks.shape
    out = jnp.zeros((n * a, n * b), blocks.dtype)
    for i in range(n):
        out = out.at[i * a:(i + 1) * a, i * b:(i + 1) * b].set(blocks[i])
    return out


def _pad_rows(w, lo, total):
    return jnp.zeros((total, w.shape[-1]), w.dtype).at[lo:lo + w.shape[0]].set(w)


def kernel(x, c, w_mod, b_mod, w_in, na_rpb, rw_conv, rw_w0, rw_w_up, rw_a0, rw_a_up, rw_g_up, rw_k_k, rw_k_a, rw_r_k, rw_gn_gain, rw_gn_bias, pool_w, pool_scale, w_out, ln1_gain, ln1_bias, ln2_gain, ln2_bias, moe_w_group, moe_b_group, moe_w_expert, moe_b_expert, moe_w_gate, moe_w_up, moe_w_down):
    batch, seq, d = x.shape
    depth = w_mod.shape[0]
    t = batch * seq
    a_w = na_rpb.shape[1] * HEAD_DIM
    b_w = rw_w0.shape[-1]
    c_w = pool_scale.shape[-1]
    lr_w = R_W + R_A + R_G
    alpha = (2 * depth) ** 0.25
    tm = min(512, seq)
    tm_prep = min(256, seq)
    assert seq % tm == 0 and seq % SCAN_CHUNK == 0 and seq % GRID_W == 0 and lr_w == LANES

    mod = _modulation(c, w_mod, b_mod)
    ones_blk = _block_diag(jnp.ones((b_w // HEAD_DIM, HEAD_DIM, HEAD_DIM), F32))
    row = lambda v: v.reshape(1, -1)

    x2 = x.reshape(t, d)
    for l in range(depth):
        modl = mod[l]
        qkv, rkv_raw, lr, praw = _inproj(x2, modl, w_in[l].astype(BF16), seq, tm, 3 * a_w, 3 * b_w, lr_w, c_w)
        ya = _natten(qkv, _na_bias_table(na_rpb[l]), batch, seq, a_w)
        prep_params = {
            "conv": rw_conv[l], "w0": rw_w0[l], "a0": rw_a0[l],
            "w_up": jnp.stack([_pad_rows(rw_w_up[l, dd], 0, lr_w) for dd in range(2)]),
            "a_up": jnp.stack([_pad_rows(rw_a_up[l, dd], R_W, lr_w) for dd in range(2)]),
            "g_up": _pad_rows(rw_g_up[l], R_W + R_A, lr_w),
            "k_k": row(rw_k_k[l]), "k_a": row(rw_k_a[l]), "r_k": row(rw_r_k[l]), "ones": ones_blk,
        }
        r, v, nkk, lw, bb, kd, bonus, g = _rwkv_prep(rkv_raw, lr, prep_params, seq, tm_prep)
        yf, yb = _rwkv_scan(r, v, nkk, lw, bb, kd, batch, seq)
        yc = _pool(praw, _block_diag(pool_w[l]), row(pool_scale[l]), seq, tm)
        w_router = jnp.zeros((d, LANES), F32).at[:, :N_GROUPS].set(moe_w_group[l])
        w_router = w_router.at[:, N_GROUPS:N_GROUPS + N_EXPERTS].set(moe_w_expert[l])
        b_router = jnp.zeros((1, LANES), F32).at[0, :N_GROUPS].set(moe_b_group[l])
        b_router = b_router.at[0, N_GROUPS:N_GROUPS + N_EXPERTS].set(moe_b_expert[l])
        wo = w_out[l].astype(BF16)
        out_params = {
            "w_out_a": wo[:a_w], "w_out_b": wo[a_w:a_w + b_w], "w_out_c": wo[a_w + b_w:],
            "gn_gain": row(rw_gn_gain[l]), "gn_bias": row(rw_gn_bias[l]), "ones": ones_blk,
            "ln1_gain": row(ln1_gain[l]), "ln1_bias": row(ln1_bias[l]),
            "w_router": w_router, "b_router": b_router,
        }
        x1, u2, route_i, route_w, counts = _outproj(ya, yf, yb, bonus, g, yc, x2, modl, out_params, seq, tm, alpha)
        n_blocks = -(-(t * TOP_K) // EXPERT_BLOCK) + N_EXPERTS
        dest, meta = _dispatch(route_i, counts, n_blocks)
        dest3 = dest.reshape(t // tm, 1, TOP_K * tm)
        xs = _scatter_rows(u2, dest3, n_blocks * EXPERT_BLOCK, tm)
        ysorted = _experts(meta, xs, moe_w_gate[l], moe_w_up[l], moe_w_down[l])
        x2 = _final(x1, ysorted, dest3, route_w, modl, row(ln2_gain[l]), row(ln2_bias[l]), seq, tm, alpha)
    return x2.reshape(batch, seq, d)
```

```python
import functools
import math

import jax
import jax.numpy as jnp
import numpy as np
from jax import lax
from jax.experimental import pallas as pl
from jax.experimental.pallas import tpu as pltpu

F32 = jnp.float32
BF16 = jnp.bfloat16
HI = lax.Precision.HIGHEST

GRID_W = 64
HEAD_DIM = 64
NA_KH = 8
NA_KW = 16
POOL_WINDOWS = (2, 4, 8, 16)
R_W = 32
R_A = 32
R_G = 64
DECAY_SCALE = math.exp(-0.5)
GN_EPS = 64e-5
N_GROUPS = 4
EXPERTS_PER_GROUP = 8
N_EXPERTS = N_GROUPS * EXPERTS_PER_GROUP
TOP_K = 2
EXPERT_BLOCK = 256
LN_EPS = 1e-5
NEG_INF = -1e30

SCAN_CHUNK = 64
HALO = 8
LANES = 128
VMEM_LIMIT = 52 * 1024 * 1024


def _ln(x):
    mu = jnp.mean(x, axis=-1, keepdims=True)
    xc = x - mu
    var = jnp.mean(xc * xc, axis=-1, keepdims=True)
    return xc * lax.rsqrt(var + LN_EPS)


def _sigmoid(x):
    return 1.0 / (1.0 + jnp.exp(-x))


def _cparams(n_axes, semantics="parallel"):
    return pltpu.CompilerParams(dimension_semantics=(semantics,) * n_axes, vmem_limit_bytes=VMEM_LIMIT)


def _mod_kernel(c_ref, w_ref, b_ref, o_ref):
    c = c_ref[...]
    s = c * _sigmoid(c)
    o_ref[0] = jnp.dot(s, w_ref[0], precision=HI, preferred_element_type=F32) + b_ref[0]


def _modulation(c, w_mod, b_mod):
    n_layers, d, d6 = w_mod.shape
    b = c.shape[0]
    bp = -(-b // 8) * 8
    cp = jnp.zeros((bp, d), F32).at[:b].set(c)
    out = pl.pallas_call(
        _mod_kernel,
        grid=(n_layers, d6 // d),
        in_specs=[pl.BlockSpec((bp, d), lambda l, j: (0, 0)),
                  pl.BlockSpec((1, d, d), lambda l, j: (l, 0, j)),
                  pl.BlockSpec((1, 1, d), lambda l, j: (l, 0, j))],
        out_specs=pl.BlockSpec((1, bp, d), lambda l, j: (l, 0, j)),
        out_shape=jax.ShapeDtypeStruct((n_layers, bp, d6), F32),
        compiler_params=_cparams(2),
        name="modulation",
    )(cp, w_mod, b_mod.reshape(n_layers, 1, d6))
    return out[:, :b].reshape(n_layers, b, d6 // d, d)


def _inproj_kernel(x_ref, mod_ref, w_ref, qkv_ref, rkv_ref, lr_ref, pool_ref, *, a3, b3, lr_w):
    m = mod_ref[0]
    u = _ln(x_ref[...]) * (1.0 + m[1:2]) + m[0:1]
    h = jnp.dot(u.astype(BF16), w_ref[...], preferred_element_type=F32)
    qkv_ref[...] = h[:, :a3].astype(BF16)
    rkv_ref[...] = h[:, a3:a3 + b3]
    lr_ref[...] = h[:, a3 + b3:a3 + b3 + lr_w]
    pool_ref[...] = h[:, a3 + b3 + lr_w:]


def _inproj(x2, modl, w_in_bf, seq, tm, a3, b3, lr_w, c_w):
    t, d = x2.shape
    tpb = seq // tm
    kern = functools.partial(_inproj_kernel, a3=a3, b3=b3, lr_w=lr_w)
    return pl.pallas_call(
        kern,
        grid=(t // tm,),
        in_specs=[pl.BlockSpec((tm, d), lambda i: (i, 0)),
                  pl.BlockSpec((1,) + modl.shape[1:], lambda i: (i // tpb, 0, 0)),
                  pl.BlockSpec(w_in_bf.shape, lambda i: (0, 0))],
        out_specs=[pl.BlockSpec((tm, a3), lambda i: (i, 0)),
                   pl.BlockSpec((tm, b3), lambda i: (i, 0)),
                   pl.BlockSpec((tm, lr_w), lambda i: (i, 0)),
                   pl.BlockSpec((tm, c_w), lambda i: (i, 0))],
        out_shape=[jax.ShapeDtypeStruct((t, a3), BF16),
                   jax.ShapeDtypeStruct((t, b3), F32),
                   jax.ShapeDtypeStruct((t, lr_w), F32),
                   jax.ShapeDtypeStruct((t, c_w), F32)],
        compiler_params=_cparams(1),
        name="inproj",
    )(x2, modl, w_in_bf)


def _na_bias_table(rpb):
    col = np.arange(GRID_W)
    cstart = np.clip(col - NA_KW // 2, 0, GRID_W - NA_KW)
    in_win = (col[None, :] >= cstart[:, None]) & (col[None, :] < cstart[:, None] + NA_KW)
    dc = np.clip(col[None, :] - col[:, None], -(NA_KW - 1), NA_KW - 1) + (NA_KW - 1)
    pick = (dc[None] == np.arange(2 * NA_KW - 1)[:, None, None]).astype(np.float32)
    cols = jnp.einsum("hrc,cqk->hrqk", rpb.astype(F32), pick, precision=HI)
    cols = jnp.where(in_win, cols, NEG_INF)
    b = jnp.stack([cols[:, NA_KH - 1 - o:2 * NA_KH - 1 - o] for o in range(NA_KH)])
    h = rpb.shape[0]
    return jnp.transpose(b, (0, 1, 3, 2, 4)).reshape(NA_KH, h * GRID_W, NA_KH * GRID_W)


def _natten_kernel(q_ref, k_ref, v_ref, bias_ref, o_ref, *, rows, heads):
    r = pl.program_id(1)
    rstart = jnp.clip(r - NA_KH // 2, 0, rows - NA_KH)
    off = r - rstart
    start = pl.multiple_of(rstart * GRID_W, GRID_W)
    nk = NA_KH * GRID_W
    kw = k_ref[pl.ds(start, nk), :]
    vw = v_ref[pl.ds(start, nk), :]
    q = q_ref[...]
    width = q.shape[1]
    head_of_lane = lax.broadcasted_iota(jnp.int32, (heads * GRID_W, width), 1) // HEAD_DIM
    head_of_row = lax.broadcasted_iota(jnp.int32, (heads * GRID_W, width), 0) // GRID_W
    own = head_of_lane == head_of_row
    qs = jnp.where(own, jnp.concatenate([q] * heads, axis=0), jnp.zeros((), q.dtype))
    s = lax.dot_general(qs, kw, (((1,), (1,)), ((), ())), preferred_element_type=F32) * (HEAD_DIM ** -0.5)
    s = s + bias_ref[off]
    mx = jnp.max(s, axis=-1, keepdims=True)
    p = jnp.exp(s - mx)
    den = jnp.sum(p, axis=-1, keepdims=True)
    o = jnp.where(own, jnp.dot(p.astype(BF16), vw, preferred_element_type=F32) / den, 0.0)
    acc = o[0:GRID_W]
    for h in range(1, heads):
        acc = acc + o[h * GRID_W:(h + 1) * GRID_W]
    o_ref[...] = acc


def _natten(qkv, bias_tab, batch, seq, width):
    rows = seq // GRID_W
    assert rows >= NA_KH
    heads = width // HEAD_DIM
    kern = functools.partial(_natten_kernel, rows=rows, heads=heads)
    return pl.pallas_call(
        kern,
        grid=(batch, rows),
        in_specs=[pl.BlockSpec((GRID_W, width), lambda b, r: (b * rows + r, 0)),
                  pl.BlockSpec((seq, width), lambda b, r: (b, 1)),
                  pl.BlockSpec((seq, width), lambda b, r: (b, 2)),
                  pl.BlockSpec(bias_tab.shape, lambda b, r: (0, 0, 0))],
        out_specs=pl.BlockSpec((GRID_W, width), lambda b, r: (b * rows + r, 0)),
        out_shape=jax.ShapeDtypeStruct((batch * seq, width), F32),
        compiler_params=_cparams(2),
        name="natten",
    )(qkv, qkv, qkv, bias_tab)


def _rwkv_prep_kernel(z_ref, zp_ref, zn_ref, lr_ref, cw_ref, w0_ref, wup_ref, a0_ref, aup_ref, gup_ref,
                      kk_ref, ka_ref, rk_ref, ones_ref,
                      r_o, v_o, nkk_o, lw_o, b_o, kd_o, bonus_o, g_o, *, tiles_per_batch, width):
    i = pl.program_id(0)
    tb = i % tiles_per_batch
    z = z_ref[...]
    tm = z.shape[0]
    prev = jnp.where(tb == 0, 0.0, zp_ref[HALO - 1:HALO, :])
    nxt = jnp.where(tb == tiles_per_batch - 1, 0.0, zn_ref[0:1, :])
    row = lax.broadcasted_iota(jnp.int32, z.shape, 0)
    zm1 = jnp.where(row == 0, prev, pltpu.roll(z, 1, 0))
    zp1 = jnp.where(row == tm - 1, nxt, pltpu.roll(z, tm - 1, 0))
    rkv = zm1 * cw_ref[0:1, :] + z * cw_ref[1:2, :] + zp1 * cw_ref[2:3, :]
    r = rkv[:, :width]
    k = rkv[:, width:2 * width]
    v = rkv[:, 2 * width:]
    lr = lr_ref[...]
    th = jnp.tanh(lr)
    sg = _sigmoid(lr)
    ones = ones_ref[...]

    def headsum(x):
        return jnp.dot(x, ones, precision=HI, preferred_element_type=F32)

    kk = k * kk_ref[...]
    kk = kk * lax.rsqrt(jnp.maximum(headsum(kk * kk), 1e-24))
    g_o[...] = jnp.dot(sg, gup_ref[...], precision=HI, preferred_element_type=F32)
    r_o[...] = r
    v_o[...] = v
    nkk_o[...] = -kk
    bonus = jnp.zeros_like(r)
    for d in range(2):
        wl = jnp.dot(th, wup_ref[d], precision=HI, preferred_element_type=F32) + w0_ref[d:d + 1, :]
        lw_o[d] = -DECAY_SCALE * _sigmoid(wl)
        a = _sigmoid(jnp.dot(lr, aup_ref[d], precision=HI, preferred_element_type=F32) + a0_ref[d:d + 1, :])
        kd = k * (1.0 + (a - 1.0) * ka_ref[...])
        kd_o[d] = kd
        b_o[d] = kk * a
        bonus = bonus + headsum(r * kd * rk_ref[...]) * v
    bonus_o[...] = bonus


def _rwkv_prep(rkv_raw, lr, p, seq, tm):
    t, w3 = rkv_raw.shape
    width = w3 // 3
    tpb = seq // tm
    hb = tm // HALO
    nhb = t // HALO
    kern = functools.partial(_rwkv_prep_kernel, tiles_per_batch=tpb, width=width)
    tok = lambda i: (i, 0)
    dtok = lambda i: (0, i, 0)
    full2 = lambda i: (0, 0)
    full3 = lambda i: (0, 0, 0)
    tw = jax.ShapeDtypeStruct((t, width), F32)
    dtw = jax.ShapeDtypeStruct((2, t, width), F32)
    return pl.pallas_call(
        kern,
        grid=(t // tm,),
        in_specs=[pl.BlockSpec((tm, w3), tok),
                  pl.BlockSpec((HALO, w3), lambda i: (jnp.maximum(i * hb - 1, 0), 0)),
                  pl.BlockSpec((HALO, w3), lambda i: (jnp.minimum((i + 1) * hb, nhb - 1), 0)),
                  pl.BlockSpec((tm, lr.shape[1]), tok),
                  pl.BlockSpec(p["conv"].shape, full2),
                  pl.BlockSpec(p["w0"].shape, full2),
                  pl.BlockSpec(p["w_up"].shape, full3),
                  pl.BlockSpec(p["a0"].shape, full2),
                  pl.BlockSpec(p["a_up"].shape, full3),
                  pl.BlockSpec(p["g_up"].shape, full2),
                  pl.BlockSpec(p["k_k"].shape, full2),
                  pl.BlockSpec(p["k_a"].shape, full2),
                  pl.BlockSpec(p["r_k"].shape, full2),
                  pl.BlockSpec(p["ones"].shape, full2)],
        out_specs=[pl.BlockSpec((tm, width), tok), pl.BlockSpec((tm, width), tok), pl.BlockSpec((tm, width), tok),
                   pl.BlockSpec((2, tm, width), dtok), pl.BlockSpec((2, tm, width), dtok),
                   pl.BlockSpec((2, tm, width), dtok),
                   pl.BlockSpec((tm, width), tok), pl.BlockSpec((tm, width), tok)],
        out_shape=[tw, tw, tw, dtw, dtw, dtw, tw, tw],
        compiler_params=_cparams(1),
        name="rwkv_prep",
    )(rkv_raw, rkv_raw, rkv_raw, lr, p["conv"], p["w0"], p["w_up"], p["a0"], p["a_up"], p["g_up"],
      p["k_k"], p["k_a"], p["r_k"], p["ones"])


def _dot_nt(a, b):
    return lax.dot_general(a, b, (((1,), (1,)), ((), ())), preferred_element_type=F32)


def _dot_tn(a, b):
    return lax.dot_general(a, b, (((0,), (0,)), ((), ())), preferred_element_type=F32)


def _mm(a, b):
    return jnp.dot(a.astype(BF16), b.astype(BF16), preferred_element_type=F32)


def _rwkv_scan_kernel(rf_ref, vf_ref, nf_ref, rb_ref, vb_ref, nb_ref, lwf_ref, bf_ref, kf_ref, lwb_ref, bb_ref, kb_ref,
                      yf_ref, yb_ref, s_ref, *, heads, batch):
    @pl.when(pl.program_id(0) == 0)
    def _():
        s_ref[...] = jnp.zeros_like(s_ref)

    n = SCAN_CHUNK
    row = lax.broadcasted_iota(jnp.int32, (n, n), 0)
    col = lax.broadcasted_iota(jnp.int32, (n, n), 1)
    levels = n.bit_length()
    same = [(row >> k) == (col >> k) for k in range(levels)]
    eye = same[0].astype(F32)
    level_masks = [same[sh + 1] & jnp.logical_not(same[sh]) for sh in range(1, levels - 1)]

    dirs = ((rf_ref, vf_ref, nf_ref, lwf_ref, bf_ref, kf_ref, yf_ref),
            (rb_ref, vb_ref, nb_ref, lwb_ref, bb_ref, kb_ref, yb_ref))
    chains = []
    for d, (r_ref, v_ref, n_ref, lw_ref, b_ref, k_ref, y_ref) in enumerate(dirs):
        order = row - col if d == 0 else col - row
        strict = order > 0
        incl = order >= 0
        incl_f = incl.astype(F32)
        for bi in range(batch):
            lw = lw_ref[0, bi]
            g_inc = jnp.dot(incl_f, lw, precision=HI, preferred_element_type=F32)
            g_tot = jnp.sum(lw, axis=0, keepdims=True)
            e_neg = jnp.exp(-g_inc)
            e_end = jnp.exp(g_tot - g_inc)
            decay = jnp.exp(g_tot)
            a_t = n_ref[bi] * jnp.exp(g_inc - lw)
            r_t = r_ref[bi] * jnp.exp(g_inc)
            bb = b_ref[0, bi]
            kd = k_ref[0, bi]
            b_t = (bb * e_neg).astype(BF16)
            k_t = (kd * e_neg).astype(BF16)
            ar_t = jnp.concatenate([a_t, r_t], axis=0).astype(BF16)
            bk_h = jnp.concatenate([bb * e_end, kd * e_end], axis=0).astype(BF16)
            v = v_ref[bi]
            for h in range(heads):
                sl = slice(h * HEAD_DIM, (h + 1) * HEAD_DIM)
                chains.append(dict(strict=strict, incl=incl, sl=sl, bi=bi, y_ref=y_ref,
                                   si=(d * batch + bi) * heads + h, decay=decay[:, sl],
                                   ar=ar_t[:, sl], b=b_t[:, sl], k=k_t[:, sl], bk_h=bk_h[:, sl], v=v[:, sl]))

    for ch in chains:
        pb = _dot_nt(ch["ar"], ch["b"])
        pk = _dot_nt(ch["ar"], ch["k"])
        ch["l_ab"] = jnp.where(ch["strict"], pb[:n], 0.0)
        ch["m_rb"] = jnp.where(ch["incl"], pb[n:], 0.0)
        ch["l_ak"] = jnp.where(ch["strict"], pk[:n], 0.0)
        ch["m_rk"] = jnp.where(ch["incl"], pk[n:], 0.0)
        ch["t"] = eye + jnp.where(same[1], ch["l_ab"], 0.0)
    for mask in level_masks:
        for ch in chains:
            ch["tc"] = _mm(ch["t"], jnp.where(mask, ch["l_ab"], 0.0))
        for ch in chains:
            ch["t"] = ch["t"] + _mm(ch["tc"], ch["t"])
    for ch in chains:
        ch["s0"] = s_ref[ch["si"]]
        ch["x"] = _dot_nt(ch["ar"], ch["s0"].astype(BF16))
    for ch in chains:
        ch["rhs"] = ch["x"][:n] + _mm(ch["l_ak"], ch["v"])
    for ch in chains:
        ch["u"] = _mm(ch["t"], ch["rhs"])
    for ch in chains:
        y = ch["x"][n:] + _mm(ch["m_rb"], ch["u"]) + _mm(ch["m_rk"], ch["v"])
        ch["y_ref"][ch["bi"], :, ch["sl"]] = y
    for ch in chains:
        uv = jnp.concatenate([ch["u"], ch["v"]], axis=0)
        s_ref[ch["si"]] = ch["s0"] * ch["decay"] + _dot_tn(uv.astype(BF16), ch["bk_h"])


def _rwkv_scan(r, v, nkk, lw, b, kd, batch, seq):
    t, width = r.shape
    heads = width // HEAD_DIM
    n = SCAN_CHUNK
    nc = seq // n
    r3, v3, n3 = (z.reshape(batch, seq, width) for z in (r, v, nkk))
    lw4, b4, k4 = (z.reshape(2, batch, seq, width) for z in (lw, b, kd))
    fwd = pl.BlockSpec((batch, n, width), lambda c: (0, c, 0))
    bwd = pl.BlockSpec((batch, n, width), lambda c: (0, nc - 1 - c, 0))
    fwd_d = pl.BlockSpec((1, batch, n, width), lambda c: (0, 0, c, 0))
    bwd_d = pl.BlockSpec((1, batch, n, width), lambda c: (1, 0, nc - 1 - c, 0))
    kern = functools.partial(_rwkv_scan_kernel, heads=heads, batch=batch)
    yf, yb = pl.pallas_call(
        kern,
        grid=(nc,),
        in_specs=[fwd, fwd, fwd, bwd, bwd, bwd, fwd_d, fwd_d, fwd_d, bwd_d, bwd_d, bwd_d],
        out_specs=[fwd, bwd],
        out_shape=[jax.ShapeDtypeStruct((batch, seq, width), F32)] * 2,
        scratch_shapes=[pltpu.VMEM((2 * batch * heads, HEAD_DIM, HEAD_DIM), F32)],
        compiler_params=_cparams(1, "arbitrary"),
        name="rwkv_scan",
    )(r3, v3, n3, r3, v3, n3, lw4, b4, k4, lw4, b4, k4)
    return yf.reshape(t, width), yb.reshape(t, width)


def _pool_kernel(p_ref, pp_ref, pn_ref, w_ref, sc_ref, o_ref, ext_ref, *, tiles_per_batch, seq):
    i = pl.program_id(0)
    tb = i % tiles_per_batch
    p = p_ref[...]
    tm, width = p.shape
    ext_ref[0:HALO, :] = jnp.where(tb == 0, 0.0, pp_ref[...])
    ext_ref[HALO:HALO + tm, :] = p
    ext_ref[HALO + tm:2 * HALO + tm, :] = jnp.where(tb == tiles_per_batch - 1, 0.0, pn_ref[...])

    def shifted(o):
        return ext_ref[HALO + o:HALO + o + tm, :]

    t = tb * tm + lax.broadcasted_iota(jnp.int32, (tm, width), 0)
    grp = lax.broadcasted_iota(jnp.int32, (tm, width), 1) // (width // len(POOL_WINDOWS))
    tot = p
    prev_half = 0
    pooled = jnp.zeros_like(p)
    for gi, win in enumerate(POOL_WINDOWS):
        half = win // 2
        for o in range(prev_half, half):
            tot = tot + shifted(-o - 1)
            if o > 0:
                tot = tot + shifted(o)
        prev_half = half
        lo = jnp.clip(t - half, 0, seq - 1)
        hi = jnp.clip(t + half - 1, 0, seq - 1)
        cnt = (hi - lo + 1).astype(F32)
        pooled = jnp.where(grp == gi, tot / cnt, pooled)
    pooled = pooled - p
    o_ref[...] = jnp.dot(pooled, w_ref[...], preferred_element_type=F32) * sc_ref[...]


def _pool(praw, w_blk, scale, seq, tm):
    t, width = praw.shape
    tpb = seq // tm
    hb = tm // HALO
    nhb = t // HALO
    kern = functools.partial(_pool_kernel, tiles_per_batch=tpb, seq=seq)
    return pl.pallas_call(
        kern,
        grid=(t // tm,),
        in_specs=[pl.BlockSpec((tm, width), lambda i: (i, 0)),
                  pl.BlockSpec((HALO, width), lambda i: (jnp.maximum(i * hb - 1, 0), 0)),
                  pl.BlockSpec((HALO, width), lambda i: (jnp.minimum((i + 1) * hb, nhb - 1), 0)),
                  pl.BlockSpec(w_blk.shape, lambda i: (0, 0)),
                  pl.BlockSpec(scale.shape, lambda i: (0, 0))],
        out_specs=pl.BlockSpec((tm, width), lambda i: (i, 0)),
        out_shape=jax.ShapeDtypeStruct((t, width), F32),
        scratch_shapes=[pltpu.VMEM((tm + 2 * HALO, width), F32)],
        compiler_params=_cparams(1),
        name="pool",
    )(praw, praw, praw, w_blk, scale)


def _outproj_kernel(ya_ref, yf_ref, yb_ref, bonus_ref, g_ref, yc_ref, x_ref, mod_ref, wa_ref, wb_ref, wc_ref,
                    gng_ref, gnb_ref, ones_ref, l1g_ref, l1b_ref, wr_ref, br_ref,
                    x1_o, u2_o, ri_o, rw_o, cnt_o, cnt_ref, *, alpha):
    m = mod_ref[0]
    ones = ones_ref[...]

    def headmean(x):
        return jnp.dot(x, ones, precision=HI, preferred_element_type=F32) * (1.0 / HEAD_DIM)

    ysum = yf_ref[...] + yb_ref[...]
    yc0 = ysum - headmean(ysum)
    yn = yc0 * lax.rsqrt(headmean(yc0 * yc0) + GN_EPS) * gng_ref[...] + gnb_ref[...]
    yb = (yn + bonus_ref[...]) * g_ref[...]
    mix = (jnp.dot(ya_ref[...].astype(BF16), wa_ref[...], preferred_element_type=F32)
           + jnp.dot(yb.astype(BF16), wb_ref[...], preferred_element_type=F32)
           + jnp.dot(yc_ref[...].astype(BF16), wc_ref[...], preferred_element_type=F32))
    x1 = _ln(alpha * x_ref[...] + m[2:3] * mix) * l1g_ref[...] + l1b_ref[...]
    x1_o[...] = x1
    u2 = _ln(x1) * (1.0 + m[4:5]) + m[3:4]
    u2_o[...] = u2

    lg = jnp.dot(u2, wr_ref[...], precision=HI, preferred_element_type=F32) + br_ref[...]
    lane = lax.broadcasted_iota(jnp.int32, lg.shape, 1)
    big = jnp.int32(1 << 20)
    gl = jnp.where(lane < N_GROUPS, lg, -jnp.inf)
    gmax = jnp.max(gl, axis=-1, keepdims=True)
    gidx = jnp.min(jnp.where(gl == gmax, lane, big), axis=-1, keepdims=True)
    pg_sel = 1.0 / jnp.sum(jnp.exp(gl - gmax), axis=-1, keepdims=True)
    e_lo = N_GROUPS + gidx * EXPERTS_PER_GROUP
    el = jnp.where((lane >= e_lo) & (lane < e_lo + EXPERTS_PER_GROUP), lg, -jnp.inf)
    m1 = jnp.max(el, axis=-1, keepdims=True)
    i1 = jnp.min(jnp.where(el == m1, lane, big), axis=-1, keepdims=True)
    el2 = jnp.where(lane == i1, -jnp.inf, el)
    m2 = jnp.max(el2, axis=-1, keepdims=True)
    i2 = jnp.min(jnp.where(el2 == m2, lane, big), axis=-1, keepdims=True)
    e21 = jnp.exp(m2 - m1)
    p1 = 1.0 / (1.0 + e21)
    p2 = e21 / (1.0 + e21)
    rw_o[...] = jnp.where(lane == 0, pg_sel * p1, jnp.where(lane == 1, pg_sel * p2, 0.0))

    @pl.when(pl.program_id(0) == 0)
    def _():
        cnt_ref[...] = jnp.zeros_like(cnt_ref)

    tm = lg.shape[0]
    earlier = (lax.broadcasted_iota(jnp.int32, (tm, tm), 1)
               < lax.broadcasted_iota(jnp.int32, (tm, tm), 0)).astype(BF16)
    oh1 = (lane == i1).astype(F32)
    oh2 = (lane == i2).astype(F32)
    run = cnt_ref[...]
    c1 = jnp.sum(oh1, axis=0, keepdims=True)
    before1 = run + jnp.dot(earlier, oh1.astype(BF16), preferred_element_type=F32)
    before2 = run + c1 + jnp.dot(earlier, oh2.astype(BF16), preferred_element_type=F32)
    rank1 = jnp.sum(oh1 * before1, axis=-1, keepdims=True).astype(jnp.int32)
    rank2 = jnp.sum(oh2 * before2, axis=-1, keepdims=True).astype(jnp.int32)
    total = run + c1 + jnp.sum(oh2, axis=0, keepdims=True)
    cnt_ref[...] = total
    cnt_o[...] = total
    ri_o[...] = jnp.where(lane == 0, i1 - N_GROUPS, jnp.where(lane == 1, i2 - N_GROUPS,
                          jnp.where(lane == 2, rank1, jnp.where(lane == 3, rank2, 0))))


def _outproj(ya, yf, yb, bonus, g, yc, x2, modl, p, seq, tm, alpha):
    t, d = x2.shape
    tpb = seq // tm
    aw, bw, cw = ya.shape[1], bonus.shape[1], yc.shape[1]
    tok = lambda i: (i, 0)
    full2 = lambda i: (0, 0)
    kern = functools.partial(_outproj_kernel, alpha=alpha)
    small = ["gn_gain", "gn_bias", "ones", "ln1_gain", "ln1_bias", "w_router", "b_router"]
    return pl.pallas_call(
        kern,
        grid=(t // tm,),
        in_specs=[pl.BlockSpec((tm, aw), tok),
                  pl.BlockSpec((tm, bw), tok), pl.BlockSpec((tm, bw), tok),
                  pl.BlockSpec((tm, bw), tok), pl.BlockSpec((tm, bw), tok),
                  pl.BlockSpec((tm, cw), tok),
                  pl.BlockSpec((tm, d), tok),
                  pl.BlockSpec((1,) + modl.shape[1:], lambda i: (i // tpb, 0, 0)),
                  pl.BlockSpec(p["w_out_a"].shape, full2),
                  pl.BlockSpec(p["w_out_b"].shape, full2),
                  pl.BlockSpec(p["w_out_c"].shape, full2)]
                 + [pl.BlockSpec(p[k].shape, full2) for k in small],
        out_specs=[pl.BlockSpec((tm, d), tok), pl.BlockSpec((tm, d), tok),
                   pl.BlockSpec((tm, LANES), tok), pl.BlockSpec((tm, LANES), tok),
                   pl.BlockSpec((1, LANES), full2)],
        out_shape=[jax.ShapeDtypeStruct((t, d), F32), jax.ShapeDtypeStruct((t, d), F32),
                   jax.ShapeDtypeStruct((t, LANES), jnp.int32), jax.ShapeDtypeStruct((t, LANES), F32),
                   jax.ShapeDtypeStruct((1, LANES), F32)],
        scratch_shapes=[pltpu.VMEM((1, LANES), F32)],
        compiler_params=_cparams(1, "arbitrary"),
        name="outproj",
    )(ya, yf, yb, bonus, g, yc, x2, modl, p["w_out_a"], p["w_out_b"], p["w_out_c"], *[p[k] for k in small])


def _dispatch(route_i, counts_lanes, n_blocks):
    counts = counts_lanes[0, N_GROUPS:N_GROUPS + N_EXPERTS].astype(jnp.int32)
    padded = ((counts + EXPERT_BLOCK - 1) // EXPERT_BLOCK) * EXPERT_BLOCK
    pends = jnp.cumsum(padded)
    pstarts = pends - padded
    e = route_i[:, :TOP_K]
    rank = route_i[:, TOP_K:2 * TOP_K]
    ids = jnp.arange(N_EXPERTS, dtype=jnp.int32)
    dest = jnp.sum(jnp.where(e[..., None] == ids, pstarts, 0), axis=-1) + rank
    block_start = jnp.arange(n_blocks, dtype=jnp.int32) * EXPERT_BLOCK
    block_e = jnp.minimum(jnp.sum((pends[None, :] <= block_start[:, None]).astype(jnp.int32), axis=1), N_EXPERTS - 1)
    meta = jnp.concatenate([block_e, (pends[-1] // EXPERT_BLOCK)[None]]).astype(jnp.int32)
    return dest, meta


def _scatter_rows_kernel(dest_ref, u_ref, xs_in_ref, xs_ref, sem):
    del xs_in_ref
    tm = u_ref.shape[0]

    def issue(r, carry):
        for k in range(TOP_K):
            dst = dest_ref[0, 0, TOP_K * r + k]
            pltpu.make_async_copy(u_ref.at[pl.ds(r, 1)], xs_ref.at[pl.ds(dst, 1)], sem).start()
        return carry

    lax.fori_loop(0, tm, issue, 0, unroll=8)
    rows = pl.ds(0, TOP_K * tm)
    pltpu.make_async_copy(xs_ref.at[rows], xs_ref.at[rows], sem).wait()


def _scatter_rows(u2, dest3, total, tm):
    t, d = u2.shape
    return pl.pallas_call(
        _scatter_rows_kernel,
        grid=(t // tm,),
        in_specs=[pl.BlockSpec((1, 1, TOP_K * tm), lambda i: (i, 0, 0), memory_space=pltpu.SMEM),
                  pl.BlockSpec((tm, d), lambda i: (i, 0)),
                  pl.BlockSpec(memory_space=pl.ANY)],
        out_specs=pl.BlockSpec(memory_space=pl.ANY),
        out_shape=jax.ShapeDtypeStruct((total, d), F32),
        scratch_shapes=[pltpu.SemaphoreType.DMA(())],
        input_output_aliases={2: 0},
        compiler_params=_cparams(1, "arbitrary"),
        name="scatter_rows",
    )(dest3, u2, jnp.zeros((total, d), F32))


def _experts_kernel(meta_ref, xs_ref, wg_ref, wu_ref, wd_ref, o_ref, wg_b, wu_b, wd_b):
    i = pl.program_id(0)
    n_used = meta_ref[pl.num_programs(0)]

    @pl.when((i == 0) | (meta_ref[i] != meta_ref[jnp.maximum(i - 1, 0)]))
    def _():
        wg_b[...] = wg_ref[0, 0].astype(BF16)
        wu_b[...] = wu_ref[0, 0].astype(BF16)
        wd_b[...] = wd_ref[0, 0].astype(BF16)

    @pl.when(i < n_used)
    def _():
        xb = xs_ref[...].astype(BF16)
        gate = jnp.dot(xb, wg_b[...], preferred_element_type=F32)
        up = jnp.dot(xb, wu_b[...], preferred_element_type=F32)
        hb = gate * _sigmoid(gate) * up
        o_ref[...] = jnp.dot(hb.astype(BF16), wd_b[...], preferred_element_type=F32)

    @pl.when(i >= n_used)
    def _():
        o_ref[...] = jnp.zeros_like(o_ref)


def _experts(meta, xs, wg, wu, wd, layer):
    total, d = xs.shape
    nb = total // EXPERT_BLOCK
    de = wg.shape[3]
    grid_spec = pltpu.PrefetchScalarGridSpec(
        num_scalar_prefetch=1,
        grid=(nb,),
        in_specs=[pl.BlockSpec((EXPERT_BLOCK, d), lambda i, m: (i, 0)),
                  pl.BlockSpec((1, 1, d, de), lambda i, m: (layer, m[i], 0, 0)),
                  pl.BlockSpec((1, 1, d, de), lambda i, m: (layer, m[i], 0, 0)),
                  pl.BlockSpec((1, 1, de, d), lambda i, m: (layer, m[i], 0, 0))],
        out_specs=pl.BlockSpec((EXPERT_BLOCK, d), lambda i, m: (i, 0)),
        scratch_shapes=[pltpu.VMEM((d, de), BF16), pltpu.VMEM((d, de), BF16), pltpu.VMEM((de, d), BF16)],
    )
    return pl.pallas_call(
        _experts_kernel,
        grid_spec=grid_spec,
        out_shape=jax.ShapeDtypeStruct((total, d), F32),
        compiler_params=_cparams(1, "arbitrary"),
        name="experts",
    )(meta, xs, wg, wu, wd)


def _final_kernel(dcur_ref, dnext_ref, x1_ref, rw_ref, mod_ref, g_ref, b_ref, ys_ref, o_ref, ybuf, sem, *, alpha):
    i = pl.program_id(0)
    tm = x1_ref.shape[0]
    slot = i % 2

    def gather(d_ref, s):
        def issue(r, carry):
            for k in range(TOP_K):
                src = d_ref[0, 0, TOP_K * r + k]
                pltpu.make_async_copy(ys_ref.at[pl.ds(src, 1)], ybuf.at[s, k, pl.ds(r, 1)], sem.at[s]).start()
            return carry

        lax.fori_loop(0, tm, issue, 0, unroll=8)

    @pl.when(i == 0)
    def _():
        gather(dcur_ref, 0)

    @pl.when(i + 1 < pl.num_programs(0))
    def _():
        gather(dnext_ref, 1 - slot)

    pltpu.make_async_copy(ybuf.at[slot], ybuf.at[slot], sem.at[slot]).wait()
    m = mod_ref[0]
    rw = rw_ref[...]
    f = rw[:, 0:1] * ybuf[slot, 0] + rw[:, 1:2] * ybuf[slot, 1]
    o_ref[...] = _ln(alpha * x1_ref[...] + m[5:6] * f) * g_ref[...] + b_ref[...]


def _final(x1, ysorted, dest3, rw, modl, gain, bias, seq, tm, alpha):
    t, d = x1.shape
    tpb = seq // tm
    n_tiles = t // tm
    tok = lambda i: (i, 0)
    kern = functools.partial(_final_kernel, alpha=alpha)
    dspec = lambda f: pl.BlockSpec((1, 1, TOP_K * tm), f, memory_space=pltpu.SMEM)
    return pl.pallas_call(
        kern,
        grid=(n_tiles,),
        in_specs=[dspec(lambda i: (i, 0, 0)), dspec(lambda i: (jnp.minimum(i + 1, n_tiles - 1), 0, 0)),
                  pl.BlockSpec((tm, d), tok), pl.BlockSpec((tm, LANES), tok),
                  pl.BlockSpec((1,) + modl.shape[1:], lambda i: (i // tpb, 0, 0)),
                  pl.BlockSpec(gain.shape, lambda i: (0, 0)), pl.BlockSpec(bias.shape, lambda i: (0, 0)),
                  pl.BlockSpec(memory_space=pl.ANY)],
        out_specs=pl.BlockSpec((tm, d), tok),
        out_shape=jax.ShapeDtypeStruct((t, d), F32),
        scratch_shapes=[pltpu.VMEM((2, TOP_K, tm, d), F32), pltpu.SemaphoreType.DMA((2,))],
        compiler_params=_cparams(1, "arbitrary"),
        name="final_ln",
    )(dest3, dest3, x1, rw, modl, gain, bias, ysorted)


def _block_diag(blocks):
    n, a, b = blocks.shape
    out = jnp.zeros((n * a, n * b), blocks.dtype)
    for i in range(n):
        out = out.at[i * a:(i + 1) * a, i * b:(i + 1) * b].set(blocks[i])
    return out


def _pad_rows(w, lo, total):
    return jnp.zeros((total, w.shape[-1]), w.dtype).at[lo:lo + w.shape[0]].set(w)


def kernel(x, c, w_mod, b_mod, w_in, na_rpb, rw_conv, rw_w0, rw_w_up, rw_a0, rw_a_up, rw_g_up, rw_k_k, rw_k_a, rw_r_k, rw_gn_gain, rw_gn_bias, pool_w, pool_scale, w_out, ln1_gain, ln1_bias, ln2_gain, ln2_bias, moe_w_group, moe_b_group, moe_w_expert, moe_b_expert, moe_w_gate, moe_w_up, moe_w_down):
    batch, seq, d = x.shape
    depth = w_mod.shape[0]
    t = batch * seq
    a_w = na_rpb.shape[1] * HEAD_DIM
    b_w = rw_w0.shape[-1]
    c_w = pool_scale.shape[-1]
    lr_w = R_W + R_A + R_G
    alpha = (2 * depth) ** 0.25
    tm = min(512, seq)
    tm_prep = min(256, seq)
    assert seq % tm == 0 and seq % SCAN_CHUNK == 0 and seq % GRID_W == 0 and lr_w == LANES

    mod = _modulation(c, w_mod, b_mod)
    ones_blk = _block_diag(jnp.ones((b_w // HEAD_DIM, HEAD_DIM, HEAD_DIM), F32))
    row = lambda v: v.reshape(1, -1)

    x2 = x.reshape(t, d)
    for l in range(depth):
        modl = mod[l]
        qkv, rkv_raw, lr, praw = _inproj(x2, modl, w_in[l].astype(BF16), seq, tm, 3 * a_w, 3 * b_w, lr_w, c_w)
        ya = _natten(qkv, _na_bias_table(na_rpb[l]), batch, seq, a_w)
        prep_params = {
            "conv": rw_conv[l], "w0": rw_w0[l], "a0": rw_a0[l],
            "w_up": jnp.stack([_pad_rows(rw_w_up[l, dd], 0, lr_w) for dd in range(2)]),
            "a_up": jnp.stack([_pad_rows(rw_a_up[l, dd], R_W, lr_w) for dd in range(2)]),
            "g_up": _pad_rows(rw_g_up[l], R_W + R_A, lr_w),
            "k_k": row(rw_k_k[l]), "k_a": row(rw_k_a[l]), "r_k": row(rw_r_k[l]), "ones": ones_blk,
        }
        r, v, nkk, lw, bb, kd, bonus, g = _rwkv_prep(rkv_raw, lr, prep_params, seq, tm_prep)
        yf, yb = _rwkv_scan(r, v, nkk, lw, bb, kd, batch, seq)
        yc = _pool(praw, _block_diag(pool_w[l]), row(pool_scale[l]), seq, tm)
        w_router = jnp.zeros((d, LANES), F32).at[:, :N_GROUPS].set(moe_w_group[l])
        w_router = w_router.at[:, N_GROUPS:N_GROUPS + N_EXPERTS].set(moe_w_expert[l])
        b_router = jnp.zeros((1, LANES), F32).at[0, :N_GROUPS].set(moe_b_group[l])
        b_router = b_router.at[0, N_GROUPS:N_GROUPS + N_EXPERTS].set(moe_b_expert[l])
        wo = w_out[l].astype(BF16)
        out_params = {
            "w_out_a": wo[:a_w], "w_out_b": wo[a_w:a_w + b_w], "w_out_c": wo[a_w + b_w:],
            "gn_gain": row(rw_gn_gain[l]), "gn_bias": row(rw_gn_bias[l]), "ones": ones_blk,
            "ln1_gain": row(ln1_gain[l]), "ln1_bias": row(ln1_bias[l]),
            "w_router": w_router, "b_router": b_router,
        }
        x1, u2, route_i, route_w, counts = _outproj(ya, yf, yb, bonus, g, yc, x2, modl, out_params, seq, tm, alpha)
        n_blocks = -(-(t * TOP_K) // EXPERT_BLOCK) + N_EXPERTS
        dest, meta = _dispatch(route_i, counts, n_blocks)
        dest3 = dest.reshape(t // tm, 1, TOP_K * tm)
        xs = _scatter_rows(u2, dest3, n_blocks * EXPERT_BLOCK, tm)
        ysorted = _experts(meta, xs, moe_w_gate, moe_w_up, moe_w_down, l)
        x2 = _final(x1, ysorted, dest3, route_w, modl, row(ln2_gain[l]), row(ln2_bias[l]), seq, tm, alpha)
    return x2.reshape(batch, seq, d)
```

```python
import functools
import math

import jax
import jax.numpy as jnp
import numpy as np
from jax import lax
from jax.experimental import pallas as pl
from jax.experimental.pallas import tpu as pltpu

F32 = jnp.float32
BF16 = jnp.bfloat16
HI = lax.Precision.HIGHEST

GRID_W = 64
HEAD_DIM = 64
NA_KH = 8
NA_KW = 16
POOL_WINDOWS = (2, 4, 8, 16)
R_W = 32
R_A = 32
R_G = 64
DECAY_SCALE = math.exp(-0.5)
GN_EPS = 64e-5
N_GROUPS = 4
EXPERTS_PER_GROUP = 8
N_EXPERTS = N_GROUPS * EXPERTS_PER_GROUP
TOP_K = 2
EXPERT_BLOCK = 512
LN_EPS = 1e-5
NEG_INF = -1e30

SCAN_CHUNK = 64
HALO = 8
LANES = 128
VMEM_LIMIT = 52 * 1024 * 1024


def _ln(x):
    mu = jnp.mean(x, axis=-1, keepdims=True)
    xc = x - mu
    var = jnp.mean(xc * xc, axis=-1, keepdims=True)
    return xc * lax.rsqrt(var + LN_EPS)


def _sigmoid(x):
    return 1.0 / (1.0 + jnp.exp(-x))


def _split_bf16(x):
    hi = x.astype(BF16)
    return hi, (x - hi.astype(F32)).astype(BF16)


def _dot_split(x, w_exact):
    hi, lo = _split_bf16(x)
    return jnp.dot(hi, w_exact, preferred_element_type=F32) + jnp.dot(lo, w_exact, preferred_element_type=F32)


def _cparams(n_axes, semantics="parallel"):
    return pltpu.CompilerParams(dimension_semantics=(semantics,) * n_axes, vmem_limit_bytes=VMEM_LIMIT)


def _mod_kernel(c_ref, w_ref, b_ref, o_ref):
    c = c_ref[...]
    s = c * _sigmoid(c)
    o_ref[0] = jnp.dot(s, w_ref[0], precision=HI, preferred_element_type=F32) + b_ref[0]


def _modulation(c, w_mod, b_mod):
    n_layers, d, d6 = w_mod.shape
    b = c.shape[0]
    bp = -(-b // 8) * 8
    cp = jnp.zeros((bp, d), F32).at[:b].set(c)
    out = pl.pallas_call(
        _mod_kernel,
        grid=(n_layers, d6 // d),
        in_specs=[pl.BlockSpec((bp, d), lambda l, j: (0, 0)),
                  pl.BlockSpec((1, d, d), lambda l, j: (l, 0, j)),
                  pl.BlockSpec((1, 1, d), lambda l, j: (l, 0, j))],
        out_specs=pl.BlockSpec((1, bp, d), lambda l, j: (l, 0, j)),
        out_shape=jax.ShapeDtypeStruct((n_layers, bp, d6), F32),
        compiler_params=_cparams(2),
        name="modulation",
    )(cp, w_mod, b_mod.reshape(n_layers, 1, d6))
    return out[:, :b].reshape(n_layers, b, d6 // d, d)


def _inproj_kernel(x_ref, mod_ref, w_ref, qkv_ref, rkv_ref, lr_ref, pool_ref, *, a3, b3, lr_w):
    m = mod_ref[0]
    u = _ln(x_ref[...]) * (1.0 + m[1:2]) + m[0:1]
    h = jnp.dot(u.astype(BF16), w_ref[...], preferred_element_type=F32)
    qkv_ref[...] = h[:, :a3].astype(BF16)
    rkv_ref[...] = h[:, a3:a3 + b3]
    lr_ref[...] = h[:, a3 + b3:a3 + b3 + lr_w]
    pool_ref[...] = h[:, a3 + b3 + lr_w:]


def _inproj(x2, modl, w_in_bf, seq, tm, a3, b3, lr_w, c_w):
    t, d = x2.shape
    tpb = seq // tm
    kern = functools.partial(_inproj_kernel, a3=a3, b3=b3, lr_w=lr_w)
    return pl.pallas_call(
        kern,
        grid=(t // tm,),
        in_specs=[pl.BlockSpec((tm, d), lambda i: (i, 0)),
                  pl.BlockSpec((1,) + modl.shape[1:], lambda i: (i // tpb, 0, 0)),
                  pl.BlockSpec(w_in_bf.shape, lambda i: (0, 0))],
        out_specs=[pl.BlockSpec((tm, a3), lambda i: (i, 0)),
                   pl.BlockSpec((tm, b3), lambda i: (i, 0)),
                   pl.BlockSpec((tm, lr_w), lambda i: (i, 0)),
                   pl.BlockSpec((tm, c_w), lambda i: (i, 0))],
        out_shape=[jax.ShapeDtypeStruct((t, a3), BF16),
                   jax.ShapeDtypeStruct((t, b3), F32),
                   jax.ShapeDtypeStruct((t, lr_w), F32),
                   jax.ShapeDtypeStruct((t, c_w), F32)],
        compiler_params=_cparams(1),
        name="inproj",
    )(x2, modl, w_in_bf)


def _na_bias_table(rpb):
    col = np.arange(GRID_W)
    cstart = np.clip(col - NA_KW // 2, 0, GRID_W - NA_KW)
    in_win = (col[None, :] >= cstart[:, None]) & (col[None, :] < cstart[:, None] + NA_KW)
    dc = np.clip(col[None, :] - col[:, None], -(NA_KW - 1), NA_KW - 1) + (NA_KW - 1)
    pick = (dc[None] == np.arange(2 * NA_KW - 1)[:, None, None]).astype(np.float32)
    cols = jnp.einsum("hrc,cqk->hrqk", rpb.astype(F32), pick, precision=HI)
    cols = jnp.where(in_win, cols, NEG_INF)
    b = jnp.stack([cols[:, NA_KH - 1 - o:2 * NA_KH - 1 - o] for o in range(NA_KH)])
    h = rpb.shape[0]
    return jnp.transpose(b, (0, 1, 3, 2, 4)).reshape(NA_KH, h * GRID_W, NA_KH * GRID_W)


def _natten_kernel(q_ref, k_ref, v_ref, bias_ref, o_ref, *, rows, heads):
    r = pl.program_id(1)
    rstart = jnp.clip(r - NA_KH // 2, 0, rows - NA_KH)
    off = r - rstart
    start = pl.multiple_of(rstart * GRID_W, GRID_W)
    nk = NA_KH * GRID_W
    kw = k_ref[pl.ds(start, nk), :]
    vw = v_ref[pl.ds(start, nk), :]
    q = q_ref[...]
    width = q.shape[1]
    head_of_lane = lax.broadcasted_iota(jnp.int32, (heads * GRID_W, width), 1) // HEAD_DIM
    head_of_row = lax.broadcasted_iota(jnp.int32, (heads * GRID_W, width), 0) // GRID_W
    own = head_of_lane == head_of_row
    qs = jnp.where(own, jnp.concatenate([q] * heads, axis=0), jnp.zeros((), q.dtype))
    s = lax.dot_general(qs, kw, (((1,), (1,)), ((), ())), preferred_element_type=F32) * (HEAD_DIM ** -0.5)
    s = s + bias_ref[off]
    mx = jnp.max(s, axis=-1, keepdims=True)
    p = jnp.exp(s - mx)
    den = jnp.sum(p, axis=-1, keepdims=True)
    o = jnp.where(own, jnp.dot(p.astype(BF16), vw, preferred_element_type=F32) / den, 0.0)
    acc = o[0:GRID_W]
    for h in range(1, heads):
        acc = acc + o[h * GRID_W:(h + 1) * GRID_W]
    o_ref[...] = acc


def _natten(qkv, bias_tab, batch, seq, width):
    rows = seq // GRID_W
    assert rows >= NA_KH
    heads = width // HEAD_DIM
    kern = functools.partial(_natten_kernel, rows=rows, heads=heads)
    return pl.pallas_call(
        kern,
        grid=(batch, rows),
        in_specs=[pl.BlockSpec((GRID_W, width), lambda b, r: (b * rows + r, 0)),
                  pl.BlockSpec((seq, width), lambda b, r: (b, 1)),
                  pl.BlockSpec((seq, width), lambda b, r: (b, 2)),
                  pl.BlockSpec(bias_tab.shape, lambda b, r: (0, 0, 0))],
        out_specs=pl.BlockSpec((GRID_W, width), lambda b, r: (b * rows + r, 0)),
        out_shape=jax.ShapeDtypeStruct((batch * seq, width), F32),
        compiler_params=_cparams(2),
        name="natten",
    )(qkv, qkv, qkv, bias_tab)


def _rwkv_prep_kernel(z_ref, zp_ref, zn_ref, lr_ref, cw_ref, w0_ref, wup_ref, a0_ref, aup_ref, gup_ref,
                      kk_ref, ka_ref, rk_ref, ones_ref,
                      r_o, v_o, nkk_o, lw_o, b_o, kd_o, bonus_o, g_o, *, tiles_per_batch, width):
    i = pl.program_id(0)
    tb = i % tiles_per_batch
    z = z_ref[...]
    tm = z.shape[0]
    prev = jnp.where(tb == 0, 0.0, zp_ref[HALO - 1:HALO, :])
    nxt = jnp.where(tb == tiles_per_batch - 1, 0.0, zn_ref[0:1, :])
    row = lax.broadcasted_iota(jnp.int32, z.shape, 0)
    zm1 = jnp.where(row == 0, prev, pltpu.roll(z, 1, 0))
    zp1 = jnp.where(row == tm - 1, nxt, pltpu.roll(z, tm - 1, 0))
    rkv = zm1 * cw_ref[0:1, :] + z * cw_ref[1:2, :] + zp1 * cw_ref[2:3, :]
    r = rkv[:, :width]
    k = rkv[:, width:2 * width]
    v = rkv[:, 2 * width:]
    lr = lr_ref[...]
    th = jnp.tanh(lr)
    sg = _sigmoid(lr)
    ones = ones_ref[...]

    def headsum(x):
        return _dot_split(x, ones)

    kk = k * kk_ref[...]
    kk = kk * lax.rsqrt(jnp.maximum(headsum(kk * kk), 1e-24))
    g_o[...] = jnp.dot(sg.astype(BF16), gup_ref[...], preferred_element_type=F32)
    r_o[...] = r
    v_o[...] = v
    nkk_o[...] = -kk
    bonus = jnp.zeros_like(r)
    th_b = th.astype(BF16)
    lr_b = lr.astype(BF16)
    for d in range(2):
        wl = jnp.dot(th_b, wup_ref[d], preferred_element_type=F32) + w0_ref[d:d + 1, :]
        lw_o[d] = -DECAY_SCALE * _sigmoid(wl)
        a = _sigmoid(jnp.dot(lr_b, aup_ref[d], preferred_element_type=F32) + a0_ref[d:d + 1, :])
        kd = k * (1.0 + (a - 1.0) * ka_ref[...])
        kd_o[d] = kd
        b_o[d] = kk * a
        bonus = bonus + headsum(r * kd * rk_ref[...]) * v
    bonus_o[...] = bonus


def _rwkv_prep(rkv_raw, lr, p, seq, tm):
    t, w3 = rkv_raw.shape
    width = w3 // 3
    tpb = seq // tm
    hb = tm // HALO
    nhb = t // HALO
    kern = functools.partial(_rwkv_prep_kernel, tiles_per_batch=tpb, width=width)
    tok = lambda i: (i, 0)
    dtok = lambda i: (0, i, 0)
    full2 = lambda i: (0, 0)
    full3 = lambda i: (0, 0, 0)
    tw = jax.ShapeDtypeStruct((t, width), F32)
    dtw = jax.ShapeDtypeStruct((2, t, width), F32)
    return pl.pallas_call(
        kern,
        grid=(t // tm,),
        in_specs=[pl.BlockSpec((tm, w3), tok),
                  pl.BlockSpec((HALO, w3), lambda i: (jnp.maximum(i * hb - 1, 0), 0)),
                  pl.BlockSpec((HALO, w3), lambda i: (jnp.minimum((i + 1) * hb, nhb - 1), 0)),
                  pl.BlockSpec((tm, lr.shape[1]), tok),
                  pl.BlockSpec(p["conv"].shape, full2),
                  pl.BlockSpec(p["w0"].shape, full2),
                  pl.BlockSpec(p["w_up"].shape, full3),
                  pl.BlockSpec(p["a0"].shape, full2),
                  pl.BlockSpec(p["a_up"].shape, full3),
                  pl.BlockSpec(p["g_up"].shape, full2),
                  pl.BlockSpec(p["k_k"].shape, full2),
                  pl.BlockSpec(p["k_a"].shape, full2),
                  pl.BlockSpec(p["r_k"].shape, full2),
                  pl.BlockSpec(p["ones"].shape, full2)],
        out_specs=[pl.BlockSpec((tm, width), tok), pl.BlockSpec((tm, width), tok), pl.BlockSpec((tm, width), tok),
                   pl.BlockSpec((2, tm, width), dtok), pl.BlockSpec((2, tm, width), dtok),
                   pl.BlockSpec((2, tm, width), dtok),
                   pl.BlockSpec((tm, width), tok), pl.BlockSpec((tm, width), tok)],
        out_shape=[tw, tw, tw, dtw, dtw, dtw, tw, tw],
        compiler_params=_cparams(1),
        name="rwkv_prep",
    )(rkv_raw, rkv_raw, rkv_raw, lr, p["conv"], p["w0"], p["w_up"], p["a0"], p["a_up"], p["g_up"],
      p["k_k"], p["k_a"], p["r_k"], p["ones"])


def _dot_nt(a, b):
    return lax.dot_general(a, b, (((1,), (1,)), ((), ())), preferred_element_type=F32)


def _dot_tn(a, b):
    return lax.dot_general(a, b, (((0,), (0,)), ((), ())), preferred_element_type=F32)


def _mm(a, b):
    return jnp.dot(a.astype(BF16), b.astype(BF16), preferred_element_type=F32)


def _rwkv_scan_kernel(rf_ref, vf_ref, nf_ref, rb_ref, vb_ref, nb_ref, lwf_ref, bf_ref, kf_ref, lwb_ref, bb_ref, kb_ref,
                      yf_ref, yb_ref, s_ref, *, heads, batch):
    @pl.when(pl.program_id(0) == 0)
    def _():
        s_ref[...] = jnp.zeros_like(s_ref)

    n = SCAN_CHUNK
    row = lax.broadcasted_iota(jnp.int32, (n, n), 0)
    col = lax.broadcasted_iota(jnp.int32, (n, n), 1)
    levels = n.bit_length()
    same = [(row >> k) == (col >> k) for k in range(levels)]
    eye = same[0].astype(F32)
    level_masks = [same[sh + 1] & jnp.logical_not(same[sh]) for sh in range(1, levels - 1)]

    dirs = ((rf_ref, vf_ref, nf_ref, lwf_ref, bf_ref, kf_ref, yf_ref),
            (rb_ref, vb_ref, nb_ref, lwb_ref, bb_ref, kb_ref, yb_ref))
    chains = []
    for d, (r_ref, v_ref, n_ref, lw_ref, b_ref, k_ref, y_ref) in enumerate(dirs):
        order = row - col if d == 0 else col - row
        strict = order > 0
        incl = order >= 0
        incl_f = incl.astype(F32)
        for bi in range(batch):
            lw = lw_ref[0, bi]
            g_inc = jnp.dot(incl_f, lw, precision=HI, preferred_element_type=F32)
            g_tot = jnp.sum(lw, axis=0, keepdims=True)
            e_neg = jnp.exp(-g_inc)
            e_end = jnp.exp(g_tot - g_inc)
            decay = jnp.exp(g_tot)
            a_t = n_ref[bi] * jnp.exp(g_inc - lw)
            r_t = r_ref[bi] * jnp.exp(g_inc)
            bb = b_ref[0, bi]
            kd = k_ref[0, bi]
            b_t = (bb * e_neg).astype(BF16)
            k_t = (kd * e_neg).astype(BF16)
            ar_t = jnp.concatenate([a_t, r_t], axis=0).astype(BF16)
            bk_h = jnp.concatenate([bb * e_end, kd * e_end], axis=0).astype(BF16)
            v = v_ref[bi]
            for h in range(heads):
                sl = slice(h * HEAD_DIM, (h + 1) * HEAD_DIM)
                chains.append(dict(strict=strict, incl=incl, sl=sl, bi=bi, y_ref=y_ref,
                                   si=(d * batch + bi) * heads + h, decay=decay[:, sl],
                                   ar=ar_t[:, sl], b=b_t[:, sl], k=k_t[:, sl], bk_h=bk_h[:, sl], v=v[:, sl]))

    for ch in chains:
        pb = _dot_nt(ch["ar"], ch["b"])
        pk = _dot_nt(ch["ar"], ch["k"])
        ch["l_ab"] = jnp.where(ch["strict"], pb[:n], 0.0)
        ch["m_rb"] = jnp.where(ch["incl"], pb[n:], 0.0)
        ch["l_ak"] = jnp.where(ch["strict"], pk[:n], 0.0)
        ch["m_rk"] = jnp.where(ch["incl"], pk[n:], 0.0)
        ch["t"] = eye + jnp.where(same[1], ch["l_ab"], 0.0)
    for mask in level_masks:
        for ch in chains:
            ch["tc"] = _mm(ch["t"], jnp.where(mask, ch["l_ab"], 0.0))
        for ch in chains:
            ch["t"] = ch["t"] + _mm(ch["tc"], ch["t"])
    for ch in chains:
        ch["s0"] = s_ref[ch["si"]]
        ch["x"] = _dot_nt(ch["ar"], ch["s0"].astype(BF16))
    for ch in chains:
        ch["rhs"] = ch["x"][:n] + _mm(ch["l_ak"], ch["v"])
    for ch in chains:
        ch["u"] = _mm(ch["t"], ch["rhs"])
    for ch in chains:
        y = ch["x"][n:] + _mm(ch["m_rb"], ch["u"]) + _mm(ch["m_rk"], ch["v"])
        ch["y_ref"][ch["bi"], :, ch["sl"]] = y
    for ch in chains:
        uv = jnp.concatenate([ch["u"], ch["v"]], axis=0)
        s_ref[ch["si"]] = ch["s0"] * ch["decay"] + _dot_tn(uv.astype(BF16), ch["bk_h"])


def _rwkv_scan(r, v, nkk, lw, b, kd, batch, seq):
    t, width = r.shape
    heads = width // HEAD_DIM
    n = SCAN_CHUNK
    nc = seq // n
    r3, v3, n3 = (z.reshape(batch, seq, width) for z in (r, v, nkk))
    lw4, b4, k4 = (z.reshape(2, batch, seq, width) for z in (lw, b, kd))
    fwd = pl.BlockSpec((batch, n, width), lambda c: (0, c, 0))
    bwd = pl.BlockSpec((batch, n, width), lambda c: (0, nc - 1 - c, 0))
    fwd_d = pl.BlockSpec((1, batch, n, width), lambda c: (0, 0, c, 0))
    bwd_d = pl.BlockSpec((1, batch, n, width), lambda c: (1, 0, nc - 1 - c, 0))
    kern = functools.partial(_rwkv_scan_kernel, heads=heads, batch=batch)
    yf, yb = pl.pallas_call(
        kern,
        grid=(nc,),
        in_specs=[fwd, fwd, fwd, bwd, bwd, bwd, fwd_d, fwd_d, fwd_d, bwd_d, bwd_d, bwd_d],
        out_specs=[fwd, bwd],
        out_shape=[jax.ShapeDtypeStruct((batch, seq, width), F32)] * 2,
        scratch_shapes=[pltpu.VMEM((2 * batch * heads, HEAD_DIM, HEAD_DIM), F32)],
        compiler_params=_cparams(1, "arbitrary"),
        name="rwkv_scan",
    )(r3, v3, n3, r3, v3, n3, lw4, b4, k4, lw4, b4, k4)
    return yf.reshape(t, width), yb.reshape(t, width)


def _pool_kernel(p_ref, pp_ref, pn_ref, w_ref, sc_ref, o_ref, ext_ref, *, tiles_per_batch, seq):
    i = pl.program_id(0)
    tb = i % tiles_per_batch
    p = p_ref[...]
    tm, width = p.shape
    ext_ref[0:HALO, :] = jnp.where(tb == 0, 0.0, pp_ref[...])
    ext_ref[HALO:HALO + tm, :] = p
    ext_ref[HALO + tm:2 * HALO + tm, :] = jnp.where(tb == tiles_per_batch - 1, 0.0, pn_ref[...])

    def shifted(o):
        return ext_ref[HALO + o:HALO + o + tm, :]

    t = tb * tm + lax.broadcasted_iota(jnp.int32, (tm, width), 0)
    grp = lax.broadcasted_iota(jnp.int32, (tm, width), 1) // (width // len(POOL_WINDOWS))
    tot = p
    prev_half = 0
    pooled = jnp.zeros_like(p)
    for gi, win in enumerate(POOL_WINDOWS):
        half = win // 2
        for o in range(prev_half, half):
            tot = tot + shifted(-o - 1)
            if o > 0:
                tot = tot + shifted(o)
        prev_half = half
        lo = jnp.clip(t - half, 0, seq - 1)
        hi = jnp.clip(t + half - 1, 0, seq - 1)
        cnt = (hi - lo + 1).astype(F32)
        pooled = jnp.where(grp == gi, tot / cnt, pooled)
    pooled = pooled - p
    o_ref[...] = jnp.dot(pooled, w_ref[...], preferred_element_type=F32) * sc_ref[...]


def _pool(praw, w_blk, scale, seq, tm):
    t, width = praw.shape
    tpb = seq // tm
    hb = tm // HALO
    nhb = t // HALO
    kern = functools.partial(_pool_kernel, tiles_per_batch=tpb, seq=seq)
    return pl.pallas_call(
        kern,
        grid=(t // tm,),
        in_specs=[pl.BlockSpec((tm, width), lambda i: (i, 0)),
                  pl.BlockSpec((HALO, width), lambda i: (jnp.maximum(i * hb - 1, 0), 0)),
                  pl.BlockSpec((HALO, width), lambda i: (jnp.minimum((i + 1) * hb, nhb - 1), 0)),
                  pl.BlockSpec(w_blk.shape, lambda i: (0, 0)),
                  pl.BlockSpec(scale.shape, lambda i: (0, 0))],
        out_specs=pl.BlockSpec((tm, width), lambda i: (i, 0)),
        out_shape=jax.ShapeDtypeStruct((t, width), F32),
        scratch_shapes=[pltpu.VMEM((tm + 2 * HALO, width), F32)],
        compiler_params=_cparams(1),
        name="pool",
    )(praw, praw, praw, w_blk, scale)


def _outproj_kernel(ya_ref, yf_ref, yb_ref, bonus_ref, g_ref, yc_ref, x_ref, mod_ref, wa_ref, wb_ref, wc_ref,
                    gng_ref, gnb_ref, ones_ref, l1g_ref, l1b_ref, wr_ref, br_ref,
                    x1_o, u2_o, ri_o, rw_o, cnt_o, cnt_ref, *, alpha):
    m = mod_ref[0]
    ones = ones_ref[...]

    def headmean(x):
        return _dot_split(x, ones) * (1.0 / HEAD_DIM)

    ysum = yf_ref[...] + yb_ref[...]
    yc0 = ysum - headmean(ysum)
    yn = yc0 * lax.rsqrt(headmean(yc0 * yc0) + GN_EPS) * gng_ref[...] + gnb_ref[...]
    yb = (yn + bonus_ref[...]) * g_ref[...]
    mix = (jnp.dot(ya_ref[...].astype(BF16), wa_ref[...], preferred_element_type=F32)
           + jnp.dot(yb.astype(BF16), wb_ref[...], preferred_element_type=F32)
           + jnp.dot(yc_ref[...].astype(BF16), wc_ref[...], preferred_element_type=F32))
    x1 = _ln(alpha * x_ref[...] + m[2:3] * mix) * l1g_ref[...] + l1b_ref[...]
    x1_o[...] = x1
    u2 = _ln(x1) * (1.0 + m[4:5]) + m[3:4]
    u2_o[...] = u2

    u_hi, u_lo = _split_bf16(u2)
    lg = (jnp.dot(u_hi, wr_ref[0], preferred_element_type=F32) + jnp.dot(u_hi, wr_ref[1], preferred_element_type=F32)
          + jnp.dot(u_lo, wr_ref[0], preferred_element_type=F32)) + br_ref[...]
    lane = lax.broadcasted_iota(jnp.int32, lg.shape, 1)
    big = jnp.int32(1 << 20)
    gl = jnp.where(lane < N_GROUPS, lg, -jnp.inf)
    gmax = jnp.max(gl, axis=-1, keepdims=True)
    gidx = jnp.min(jnp.where(gl == gmax, lane, big), axis=-1, keepdims=True)
    pg_sel = 1.0 / jnp.sum(jnp.exp(gl - gmax), axis=-1, keepdims=True)
    e_lo = N_GROUPS + gidx * EXPERTS_PER_GROUP
    el = jnp.where((lane >= e_lo) & (lane < e_lo + EXPERTS_PER_GROUP), lg, -jnp.inf)
    m1 = jnp.max(el, axis=-1, keepdims=True)
    i1 = jnp.min(jnp.where(el == m1, lane, big), axis=-1, keepdims=True)
    el2 = jnp.where(lane == i1, -jnp.inf, el)
    m2 = jnp.max(el2, axis=-1, keepdims=True)
    i2 = jnp.min(jnp.where(el2 == m2, lane, big), axis=-1, keepdims=True)
    e21 = jnp.exp(m2 - m1)
    p1 = 1.0 / (1.0 + e21)
    p2 = e21 / (1.0 + e21)
    rw_o[...] = jnp.where(lane == 0, pg_sel * p1, jnp.where(lane == 1, pg_sel * p2, 0.0))

    @pl.when(pl.program_id(0) == 0)
    def _():
        cnt_ref[...] = jnp.zeros_like(cnt_ref)

    tm = lg.shape[0]
    earlier = (lax.broadcasted_iota(jnp.int32, (tm, tm), 1)
               < lax.broadcasted_iota(jnp.int32, (tm, tm), 0)).astype(BF16)
    oh1 = (lane == i1).astype(F32)
    oh2 = (lane == i2).astype(F32)
    run = cnt_ref[...]
    c1 = jnp.sum(oh1, axis=0, keepdims=True)
    before1 = run + jnp.dot(earlier, oh1.astype(BF16), preferred_element_type=F32)
    before2 = run + c1 + jnp.dot(earlier, oh2.astype(BF16), preferred_element_type=F32)
    rank1 = jnp.sum(oh1 * before1, axis=-1, keepdims=True).astype(jnp.int32)
    rank2 = jnp.sum(oh2 * before2, axis=-1, keepdims=True).astype(jnp.int32)
    total = run + c1 + jnp.sum(oh2, axis=0, keepdims=True)
    cnt_ref[...] = total
    cnt_o[...] = total
    ri_o[...] = jnp.where(lane == 0, i1 - N_GROUPS, jnp.where(lane == 1, i2 - N_GROUPS,
                          jnp.where(lane == 2, rank1, jnp.where(lane == 3, rank2, 0))))


def _outproj(ya, yf, yb, bonus, g, yc, x2, modl, p, seq, tm, alpha):
    t, d = x2.shape
    tpb = seq // tm
    aw, bw, cw = ya.shape[1], bonus.shape[1], yc.shape[1]
    tok = lambda i: (i, 0)
    full2 = lambda i: (0, 0)
    kern = functools.partial(_outproj_kernel, alpha=alpha)
    small = ["gn_gain", "gn_bias", "ones", "ln1_gain", "ln1_bias", "w_router", "b_router"]
    return pl.pallas_call(
        kern,
        grid=(t // tm,),
        in_specs=[pl.BlockSpec((tm, aw), tok),
                  pl.BlockSpec((tm, bw), tok), pl.BlockSpec((tm, bw), tok),
                  pl.BlockSpec((tm, bw), tok), pl.BlockSpec((tm, bw), tok),
                  pl.BlockSpec((tm, cw), tok),
                  pl.BlockSpec((tm, d), tok),
                  pl.BlockSpec((1,) + modl.shape[1:], lambda i: (i // tpb, 0, 0)),
                  pl.BlockSpec(p["w_out_a"].shape, full2),
                  pl.BlockSpec(p["w_out_b"].shape, full2),
                  pl.BlockSpec(p["w_out_c"].shape, full2)]
                 + [pl.BlockSpec(p[k].shape, functools.partial(lambda nd, i: (0,) * nd, p[k].ndim)) for k in small],
        out_specs=[pl.BlockSpec((tm, d), tok), pl.BlockSpec((tm, d), tok),
                   pl.BlockSpec((tm, LANES), tok), pl.BlockSpec((tm, LANES), tok),
                   pl.BlockSpec((1, LANES), full2)],
        out_shape=[jax.ShapeDtypeStruct((t, d), F32), jax.ShapeDtypeStruct((t, d), F32),
                   jax.ShapeDtypeStruct((t, LANES), jnp.int32), jax.ShapeDtypeStruct((t, LANES), F32),
                   jax.ShapeDtypeStruct((1, LANES), F32)],
        scratch_shapes=[pltpu.VMEM((1, LANES), F32)],
        compiler_params=_cparams(1, "arbitrary"),
        name="outproj",
    )(ya, yf, yb, bonus, g, yc, x2, modl, p["w_out_a"], p["w_out_b"], p["w_out_c"], *[p[k] for k in small])


def _dispatch(route_i, counts_lanes, n_blocks):
    counts = counts_lanes[0, N_GROUPS:N_GROUPS + N_EXPERTS].astype(jnp.int32)
    padded = ((counts + EXPERT_BLOCK - 1) // EXPERT_BLOCK) * EXPERT_BLOCK
    pends = jnp.cumsum(padded)
    pstarts = pends - padded
    e = route_i[:, :TOP_K]
    rank = route_i[:, TOP_K:2 * TOP_K]
    ids = jnp.arange(N_EXPERTS, dtype=jnp.int32)
    dest = jnp.sum(jnp.where(e[..., None] == ids, pstarts, 0), axis=-1) + rank
    block_start = jnp.arange(n_blocks, dtype=jnp.int32) * EXPERT_BLOCK
    block_e = jnp.minimum(jnp.sum((pends[None, :] <= block_start[:, None]).astype(jnp.int32), axis=1), N_EXPERTS - 1)
    meta = jnp.concatenate([block_e, (pends[-1] // EXPERT_BLOCK)[None]]).astype(jnp.int32)
    return dest, meta


def _scatter_rows_kernel(dest_ref, u_ref, xs_in_ref, xs_ref, sem):
    del xs_in_ref
    tm = u_ref.shape[0]

    def issue(r, carry):
        for k in range(TOP_K):
            dst = dest_ref[0, 0, TOP_K * r + k]
            pltpu.make_async_copy(u_ref.at[pl.ds(r, 1)], xs_ref.at[pl.ds(dst, 1)], sem).start()
        return carry

    lax.fori_loop(0, tm, issue, 0, unroll=8)
    rows = pl.ds(0, TOP_K * tm)
    pltpu.make_async_copy(xs_ref.at[rows], xs_ref.at[rows], sem).wait()


def _scatter_rows(u2, dest3, total, tm):
    t, d = u2.shape
    return pl.pallas_call(
        _scatter_rows_kernel,
        grid=(t // tm,),
        in_specs=[pl.BlockSpec((1, 1, TOP_K * tm), lambda i: (i, 0, 0), memory_space=pltpu.SMEM),
                  pl.BlockSpec((tm, d), lambda i: (i, 0)),
                  pl.BlockSpec(memory_space=pl.ANY)],
        out_specs=pl.BlockSpec(memory_space=pl.ANY),
        out_shape=jax.ShapeDtypeStruct((total, d), F32),
        scratch_shapes=[pltpu.SemaphoreType.DMA(())],
        input_output_aliases={2: 0},
        compiler_params=_cparams(1, "arbitrary"),
        name="scatter_rows",
    )(dest3, u2, jnp.zeros((total, d), F32))


def _experts_kernel(meta_ref, xs_ref, wg_ref, wu_ref, wd_ref, o_ref, wg_b, wu_b, wd_b):
    i = pl.program_id(0)
    n_used = meta_ref[pl.num_programs(0)]

    @pl.when((i == 0) | (meta_ref[i] != meta_ref[jnp.maximum(i - 1, 0)]))
    def _():
        wg_b[...] = wg_ref[0, 0].astype(BF16)
        wu_b[...] = wu_ref[0, 0].astype(BF16)
        wd_b[...] = wd_ref[0, 0].astype(BF16)

    @pl.when(i < n_used)
    def _():
        xb = xs_ref[...].astype(BF16)
        gate = jnp.dot(xb, wg_b[...], preferred_element_type=F32)
        up = jnp.dot(xb, wu_b[...], preferred_element_type=F32)
        hb = gate * _sigmoid(gate) * up
        o_ref[...] = jnp.dot(hb.astype(BF16), wd_b[...], preferred_element_type=F32)

    @pl.when(i >= n_used)
    def _():
        o_ref[...] = jnp.zeros_like(o_ref)


def _experts(meta, xs, wg, wu, wd, layer):
    total, d = xs.shape
    nb = total // EXPERT_BLOCK
    de = wg.shape[3]
    grid_spec = pltpu.PrefetchScalarGridSpec(
        num_scalar_prefetch=1,
        grid=(nb,),
        in_specs=[pl.BlockSpec((EXPERT_BLOCK, d), lambda i, m: (i, 0)),
                  pl.BlockSpec((1, 1, d, de), lambda i, m: (layer, m[i], 0, 0)),
                  pl.BlockSpec((1, 1, d, de), lambda i, m: (layer, m[i], 0, 0)),
                  pl.BlockSpec((1, 1, de, d), lambda i, m: (layer, m[i], 0, 0))],
        out_specs=pl.BlockSpec((EXPERT_BLOCK, d), lambda i, m: (i, 0)),
        scratch_shapes=[pltpu.VMEM((d, de), BF16), pltpu.VMEM((d, de), BF16), pltpu.VMEM((de, d), BF16)],
    )
    return pl.pallas_call(
        _experts_kernel,
        grid_spec=grid_spec,
        out_shape=jax.ShapeDtypeStruct((total, d), F32),
        compiler_params=_cparams(1, "arbitrary"),
        name="experts",
    )(meta, xs, wg, wu, wd)


def _final_kernel(dcur_ref, dnext_ref, x1_ref, rw_ref, mod_ref, g_ref, b_ref, ys_ref, o_ref, ybuf, sem, *, alpha):
    i = pl.program_id(0)
    tm = x1_ref.shape[0]
    slot = i % 2

    def gather(d_ref, s):
        def issue(r, carry):
            for k in range(TOP_K):
                src = d_ref[0, 0, TOP_K * r + k]
                pltpu.make_async_copy(ys_ref.at[pl.ds(src, 1)], ybuf.at[s, k, pl.ds(r, 1)], sem.at[s]).start()
            return carry

        lax.fori_loop(0, tm, issue, 0, unroll=8)

    @pl.when(i == 0)
    def _():
        gather(dcur_ref, 0)

    @pl.when(i + 1 < pl.num_programs(0))
    def _():
        gather(dnext_ref, 1 - slot)

    pltpu.make_async_copy(ybuf.at[slot], ybuf.at[slot], sem.at[slot]).wait()
    m = mod_ref[0]
    rw = rw_ref[...]
    f = rw[:, 0:1] * ybuf[slot, 0] + rw[:, 1:2] * ybuf[slot, 1]
    o_ref[...] = _ln(alpha * x1_ref[...] + m[5:6] * f) * g_ref[...] + b_ref[...]


def _final(x1, ysorted, dest3, rw, modl, gain, bias, seq, tm, alpha):
    t, d = x1.shape
    tpb = seq // tm
    n_tiles = t // tm
    tok = lambda i: (i, 0)
    kern = functools.partial(_final_kernel, alpha=alpha)
    dspec = lambda f: pl.BlockSpec((1, 1, TOP_K * tm), f, memory_space=pltpu.SMEM)
    return pl.pallas_call(
        kern,
        grid=(n_tiles,),
        in_specs=[dspec(lambda i: (i, 0, 0)), dspec(lambda i: (jnp.minimum(i + 1, n_tiles - 1), 0, 0)),
                  pl.BlockSpec((tm, d), tok), pl.BlockSpec((tm, LANES), tok),
                  pl.BlockSpec((1,) + modl.shape[1:], lambda i: (i // tpb, 0, 0)),
                  pl.BlockSpec(gain.shape, lambda i: (0, 0)), pl.BlockSpec(bias.shape, lambda i: (0, 0)),
                  pl.BlockSpec(memory_space=pl.ANY)],
        out_specs=pl.BlockSpec((tm, d), tok),
        out_shape=jax.ShapeDtypeStruct((t, d), F32),
        scratch_shapes=[pltpu.VMEM((2, TOP_K, tm, d), F32), pltpu.SemaphoreType.DMA((2,))],
        compiler_params=_cparams(1, "arbitrary"),
        name="final_ln",
    )(dest3, dest3, x1, rw, modl, gain, bias, ysorted)


def _block_diag(blocks):
    n, a, b = blocks.shape
    out = jnp.zeros((n * a, n * b), blocks.dtype)
    for i in range(n):
        out = out.at[i * a:(i + 1) * a, i * b:(i + 1) * b].set(blocks[i])
    return out


def _pad_rows(w, lo, total):
    return jnp.zeros((total, w.shape[-1]), w.dtype).at[lo:lo + w.shape[0]].set(w)


def kernel(x, c, w_mod, b_mod, w_in, na_rpb, rw_conv, rw_w0, rw_w_up, rw_a0, rw_a_up, rw_g_up, rw_k_k, rw_k_a, rw_r_k, rw_gn_gain, rw_gn_bias, pool_w, pool_scale, w_out, ln1_gain, ln1_bias, ln2_gain, ln2_bias, moe_w_group, moe_b_group, moe_w_expert, moe_b_expert, moe_w_gate, moe_w_up, moe_w_down):
    batch, seq, d = x.shape
    depth = w_mod.shape[0]
    t = batch * seq
    a_w = na_rpb.shape[1] * HEAD_DIM
    b_w = rw_w0.shape[-1]
    c_w = pool_scale.shape[-1]
    lr_w = R_W + R_A + R_G
    alpha = (2 * depth) ** 0.25
    tm = min(512, seq)
    tm_prep = min(256, seq)
    assert seq % tm == 0 and seq % SCAN_CHUNK == 0 and seq % GRID_W == 0 and lr_w == LANES

    mod = _modulation(c, w_mod, b_mod)
    ones_blk = _block_diag(jnp.ones((b_w // HEAD_DIM, HEAD_DIM, HEAD_DIM), BF16))
    row = lambda v: v.reshape(1, -1)

    x2 = x.reshape(t, d)
    for l in range(depth):
        modl = mod[l]
        qkv, rkv_raw, lr, praw = _inproj(x2, modl, w_in[l].astype(BF16), seq, tm, 3 * a_w, 3 * b_w, lr_w, c_w)
        ya = _natten(qkv, _na_bias_table(na_rpb[l]), batch, seq, a_w)
        prep_params = {
            "conv": rw_conv[l], "w0": rw_w0[l], "a0": rw_a0[l],
            "w_up": jnp.stack([_pad_rows(rw_w_up[l, dd], 0, lr_w) for dd in range(2)]).astype(BF16),
            "a_up": jnp.stack([_pad_rows(rw_a_up[l, dd], R_W, lr_w) for dd in range(2)]).astype(BF16),
            "g_up": _pad_rows(rw_g_up[l], R_W + R_A, lr_w).astype(BF16),
            "k_k": row(rw_k_k[l]), "k_a": row(rw_k_a[l]), "r_k": row(rw_r_k[l]), "ones": ones_blk,
        }
        r, v, nkk, lw, bb, kd, bonus, g = _rwkv_prep(rkv_raw, lr, prep_params, seq, tm_prep)
        yf, yb = _rwkv_scan(r, v, nkk, lw, bb, kd, batch, seq)
        yc = _pool(praw, _block_diag(pool_w[l]), row(pool_scale[l]), seq, tm)
        w_router = jnp.zeros((d, LANES), F32).at[:, :N_GROUPS].set(moe_w_group[l])
        w_router = w_router.at[:, N_GROUPS:N_GROUPS + N_EXPERTS].set(moe_w_expert[l])
        b_router = jnp.zeros((1, LANES), F32).at[0, :N_GROUPS].set(moe_b_group[l])
        b_router = b_router.at[0, N_GROUPS:N_GROUPS + N_EXPERTS].set(moe_b_expert[l])
        wo = w_out[l].astype(BF16)
        out_params = {
            "w_out_a": wo[:a_w], "w_out_b": wo[a_w:a_w + b_w], "w_out_c": wo[a_w + b_w:],
            "gn_gain": row(rw_gn_gain[l]), "gn_bias": row(rw_gn_bias[l]), "ones": ones_blk,
            "ln1_gain": row(ln1_gain[l]), "ln1_bias": row(ln1_bias[l]),
            "w_router": jnp.stack(_split_bf16(w_router)), "b_router": b_router,
        }
        x1, u2, route_i, route_w, counts = _outproj(ya, yf, yb, bonus, g, yc, x2, modl, out_params, seq, tm, alpha)
        n_blocks = -(-(t * TOP_K) // EXPERT_BLOCK) + N_EXPERTS
        dest, meta = _dispatch(route_i, counts, n_blocks)
        dest3 = dest.reshape(t // tm, 1, TOP_K * tm)
        xs = _scatter_rows(u2, dest3, n_blocks * EXPERT_BLOCK, tm)
        ysorted = _experts(meta, xs, moe_w_gate, moe_w_up, moe_w_down, l)
        x2 = _final(x1, ysorted, dest3, route_w, modl, row(ln2_gain[l]), row(ln2_bias[l]), seq, tm, alpha)
    return x2.reshape(batch, seq, d)
```

```python
import functools
import math

import jax
import jax.numpy as jnp
import numpy as np
from jax import lax
from jax.experimental import pallas as pl
from jax.experimental.pallas import tpu as pltpu

F32 = jnp.float32
BF16 = jnp.bfloat16
HI = lax.Precision.HIGHEST

GRID_W = 64
HEAD_DIM = 64
NA_KH = 8
NA_KW = 16
POOL_WINDOWS = (2, 4, 8, 16)
R_W = 32
R_A = 32
R_G = 64
DECAY_SCALE = math.exp(-0.5)
GN_EPS = 64e-5
N_GROUPS = 4
EXPERTS_PER_GROUP = 8
N_EXPERTS = N_GROUPS * EXPERTS_PER_GROUP
TOP_K = 2
EXPERT_BLOCK = 512
LN_EPS = 1e-5
NEG_INF = -1e30

NA_ROWS_PER_STEP = 4
SCAN_CHUNK = 64
HALO = 8
LANES = 128
VMEM_LIMIT = 52 * 1024 * 1024


def _ln(x):
    mu = jnp.mean(x, axis=-1, keepdims=True)
    xc = x - mu
    var = jnp.mean(xc * xc, axis=-1, keepdims=True)
    return xc * lax.rsqrt(var + LN_EPS)


def _sigmoid(x):
    return 1.0 / (1.0 + jnp.exp(-x))


def _split_bf16(x):
    hi = x.astype(BF16)
    return hi, (x - hi.astype(F32)).astype(BF16)


def _dot_split(x, w_exact):
    hi, lo = _split_bf16(x)
    return jnp.dot(hi, w_exact, preferred_element_type=F32) + jnp.dot(lo, w_exact, preferred_element_type=F32)


def _cparams(n_axes, semantics="parallel"):
    return pltpu.CompilerParams(dimension_semantics=(semantics,) * n_axes, vmem_limit_bytes=VMEM_LIMIT)


def _mod_kernel(c_ref, w_ref, b_ref, o_ref):
    c = c_ref[...]
    s = c * _sigmoid(c)
    o_ref[0] = jnp.dot(s, w_ref[0], precision=HI, preferred_element_type=F32) + b_ref[0]


def _modulation(c, w_mod, b_mod):
    n_layers, d, d6 = w_mod.shape
    b = c.shape[0]
    bp = -(-b // 8) * 8
    cp = jnp.zeros((bp, d), F32).at[:b].set(c)
    out = pl.pallas_call(
        _mod_kernel,
        grid=(n_layers, d6 // d),
        in_specs=[pl.BlockSpec((bp, d), lambda l, j: (0, 0)),
                  pl.BlockSpec((1, d, d), lambda l, j: (l, 0, j)),
                  pl.BlockSpec((1, 1, d), lambda l, j: (l, 0, j))],
        out_specs=pl.BlockSpec((1, bp, d), lambda l, j: (l, 0, j)),
        out_shape=jax.ShapeDtypeStruct((n_layers, bp, d6), F32),
        compiler_params=_cparams(2),
        name="modulation",
    )(cp, w_mod, b_mod.reshape(n_layers, 1, d6))
    return out[:, :b].reshape(n_layers, b, d6 // d, d)


def _inproj_kernel(x_ref, mod_ref, w_ref, qkv_ref, rkv_ref, lr_ref, pool_ref, *, a3, b3, lr_w):
    m = mod_ref[0]
    u = _ln(x_ref[...]) * (1.0 + m[1:2]) + m[0:1]
    h = jnp.dot(u.astype(BF16), w_ref[...], preferred_element_type=F32)
    qkv_ref[...] = h[:, :a3].astype(BF16)
    rkv_ref[...] = h[:, a3:a3 + b3]
    lr_ref[...] = h[:, a3 + b3:a3 + b3 + lr_w]
    pool_ref[...] = h[:, a3 + b3 + lr_w:]


def _inproj(x2, modl, w_in_bf, seq, tm, a3, b3, lr_w, c_w):
    t, d = x2.shape
    tpb = seq // tm
    kern = functools.partial(_inproj_kernel, a3=a3, b3=b3, lr_w=lr_w)
    return pl.pallas_call(
        kern,
        grid=(t // tm,),
        in_specs=[pl.BlockSpec((tm, d), lambda i: (i, 0)),
                  pl.BlockSpec((1,) + modl.shape[1:], lambda i: (i // tpb, 0, 0)),
                  pl.BlockSpec(w_in_bf.shape, lambda i: (0, 0))],
        out_specs=[pl.BlockSpec((tm, a3), lambda i: (i, 0)),
                   pl.BlockSpec((tm, b3), lambda i: (i, 0)),
                   pl.BlockSpec((tm, lr_w), lambda i: (i, 0)),
                   pl.BlockSpec((tm, c_w), lambda i: (i, 0))],
        out_shape=[jax.ShapeDtypeStruct((t, a3), BF16),
                   jax.ShapeDtypeStruct((t, b3), F32),
                   jax.ShapeDtypeStruct((t, lr_w), F32),
                   jax.ShapeDtypeStruct((t, c_w), F32)],
        compiler_params=_cparams(1),
        name="inproj",
    )(x2, modl, w_in_bf)


def _na_bias_table(rpb):
    col = np.arange(GRID_W)
    cstart = np.clip(col - NA_KW // 2, 0, GRID_W - NA_KW)
    in_win = (col[None, :] >= cstart[:, None]) & (col[None, :] < cstart[:, None] + NA_KW)
    dc = np.clip(col[None, :] - col[:, None], -(NA_KW - 1), NA_KW - 1) + (NA_KW - 1)
    pick = (dc[None] == np.arange(2 * NA_KW - 1)[:, None, None]).astype(np.float32)
    cols = jnp.einsum("hrc,cqk->hrqk", rpb.astype(F32), pick, precision=HI)
    cols = jnp.where(in_win, cols, NEG_INF)
    b = jnp.stack([cols[:, NA_KH - 1 - o:2 * NA_KH - 1 - o] for o in range(NA_KH)])
    h = rpb.shape[0]
    return jnp.transpose(b, (0, 1, 3, 2, 4)).reshape(NA_KH, h * GRID_W, NA_KH * GRID_W)


def _natten_kernel(q_ref, k_ref, v_ref, bias_ref, o_ref, *, rows, heads):
    width = q_ref.shape[1]
    nk = NA_KH * GRID_W
    head_of_lane = lax.broadcasted_iota(jnp.int32, (heads * GRID_W, width), 1) // HEAD_DIM
    head_of_row = lax.broadcasted_iota(jnp.int32, (heads * GRID_W, width), 0) // GRID_W
    own = head_of_lane == head_of_row
    for j in range(NA_ROWS_PER_STEP):
        r = pl.program_id(1) * NA_ROWS_PER_STEP + j
        rstart = jnp.clip(r - NA_KH // 2, 0, rows - NA_KH)
        off = r - rstart
        start = pl.multiple_of(rstart * GRID_W, GRID_W)
        kw = k_ref[pl.ds(start, nk), :]
        vw = v_ref[pl.ds(start, nk), :]
        q = q_ref[j * GRID_W:(j + 1) * GRID_W, :]
        qs = jnp.where(own, jnp.concatenate([q] * heads, axis=0), jnp.zeros((), q.dtype))
        s = lax.dot_general(qs, kw, (((1,), (1,)), ((), ())), preferred_element_type=F32) * (HEAD_DIM ** -0.5)
        s = s + bias_ref[off]
        mx = jnp.max(s, axis=-1, keepdims=True)
        p = jnp.exp(s - mx)
        den = jnp.sum(p, axis=-1, keepdims=True)
        o = jnp.where(own, jnp.dot(p.astype(BF16), vw, preferred_element_type=F32) / den, 0.0)
        acc = o[0:GRID_W]
        for h in range(1, heads):
            acc = acc + o[h * GRID_W:(h + 1) * GRID_W]
        o_ref[j * GRID_W:(j + 1) * GRID_W, :] = acc


def _natten(qkv, bias_tab, batch, seq, width):
    rows = seq // GRID_W
    assert rows >= NA_KH
    heads = width // HEAD_DIM
    steps = rows // NA_ROWS_PER_STEP
    assert steps * NA_ROWS_PER_STEP == rows
    tq = NA_ROWS_PER_STEP * GRID_W
    kern = functools.partial(_natten_kernel, rows=rows, heads=heads)
    return pl.pallas_call(
        kern,
        grid=(batch, steps),
        in_specs=[pl.BlockSpec((tq, width), lambda b, r: (b * steps + r, 0)),
                  pl.BlockSpec((seq, width), lambda b, r: (b, 1)),
                  pl.BlockSpec((seq, width), lambda b, r: (b, 2)),
                  pl.BlockSpec(bias_tab.shape, lambda b, r: (0, 0, 0))],
        out_specs=pl.BlockSpec((tq, width), lambda b, r: (b * steps + r, 0)),
        out_shape=jax.ShapeDtypeStruct((batch * seq, width), F32),
        compiler_params=_cparams(2),
        name="natten",
    )(qkv, qkv, qkv, bias_tab)


def _rwkv_prep_kernel(z_ref, zp_ref, zn_ref, lr_ref, cw_ref, w0_ref, wup_ref, a0_ref, aup_ref, gup_ref,
                      kk_ref, ka_ref, rk_ref, ones_ref,
                      r_o, v_o, nkk_o, lw_o, b_o, kd_o, bonus_o, g_o, *, tiles_per_batch, width):
    i = pl.program_id(0)
    tb = i % tiles_per_batch
    z = z_ref[...]
    tm = z.shape[0]
    prev = jnp.where(tb == 0, 0.0, zp_ref[HALO - 1:HALO, :])
    nxt = jnp.where(tb == tiles_per_batch - 1, 0.0, zn_ref[0:1, :])
    row = lax.broadcasted_iota(jnp.int32, z.shape, 0)
    zm1 = jnp.where(row == 0, prev, pltpu.roll(z, 1, 0))
    zp1 = jnp.where(row == tm - 1, nxt, pltpu.roll(z, tm - 1, 0))
    rkv = zm1 * cw_ref[0:1, :] + z * cw_ref[1:2, :] + zp1 * cw_ref[2:3, :]
    r = rkv[:, :width]
    k = rkv[:, width:2 * width]
    v = rkv[:, 2 * width:]
    lr = lr_ref[...]
    th = jnp.tanh(lr)
    sg = _sigmoid(lr)
    ones = ones_ref[...]

    def headsum(x):
        return _dot_split(x, ones)

    kk = k * kk_ref[...]
    kk = kk * lax.rsqrt(jnp.maximum(headsum(kk * kk), 1e-24))
    g_o[...] = jnp.dot(sg.astype(BF16), gup_ref[...], preferred_element_type=F32)
    r_o[...] = r
    v_o[...] = v
    nkk_o[...] = -kk
    bonus = jnp.zeros_like(r)
    th_b = th.astype(BF16)
    lr_b = lr.astype(BF16)
    for d in range(2):
        wl = jnp.dot(th_b, wup_ref[d], preferred_element_type=F32) + w0_ref[d:d + 1, :]
        lw_o[d] = -DECAY_SCALE * _sigmoid(wl)
        a = _sigmoid(jnp.dot(lr_b, aup_ref[d], preferred_element_type=F32) + a0_ref[d:d + 1, :])
        kd = k * (1.0 + (a - 1.0) * ka_ref[...])
        kd_o[d] = kd
        b_o[d] = kk * a
        bonus = bonus + headsum(r * kd * rk_ref[...]) * v
    bonus_o[...] = bonus


def _rwkv_prep(rkv_raw, lr, p, seq, tm):
    t, w3 = rkv_raw.shape
    width = w3 // 3
    tpb = seq // tm
    hb = tm // HALO
    nhb = t // HALO
    kern = functools.partial(_rwkv_prep_kernel, tiles_per_batch=tpb, width=width)
    tok = lambda i: (i, 0)
    dtok = lambda i: (0, i, 0)
    full2 = lambda i: (0, 0)
    full3 = lambda i: (0, 0, 0)
    tw = jax.ShapeDtypeStruct((t, width), F32)
    dtw = jax.ShapeDtypeStruct((2, t, width), F32)
    return pl.pallas_call(
        kern,
        grid=(t // tm,),
        in_specs=[pl.BlockSpec((tm, w3), tok),
                  pl.BlockSpec((HALO, w3), lambda i: (jnp.maximum(i * hb - 1, 0), 0)),
                  pl.BlockSpec((HALO, w3), lambda i: (jnp.minimum((i + 1) * hb, nhb - 1), 0)),
                  pl.BlockSpec((tm, lr.shape[1]), tok),
                  pl.BlockSpec(p["conv"].shape, full2),
                  pl.BlockSpec(p["w0"].shape, full2),
                  pl.BlockSpec(p["w_up"].shape, full3),
                  pl.BlockSpec(p["a0"].shape, full2),
                  pl.BlockSpec(p["a_up"].shape, full3),
                  pl.BlockSpec(p["g_up"].shape, full2),
                  pl.BlockSpec(p["k_k"].shape, full2),
                  pl.BlockSpec(p["k_a"].shape, full2),
                  pl.BlockSpec(p["r_k"].shape, full2),
                  pl.BlockSpec(p["ones"].shape, full2)],
        out_specs=[pl.BlockSpec((tm, width), tok), pl.BlockSpec((tm, width), tok), pl.BlockSpec((tm, width), tok),
                   pl.BlockSpec((2, tm, width), dtok), pl.BlockSpec((2, tm, width), dtok),
                   pl.BlockSpec((2, tm, width), dtok),
                   pl.BlockSpec((tm, width), tok), pl.BlockSpec((tm, width), tok)],
        out_shape=[tw, tw, tw, dtw, dtw, dtw, tw, tw],
        compiler_params=_cparams(1),
        name="rwkv_prep",
    )(rkv_raw, rkv_raw, rkv_raw, lr, p["conv"], p["w0"], p["w_up"], p["a0"], p["a_up"], p["g_up"],
      p["k_k"], p["k_a"], p["r_k"], p["ones"])


def _dot_nt(a, b):
    return lax.dot_general(a, b, (((1,), (1,)), ((), ())), preferred_element_type=F32)


def _dot_tn(a, b):
    return lax.dot_general(a, b, (((0,), (0,)), ((), ())), preferred_element_type=F32)


def _mm(a, b):
    return jnp.dot(a.astype(BF16), b.astype(BF16), preferred_element_type=F32)


def _rwkv_scan_kernel(rf_ref, vf_ref, nf_ref, rb_ref, vb_ref, nb_ref, lwf_ref, bf_ref, kf_ref, lwb_ref, bb_ref, kb_ref,
                      yf_ref, yb_ref, s_ref, *, heads, batch):
    @pl.when(pl.program_id(0) == 0)
    def _():
        s_ref[...] = jnp.zeros_like(s_ref)

    n = SCAN_CHUNK
    pair_w = 2 * HEAD_DIM
    row = lax.broadcasted_iota(jnp.int32, (n, pair_w), 0)
    lane = lax.broadcasted_iota(jnp.int32, (n, pair_w), 1)
    col = lane & (HEAD_DIM - 1)
    even = lane < HEAD_DIM
    levels = n.bit_length()
    same = [(row >> k) == (col >> k) for k in range(levels)]
    eye = same[0].astype(F32)
    level_masks = [same[sh + 1] & jnp.logical_not(same[sh]) for sh in range(1, levels - 1)]

    def blockdiag(x2):
        xb = x2.astype(BF16)
        zero = jnp.zeros((), BF16)
        return jnp.concatenate([jnp.where(even, xb, zero), jnp.where(even, zero, xb)], axis=0)

    def mm(x2, y2):
        return jnp.dot(x2.astype(BF16), blockdiag(y2), preferred_element_type=F32)

    def mm_nt(x2, y2):
        return _dot_nt(x2.astype(BF16), blockdiag(y2))

    dirs = ((rf_ref, vf_ref, nf_ref, lwf_ref, bf_ref, kf_ref, yf_ref),
            (rb_ref, vb_ref, nb_ref, lwb_ref, bb_ref, kb_ref, yb_ref))
    chains = []
    for d, (r_ref, v_ref, n_ref, lw_ref, b_ref, k_ref, y_ref) in enumerate(dirs):
        order = row - col if d == 0 else col - row
        strict = order > 0
        incl = order >= 0
        incl_f = incl[:, :n].astype(F32)
        for bi in range(batch):
            lw = lw_ref[0, bi]
            g_inc = jnp.dot(incl_f, lw, precision=HI, preferred_element_type=F32)
            g_tot = jnp.sum(lw, axis=0, keepdims=True)
            e_neg = jnp.exp(-g_inc)
            e_end = jnp.exp(g_tot - g_inc)
            decay = jnp.exp(g_tot)
            a_t = n_ref[bi] * jnp.exp(g_inc - lw)
            r_t = r_ref[bi] * jnp.exp(g_inc)
            bb = b_ref[0, bi]
            kd = k_ref[0, bi]
            b_t = bb * e_neg
            k_t = kd * e_neg
            ar_t = jnp.concatenate([a_t, r_t], axis=0).astype(BF16)
            bk_h = jnp.concatenate([bb * e_end, kd * e_end], axis=0).astype(BF16)
            v = v_ref[bi]
            for p in range(heads // 2):
                sl = slice(p * pair_w, (p + 1) * pair_w)
                chains.append(dict(strict=strict, incl=incl, sl=sl, bi=bi, y_ref=y_ref,
                                   si=(d * batch + bi) * (heads // 2) + p, decay=decay[:, sl],
                                   ar=ar_t[:, sl], b=b_t[:, sl], k=k_t[:, sl], bk_h=bk_h[:, sl], v=v[:, sl]))

    for ch in chains:
        pb = mm_nt(ch["ar"], ch["b"])
        pk = mm_nt(ch["ar"], ch["k"])
        ch["l_ab"] = jnp.where(ch["strict"], pb[:n], 0.0)
        ch["m_rb"] = jnp.where(ch["incl"], pb[n:], 0.0)
        ch["l_ak"] = jnp.where(ch["strict"], pk[:n], 0.0)
        ch["m_rk"] = jnp.where(ch["incl"], pk[n:], 0.0)
        ch["t"] = eye + jnp.where(same[1], ch["l_ab"], 0.0)
    for mask in level_masks:
        for ch in chains:
            ch["tc"] = mm(ch["t"], jnp.where(mask, ch["l_ab"], 0.0))
        for ch in chains:
            ch["t"] = ch["t"] + mm(ch["tc"], ch["t"])
    for ch in chains:
        ch["s0"] = s_ref[ch["si"]]
        ch["x"] = mm_nt(ch["ar"], ch["s0"])
    for ch in chains:
        ch["rhs"] = ch["x"][:n] + mm(ch["l_ak"], ch["v"])
    for ch in chains:
        ch["u"] = mm(ch["t"], ch["rhs"])
    for ch in chains:
        y = ch["x"][n:] + mm(ch["m_rb"], ch["u"]) + mm(ch["m_rk"], ch["v"])
        ch["y_ref"][ch["bi"], :, ch["sl"]] = y
    for ch in chains:
        uv = jnp.concatenate([ch["u"], ch["v"]], axis=0).astype(BF16)
        full = _dot_tn(uv, ch["bk_h"])
        s_ref[ch["si"]] = ch["s0"] * ch["decay"] + jnp.where(even, full[:HEAD_DIM], full[HEAD_DIM:])


def _rwkv_scan(r, v, nkk, lw, b, kd, batch, seq):
    t, width = r.shape
    heads = width // HEAD_DIM
    n = SCAN_CHUNK
    nc = seq // n
    r3, v3, n3 = (z.reshape(batch, seq, width) for z in (r, v, nkk))
    lw4, b4, k4 = (z.reshape(2, batch, seq, width) for z in (lw, b, kd))
    fwd = pl.BlockSpec((batch, n, width), lambda c: (0, c, 0))
    bwd = pl.BlockSpec((batch, n, width), lambda c: (0, nc - 1 - c, 0))
    fwd_d = pl.BlockSpec((1, batch, n, width), lambda c: (0, 0, c, 0))
    bwd_d = pl.BlockSpec((1, batch, n, width), lambda c: (1, 0, nc - 1 - c, 0))
    kern = functools.partial(_rwkv_scan_kernel, heads=heads, batch=batch)
    yf, yb = pl.pallas_call(
        kern,
        grid=(nc,),
        in_specs=[fwd, fwd, fwd, bwd, bwd, bwd, fwd_d, fwd_d, fwd_d, bwd_d, bwd_d, bwd_d],
        out_specs=[fwd, bwd],
        out_shape=[jax.ShapeDtypeStruct((batch, seq, width), F32)] * 2,
        scratch_shapes=[pltpu.VMEM((batch * heads, HEAD_DIM, 2 * HEAD_DIM), F32)],
        compiler_params=_cparams(1, "arbitrary"),
        name="rwkv_scan",
    )(r3, v3, n3, r3, v3, n3, lw4, b4, k4, lw4, b4, k4)
    return yf.reshape(t, width), yb.reshape(t, width)


def _pool_kernel(p_ref, pp_ref, pn_ref, w_ref, sc_ref, o_ref, ext_ref, *, tiles_per_batch, seq):
    i = pl.program_id(0)
    tb = i % tiles_per_batch
    p = p_ref[...]
    tm, width = p.shape
    ext_ref[0:HALO, :] = jnp.where(tb == 0, 0.0, pp_ref[...])
    ext_ref[HALO:HALO + tm, :] = p
    ext_ref[HALO + tm:2 * HALO + tm, :] = jnp.where(tb == tiles_per_batch - 1, 0.0, pn_ref[...])

    def shifted(o):
        return ext_ref[HALO + o:HALO + o + tm, :]

    t = tb * tm + lax.broadcasted_iota(jnp.int32, (tm, width), 0)
    grp = lax.broadcasted_iota(jnp.int32, (tm, width), 1) // (width // len(POOL_WINDOWS))
    tot = p
    prev_half = 0
    pooled = jnp.zeros_like(p)
    for gi, win in enumerate(POOL_WINDOWS):
        half = win // 2
        for o in range(prev_half, half):
            tot = tot + shifted(-o - 1)
            if o > 0:
                tot = tot + shifted(o)
        prev_half = half
        lo = jnp.clip(t - half, 0, seq - 1)
        hi = jnp.clip(t + half - 1, 0, seq - 1)
        cnt = (hi - lo + 1).astype(F32)
        pooled = jnp.where(grp == gi, tot / cnt, pooled)
    pooled = pooled - p
    o_ref[...] = jnp.dot(pooled, w_ref[...], preferred_element_type=F32) * sc_ref[...]


def _pool(praw, w_blk, scale, seq, tm):
    t, width = praw.shape
    tpb = seq // tm
    hb = tm // HALO
    nhb = t // HALO
    kern = functools.partial(_pool_kernel, tiles_per_batch=tpb, seq=seq)
    return pl.pallas_call(
        kern,
        grid=(t // tm,),
        in_specs=[pl.BlockSpec((tm, width), lambda i: (i, 0)),
                  pl.BlockSpec((HALO, width), lambda i: (jnp.maximum(i * hb - 1, 0), 0)),
                  pl.BlockSpec((HALO, width), lambda i: (jnp.minimum((i + 1) * hb, nhb - 1), 0)),
                  pl.BlockSpec(w_blk.shape, lambda i: (0, 0)),
                  pl.BlockSpec(scale.shape, lambda i: (0, 0))],
        out_specs=pl.BlockSpec((tm, width), lambda i: (i, 0)),
        out_shape=jax.ShapeDtypeStruct((t, width), F32),
        scratch_shapes=[pltpu.VMEM((tm + 2 * HALO, width), F32)],
        compiler_params=_cparams(1),
        name="pool",
    )(praw, praw, praw, w_blk, scale)


def _outproj_kernel(ya_ref, yf_ref, yb_ref, bonus_ref, g_ref, yc_ref, x_ref, mod_ref, wa_ref, wb_ref, wc_ref,
                    gng_ref, gnb_ref, ones_ref, l1g_ref, l1b_ref, wr_ref, br_ref,
                    x1_o, u2_o, ri_o, rw_o, cnt_o, cnt_ref, *, alpha):
    m = mod_ref[0]
    ones = ones_ref[...]

    def headmean(x):
        return _dot_split(x, ones) * (1.0 / HEAD_DIM)

    ysum = yf_ref[...] + yb_ref[...]
    yc0 = ysum - headmean(ysum)
    yn = yc0 * lax.rsqrt(headmean(yc0 * yc0) + GN_EPS) * gng_ref[...] + gnb_ref[...]
    yb = (yn + bonus_ref[...]) * g_ref[...]
    mix = (jnp.dot(ya_ref[...].astype(BF16), wa_ref[...], preferred_element_type=F32)
           + jnp.dot(yb.astype(BF16), wb_ref[...], preferred_element_type=F32)
           + jnp.dot(yc_ref[...].astype(BF16), wc_ref[...], preferred_element_type=F32))
    x1 = _ln(alpha * x_ref[...] + m[2:3] * mix) * l1g_ref[...] + l1b_ref[...]
    x1_o[...] = x1
    u2 = _ln(x1) * (1.0 + m[4:5]) + m[3:4]
    u2_o[...] = u2

    u_hi, u_lo = _split_bf16(u2)
    hi_both = jnp.dot(u_hi, wr_ref[...], preferred_element_type=F32)
    lg = (hi_both[:, :LANES] + hi_both[:, LANES:]
          + jnp.dot(u_lo, wr_ref[:, :LANES], preferred_element_type=F32)) + br_ref[...]
    lane = lax.broadcasted_iota(jnp.int32, lg.shape, 1)
    big = jnp.int32(1 << 20)
    gl = jnp.where(lane < N_GROUPS, lg, -jnp.inf)
    gmax = jnp.max(gl, axis=-1, keepdims=True)
    gidx = jnp.min(jnp.where(gl == gmax, lane, big), axis=-1, keepdims=True)
    pg_sel = 1.0 / jnp.sum(jnp.exp(gl - gmax), axis=-1, keepdims=True)
    e_lo = N_GROUPS + gidx * EXPERTS_PER_GROUP
    el = jnp.where((lane >= e_lo) & (lane < e_lo + EXPERTS_PER_GROUP), lg, -jnp.inf)
    m1 = jnp.max(el, axis=-1, keepdims=True)
    i1 = jnp.min(jnp.where(el == m1, lane, big), axis=-1, keepdims=True)
    el2 = jnp.where(lane == i1, -jnp.inf, el)
    m2 = jnp.max(el2, axis=-1, keepdims=True)
    i2 = jnp.min(jnp.where(el2 == m2, lane, big), axis=-1, keepdims=True)
    e21 = jnp.exp(m2 - m1)
    p1 = 1.0 / (1.0 + e21)
    p2 = e21 / (1.0 + e21)
    rw_o[...] = jnp.where(lane == 0, pg_sel * p1, jnp.where(lane == 1, pg_sel * p2, 0.0))

    @pl.when(pl.program_id(0) == 0)
    def _():
        cnt_ref[...] = jnp.zeros_like(cnt_ref)

    tm = lg.shape[0]
    earlier = (lax.broadcasted_iota(jnp.int32, (tm, tm), 1)
               < lax.broadcasted_iota(jnp.int32, (tm, tm), 0)).astype(BF16)
    oh1 = (lane == i1).astype(F32)
    oh2 = (lane == i2).astype(F32)
    run = cnt_ref[...]
    c1 = jnp.sum(oh1, axis=0, keepdims=True)
    before1 = run + jnp.dot(earlier, oh1.astype(BF16), preferred_element_type=F32)
    before2 = run + c1 + jnp.dot(earlier, oh2.astype(BF16), preferred_element_type=F32)
    rank1 = jnp.sum(oh1 * before1, axis=-1, keepdims=True).astype(jnp.int32)
    rank2 = jnp.sum(oh2 * before2, axis=-1, keepdims=True).astype(jnp.int32)
    total = run + c1 + jnp.sum(oh2, axis=0, keepdims=True)
    cnt_ref[...] = total
    cnt_o[...] = total
    ri_o[...] = jnp.where(lane == 0, i1 - N_GROUPS, jnp.where(lane == 1, i2 - N_GROUPS,
                          jnp.where(lane == 2, rank1, jnp.where(lane == 3, rank2, 0))))


def _outproj(ya, yf, yb, bonus, g, yc, x2, modl, p, seq, tm, alpha):
    t, d = x2.shape
    tpb = seq // tm
    aw, bw, cw = ya.shape[1], bonus.shape[1], yc.shape[1]
    tok = lambda i: (i, 0)
    full2 = lambda i: (0, 0)
    kern = functools.partial(_outproj_kernel, alpha=alpha)
    small = ["gn_gain", "gn_bias", "ones", "ln1_gain", "ln1_bias", "w_router", "b_router"]
    return pl.pallas_call(
        kern,
        grid=(t // tm,),
        in_specs=[pl.BlockSpec((tm, aw), tok),
                  pl.BlockSpec((tm, bw), tok), pl.BlockSpec((tm, bw), tok),
                  pl.BlockSpec((tm, bw), tok), pl.BlockSpec((tm, bw), tok),
                  pl.BlockSpec((tm, cw), tok),
                  pl.BlockSpec((tm, d), tok),
                  pl.BlockSpec((1,) + modl.shape[1:], lambda i: (i // tpb, 0, 0)),
                  pl.BlockSpec(p["w_out_a"].shape, full2),
                  pl.BlockSpec(p["w_out_b"].shape, full2),
                  pl.BlockSpec(p["w_out_c"].shape, full2)]
                 + [pl.BlockSpec(p[k].shape, functools.partial(lambda nd, i: (0,) * nd, p[k].ndim)) for k in small],
        out_specs=[pl.BlockSpec((tm, d), tok), pl.BlockSpec((tm, d), tok),
                   pl.BlockSpec((tm, LANES), tok), pl.BlockSpec((tm, LANES), tok),
                   pl.BlockSpec((1, LANES), full2)],
        out_shape=[jax.ShapeDtypeStruct((t, d), F32), jax.ShapeDtypeStruct((t, d), F32),
                   jax.ShapeDtypeStruct((t, LANES), jnp.int32), jax.ShapeDtypeStruct((t, LANES), F32),
                   jax.ShapeDtypeStruct((1, LANES), F32)],
        scratch_shapes=[pltpu.VMEM((1, LANES), F32)],
        compiler_params=_cparams(1, "arbitrary"),
        name="outproj",
    )(ya, yf, yb, bonus, g, yc, x2, modl, p["w_out_a"], p["w_out_b"], p["w_out_c"], *[p[k] for k in small])


def _dispatch(route_i, counts_lanes, n_blocks):
    counts = counts_lanes[0, N_GROUPS:N_GROUPS + N_EXPERTS].astype(jnp.int32)
    padded = ((counts + EXPERT_BLOCK - 1) // EXPERT_BLOCK) * EXPERT_BLOCK
    pends = jnp.cumsum(padded)
    pstarts = pends - padded
    e = route_i[:, :TOP_K]
    rank = route_i[:, TOP_K:2 * TOP_K]
    ids = jnp.arange(N_EXPERTS, dtype=jnp.int32)
    dest = jnp.sum(jnp.where(e[..., None] == ids, pstarts, 0), axis=-1) + rank
    block_start = jnp.arange(n_blocks, dtype=jnp.int32) * EXPERT_BLOCK
    block_e = jnp.minimum(jnp.sum((pends[None, :] <= block_start[:, None]).astype(jnp.int32), axis=1), N_EXPERTS - 1)
    meta = jnp.concatenate([block_e, (pends[-1] // EXPERT_BLOCK)[None]]).astype(jnp.int32)
    return dest, meta


def _scatter_rows_kernel(dest_ref, u_ref, xs_in_ref, xs_ref, sem):
    del xs_in_ref
    tm = u_ref.shape[0]

    def issue(r, carry):
        for k in range(TOP_K):
            dst = dest_ref[0, 0, TOP_K * r + k]
            pltpu.make_async_copy(u_ref.at[pl.ds(r, 1)], xs_ref.at[pl.ds(dst, 1)], sem).start()
        return carry

    lax.fori_loop(0, tm, issue, 0, unroll=8)
    rows = pl.ds(0, TOP_K * tm)
    pltpu.make_async_copy(xs_ref.at[rows], xs_ref.at[rows], sem).wait()


def _scatter_rows(u2, dest3, total, tm):
    t, d = u2.shape
    return pl.pallas_call(
        _scatter_rows_kernel,
        grid=(t // tm,),
        in_specs=[pl.BlockSpec((1, 1, TOP_K * tm), lambda i: (i, 0, 0), memory_space=pltpu.SMEM),
                  pl.BlockSpec((tm, d), lambda i: (i, 0)),
                  pl.BlockSpec(memory_space=pl.ANY)],
        out_specs=pl.BlockSpec(memory_space=pl.ANY),
        out_shape=jax.ShapeDtypeStruct((total, d), F32),
        scratch_shapes=[pltpu.SemaphoreType.DMA(())],
        input_output_aliases={2: 0},
        compiler_params=_cparams(1, "arbitrary"),
        name="scatter_rows",
    )(dest3, u2, jnp.zeros((total, d), F32))


def _experts_kernel(meta_ref, xs_ref, wg_ref, wu_ref, wd_ref, o_ref, wg_b, wu_b, wd_b):
    i = pl.program_id(0)
    n_used = meta_ref[pl.num_programs(0)]

    @pl.when((i == 0) | (meta_ref[i] != meta_ref[jnp.maximum(i - 1, 0)]))
    def _():
        wg_b[...] = wg_ref[0, 0].astype(BF16)
        wu_b[...] = wu_ref[0, 0].astype(BF16)
        wd_b[...] = wd_ref[0, 0].astype(BF16)

    @pl.when(i < n_used)
    def _():
        xb = xs_ref[...].astype(BF16)
        gate = jnp.dot(xb, wg_b[...], preferred_element_type=F32)
        up = jnp.dot(xb, wu_b[...], preferred_element_type=F32)
        hb = gate * _sigmoid(gate) * up
        o_ref[...] = jnp.dot(hb.astype(BF16), wd_b[...], preferred_element_type=F32)

    @pl.when(i >= n_used)
    def _():
        o_ref[...] = jnp.zeros_like(o_ref)


def _experts(meta, xs, wg, wu, wd, layer):
    total, d = xs.shape
    nb = total // EXPERT_BLOCK
    de = wg.shape[3]
    grid_spec = pltpu.PrefetchScalarGridSpec(
        num_scalar_prefetch=1,
        grid=(nb,),
        in_specs=[pl.BlockSpec((EXPERT_BLOCK, d), lambda i, m: (i, 0)),
                  pl.BlockSpec((1, 1, d, de), lambda i, m: (layer, m[i], 0, 0)),
                  pl.BlockSpec((1, 1, d, de), lambda i, m: (layer, m[i], 0, 0)),
                  pl.BlockSpec((1, 1, de, d), lambda i, m: (layer, m[i], 0, 0))],
        out_specs=pl.BlockSpec((EXPERT_BLOCK, d), lambda i, m: (i, 0)),
        scratch_shapes=[pltpu.VMEM((d, de), BF16), pltpu.VMEM((d, de), BF16), pltpu.VMEM((de, d), BF16)],
    )
    return pl.pallas_call(
        _experts_kernel,
        grid_spec=grid_spec,
        out_shape=jax.ShapeDtypeStruct((total, d), F32),
        compiler_params=_cparams(1, "arbitrary"),
        name="experts",
    )(meta, xs, wg, wu, wd)


def _final_kernel(dcur_ref, dnext_ref, x1_ref, rw_ref, mod_ref, g_ref, b_ref, ys_ref, o_ref, ybuf, sem, *, alpha):
    i = pl.program_id(0)
    tm = x1_ref.shape[0]
    slot = i % 2

    def gather(d_ref, s):
        def issue(r, carry):
            for k in range(TOP_K):
                src = d_ref[0, 0, TOP_K * r + k]
                pltpu.make_async_copy(ys_ref.at[pl.ds(src, 1)], ybuf.at[s, k, pl.ds(r, 1)], sem.at[s]).start()
            return carry

        lax.fori_loop(0, tm, issue, 0, unroll=8)

    @pl.when(i == 0)
    def _():
        gather(dcur_ref, 0)

    @pl.when(i + 1 < pl.num_programs(0))
    def _():
        gather(dnext_ref, 1 - slot)

    pltpu.make_async_copy(ybuf.at[slot], ybuf.at[slot], sem.at[slot]).wait()
    m = mod_ref[0]
    rw = rw_ref[...]
    f = rw[:, 0:1] * ybuf[slot, 0] + rw[:, 1:2] * ybuf[slot, 1]
    o_ref[...] = _ln(alpha * x1_ref[...] + m[5:6] * f) * g_ref[...] + b_ref[...]


def _final(x1, ysorted, dest3, rw, modl, gain, bias, seq, tm, alpha):
    t, d = x1.shape
    tpb = seq // tm
    n_tiles = t // tm
    tok = lambda i: (i, 0)
    kern = functools.partial(_final_kernel, alpha=alpha)
    dspec = lambda f: pl.BlockSpec((1, 1, TOP_K * tm), f, memory_space=pltpu.SMEM)
    return pl.pallas_call(
        kern,
        grid=(n_tiles,),
        in_specs=[dspec(lambda i: (i, 0, 0)), dspec(lambda i: (jnp.minimum(i + 1, n_tiles - 1), 0, 0)),
                  pl.BlockSpec((tm, d), tok), pl.BlockSpec((tm, LANES), tok),
                  pl.BlockSpec((1,) + modl.shape[1:], lambda i: (i // tpb, 0, 0)),
                  pl.BlockSpec(gain.shape, lambda i: (0, 0)), pl.BlockSpec(bias.shape, lambda i: (0, 0)),
                  pl.BlockSpec(memory_space=pl.ANY)],
        out_specs=pl.BlockSpec((tm, d), tok),
        out_shape=jax.ShapeDtypeStruct((t, d), F32),
        scratch_shapes=[pltpu.VMEM((2, TOP_K, tm, d), F32), pltpu.SemaphoreType.DMA((2,))],
        compiler_params=_cparams(1, "arbitrary"),
        name="final_ln",
    )(dest3, dest3, x1, rw, modl, gain, bias, ysorted)


def _block_diag(blocks):
    n, a, b = blocks.shape
    out = jnp.zeros((n * a, n * b), blocks.dtype)
    for i in range(n):
        out = out.at[i * a:(i + 1) * a, i * b:(i + 1) * b].set(blocks[i])
    return out


def _pad_rows(w, lo, total):
    return jnp.zeros((total, w.shape[-1]), w.dtype).at[lo:lo + w.shape[0]].set(w)


def kernel(x, c, w_mod, b_mod, w_in, na_rpb, rw_conv, rw_w0, rw_w_up, rw_a0, rw_a_up, rw_g_up, rw_k_k, rw_k_a, rw_r_k, rw_gn_gain, rw_gn_bias, pool_w, pool_scale, w_out, ln1_gain, ln1_bias, ln2_gain, ln2_bias, moe_w_group, moe_b_group, moe_w_expert, moe_b_expert, moe_w_gate, moe_w_up, moe_w_down):
    batch, seq, d = x.shape
    depth = w_mod.shape[0]
    t = batch * seq
    a_w = na_rpb.shape[1] * HEAD_DIM
    b_w = rw_w0.shape[-1]
    c_w = pool_scale.shape[-1]
    lr_w = R_W + R_A + R_G
    alpha = (2 * depth) ** 0.25
    tm = min(512, seq)
    tm_prep = min(256, seq)
    assert seq % tm == 0 and seq % SCAN_CHUNK == 0 and seq % GRID_W == 0 and lr_w == LANES

    mod = _modulation(c, w_mod, b_mod)
    ones_blk = _block_diag(jnp.ones((b_w // HEAD_DIM, HEAD_DIM, HEAD_DIM), BF16))
    row = lambda v: v.reshape(1, -1)

    x2 = x.reshape(t, d)
    for l in range(depth):
        modl = mod[l]
        qkv, rkv_raw, lr, praw = _inproj(x2, modl, w_in[l].astype(BF16), seq, tm, 3 * a_w, 3 * b_w, lr_w, c_w)
        ya = _natten(qkv, _na_bias_table(na_rpb[l]), batch, seq, a_w)
        prep_params = {
            "conv": rw_conv[l], "w0": rw_w0[l], "a0": rw_a0[l],
            "w_up": jnp.stack([_pad_rows(rw_w_up[l, dd], 0, lr_w) for dd in range(2)]).astype(BF16),
            "a_up": jnp.stack([_pad_rows(rw_a_up[l, dd], R_W, lr_w) for dd in range(2)]).astype(BF16),
            "g_up": _pad_rows(rw_g_up[l], R_W + R_A, lr_w).astype(BF16),
            "k_k": row(rw_k_k[l]), "k_a": row(rw_k_a[l]), "r_k": row(rw_r_k[l]), "ones": ones_blk,
        }
        r, v, nkk, lw, bb, kd, bonus, g = _rwkv_prep(rkv_raw, lr, prep_params, seq, tm_prep)
        yf, yb = _rwkv_scan(r, v, nkk, lw, bb, kd, batch, seq)
        yc = _pool(praw, _block_diag(pool_w[l]), row(pool_scale[l]), seq, tm)
        w_router = jnp.zeros((d, LANES), F32).at[:, :N_GROUPS].set(moe_w_group[l])
        w_router = w_router.at[:, N_GROUPS:N_GROUPS + N_EXPERTS].set(moe_w_expert[l])
        b_router = jnp.zeros((1, LANES), F32).at[0, :N_GROUPS].set(moe_b_group[l])
        b_router = b_router.at[0, N_GROUPS:N_GROUPS + N_EXPERTS].set(moe_b_expert[l])
        wo = w_out[l].astype(BF16)
        out_params = {
            "w_out_a": wo[:a_w], "w_out_b": wo[a_w:a_w + b_w], "w_out_c": wo[a_w + b_w:],
            "gn_gain": row(rw_gn_gain[l]), "gn_bias": row(rw_gn_bias[l]), "ones": ones_blk,
            "ln1_gain": row(ln1_gain[l]), "ln1_bias": row(ln1_bias[l]),
            "w_router": jnp.concatenate(_split_bf16(w_router), axis=1), "b_router": b_router,
        }
        x1, u2, route_i, route_w, counts = _outproj(ya, yf, yb, bonus, g, yc, x2, modl, out_params, seq, tm, alpha)
        n_blocks = -(-(t * TOP_K) // EXPERT_BLOCK) + N_EXPERTS
        dest, meta = _dispatch(route_i, counts, n_blocks)
        dest3 = dest.reshape(t // tm, 1, TOP_K * tm)
        xs = _scatter_rows(u2, dest3, n_blocks * EXPERT_BLOCK, tm)
        ysorted = _experts(meta, xs, moe_w_gate, moe_w_up, moe_w_down, l)
        x2 = _final(x1, ysorted, dest3, route_w, modl, row(ln2_gain[l]), row(ln2_bias[l]), seq, tm, alpha)
    return x2.reshape(batch, seq, d)
```

```python
import functools
import math

import jax
import jax.numpy as jnp
import numpy as np
from jax import lax
from jax.experimental import pallas as pl
from jax.experimental.pallas import tpu as pltpu

F32 = jnp.float32
BF16 = jnp.bfloat16
HI = lax.Precision.HIGHEST

GRID_W = 64
HEAD_DIM = 64
NA_KH = 8
NA_KW = 16
POOL_WINDOWS = (2, 4, 8, 16)
R_W = 32
R_A = 32
R_G = 64
DECAY_SCALE = math.exp(-0.5)
GN_EPS = 64e-5
N_GROUPS = 4
EXPERTS_PER_GROUP = 8
N_EXPERTS = N_GROUPS * EXPERTS_PER_GROUP
TOP_K = 2
EXPERT_BLOCK = 512
LN_EPS = 1e-5
NEG_INF = -1e30

NA_ROWS_PER_STEP = 4
SCAN_CHUNK = 64
HALO = 8
LANES = 128
VMEM_LIMIT = 52 * 1024 * 1024


def _ln(x):
    mu = jnp.mean(x, axis=-1, keepdims=True)
    xc = x - mu
    var = jnp.mean(xc * xc, axis=-1, keepdims=True)
    return xc * lax.rsqrt(var + LN_EPS)


def _sigmoid(x):
    return 1.0 / (1.0 + jnp.exp(-x))


def _split_bf16(x):
    hi = x.astype(BF16)
    return hi, (x - hi.astype(F32)).astype(BF16)


def _dot_split(x, w_exact):
    hi, lo = _split_bf16(x)
    return jnp.dot(hi, w_exact, preferred_element_type=F32) + jnp.dot(lo, w_exact, preferred_element_type=F32)


def _cparams(n_axes, semantics="parallel"):
    return pltpu.CompilerParams(dimension_semantics=(semantics,) * n_axes, vmem_limit_bytes=VMEM_LIMIT)


def _mod_kernel(c_ref, w_ref, b_ref, o_ref):
    c = c_ref[...]
    s = c * _sigmoid(c)
    o_ref[0] = jnp.dot(s, w_ref[0], precision=HI, preferred_element_type=F32) + b_ref[0]


def _modulation(c, w_mod, b_mod):
    n_layers, d, d6 = w_mod.shape
    b = c.shape[0]
    bp = -(-b // 8) * 8
    cp = jnp.zeros((bp, d), F32).at[:b].set(c)
    out = pl.pallas_call(
        _mod_kernel,
        grid=(n_layers, d6 // d),
        in_specs=[pl.BlockSpec((bp, d), lambda l, j: (0, 0)),
                  pl.BlockSpec((1, d, d), lambda l, j: (l, 0, j)),
                  pl.BlockSpec((1, 1, d), lambda l, j: (l, 0, j))],
        out_specs=pl.BlockSpec((1, bp, d), lambda l, j: (l, 0, j)),
        out_shape=jax.ShapeDtypeStruct((n_layers, bp, d6), F32),
        compiler_params=_cparams(2),
        name="modulation",
    )(cp, w_mod, b_mod.reshape(n_layers, 1, d6))
    return out[:, :b].reshape(n_layers, b, d6 // d, d)


def _inproj_kernel(x_ref, mod_ref, w_ref, qkv_ref, rkv_ref, lr_ref, pool_ref, *, a3, b3, lr_w):
    m = mod_ref[0]
    u = _ln(x_ref[...]) * (1.0 + m[1:2]) + m[0:1]
    h = jnp.dot(u.astype(BF16), w_ref[...], preferred_element_type=F32)
    qkv_ref[...] = h[:, :a3].astype(BF16)
    rkv_ref[...] = h[:, a3:a3 + b3]
    lr_ref[...] = h[:, a3 + b3:a3 + b3 + lr_w]
    pool_ref[...] = h[:, a3 + b3 + lr_w:]


def _inproj(x2, modl, w_in_bf, seq, tm, a3, b3, lr_w, c_w):
    t, d = x2.shape
    tpb = seq // tm
    kern = functools.partial(_inproj_kernel, a3=a3, b3=b3, lr_w=lr_w)
    return pl.pallas_call(
        kern,
        grid=(t // tm,),
        in_specs=[pl.BlockSpec((tm, d), lambda i: (i, 0)),
                  pl.BlockSpec((1,) + modl.shape[1:], lambda i: (i // tpb, 0, 0)),
                  pl.BlockSpec(w_in_bf.shape, lambda i: (0, 0))],
        out_specs=[pl.BlockSpec((tm, a3), lambda i: (i, 0)),
                   pl.BlockSpec((tm, b3), lambda i: (i, 0)),
                   pl.BlockSpec((tm, lr_w), lambda i: (i, 0)),
                   pl.BlockSpec((tm, c_w), lambda i: (i, 0))],
        out_shape=[jax.ShapeDtypeStruct((t, a3), BF16),
                   jax.ShapeDtypeStruct((t, b3), F32),
                   jax.ShapeDtypeStruct((t, lr_w), F32),
                   jax.ShapeDtypeStruct((t, c_w), F32)],
        compiler_params=_cparams(1),
        name="inproj",
    )(x2, modl, w_in_bf)


def _na_bias_table(rpb):
    col = np.arange(GRID_W)
    cstart = np.clip(col - NA_KW // 2, 0, GRID_W - NA_KW)
    in_win = (col[None, :] >= cstart[:, None]) & (col[None, :] < cstart[:, None] + NA_KW)
    dc = np.clip(col[None, :] - col[:, None], -(NA_KW - 1), NA_KW - 1) + (NA_KW - 1)
    pick = (dc[None] == np.arange(2 * NA_KW - 1)[:, None, None]).astype(np.float32)
    cols = jnp.einsum("hrc,cqk->hrqk", rpb.astype(F32), pick, precision=HI)
    cols = jnp.where(in_win, cols, NEG_INF)
    b = jnp.stack([cols[:, NA_KH - 1 - o:2 * NA_KH - 1 - o] for o in range(NA_KH)])
    h = rpb.shape[0]
    return jnp.transpose(b, (0, 1, 3, 2, 4)).reshape(NA_KH, h * GRID_W, NA_KH * GRID_W)


def _natten_kernel(q_ref, k_ref, v_ref, bias_ref, o_ref, *, rows, heads):
    width = q_ref.shape[1]
    nk = NA_KH * GRID_W
    head_of_lane = lax.broadcasted_iota(jnp.int32, (heads * GRID_W, width), 1) // HEAD_DIM
    head_of_row = lax.broadcasted_iota(jnp.int32, (heads * GRID_W, width), 0) // GRID_W
    own = head_of_lane == head_of_row
    for j in range(NA_ROWS_PER_STEP):
        r = pl.program_id(1) * NA_ROWS_PER_STEP + j
        rstart = jnp.clip(r - NA_KH // 2, 0, rows - NA_KH)
        off = r - rstart
        start = pl.multiple_of(rstart * GRID_W, GRID_W)
        kw = k_ref[pl.ds(start, nk), :]
        vw = v_ref[pl.ds(start, nk), :]
        q = q_ref[j * GRID_W:(j + 1) * GRID_W, :]
        qs = jnp.where(own, jnp.concatenate([q] * heads, axis=0), jnp.zeros((), q.dtype))
        s = lax.dot_general(qs, kw, (((1,), (1,)), ((), ())), preferred_element_type=F32) * (HEAD_DIM ** -0.5)
        s = s + bias_ref[off]
        mx = jnp.max(s, axis=-1, keepdims=True)
        p = jnp.exp(s - mx)
        den = jnp.sum(p, axis=-1, keepdims=True)
        o = jnp.where(own, jnp.dot(p.astype(BF16), vw, preferred_element_type=F32) / den, 0.0)
        acc = o[0:GRID_W]
        for h in range(1, heads):
            acc = acc + o[h * GRID_W:(h + 1) * GRID_W]
        o_ref[j * GRID_W:(j + 1) * GRID_W, :] = acc.astype(o_ref.dtype)


def _natten(qkv, bias_tab, batch, seq, width):
    rows = seq // GRID_W
    assert rows >= NA_KH
    heads = width // HEAD_DIM
    steps = rows // NA_ROWS_PER_STEP
    assert steps * NA_ROWS_PER_STEP == rows
    tq = NA_ROWS_PER_STEP * GRID_W
    kern = functools.partial(_natten_kernel, rows=rows, heads=heads)
    return pl.pallas_call(
        kern,
        grid=(batch, steps),
        in_specs=[pl.BlockSpec((tq, width), lambda b, r: (b * steps + r, 0)),
                  pl.BlockSpec((seq, width), lambda b, r: (b, 1)),
                  pl.BlockSpec((seq, width), lambda b, r: (b, 2)),
                  pl.BlockSpec(bias_tab.shape, lambda b, r: (0, 0, 0))],
        out_specs=pl.BlockSpec((tq, width), lambda b, r: (b * steps + r, 0)),
        out_shape=jax.ShapeDtypeStruct((batch * seq, width), BF16),
        compiler_params=_cparams(2),
        name="natten",
    )(qkv, qkv, qkv, bias_tab)


def _rwkv_prep_kernel(z_ref, zp_ref, zn_ref, lr_ref, cw_ref, w0_ref, wup_ref, a0_ref, aup_ref, gup_ref,
                      kk_ref, ka_ref, rk_ref, ones_ref,
                      r_o, v_o, nkk_o, lw_o, b_o, kd_o, bonus_o, g_o, *, tiles_per_batch, width):
    i = pl.program_id(0)
    tb = i % tiles_per_batch
    z = z_ref[...]
    tm = z.shape[0]
    prev = jnp.where(tb == 0, 0.0, zp_ref[HALO - 1:HALO, :])
    nxt = jnp.where(tb == tiles_per_batch - 1, 0.0, zn_ref[0:1, :])
    row = lax.broadcasted_iota(jnp.int32, z.shape, 0)
    zm1 = jnp.where(row == 0, prev, pltpu.roll(z, 1, 0))
    zp1 = jnp.where(row == tm - 1, nxt, pltpu.roll(z, tm - 1, 0))
    rkv = zm1 * cw_ref[0:1, :] + z * cw_ref[1:2, :] + zp1 * cw_ref[2:3, :]
    r = rkv[:, :width]
    k = rkv[:, width:2 * width]
    v = rkv[:, 2 * width:]
    lr = lr_ref[...]
    th = jnp.tanh(lr)
    sg = _sigmoid(lr)
    ones = ones_ref[...]

    def headsum(x):
        return _dot_split(x, ones)

    kk = k * kk_ref[...]
    kk = kk * lax.rsqrt(jnp.maximum(headsum(kk * kk), 1e-24))
    g_o[...] = jnp.dot(sg.astype(BF16), gup_ref[...], preferred_element_type=F32)
    r_o[...] = r
    v_o[...] = v
    nkk_o[...] = -kk
    bonus = jnp.zeros_like(r)
    th_b = th.astype(BF16)
    lr_b = lr.astype(BF16)
    for d in range(2):
        wl = jnp.dot(th_b, wup_ref[d], preferred_element_type=F32) + w0_ref[d:d + 1, :]
        lw_o[d] = -DECAY_SCALE * _sigmoid(wl)
        a = _sigmoid(jnp.dot(lr_b, aup_ref[d], preferred_element_type=F32) + a0_ref[d:d + 1, :])
        kd = k * (1.0 + (a - 1.0) * ka_ref[...])
        kd_o[d] = kd
        b_o[d] = kk * a
        bonus = bonus + headsum(r * kd * rk_ref[...]) * v
    bonus_o[...] = bonus


def _rwkv_prep(rkv_raw, lr, p, seq, tm):
    t, w3 = rkv_raw.shape
    width = w3 // 3
    tpb = seq // tm
    hb = tm // HALO
    nhb = t // HALO
    kern = functools.partial(_rwkv_prep_kernel, tiles_per_batch=tpb, width=width)
    tok = lambda i: (i, 0)
    dtok = lambda i: (0, i, 0)
    full2 = lambda i: (0, 0)
    full3 = lambda i: (0, 0, 0)
    tw = jax.ShapeDtypeStruct((t, width), F32)
    dtw = jax.ShapeDtypeStruct((2, t, width), F32)
    return pl.pallas_call(
        kern,
        grid=(t // tm,),
        in_specs=[pl.BlockSpec((tm, w3), tok),
                  pl.BlockSpec((HALO, w3), lambda i: (jnp.maximum(i * hb - 1, 0), 0)),
                  pl.BlockSpec((HALO, w3), lambda i: (jnp.minimum((i + 1) * hb, nhb - 1), 0)),
                  pl.BlockSpec((tm, lr.shape[1]), tok),
                  pl.BlockSpec(p["conv"].shape, full2),
                  pl.BlockSpec(p["w0"].shape, full2),
                  pl.BlockSpec(p["w_up"].shape, full3),
                  pl.BlockSpec(p["a0"].shape, full2),
                  pl.BlockSpec(p["a_up"].shape, full3),
                  pl.BlockSpec(p["g_up"].shape, full2),
                  pl.BlockSpec(p["k_k"].shape, full2),
                  pl.BlockSpec(p["k_a"].shape, full2),
                  pl.BlockSpec(p["r_k"].shape, full2),
                  pl.BlockSpec(p["ones"].shape, full2)],
        out_specs=[pl.BlockSpec((tm, width), tok), pl.BlockSpec((tm, width), tok), pl.BlockSpec((tm, width), tok),
                   pl.BlockSpec((2, tm, width), dtok), pl.BlockSpec((2, tm, width), dtok),
                   pl.BlockSpec((2, tm, width), dtok),
                   pl.BlockSpec((tm, width), tok), pl.BlockSpec((tm, width), tok)],
        out_shape=[tw, tw, tw, dtw, dtw, dtw, tw, tw],
        compiler_params=_cparams(1),
        name="rwkv_prep",
    )(rkv_raw, rkv_raw, rkv_raw, lr, p["conv"], p["w0"], p["w_up"], p["a0"], p["a_up"], p["g_up"],
      p["k_k"], p["k_a"], p["r_k"], p["ones"])


def _dot_nt(a, b):
    return lax.dot_general(a, b, (((1,), (1,)), ((), ())), preferred_element_type=F32)


def _dot_tn(a, b):
    return lax.dot_general(a, b, (((0,), (0,)), ((), ())), preferred_element_type=F32)


def _mm(a, b):
    return jnp.dot(a.astype(BF16), b.astype(BF16), preferred_element_type=F32)


def _rwkv_scan_kernel(rf_ref, vf_ref, nf_ref, rb_ref, vb_ref, nb_ref, lwf_ref, bf_ref, kf_ref, lwb_ref, bb_ref, kb_ref,
                      yf_ref, yb_ref, s_ref, *, heads, batch):
    @pl.when(pl.program_id(0) == 0)
    def _():
        s_ref[...] = jnp.zeros_like(s_ref)

    n = SCAN_CHUNK
    pair_w = 2 * HEAD_DIM
    row = lax.broadcasted_iota(jnp.int32, (n, pair_w), 0)
    lane = lax.broadcasted_iota(jnp.int32, (n, pair_w), 1)
    col = lane & (HEAD_DIM - 1)
    even = lane < HEAD_DIM
    levels = n.bit_length()
    same = [(row >> k) == (col >> k) for k in range(levels)]
    eye = same[0].astype(F32)
    level_masks = [same[sh + 1] & jnp.logical_not(same[sh]) for sh in range(1, levels - 1)]

    def blockdiag(x2):
        xb = x2.astype(BF16)
        zero = jnp.zeros((), BF16)
        return jnp.concatenate([jnp.where(even, xb, zero), jnp.where(even, zero, xb)], axis=0)

    def mm(x2, y2):
        return jnp.dot(x2.astype(BF16), blockdiag(y2), preferred_element_type=F32)

    def mm_nt(x2, y2):
        return _dot_nt(x2.astype(BF16), blockdiag(y2))

    dirs = ((rf_ref, vf_ref, nf_ref, lwf_ref, bf_ref, kf_ref, yf_ref),
            (rb_ref, vb_ref, nb_ref, lwb_ref, bb_ref, kb_ref, yb_ref))
    chains = []
    for d, (r_ref, v_ref, n_ref, lw_ref, b_ref, k_ref, y_ref) in enumerate(dirs):
        order = row - col if d == 0 else col - row
        strict = order > 0
        incl = order >= 0
        incl_b = jnp.where(incl[:, :n], 1.0, 0.0).astype(BF16)
        for bi in range(batch):
            lw = lw_ref[0, bi]
            lw_hi, lw_mid = _split_bf16(lw)
            lw_lo = (lw - lw_hi.astype(F32) - lw_mid.astype(F32)).astype(BF16)
            g_inc = ((jnp.dot(incl_b, lw_lo, preferred_element_type=F32)
                      + jnp.dot(incl_b, lw_mid, preferred_element_type=F32))
                     + jnp.dot(incl_b, lw_hi, preferred_element_type=F32))
            g_tot = jnp.sum(lw, axis=0, keepdims=True)
            e_neg = jnp.exp(-g_inc)
            e_end = jnp.exp(g_tot - g_inc)
            decay = jnp.exp(g_tot)
            a_t = n_ref[bi] * jnp.exp(g_inc - lw)
            r_t = r_ref[bi] * jnp.exp(g_inc)
            bb = b_ref[0, bi]
            kd = k_ref[0, bi]
            b_t = bb * e_neg
            k_t = kd * e_neg
            ar_t = jnp.concatenate([a_t, r_t], axis=0).astype(BF16)
            bk_h = jnp.concatenate([bb * e_end, kd * e_end], axis=0).astype(BF16)
            v = v_ref[bi]
            for p in range(heads // 2):
                sl = slice(p * pair_w, (p + 1) * pair_w)
                chains.append(dict(strict=strict, incl=incl, sl=sl, bi=bi, y_ref=y_ref,
                                   si=(d * batch + bi) * (heads // 2) + p, decay=decay[:, sl],
                                   ar=ar_t[:, sl], b=b_t[:, sl], k=k_t[:, sl], bk_h=bk_h[:, sl], v=v[:, sl]))

    for ch in chains:
        pb = mm_nt(ch["ar"], ch["b"])
        pk = mm_nt(ch["ar"], ch["k"])
        ch["l_ab"] = jnp.where(ch["strict"], pb[:n], 0.0)
        ch["m_rb"] = jnp.where(ch["incl"], pb[n:], 0.0)
        ch["l_ak"] = jnp.where(ch["strict"], pk[:n], 0.0)
        ch["m_rk"] = jnp.where(ch["incl"], pk[n:], 0.0)
        ch["t"] = eye + jnp.where(same[1], ch["l_ab"], 0.0)
    for mask in level_masks:
        for ch in chains:
            ch["tc"] = mm(ch["t"], jnp.where(mask, ch["l_ab"], 0.0))
        for ch in chains:
            ch["t"] = ch["t"] + mm(ch["tc"], ch["t"])
    for ch in chains:
        ch["s0"] = s_ref[ch["si"]]
        ch["x"] = mm_nt(ch["ar"], ch["s0"])
    for ch in chains:
        ch["rhs"] = ch["x"][:n] + mm(ch["l_ak"], ch["v"])
    for ch in chains:
        ch["u"] = mm(ch["t"], ch["rhs"])
    for ch in chains:
        y = ch["x"][n:] + mm(ch["m_rb"], ch["u"]) + mm(ch["m_rk"], ch["v"])
        ch["y_ref"][ch["bi"], :, ch["sl"]] = y
    for ch in chains:
        uv = jnp.concatenate([ch["u"], ch["v"]], axis=0).astype(BF16)
        full = _dot_tn(uv, ch["bk_h"])
        s_ref[ch["si"]] = ch["s0"] * ch["decay"] + jnp.where(even, full[:HEAD_DIM], full[HEAD_DIM:])


def _rwkv_scan(r, v, nkk, lw, b, kd, batch, seq):
    t, width = r.shape
    heads = width // HEAD_DIM
    n = SCAN_CHUNK
    nc = seq // n
    r3, v3, n3 = (z.reshape(batch, seq, width) for z in (r, v, nkk))
    lw4, b4, k4 = (z.reshape(2, batch, seq, width) for z in (lw, b, kd))
    fwd = pl.BlockSpec((batch, n, width), lambda c: (0, c, 0))
    bwd = pl.BlockSpec((batch, n, width), lambda c: (0, nc - 1 - c, 0))
    fwd_d = pl.BlockSpec((1, batch, n, width), lambda c: (0, 0, c, 0))
    bwd_d = pl.BlockSpec((1, batch, n, width), lambda c: (1, 0, nc - 1 - c, 0))
    kern = functools.partial(_rwkv_scan_kernel, heads=heads, batch=batch)
    yf, yb = pl.pallas_call(
        kern,
        grid=(nc,),
        in_specs=[fwd, fwd, fwd, bwd, bwd, bwd, fwd_d, fwd_d, fwd_d, bwd_d, bwd_d, bwd_d],
        out_specs=[fwd, bwd],
        out_shape=[jax.ShapeDtypeStruct((batch, seq, width), F32)] * 2,
        scratch_shapes=[pltpu.VMEM((batch * heads, HEAD_DIM, 2 * HEAD_DIM), F32)],
        compiler_params=_cparams(1, "arbitrary"),
        name="rwkv_scan",
    )(r3, v3, n3, r3, v3, n3, lw4, b4, k4, lw4, b4, k4)
    return yf.reshape(t, width), yb.reshape(t, width)


def _pool_kernel(p_ref, pp_ref, pn_ref, w_ref, sc_ref, o_ref, ext_ref, *, tiles_per_batch, seq):
    i = pl.program_id(0)
    tb = i % tiles_per_batch
    p = p_ref[...]
    tm, width = p.shape
    ext_ref[0:HALO, :] = jnp.where(tb == 0, 0.0, pp_ref[...])
    ext_ref[HALO:HALO + tm, :] = p
    ext_ref[HALO + tm:2 * HALO + tm, :] = jnp.where(tb == tiles_per_batch - 1, 0.0, pn_ref[...])

    def shifted(o):
        return ext_ref[HALO + o:HALO + o + tm, :]

    t = tb * tm + lax.broadcasted_iota(jnp.int32, (tm, width), 0)
    grp = lax.broadcasted_iota(jnp.int32, (tm, width), 1) // (width // len(POOL_WINDOWS))
    tot = p
    prev_half = 0
    pooled = jnp.zeros_like(p)
    for gi, win in enumerate(POOL_WINDOWS):
        half = win // 2
        for o in range(prev_half, half):
            tot = tot + shifted(-o - 1)
            if o > 0:
                tot = tot + shifted(o)
        prev_half = half
        lo = jnp.clip(t - half, 0, seq - 1)
        hi = jnp.clip(t + half - 1, 0, seq - 1)
        cnt = (hi - lo + 1).astype(F32)
        pooled = jnp.where(grp == gi, tot / cnt, pooled)
    pooled = pooled - p
    o_ref[...] = jnp.dot(pooled, w_ref[...], preferred_element_type=F32) * sc_ref[...]


def _pool(praw, w_blk, scale, seq, tm):
    t, width = praw.shape
    tpb = seq // tm
    hb = tm // HALO
    nhb = t // HALO
    kern = functools.partial(_pool_kernel, tiles_per_batch=tpb, seq=seq)
    return pl.pallas_call(
        kern,
        grid=(t // tm,),
        in_specs=[pl.BlockSpec((tm, width), lambda i: (i, 0)),
                  pl.BlockSpec((HALO, width), lambda i: (jnp.maximum(i * hb - 1, 0), 0)),
                  pl.BlockSpec((HALO, width), lambda i: (jnp.minimum((i + 1) * hb, nhb - 1), 0)),
                  pl.BlockSpec(w_blk.shape, lambda i: (0, 0)),
                  pl.BlockSpec(scale.shape, lambda i: (0, 0))],
        out_specs=pl.BlockSpec((tm, width), lambda i: (i, 0)),
        out_shape=jax.ShapeDtypeStruct((t, width), F32),
        scratch_shapes=[pltpu.VMEM((tm + 2 * HALO, width), F32)],
        compiler_params=_cparams(1),
        name="pool",
    )(praw, praw, praw, w_blk, scale)


def _outproj_kernel(ya_ref, yf_ref, yb_ref, bonus_ref, g_ref, yc_ref, x_ref, mod_ref, wa_ref, wb_ref, wc_ref,
                    gng_ref, gnb_ref, ones_ref, l1g_ref, l1b_ref, wr_ref, br_ref,
                    x1_o, u2_o, ri_o, rw_o, cnt_o, cnt_ref, *, alpha):
    m = mod_ref[0]
    ones = ones_ref[...]

    def headmean(x):
        return _dot_split(x, ones) * (1.0 / HEAD_DIM)

    ysum = yf_ref[...] + yb_ref[...]
    yc0 = ysum - headmean(ysum)
    yn = yc0 * lax.rsqrt(headmean(yc0 * yc0) + GN_EPS) * gng_ref[...] + gnb_ref[...]
    yb = (yn + bonus_ref[...]) * g_ref[...]
    mix = (jnp.dot(ya_ref[...].astype(BF16), wa_ref[...], preferred_element_type=F32)
           + jnp.dot(yb.astype(BF16), wb_ref[...], preferred_element_type=F32)
           + jnp.dot(yc_ref[...].astype(BF16), wc_ref[...], preferred_element_type=F32))
    x1 = _ln(alpha * x_ref[...] + m[2:3] * mix) * l1g_ref[...] + l1b_ref[...]
    x1_o[...] = x1
    u2 = _ln(x1) * (1.0 + m[4:5]) + m[3:4]
    u2_o[...] = u2

    u_hi, u_lo = _split_bf16(u2)
    hi_both = jnp.dot(u_hi, wr_ref[...], preferred_element_type=F32)
    lg = (hi_both[:, :LANES] + hi_both[:, LANES:]
          + jnp.dot(u_lo, wr_ref[:, :LANES], preferred_element_type=F32)) + br_ref[...]
    lane = lax.broadcasted_iota(jnp.int32, lg.shape, 1)
    big = jnp.int32(1 << 20)
    gl = jnp.where(lane < N_GROUPS, lg, -jnp.inf)
    gmax = jnp.max(gl, axis=-1, keepdims=True)
    gidx = jnp.min(jnp.where(gl == gmax, lane, big), axis=-1, keepdims=True)
    pg_sel = 1.0 / jnp.sum(jnp.exp(gl - gmax), axis=-1, keepdims=True)
    e_lo = N_GROUPS + gidx * EXPERTS_PER_GROUP
    el = jnp.where((lane >= e_lo) & (lane < e_lo + EXPERTS_PER_GROUP), lg, -jnp.inf)
    m1 = jnp.max(el, axis=-1, keepdims=True)
    i1 = jnp.min(jnp.where(el == m1, lane, big), axis=-1, keepdims=True)
    el2 = jnp.where(lane == i1, -jnp.inf, el)
    m2 = jnp.max(el2, axis=-1, keepdims=True)
    i2 = jnp.min(jnp.where(el2 == m2, lane, big), axis=-1, keepdims=True)
    e21 = jnp.exp(m2 - m1)
    p1 = 1.0 / (1.0 + e21)
    p2 = e21 / (1.0 + e21)
    rw_o[...] = jnp.where(lane == 0, pg_sel * p1, jnp.where(lane == 1, pg_sel * p2, 0.0))

    @pl.when(pl.program_id(0) == 0)
    def _():
        cnt_ref[...] = jnp.zeros_like(cnt_ref)

    tm = lg.shape[0]
    earlier = (lax.broadcasted_iota(jnp.int32, (tm, tm), 1)
               < lax.broadcasted_iota(jnp.int32, (tm, tm), 0)).astype(BF16)
    oh1 = (lane == i1).astype(F32)
    oh2 = (lane == i2).astype(F32)
    run = cnt_ref[...]
    c1 = jnp.sum(oh1, axis=0, keepdims=True)
    before1 = run + jnp.dot(earlier, oh1.astype(BF16), preferred_element_type=F32)
    before2 = run + c1 + jnp.dot(earlier, oh2.astype(BF16), preferred_element_type=F32)
    rank1 = jnp.sum(oh1 * before1, axis=-1, keepdims=True).astype(jnp.int32)
    rank2 = jnp.sum(oh2 * before2, axis=-1, keepdims=True).astype(jnp.int32)
    total = run + c1 + jnp.sum(oh2, axis=0, keepdims=True)
    cnt_ref[...] = total
    cnt_o[...] = total
    ri_o[...] = jnp.where(lane == 0, i1 - N_GROUPS, jnp.where(lane == 1, i2 - N_GROUPS,
                          jnp.where(lane == 2, rank1, jnp.where(lane == 3, rank2, 0))))


def _outproj(ya, yf, yb, bonus, g, yc, x2, modl, p, seq, tm, alpha):
    t, d = x2.shape
    tpb = seq // tm
    aw, bw, cw = ya.shape[1], bonus.shape[1], yc.shape[1]
    tok = lambda i: (i, 0)
    full2 = lambda i: (0, 0)
    kern = functools.partial(_outproj_kernel, alpha=alpha)
    small = ["gn_gain", "gn_bias", "ones", "ln1_gain", "ln1_bias", "w_router", "b_router"]
    return pl.pallas_call(
        kern,
        grid=(t // tm,),
        in_specs=[pl.BlockSpec((tm, aw), tok),
                  pl.BlockSpec((tm, bw), tok), pl.BlockSpec((tm, bw), tok),
                  pl.BlockSpec((tm, bw), tok), pl.BlockSpec((tm, bw), tok),
                  pl.BlockSpec((tm, cw), tok),
                  pl.BlockSpec((tm, d), tok),
                  pl.BlockSpec((1,) + modl.shape[1:], lambda i: (i // tpb, 0, 0)),
                  pl.BlockSpec(p["w_out_a"].shape, full2),
                  pl.BlockSpec(p["w_out_b"].shape, full2),
                  pl.BlockSpec(p["w_out_c"].shape, full2)]
                 + [pl.BlockSpec(p[k].shape, functools.partial(lambda nd, i: (0,) * nd, p[k].ndim)) for k in small],
        out_specs=[pl.BlockSpec((tm, d), tok), pl.BlockSpec((tm, d), tok),
                   pl.BlockSpec((tm, LANES), tok), pl.BlockSpec((tm, LANES), tok),
                   pl.BlockSpec((1, LANES), full2)],
        out_shape=[jax.ShapeDtypeStruct((t, d), F32), jax.ShapeDtypeStruct((t, d), F32),
                   jax.ShapeDtypeStruct((t, LANES), jnp.int32), jax.ShapeDtypeStruct((t, LANES), F32),
                   jax.ShapeDtypeStruct((1, LANES), F32)],
        scratch_shapes=[pltpu.VMEM((1, LANES), F32)],
        compiler_params=_cparams(1, "arbitrary"),
        name="outproj",
    )(ya, yf, yb, bonus, g, yc, x2, modl, p["w_out_a"], p["w_out_b"], p["w_out_c"], *[p[k] for k in small])


def _dispatch(route_i, counts_lanes, n_blocks):
    counts = counts_lanes[0, N_GROUPS:N_GROUPS + N_EXPERTS].astype(jnp.int32)
    padded = ((counts + EXPERT_BLOCK - 1) // EXPERT_BLOCK) * EXPERT_BLOCK
    pends = jnp.cumsum(padded)
    pstarts = pends - padded
    e = route_i[:, :TOP_K]
    rank = route_i[:, TOP_K:2 * TOP_K]
    ids = jnp.arange(N_EXPERTS, dtype=jnp.int32)
    dest = jnp.sum(jnp.where(e[..., None] == ids, pstarts, 0), axis=-1) + rank
    block_start = jnp.arange(n_blocks, dtype=jnp.int32) * EXPERT_BLOCK
    block_e = jnp.minimum(jnp.sum((pends[None, :] <= block_start[:, None]).astype(jnp.int32), axis=1), N_EXPERTS - 1)
    meta = jnp.concatenate([block_e, (pends[-1] // EXPERT_BLOCK)[None]]).astype(jnp.int32)
    return dest, meta, pends.astype(jnp.int32)


def _scatter_rows_kernel(pends_ref, dest_ref, u_ref, xs_ref, zeros_ref, sem, zsem):
    tm = u_ref.shape[0]

    @pl.when(pl.program_id(0) == 0)
    def _():
        zeros_ref[...] = jnp.zeros_like(zeros_ref)

        def tail_copy(e):
            tail = pl.ds(pl.multiple_of(pends_ref[e] - EXPERT_BLOCK, EXPERT_BLOCK), EXPERT_BLOCK)
            return pltpu.make_async_copy(zeros_ref, xs_ref.at[tail], zsem)

        def has_rows(e):
            return pends_ref[e] > (pends_ref[e - 1] if e > 0 else 0)

        def unused_copy(j):
            return pltpu.make_async_copy(zeros_ref, xs_ref.at[pl.ds(j * EXPERT_BLOCK, EXPERT_BLOCK)], zsem)

        def is_unused(j):
            return j * EXPERT_BLOCK >= pends_ref[N_EXPERTS - 1]

        n_blocks = xs_ref.shape[0] // EXPERT_BLOCK
        for e in range(N_EXPERTS):
            pl.when(has_rows(e))(lambda e=e: tail_copy(e).start())
        for j in range(n_blocks):
            pl.when(is_unused(j))(lambda j=j: unused_copy(j).start())
        for e in range(N_EXPERTS):
            pl.when(has_rows(e))(lambda e=e: tail_copy(e).wait())
        for j in range(n_blocks):
            pl.when(is_unused(j))(lambda j=j: unused_copy(j).wait())

    def issue(r, carry):
        for k in range(TOP_K):
            dst = dest_ref[0, 0, TOP_K * r + k]
            pltpu.make_async_copy(u_ref.at[pl.ds(r, 1)], xs_ref.at[pl.ds(dst, 1)], sem).start()
        return carry

    lax.fori_loop(0, tm, issue, 0, unroll=8)
    rows = pl.ds(0, TOP_K * tm)
    pltpu.make_async_copy(xs_ref.at[rows], xs_ref.at[rows], sem).wait()


def _scatter_rows(pends, u2, dest3, total, tm):
    t, d = u2.shape
    grid_spec = pltpu.PrefetchScalarGridSpec(
        num_scalar_prefetch=1,
        grid=(t // tm,),
        in_specs=[pl.BlockSpec((1, 1, TOP_K * tm), lambda i, p: (i, 0, 0), memory_space=pltpu.SMEM),
                  pl.BlockSpec((tm, d), lambda i, p: (i, 0))],
        out_specs=pl.BlockSpec(memory_space=pl.ANY),
        scratch_shapes=[pltpu.VMEM((EXPERT_BLOCK, d), F32), pltpu.SemaphoreType.DMA(()),
                        pltpu.SemaphoreType.DMA(())],
    )
    return pl.pallas_call(
        _scatter_rows_kernel,
        grid_spec=grid_spec,
        out_shape=jax.ShapeDtypeStruct((total, d), F32),
        compiler_params=_cparams(1, "arbitrary"),
        name="scatter_rows",
    )(pends, dest3, u2)


def _experts_kernel(meta_ref, xs_ref, wg_ref, wu_ref, wd_ref, o_ref, wg_b, wu_b, wd_b):
    i = pl.program_id(0)
    n_used = meta_ref[pl.num_programs(0)]

    @pl.when((i == 0) | (meta_ref[i] != meta_ref[jnp.maximum(i - 1, 0)]))
    def _():
        wg_b[...] = wg_ref[0, 0].astype(BF16)
        wu_b[...] = wu_ref[0, 0].astype(BF16)
        wd_b[...] = wd_ref[0, 0].astype(BF16)

    @pl.when(i < n_used)
    def _():
        xb = xs_ref[...].astype(BF16)
        gate = jnp.dot(xb, wg_b[...], preferred_element_type=F32)
        up = jnp.dot(xb, wu_b[...], preferred_element_type=F32)
        hb = gate * _sigmoid(gate) * up
        o_ref[...] = jnp.dot(hb.astype(BF16), wd_b[...], preferred_element_type=F32)

    @pl.when(i >= n_used)
    def _():
        o_ref[...] = jnp.zeros_like(o_ref)


def _experts(meta, xs, wg, wu, wd, layer):
    total, d = xs.shape
    nb = total // EXPERT_BLOCK
    de = wg.shape[3]
    grid_spec = pltpu.PrefetchScalarGridSpec(
        num_scalar_prefetch=1,
        grid=(nb,),
        in_specs=[pl.BlockSpec((EXPERT_BLOCK, d), lambda i, m: (jnp.minimum(i, m[nb] - 1), 0)),
                  pl.BlockSpec((1, 1, d, de), lambda i, m: (layer, m[i], 0, 0)),
                  pl.BlockSpec((1, 1, d, de), lambda i, m: (layer, m[i], 0, 0)),
                  pl.BlockSpec((1, 1, de, d), lambda i, m: (layer, m[i], 0, 0))],
        out_specs=pl.BlockSpec((EXPERT_BLOCK, d), lambda i, m: (i, 0)),
        scratch_shapes=[pltpu.VMEM((d, de), BF16), pltpu.VMEM((d, de), BF16), pltpu.VMEM((de, d), BF16)],
    )
    return pl.pallas_call(
        _experts_kernel,
        grid_spec=grid_spec,
        out_shape=jax.ShapeDtypeStruct((total, d), F32),
        compiler_params=_cparams(1, "arbitrary"),
        name="experts",
    )(meta, xs, wg, wu, wd)


def _final_kernel(dcur_ref, dnext_ref, x1_ref, rw_ref, mod_ref, g_ref, b_ref, ys_ref, o_ref, ybuf, sem, *, alpha):
    i = pl.program_id(0)
    tm = x1_ref.shape[0]
    slot = i % 2

    def gather(d_ref, s):
        def issue(r, carry):
            for k in range(TOP_K):
                src = d_ref[0, 0, TOP_K * r + k]
                pltpu.make_async_copy(ys_ref.at[pl.ds(src, 1)], ybuf.at[s, k, pl.ds(r, 1)], sem.at[s]).start()
            return carry

        lax.fori_loop(0, tm, issue, 0, unroll=8)

    @pl.when(i == 0)
    def _():
        gather(dcur_ref, 0)

    @pl.when(i + 1 < pl.num_programs(0))
    def _():
        gather(dnext_ref, 1 - slot)

    pltpu.make_async_copy(ybuf.at[slot], ybuf.at[slot], sem.at[slot]).wait()
    m = mod_ref[0]
    rw = rw_ref[...]
    f = rw[:, 0:1] * ybuf[slot, 0] + rw[:, 1:2] * ybuf[slot, 1]
    o_ref[...] = _ln(alpha * x1_ref[...] + m[5:6] * f) * g_ref[...] + b_ref[...]


def _final(x1, ysorted, dest3, rw, modl, gain, bias, seq, tm, alpha):
    t, d = x1.shape
    tpb = seq // tm
    n_tiles = t // tm
    tok = lambda i: (i, 0)
    kern = functools.partial(_final_kernel, alpha=alpha)
    dspec = lambda f: pl.BlockSpec((1, 1, TOP_K * tm), f, memory_space=pltpu.SMEM)
    return pl.pallas_call(
        kern,
        grid=(n_tiles,),
        in_specs=[dspec(lambda i: (i, 0, 0)), dspec(lambda i: (jnp.minimum(i + 1, n_tiles - 1), 0, 0)),
                  pl.BlockSpec((tm, d), tok), pl.BlockSpec((tm, LANES), tok),
                  pl.BlockSpec((1,) + modl.shape[1:], lambda i: (i // tpb, 0, 0)),
                  pl.BlockSpec(gain.shape, lambda i: (0, 0)), pl.BlockSpec(bias.shape, lambda i: (0, 0)),
                  pl.BlockSpec(memory_space=pl.ANY)],
        out_specs=pl.BlockSpec((tm, d), tok),
        out_shape=jax.ShapeDtypeStruct((t, d), F32),
        scratch_shapes=[pltpu.VMEM((2, TOP_K, tm, d), F32), pltpu.SemaphoreType.DMA((2,))],
        compiler_params=_cparams(1, "arbitrary"),
        name="final_ln",
    )(dest3, dest3, x1, rw, modl, gain, bias, ysorted)


def _block_diag(blocks):
    n, a, b = blocks.shape
    out = jnp.zeros((n * a, n * b), blocks.dtype)
    for i in range(n):
        out = out.at[i * a:(i + 1) * a, i * b:(i + 1) * b].set(blocks[i])
    return out


def _pad_rows(w, lo, total):
    return jnp.zeros((total, w.shape[-1]), w.dtype).at[lo:lo + w.shape[0]].set(w)


def kernel(x, c, w_mod, b_mod, w_in, na_rpb, rw_conv, rw_w0, rw_w_up, rw_a0, rw_a_up, rw_g_up, rw_k_k, rw_k_a, rw_r_k, rw_gn_gain, rw_gn_bias, pool_w, pool_scale, w_out, ln1_gain, ln1_bias, ln2_gain, ln2_bias, moe_w_group, moe_b_group, moe_w_expert, moe_b_expert, moe_w_gate, moe_w_up, moe_w_down):
    batch, seq, d = x.shape
    depth = w_mod.shape[0]
    t = batch * seq
    a_w = na_rpb.shape[1] * HEAD_DIM
    b_w = rw_w0.shape[-1]
    c_w = pool_scale.shape[-1]
    lr_w = R_W + R_A + R_G
    alpha = (2 * depth) ** 0.25
    tm = min(512, seq)
    tm_prep = min(256, seq)
    assert seq % tm == 0 and seq % SCAN_CHUNK == 0 and seq % GRID_W == 0 and lr_w == LANES

    mod = _modulation(c, w_mod, b_mod)
    ones_blk = _block_diag(jnp.ones((b_w // HEAD_DIM, HEAD_DIM, HEAD_DIM), BF16))
    row = lambda v: v.reshape(1, -1)

    x2 = x.reshape(t, d)
    for l in range(depth):
        modl = mod[l]
        qkv, rkv_raw, lr, praw = _inproj(x2, modl, w_in[l].astype(BF16), seq, tm, 3 * a_w, 3 * b_w, lr_w, c_w)
        ya = _natten(qkv, _na_bias_table(na_rpb[l]), batch, seq, a_w)
        prep_params = {
            "conv": rw_conv[l], "w0": rw_w0[l], "a0": rw_a0[l],
            "w_up": jnp.stack([_pad_rows(rw_w_up[l, dd], 0, lr_w) for dd in range(2)]).astype(BF16),
            "a_up": jnp.stack([_pad_rows(rw_a_up[l, dd], R_W, lr_w) for dd in range(2)]).astype(BF16),
            "g_up": _pad_rows(rw_g_up[l], R_W + R_A, lr_w).astype(BF16),
            "k_k": row(rw_k_k[l]), "k_a": row(rw_k_a[l]), "r_k": row(rw_r_k[l]), "ones": ones_blk,
        }
        r, v, nkk, lw, bb, kd, bonus, g = _rwkv_prep(rkv_raw, lr, prep_params, seq, tm_prep)
        yf, yb = _rwkv_scan(r, v, nkk, lw, bb, kd, batch, seq)
        yc = _pool(praw, _block_diag(pool_w[l]), row(pool_scale[l]), seq, tm)
        w_router = jnp.zeros((d, LANES), F32).at[:, :N_GROUPS].set(moe_w_group[l])
        w_router = w_router.at[:, N_GROUPS:N_GROUPS + N_EXPERTS].set(moe_w_expert[l])
        b_router = jnp.zeros((1, LANES), F32).at[0, :N_GROUPS].set(moe_b_group[l])
        b_router = b_router.at[0, N_GROUPS:N_GROUPS + N_EXPERTS].set(moe_b_expert[l])
        wo = w_out[l].astype(BF16)
        out_params = {
            "w_out_a": wo[:a_w], "w_out_b": wo[a_w:a_w + b_w], "w_out_c": wo[a_w + b_w:],
            "gn_gain": row(rw_gn_gain[l]), "gn_bias": row(rw_gn_bias[l]), "ones": ones_blk,
            "ln1_gain": row(ln1_gain[l]), "ln1_bias": row(ln1_bias[l]),
            "w_router": jnp.concatenate(_split_bf16(w_router), axis=1), "b_router": b_router,
        }
        x1, u2, route_i, route_w, counts = _outproj(ya, yf, yb, bonus, g, yc, x2, modl, out_params, seq, tm, alpha)
        n_blocks = -(-(t * TOP_K) // EXPERT_BLOCK) + N_EXPERTS
        dest, meta, pends = _dispatch(route_i, counts, n_blocks)
        dest3 = dest.reshape(t // tm, 1, TOP_K * tm)
        xs = _scatter_rows(pends, u2, dest3, n_blocks * EXPERT_BLOCK, tm)
        ysorted = _experts(meta, xs, moe_w_gate, moe_w_up, moe_w_down, l)
        x2 = _final(x1, ysorted, dest3, route_w, modl, row(ln2_gain[l]), row(ln2_bias[l]), seq, tm, alpha)
    return x2.reshape(batch, seq, d)
```

```python
import functools
import math

import jax
import jax.numpy as jnp
import numpy as np
from jax import lax
from jax.experimental import pallas as pl
from jax.experimental.pallas import tpu as pltpu

F32 = jnp.float32
BF16 = jnp.bfloat16
HI = lax.Precision.HIGHEST

GRID_W = 64
HEAD_DIM = 64
NA_KH = 8
NA_KW = 16
POOL_WINDOWS = (2, 4, 8, 16)
R_W = 32
R_A = 32
R_G = 64
DECAY_SCALE = math.exp(-0.5)
GN_EPS = 64e-5
N_GROUPS = 4
EXPERTS_PER_GROUP = 8
N_EXPERTS = N_GROUPS * EXPERTS_PER_GROUP
TOP_K = 2
EXPERT_BLOCK = 512
LN_EPS = 1e-5
NEG_INF = -1e30

NA_ROWS_PER_STEP = 4
SCAN_CHUNK = 64
SUBLANES = 8
HALO = 8
LANES = 128
VMEM_LIMIT = 52 * 1024 * 1024


def _ln(x):
    mu = jnp.mean(x, axis=-1, keepdims=True)
    xc = x - mu
    var = jnp.mean(xc * xc, axis=-1, keepdims=True)
    return xc * lax.rsqrt(var + LN_EPS)


def _sigmoid(x):
    return 1.0 / (1.0 + jnp.exp(-x))


def _split_bf16(x):
    hi = x.astype(BF16)
    return hi, (x - hi.astype(F32)).astype(BF16)


def _dot_split(x, w_exact):
    hi, lo = _split_bf16(x)
    return jnp.dot(hi, w_exact, preferred_element_type=F32) + jnp.dot(lo, w_exact, preferred_element_type=F32)


def _cparams(n_axes, semantics="parallel"):
    return pltpu.CompilerParams(dimension_semantics=(semantics,) * n_axes, vmem_limit_bytes=VMEM_LIMIT)


def _mod_kernel(c_ref, w_ref, b_ref, o_ref):
    c = c_ref[...]
    s = c * _sigmoid(c)
    o_ref[0] = jnp.dot(s, w_ref[0], precision=HI, preferred_element_type=F32) + b_ref[0]


def _modulation(c, w_mod, b_mod):
    n_layers, d, d6 = w_mod.shape
    b = c.shape[0]
    bp = -(-b // 8) * 8
    cp = jnp.zeros((bp, d), F32).at[:b].set(c)
    out = pl.pallas_call(
        _mod_kernel,
        grid=(n_layers, d6 // d),
        in_specs=[pl.BlockSpec((bp, d), lambda l, j: (0, 0)),
                  pl.BlockSpec((1, d, d), lambda l, j: (l, 0, j)),
                  pl.BlockSpec((1, 1, d), lambda l, j: (l, 0, j))],
        out_specs=pl.BlockSpec((1, bp, d), lambda l, j: (l, 0, j)),
        out_shape=jax.ShapeDtypeStruct((n_layers, bp, d6), F32),
        compiler_params=_cparams(2),
        name="modulation",
    )(cp, w_mod, b_mod.reshape(n_layers, 1, d6))
    return out[:, :b].reshape(n_layers, b, d6 // d, d)


def _inproj_kernel(x_ref, mod_ref, w_ref, qkv_ref, rkv_ref, lr_ref, pool_ref, *, a3, b3, lr_w):
    m = mod_ref[0]
    u = _ln(x_ref[...]) * (1.0 + m[1:2]) + m[0:1]
    h = jnp.dot(u.astype(BF16), w_ref[...], preferred_element_type=F32)
    qkv_ref[...] = h[:, :a3].astype(BF16)
    rkv_ref[...] = h[:, a3:a3 + b3]
    lr_ref[...] = h[:, a3 + b3:a3 + b3 + lr_w]
    pool_ref[...] = h[:, a3 + b3 + lr_w:]


def _inproj(x2, modl, w_in_bf, seq, tm, a3, b3, lr_w, c_w):
    t, d = x2.shape
    tpb = seq // tm
    kern = functools.partial(_inproj_kernel, a3=a3, b3=b3, lr_w=lr_w)
    return pl.pallas_call(
        kern,
        grid=(t // tm,),
        in_specs=[pl.BlockSpec((tm, d), lambda i: (i, 0)),
                  pl.BlockSpec((1,) + modl.shape[1:], lambda i: (i // tpb, 0, 0)),
                  pl.BlockSpec(w_in_bf.shape, lambda i: (0, 0))],
        out_specs=[pl.BlockSpec((tm, a3), lambda i: (i, 0)),
                   pl.BlockSpec((tm, b3), lambda i: (i, 0)),
                   pl.BlockSpec((tm, lr_w), lambda i: (i, 0)),
                   pl.BlockSpec((tm, c_w), lambda i: (i, 0))],
        out_shape=[jax.ShapeDtypeStruct((t, a3), BF16),
                   jax.ShapeDtypeStruct((t, b3), F32),
                   jax.ShapeDtypeStruct((t, lr_w), F32),
                   jax.ShapeDtypeStruct((t, c_w), F32)],
        compiler_params=_cparams(1),
        name="inproj",
    )(x2, modl, w_in_bf)


def _na_bias_table(rpb):
    col = np.arange(GRID_W)
    cstart = np.clip(col - NA_KW // 2, 0, GRID_W - NA_KW)
    in_win = (col[None, :] >= cstart[:, None]) & (col[None, :] < cstart[:, None] + NA_KW)
    dc = np.clip(col[None, :] - col[:, None], -(NA_KW - 1), NA_KW - 1) + (NA_KW - 1)
    pick = (dc[None] == np.arange(2 * NA_KW - 1)[:, None, None]).astype(np.float32)
    cols = jnp.einsum("hrc,cqk->hrqk", rpb.astype(F32), pick, precision=HI)
    cols = jnp.where(in_win, cols, NEG_INF)
    b = jnp.stack([cols[:, NA_KH - 1 - o:2 * NA_KH - 1 - o] for o in range(NA_KH)])
    h = rpb.shape[0]
    return jnp.transpose(b, (0, 1, 3, 2, 4)).reshape(NA_KH, h * GRID_W, NA_KH * GRID_W)


def _natten_kernel(q_ref, k_ref, v_ref, bias_ref, o_ref, *, rows, heads):
    width = q_ref.shape[1]
    nk = NA_KH * GRID_W
    head_of_lane = lax.broadcasted_iota(jnp.int32, (heads * GRID_W, width), 1) // HEAD_DIM
    head_of_row = lax.broadcasted_iota(jnp.int32, (heads * GRID_W, width), 0) // GRID_W
    own = head_of_lane == head_of_row
    for j in range(NA_ROWS_PER_STEP):
        r = pl.program_id(1) * NA_ROWS_PER_STEP + j
        rstart = jnp.clip(r - NA_KH // 2, 0, rows - NA_KH)
        off = r - rstart
        start = pl.multiple_of(rstart * GRID_W, GRID_W)
        kw = k_ref[pl.ds(start, nk), :]
        vw = v_ref[pl.ds(start, nk), :]
        q = q_ref[j * GRID_W:(j + 1) * GRID_W, :]
        qs = jnp.where(own, jnp.concatenate([q] * heads, axis=0), jnp.zeros((), q.dtype))
        s = lax.dot_general(qs, kw, (((1,), (1,)), ((), ())), preferred_element_type=F32) * (HEAD_DIM ** -0.5)
        s = s + bias_ref[off]
        mx = jnp.max(s, axis=-1, keepdims=True)
        p = jnp.exp(s - mx)
        den = jnp.sum(p, axis=-1, keepdims=True)
        o = jnp.where(own, jnp.dot(p.astype(BF16), vw, preferred_element_type=F32) / den, 0.0)
        acc = o[0:GRID_W]
        for h in range(1, heads):
            acc = acc + o[h * GRID_W:(h + 1) * GRID_W]
        o_ref[j * GRID_W:(j + 1) * GRID_W, :] = acc.astype(o_ref.dtype)


def _natten(qkv, bias_tab, batch, seq, width):
    rows = seq // GRID_W
    assert rows >= NA_KH
    heads = width // HEAD_DIM
    steps = rows // NA_ROWS_PER_STEP
    assert steps * NA_ROWS_PER_STEP == rows
    tq = NA_ROWS_PER_STEP * GRID_W
    kern = functools.partial(_natten_kernel, rows=rows, heads=heads)
    return pl.pallas_call(
        kern,
        grid=(batch, steps),
        in_specs=[pl.BlockSpec((tq, width), lambda b, r: (b * steps + r, 0)),
                  pl.BlockSpec((seq, width), lambda b, r: (b, 1)),
                  pl.BlockSpec((seq, width), lambda b, r: (b, 2)),
                  pl.BlockSpec(bias_tab.shape, lambda b, r: (0, 0, 0))],
        out_specs=pl.BlockSpec((tq, width), lambda b, r: (b * steps + r, 0)),
        out_shape=jax.ShapeDtypeStruct((batch * seq, width), BF16),
        compiler_params=_cparams(2),
        name="natten",
    )(qkv, qkv, qkv, bias_tab)


def _rwkv_prep_kernel(z_ref, zp_ref, zn_ref, lr_ref, cw_ref, w0_ref, wup_ref, a0_ref, aup_ref, gup_ref,
                      kk_ref, ka_ref, rk_ref, ones_ref,
                      r_o, v_o, nkk_o, lw_o, b_o, kd_o, bonus_o, g_o, *, tiles_per_batch, width):
    i = pl.program_id(0)
    tb = i % tiles_per_batch
    z = z_ref[...]
    tm = z.shape[0]
    prev = jnp.where(tb == 0, 0.0, zp_ref[HALO - 1:HALO, :])
    nxt = jnp.where(tb == tiles_per_batch - 1, 0.0, zn_ref[0:1, :])
    row = lax.broadcasted_iota(jnp.int32, z.shape, 0)
    zm1 = jnp.where(row == 0, prev, pltpu.roll(z, 1, 0))
    zp1 = jnp.where(row == tm - 1, nxt, pltpu.roll(z, tm - 1, 0))
    rkv = zm1 * cw_ref[0:1, :] + z * cw_ref[1:2, :] + zp1 * cw_ref[2:3, :]
    r = rkv[:, :width]
    k = rkv[:, width:2 * width]
    v = rkv[:, 2 * width:]
    lr = lr_ref[...]
    th = jnp.tanh(lr)
    sg = _sigmoid(lr)
    ones = ones_ref[...]

    def headsum(x):
        return _dot_split(x, ones)

    kk = k * kk_ref[...]
    kk = kk * lax.rsqrt(jnp.maximum(headsum(kk * kk), 1e-24))
    g_o[...] = jnp.dot(sg.astype(BF16), gup_ref[...], preferred_element_type=F32)
    r_o[...] = r
    v_o[...] = v
    nkk_o[...] = -kk
    bonus = jnp.zeros_like(r)
    th_b = th.astype(BF16)
    lr_b = lr.astype(BF16)
    for d in range(2):
        wl = jnp.dot(th_b, wup_ref[d], preferred_element_type=F32) + w0_ref[d:d + 1, :]
        lw_o[d] = -DECAY_SCALE * _sigmoid(wl)
        a = _sigmoid(jnp.dot(lr_b, aup_ref[d], preferred_element_type=F32) + a0_ref[d:d + 1, :])
        kd = k * (1.0 + (a - 1.0) * ka_ref[...])
        kd_o[d] = kd
        b_o[d] = kk * a
        bonus = bonus + headsum(r * kd * rk_ref[...]) * v
    bonus_o[...] = bonus


def _rwkv_prep(rkv_raw, lr, p, seq, tm):
    t, w3 = rkv_raw.shape
    width = w3 // 3
    tpb = seq // tm
    hb = tm // HALO
    nhb = t // HALO
    kern = functools.partial(_rwkv_prep_kernel, tiles_per_batch=tpb, width=width)
    tok = lambda i: (i, 0)
    dtok = lambda i: (0, i, 0)
    full2 = lambda i: (0, 0)
    full3 = lambda i: (0, 0, 0)
    tw = jax.ShapeDtypeStruct((t, width), F32)
    dtw = jax.ShapeDtypeStruct((2, t, width), F32)
    return pl.pallas_call(
        kern,
        grid=(t // tm,),
        in_specs=[pl.BlockSpec((tm, w3), tok),
                  pl.BlockSpec((HALO, w3), lambda i: (jnp.maximum(i * hb - 1, 0), 0)),
                  pl.BlockSpec((HALO, w3), lambda i: (jnp.minimum((i + 1) * hb, nhb - 1), 0)),
                  pl.BlockSpec((tm, lr.shape[1]), tok),
                  pl.BlockSpec(p["conv"].shape, full2),
                  pl.BlockSpec(p["w0"].shape, full2),
                  pl.BlockSpec(p["w_up"].shape, full3),
                  pl.BlockSpec(p["a0"].shape, full2),
                  pl.BlockSpec(p["a_up"].shape, full3),
                  pl.BlockSpec(p["g_up"].shape, full2),
                  pl.BlockSpec(p["k_k"].shape, full2),
                  pl.BlockSpec(p["k_a"].shape, full2),
                  pl.BlockSpec(p["r_k"].shape, full2),
                  pl.BlockSpec(p["ones"].shape, full2)],
        out_specs=[pl.BlockSpec((tm, width), tok), pl.BlockSpec((tm, width), tok), pl.BlockSpec((tm, width), tok),
                   pl.BlockSpec((2, tm, width), dtok), pl.BlockSpec((2, tm, width), dtok),
                   pl.BlockSpec((2, tm, width), dtok),
                   pl.BlockSpec((tm, width), tok), pl.BlockSpec((tm, width), tok)],
        out_shape=[tw, tw, tw, dtw, dtw, dtw, tw, tw],
        compiler_params=_cparams(1),
        name="rwkv_prep",
    )(rkv_raw, rkv_raw, rkv_raw, lr, p["conv"], p["w0"], p["w_up"], p["a0"], p["a_up"], p["g_up"],
      p["k_k"], p["k_a"], p["r_k"], p["ones"])


def _dot_nt(a, b):
    return lax.dot_general(a, b, (((1,), (1,)), ((), ())), preferred_element_type=F32)


def _dot_tn(a, b):
    return lax.dot_general(a, b, (((0,), (0,)), ((), ())), preferred_element_type=F32)


def _mm(a, b):
    return jnp.dot(a.astype(BF16), b.astype(BF16), preferred_element_type=F32)


def _rwkv_scan_kernel(rf_ref, vf_ref, nf_ref, rb_ref, vb_ref, nb_ref, lwf_ref, bf_ref, kf_ref, lwb_ref, bb_ref, kb_ref,
                      yf_ref, yb_ref, s_ref, *, heads, batch):
    @pl.when(pl.program_id(0) == 0)
    def _():
        s_ref[...] = jnp.zeros_like(s_ref)

    n = SCAN_CHUNK
    pair_w = 2 * HEAD_DIM
    row = lax.broadcasted_iota(jnp.int32, (n, pair_w), 0)
    lane = lax.broadcasted_iota(jnp.int32, (n, pair_w), 1)
    col = lane & (HEAD_DIM - 1)
    even = lane < HEAD_DIM
    levels = n.bit_length()
    same = [(row >> k) == (col >> k) for k in range(levels)]
    eye = same[0].astype(F32)
    level_masks = [same[sh + 1] & jnp.logical_not(same[sh]) for sh in range(1, levels - 1)]

    def blockdiag(x2):
        xb = x2.astype(BF16)
        zero = jnp.zeros((), BF16)
        return jnp.concatenate([jnp.where(even, xb, zero), jnp.where(even, zero, xb)], axis=0)

    def mm(x2, y2):
        return jnp.dot(x2.astype(BF16), blockdiag(y2), preferred_element_type=F32)

    def mm_nt(x2, y2):
        return _dot_nt(x2.astype(BF16), blockdiag(y2))

    dirs = ((rf_ref, vf_ref, nf_ref, lwf_ref, bf_ref, kf_ref, yf_ref),
            (rb_ref, vb_ref, nb_ref, lwb_ref, bb_ref, kb_ref, yb_ref))
    chains = []
    for d, (r_ref, v_ref, n_ref, lw_ref, b_ref, k_ref, y_ref) in enumerate(dirs):
        order = row - col if d == 0 else col - row
        strict = order > 0
        incl = order >= 0
        incl_b = jnp.where(incl[:, :n], 1.0, 0.0).astype(BF16)
        for bi in range(batch):
            lw = lw_ref[0, bi]
            lw_hi, lw_mid = _split_bf16(lw)
            lw_lo = (lw - lw_hi.astype(F32) - lw_mid.astype(F32)).astype(BF16)
            g_inc = ((jnp.dot(incl_b, lw_lo, preferred_element_type=F32)
                      + jnp.dot(incl_b, lw_mid, preferred_element_type=F32))
                     + jnp.dot(incl_b, lw_hi, preferred_element_type=F32))
            g_tot = jnp.sum(lw, axis=0, keepdims=True)
            e_neg = jnp.exp(-g_inc)
            e_end = jnp.exp(g_tot - g_inc)
            decay = jnp.exp(g_tot)
            a_t = n_ref[bi] * jnp.exp(g_inc - lw)
            r_t = r_ref[bi] * jnp.exp(g_inc)
            bb = b_ref[0, bi]
            kd = k_ref[0, bi]
            b_t = bb * e_neg
            k_t = kd * e_neg
            ar_t = jnp.concatenate([a_t, r_t], axis=0).astype(BF16)
            bk_h = jnp.concatenate([bb * e_end, kd * e_end], axis=0).astype(BF16)
            v = v_ref[bi]
            for p in range(heads // 2):
                sl = slice(p * pair_w, (p + 1) * pair_w)
                chains.append(dict(strict=strict, incl=incl, sl=sl, bi=bi, y_ref=y_ref,
                                   si=(d * batch + bi) * (heads // 2) + p, decay=decay[:, sl],
                                   ar=ar_t[:, sl], b=b_t[:, sl], k=k_t[:, sl], bk_h=bk_h[:, sl], v=v[:, sl]))

    for ch in chains:
        pb = mm_nt(ch["ar"], ch["b"])
        pk = mm_nt(ch["ar"], ch["k"])
        ch["l_ab"] = jnp.where(ch["strict"], pb[:n], 0.0)
        ch["m_rb"] = jnp.where(ch["incl"], pb[n:], 0.0)
        ch["l_ak"] = jnp.where(ch["strict"], pk[:n], 0.0)
        ch["m_rk"] = jnp.where(ch["incl"], pk[n:], 0.0)
        ch["t"] = eye + jnp.where(same[1], ch["l_ab"], 0.0)
    for mask in level_masks:
        for ch in chains:
            ch["tc"] = mm(ch["t"], jnp.where(mask, ch["l_ab"], 0.0))
        for ch in chains:
            ch["t"] = ch["t"] + mm(ch["tc"], ch["t"])
    for ch in chains:
        ch["s0"] = s_ref[ch["si"]]
        ch["x"] = mm_nt(ch["ar"], ch["s0"])
    for ch in chains:
        ch["rhs"] = ch["x"][:n] + mm(ch["l_ak"], ch["v"])
    for ch in chains:
        ch["u"] = mm(ch["t"], ch["rhs"])
    for ch in chains:
        y = ch["x"][n:] + mm(ch["m_rb"], ch["u"]) + mm(ch["m_rk"], ch["v"])
        ch["y_ref"][ch["bi"], :, ch["sl"]] = y
    for ch in chains:
        uv = jnp.concatenate([ch["u"], ch["v"]], axis=0).astype(BF16)
        full = _dot_tn(uv, ch["bk_h"])
        s_ref[ch["si"]] = ch["s0"] * ch["decay"] + jnp.where(even, full[:HEAD_DIM], full[HEAD_DIM:])


def _rwkv_scan(r, v, nkk, lw, b, kd, batch, seq):
    t, width = r.shape
    heads = width // HEAD_DIM
    n = SCAN_CHUNK
    nc = seq // n
    r3, v3, n3 = (z.reshape(batch, seq, width) for z in (r, v, nkk))
    lw4, b4, k4 = (z.reshape(2, batch, seq, width) for z in (lw, b, kd))
    fwd = pl.BlockSpec((batch, n, width), lambda c: (0, c, 0))
    bwd = pl.BlockSpec((batch, n, width), lambda c: (0, nc - 1 - c, 0))
    fwd_d = pl.BlockSpec((1, batch, n, width), lambda c: (0, 0, c, 0))
    bwd_d = pl.BlockSpec((1, batch, n, width), lambda c: (1, 0, nc - 1 - c, 0))
    kern = functools.partial(_rwkv_scan_kernel, heads=heads, batch=batch)
    yf, yb = pl.pallas_call(
        kern,
        grid=(nc,),
        in_specs=[fwd, fwd, fwd, bwd, bwd, bwd, fwd_d, fwd_d, fwd_d, bwd_d, bwd_d, bwd_d],
        out_specs=[fwd, bwd],
        out_shape=[jax.ShapeDtypeStruct((batch, seq, width), F32)] * 2,
        scratch_shapes=[pltpu.VMEM((batch * heads, HEAD_DIM, 2 * HEAD_DIM), F32)],
        compiler_params=_cparams(1, "arbitrary"),
        name="rwkv_scan",
    )(r3, v3, n3, r3, v3, n3, lw4, b4, k4, lw4, b4, k4)
    return yf.reshape(t, width), yb.reshape(t, width)


def _pool_tile(p_ref, pp_ref, pn_ref, w_ref, sc_ref, ext_ref, tb, tiles_per_batch, seq):
    p = p_ref[...]
    tm, width = p.shape
    ext_ref[0:HALO, :] = jnp.where(tb == 0, 0.0, pp_ref[...])
    ext_ref[HALO:HALO + tm, :] = p
    ext_ref[HALO + tm:2 * HALO + tm, :] = jnp.where(tb == tiles_per_batch - 1, 0.0, pn_ref[...])

    def shifted(o):
        return ext_ref[HALO + o:HALO + o + tm, :]

    t = tb * tm + lax.broadcasted_iota(jnp.int32, (tm, width), 0)
    grp = lax.broadcasted_iota(jnp.int32, (tm, width), 1) // (width // len(POOL_WINDOWS))
    tot = p
    prev_half = 0
    pooled = jnp.zeros_like(p)
    for gi, win in enumerate(POOL_WINDOWS):
        half = win // 2
        for o in range(prev_half, half):
            tot = tot + shifted(-o - 1)
            if o > 0:
                tot = tot + shifted(o)
        prev_half = half
        lo = jnp.clip(t - half, 0, seq - 1)
        hi = jnp.clip(t + half - 1, 0, seq - 1)
        cnt = (hi - lo + 1).astype(F32)
        pooled = jnp.where(grp == gi, tot / cnt, pooled)
    pooled = pooled - p
    return jnp.dot(pooled, w_ref[...], preferred_element_type=F32) * sc_ref[...]


def _outproj_kernel(ya_ref, yf_ref, yb_ref, bonus_ref, g_ref, p_ref, pp_ref, pn_ref, x_ref, mod_ref,
                    wa_ref, wb_ref, wc_ref, pw_ref, psc_ref,
                    gng_ref, gnb_ref, ones_ref, l1g_ref, l1b_ref, wr_ref, br_ref,
                    x1_o, u2_o, ri_o, rw_o, cnt_o, cnt_ref, ext_ref, *, alpha, tiles_per_batch, seq):
    yc = _pool_tile(p_ref, pp_ref, pn_ref, pw_ref, psc_ref, ext_ref,
                    pl.program_id(0) % tiles_per_batch, tiles_per_batch, seq)
    m = mod_ref[0]
    ones = ones_ref[...]

    def headmean(x):
        return _dot_split(x, ones) * (1.0 / HEAD_DIM)

    ysum = yf_ref[...] + yb_ref[...]
    yc0 = ysum - headmean(ysum)
    yn = yc0 * lax.rsqrt(headmean(yc0 * yc0) + GN_EPS) * gng_ref[...] + gnb_ref[...]
    yb = (yn + bonus_ref[...]) * g_ref[...]
    mix = (jnp.dot(ya_ref[...].astype(BF16), wa_ref[...], preferred_element_type=F32)
           + jnp.dot(yb.astype(BF16), wb_ref[...], preferred_element_type=F32)
           + jnp.dot(yc.astype(BF16), wc_ref[...], preferred_element_type=F32))
    x1 = _ln(alpha * x_ref[...] + m[2:3] * mix) * l1g_ref[...] + l1b_ref[...]
    x1_o[...] = x1
    u2 = _ln(x1) * (1.0 + m[4:5]) + m[3:4]
    u2_o[...] = u2

    u_hi, u_lo = _split_bf16(u2)
    hi_both = jnp.dot(u_hi, wr_ref[...], preferred_element_type=F32)
    lg = (hi_both[:, :LANES] + hi_both[:, LANES:]
          + jnp.dot(u_lo, wr_ref[:, :LANES], preferred_element_type=F32)) + br_ref[...]
    lane = lax.broadcasted_iota(jnp.int32, lg.shape, 1)
    big = jnp.int32(1 << 20)
    gl = jnp.where(lane < N_GROUPS, lg, -jnp.inf)
    gmax = jnp.max(gl, axis=-1, keepdims=True)
    gidx = jnp.min(jnp.where(gl == gmax, lane, big), axis=-1, keepdims=True)
    pg_sel = 1.0 / jnp.sum(jnp.exp(gl - gmax), axis=-1, keepdims=True)
    e_lo = N_GROUPS + gidx * EXPERTS_PER_GROUP
    el = jnp.where((lane >= e_lo) & (lane < e_lo + EXPERTS_PER_GROUP), lg, -jnp.inf)
    m1 = jnp.max(el, axis=-1, keepdims=True)
    i1 = jnp.min(jnp.where(el == m1, lane, big), axis=-1, keepdims=True)
    el2 = jnp.where(lane == i1, -jnp.inf, el)
    m2 = jnp.max(el2, axis=-1, keepdims=True)
    i2 = jnp.min(jnp.where(el2 == m2, lane, big), axis=-1, keepdims=True)
    e21 = jnp.exp(m2 - m1)
    p1 = 1.0 / (1.0 + e21)
    p2 = e21 / (1.0 + e21)
    rw_o[...] = jnp.where(lane == 0, pg_sel * p1, jnp.where(lane == 1, pg_sel * p2, 0.0))

    @pl.when(pl.program_id(0) == 0)
    def _():
        cnt_ref[...] = jnp.zeros_like(cnt_ref)

    tm = lg.shape[0]
    earlier = (lax.broadcasted_iota(jnp.int32, (tm, tm), 1)
               < lax.broadcasted_iota(jnp.int32, (tm, tm), 0)).astype(BF16)
    oh1 = (lane == i1).astype(F32)
    oh2 = (lane == i2).astype(F32)
    run = cnt_ref[...]
    c1 = jnp.sum(oh1, axis=0, keepdims=True)
    before1 = run + jnp.dot(earlier, oh1.astype(BF16), preferred_element_type=F32)
    before2 = run + c1 + jnp.dot(earlier, oh2.astype(BF16), preferred_element_type=F32)
    rank1 = jnp.sum(oh1 * before1, axis=-1, keepdims=True).astype(jnp.int32)
    rank2 = jnp.sum(oh2 * before2, axis=-1, keepdims=True).astype(jnp.int32)
    total = run + c1 + jnp.sum(oh2, axis=0, keepdims=True)
    cnt_ref[...] = total
    cnt_o[...] = total
    ri_o[...] = jnp.where(lane == 0, i1 - N_GROUPS, jnp.where(lane == 1, i2 - N_GROUPS,
                          jnp.where(lane == 2, rank1, jnp.where(lane == 3, rank2, 0))))


def _outproj(ya, yf, yb, bonus, g, praw, x2, modl, p, seq, tm, alpha):
    t, d = x2.shape
    tpb = seq // tm
    aw, bw, cw = ya.shape[1], bonus.shape[1], praw.shape[1]
    hb = tm // HALO
    nhb = t // HALO
    tok = lambda i: (i, 0)
    full2 = lambda i: (0, 0)
    kern = functools.partial(_outproj_kernel, alpha=alpha, tiles_per_batch=tpb, seq=seq)
    small = ["w_out_a", "w_out_b", "w_out_c", "pool_w", "pool_scale",
             "gn_gain", "gn_bias", "ones", "ln1_gain", "ln1_bias", "w_router", "b_router"]
    return pl.pallas_call(
        kern,
        grid=(t // tm,),
        in_specs=[pl.BlockSpec((tm, aw), tok),
                  pl.BlockSpec((tm, bw), tok), pl.BlockSpec((tm, bw), tok),
                  pl.BlockSpec((tm, bw), tok), pl.BlockSpec((tm, bw), tok),
                  pl.BlockSpec((tm, cw), tok),
                  pl.BlockSpec((HALO, cw), lambda i: (jnp.maximum(i * hb - 1, 0), 0)),
                  pl.BlockSpec((HALO, cw), lambda i: (jnp.minimum((i + 1) * hb, nhb - 1), 0)),
                  pl.BlockSpec((tm, d), tok),
                  pl.BlockSpec((1,) + modl.shape[1:], lambda i: (i // tpb, 0, 0))]
                 + [pl.BlockSpec(p[k].shape, functools.partial(lambda nd, i: (0,) * nd, p[k].ndim)) for k in small],
        out_specs=[pl.BlockSpec((tm, d), tok), pl.BlockSpec((tm, d), tok),
                   pl.BlockSpec((tm, LANES), tok), pl.BlockSpec((tm, LANES), tok),
                   pl.BlockSpec((1, LANES), full2)],
        out_shape=[jax.ShapeDtypeStruct((t, d), F32), jax.ShapeDtypeStruct((t, d), F32),
                   jax.ShapeDtypeStruct((t, LANES), jnp.int32), jax.ShapeDtypeStruct((t, LANES), F32),
                   jax.ShapeDtypeStruct((1, LANES), F32)],
        scratch_shapes=[pltpu.VMEM((1, LANES), F32), pltpu.VMEM((tm + 2 * HALO, cw), F32)],
        compiler_params=_cparams(1, "arbitrary"),
        name="outproj",
    )(ya, yf, yb, bonus, g, praw, praw, praw, x2, modl, *[p[k] for k in small])


def _dispatch(route_i, counts_lanes, n_blocks):
    counts = counts_lanes[0, N_GROUPS:N_GROUPS + N_EXPERTS].astype(jnp.int32)
    padded = ((counts + EXPERT_BLOCK - 1) // EXPERT_BLOCK) * EXPERT_BLOCK
    pends = jnp.cumsum(padded)
    pstarts = pends - padded
    e = route_i[:, :TOP_K]
    rank = route_i[:, TOP_K:2 * TOP_K]
    ids = jnp.arange(N_EXPERTS, dtype=jnp.int32)
    dest = jnp.sum(jnp.where(e[..., None] == ids, pstarts, 0), axis=-1) + rank
    block_start = jnp.arange(n_blocks, dtype=jnp.int32) * EXPERT_BLOCK
    block_e = jnp.minimum(jnp.sum((pends[None, :] <= block_start[:, None]).astype(jnp.int32), axis=1), N_EXPERTS - 1)
    meta = jnp.concatenate([block_e, (pends[-1] // EXPERT_BLOCK)[None]]).astype(jnp.int32)
    return dest, meta, pends.astype(jnp.int32)


def _scatter_rows_kernel(pends_ref, dest_ref, u_ref, xs_ref, zeros_ref, sem, zsem):
    tm = u_ref.shape[0] * SUBLANES

    @pl.when(pl.program_id(0) == 0)
    def _():
        zeros_ref[...] = jnp.zeros_like(zeros_ref)

        def tail_copy(e):
            tail = pl.ds(pl.multiple_of(pends_ref[e] - EXPERT_BLOCK, EXPERT_BLOCK), EXPERT_BLOCK)
            return pltpu.make_async_copy(zeros_ref, xs_ref.at[tail], zsem)

        def has_rows(e):
            return pends_ref[e] > (pends_ref[e - 1] if e > 0 else 0)

        def unused_copy(j):
            return pltpu.make_async_copy(zeros_ref, xs_ref.at[pl.ds(j * EXPERT_BLOCK, EXPERT_BLOCK)], zsem)

        def is_unused(j):
            return j * EXPERT_BLOCK >= pends_ref[N_EXPERTS - 1]

        n_blocks = xs_ref.shape[0] // EXPERT_BLOCK
        for e in range(N_EXPERTS):
            pl.when(has_rows(e))(lambda e=e: tail_copy(e).start())
        for j in range(n_blocks):
            pl.when(is_unused(j))(lambda j=j: unused_copy(j).start())
        for e in range(N_EXPERTS):
            pl.when(has_rows(e))(lambda e=e: tail_copy(e).wait())
        for j in range(n_blocks):
            pl.when(is_unused(j))(lambda j=j: unused_copy(j).wait())

    def issue(grp, carry):
        for j in range(SUBLANES):
            for k in range(TOP_K):
                dst = dest_ref[0, 0, TOP_K * SUBLANES * grp + TOP_K * j + k]
                pltpu.make_async_copy(u_ref.at[grp, pl.ds(j, 1)], xs_ref.at[pl.ds(dst, 1)], sem).start()
        return carry

    lax.fori_loop(0, tm // SUBLANES, issue, 0)
    rows = pl.ds(0, TOP_K * tm)
    pltpu.make_async_copy(xs_ref.at[rows], xs_ref.at[rows], sem).wait()


def _scatter_rows(pends, u2, dest3, total, tm):
    t, d = u2.shape
    grid_spec = pltpu.PrefetchScalarGridSpec(
        num_scalar_prefetch=1,
        grid=(t // tm,),
        in_specs=[pl.BlockSpec((1, 1, TOP_K * tm), lambda i, p: (i, 0, 0), memory_space=pltpu.SMEM),
                  pl.BlockSpec((tm // SUBLANES, SUBLANES, d), lambda i, p: (i, 0, 0))],
        out_specs=pl.BlockSpec(memory_space=pl.ANY),
        scratch_shapes=[pltpu.VMEM((EXPERT_BLOCK, d), F32), pltpu.SemaphoreType.DMA(()),
                        pltpu.SemaphoreType.DMA(())],
    )
    return pl.pallas_call(
        _scatter_rows_kernel,
        grid_spec=grid_spec,
        out_shape=jax.ShapeDtypeStruct((total, d), F32),
        compiler_params=_cparams(1, "arbitrary"),
        name="scatter_rows",
    )(pends, dest3, u2.reshape(t // SUBLANES, SUBLANES, d))


def _experts_kernel(meta_ref, xs_ref, wg_ref, wu_ref, wd_ref, o_ref, wg_b, wu_b, wd_b):
    i = pl.program_id(0)
    n_used = meta_ref[pl.num_programs(0)]

    @pl.when((i == 0) | (meta_ref[i] != meta_ref[jnp.maximum(i - 1, 0)]))
    def _():
        wg_b[...] = wg_ref[0, 0].astype(BF16)
        wu_b[...] = wu_ref[0, 0].astype(BF16)
        wd_b[...] = wd_ref[0, 0].astype(BF16)

    @pl.when(i < n_used)
    def _():
        xb = xs_ref[...].astype(BF16)
        gate = jnp.dot(xb, wg_b[...], preferred_element_type=F32)
        up = jnp.dot(xb, wu_b[...], preferred_element_type=F32)
        hb = gate * _sigmoid(gate) * up
        o_ref[...] = jnp.dot(hb.astype(BF16), wd_b[...], preferred_element_type=F32)

    @pl.when(i >= n_used)
    def _():
        o_ref[...] = jnp.zeros_like(o_ref)


def _experts(meta, xs, wg, wu, wd, layer):
    total, d = xs.shape
    nb = total // EXPERT_BLOCK
    de = wg.shape[3]
    grid_spec = pltpu.PrefetchScalarGridSpec(
        num_scalar_prefetch=1,
        grid=(nb,),
        in_specs=[pl.BlockSpec((EXPERT_BLOCK, d), lambda i, m: (jnp.minimum(i, m[nb] - 1), 0)),
                  pl.BlockSpec((1, 1, d, de), lambda i, m: (layer, m[i], 0, 0)),
                  pl.BlockSpec((1, 1, d, de), lambda i, m: (layer, m[i], 0, 0)),
                  pl.BlockSpec((1, 1, de, d), lambda i, m: (layer, m[i], 0, 0))],
        out_specs=pl.BlockSpec((EXPERT_BLOCK, d), lambda i, m: (i, 0)),
        scratch_shapes=[pltpu.VMEM((d, de), BF16), pltpu.VMEM((d, de), BF16), pltpu.VMEM((de, d), BF16)],
    )
    return pl.pallas_call(
        _experts_kernel,
        grid_spec=grid_spec,
        out_shape=jax.ShapeDtypeStruct((total, d), F32),
        compiler_params=_cparams(1, "arbitrary"),
        name="experts",
    )(meta, xs, wg, wu, wd)


def _final_kernel(dcur_ref, dnext_ref, x1_ref, rw_ref, mod_ref, g_ref, b_ref, ys_ref, o_ref, ybuf, sem, *, alpha):
    i = pl.program_id(0)
    tm = x1_ref.shape[0]
    slot = i % 2

    def gather(d_ref, s):
        def issue(grp, carry):
            for j in range(SUBLANES):
                for k in range(TOP_K):
                    src = d_ref[0, 0, TOP_K * SUBLANES * grp + TOP_K * j + k]
                    pltpu.make_async_copy(ys_ref.at[pl.ds(src, 1)], ybuf.at[s, k, grp, pl.ds(j, 1)],
                                          sem.at[s]).start()
            return carry

        lax.fori_loop(0, tm // SUBLANES, issue, 0)

    @pl.when(i == 0)
    def _():
        gather(dcur_ref, 0)

    @pl.when(i + 1 < pl.num_programs(0))
    def _():
        gather(dnext_ref, 1 - slot)

    pltpu.make_async_copy(ybuf.at[slot], ybuf.at[slot], sem.at[slot]).wait()
    m = mod_ref[0]
    rw = rw_ref[...]
    d = x1_ref.shape[1]
    f = rw[:, 0:1] * ybuf[slot, 0].reshape(tm, d) + rw[:, 1:2] * ybuf[slot, 1].reshape(tm, d)
    o_ref[...] = _ln(alpha * x1_ref[...] + m[5:6] * f) * g_ref[...] + b_ref[...]


def _final(x1, ysorted, dest3, rw, modl, gain, bias, seq, tm, alpha):
    t, d = x1.shape
    tpb = seq // tm
    n_tiles = t // tm
    tok = lambda i: (i, 0)
    kern = functools.partial(_final_kernel, alpha=alpha)
    dspec = lambda f: pl.BlockSpec((1, 1, TOP_K * tm), f, memory_space=pltpu.SMEM)
    return pl.pallas_call(
        kern,
        grid=(n_tiles,),
        in_specs=[dspec(lambda i: (i, 0, 0)), dspec(lambda i: (jnp.minimum(i + 1, n_tiles - 1), 0, 0)),
                  pl.BlockSpec((tm, d), tok), pl.BlockSpec((tm, LANES), tok),
                  pl.BlockSpec((1,) + modl.shape[1:], lambda i: (i // tpb, 0, 0)),
                  pl.BlockSpec(gain.shape, lambda i: (0, 0)), pl.BlockSpec(bias.shape, lambda i: (0, 0)),
                  pl.BlockSpec(memory_space=pl.ANY)],
        out_specs=pl.BlockSpec((tm, d), tok),
        out_shape=jax.ShapeDtypeStruct((t, d), F32),
        scratch_shapes=[pltpu.VMEM((2, TOP_K, tm // SUBLANES, SUBLANES, d), F32), pltpu.SemaphoreType.DMA((2,))],
        compiler_params=_cparams(1, "arbitrary"),
        name="final_ln",
    )(dest3, dest3, x1, rw, modl, gain, bias, ysorted)


def _block_diag(blocks):
    n, a, b = blocks.shape
    out = jnp.zeros((n * a, n * b), blocks.dtype)
    for i in range(n):
        out = out.at[i * a:(i + 1) * a, i * b:(i + 1) * b].set(blocks[i])
    return out


def _pad_rows(w, lo, total):
    return jnp.zeros((total, w.shape[-1]), w.dtype).at[lo:lo + w.shape[0]].set(w)


def kernel(x, c, w_mod, b_mod, w_in, na_rpb, rw_conv, rw_w0, rw_w_up, rw_a0, rw_a_up, rw_g_up, rw_k_k, rw_k_a, rw_r_k, rw_gn_gain, rw_gn_bias, pool_w, pool_scale, w_out, ln1_gain, ln1_bias, ln2_gain, ln2_bias, moe_w_group, moe_b_group, moe_w_expert, moe_b_expert, moe_w_gate, moe_w_up, moe_w_down):
    batch, seq, d = x.shape
    depth = w_mod.shape[0]
    t = batch * seq
    a_w = na_rpb.shape[1] * HEAD_DIM
    b_w = rw_w0.shape[-1]
    c_w = pool_scale.shape[-1]
    lr_w = R_W + R_A + R_G
    alpha = (2 * depth) ** 0.25
    tm = min(512, seq)
    tm_prep = min(256, seq)
    assert seq % tm == 0 and seq % SCAN_CHUNK == 0 and seq % GRID_W == 0 and lr_w == LANES

    mod = _modulation(c, w_mod, b_mod)
    ones_blk = _block_diag(jnp.ones((b_w // HEAD_DIM, HEAD_DIM, HEAD_DIM), BF16))
    row = lambda v: v.reshape(1, -1)

    x2 = x.reshape(t, d)
    for l in range(depth):
        modl = mod[l]
        qkv, rkv_raw, lr, praw = _inproj(x2, modl, w_in[l].astype(BF16), seq, tm, 3 * a_w, 3 * b_w, lr_w, c_w)
        ya = _natten(qkv, _na_bias_table(na_rpb[l]), batch, seq, a_w)
        prep_params = {
            "conv": rw_conv[l], "w0": rw_w0[l], "a0": rw_a0[l],
            "w_up": jnp.stack([_pad_rows(rw_w_up[l, dd], 0, lr_w) for dd in range(2)]).astype(BF16),
            "a_up": jnp.stack([_pad_rows(rw_a_up[l, dd], R_W, lr_w) for dd in range(2)]).astype(BF16),
            "g_up": _pad_rows(rw_g_up[l], R_W + R_A, lr_w).astype(BF16),
            "k_k": row(rw_k_k[l]), "k_a": row(rw_k_a[l]), "r_k": row(rw_r_k[l]), "ones": ones_blk,
        }
        r, v, nkk, lw, bb, kd, bonus, g = _rwkv_prep(rkv_raw, lr, prep_params, seq, tm_prep)
        yf, yb = _rwkv_scan(r, v, nkk, lw, bb, kd, batch, seq)
        w_router = jnp.zeros((d, LANES), F32).at[:, :N_GROUPS].set(moe_w_group[l])
        w_router = w_router.at[:, N_GROUPS:N_GROUPS + N_EXPERTS].set(moe_w_expert[l])
        b_router = jnp.zeros((1, LANES), F32).at[0, :N_GROUPS].set(moe_b_group[l])
        b_router = b_router.at[0, N_GROUPS:N_GROUPS + N_EXPERTS].set(moe_b_expert[l])
        wo = w_out[l].astype(BF16)
        out_params = {
            "w_out_a": wo[:a_w], "w_out_b": wo[a_w:a_w + b_w], "w_out_c": wo[a_w + b_w:],
            "pool_w": _block_diag(pool_w[l]), "pool_scale": row(pool_scale[l]),
            "gn_gain": row(rw_gn_gain[l]), "gn_bias": row(rw_gn_bias[l]), "ones": ones_blk,
            "ln1_gain": row(ln1_gain[l]), "ln1_bias": row(ln1_bias[l]),
            "w_router": jnp.concatenate(_split_bf16(w_router), axis=1), "b_router": b_router,
        }
        x1, u2, route_i, route_w, counts = _outproj(ya, yf, yb, bonus, g, praw, x2, modl, out_params, seq, tm, alpha)
        n_blocks = -(-(t * TOP_K) // EXPERT_BLOCK) + N_EXPERTS
        dest, meta, pends = _dispatch(route_i, counts, n_blocks)
        dest3 = dest.reshape(t // tm, 1, TOP_K * tm)
        xs = _scatter_rows(pends, u2, dest3, n_blocks * EXPERT_BLOCK, tm)
        ysorted = _experts(meta, xs, moe_w_gate, moe_w_up, moe_w_down, l)
        x2 = _final(x1, ysorted, dest3, route_w, modl, row(ln2_gain[l]), row(ln2_bias[l]), seq, tm, alpha)
    return x2.reshape(batch, seq, d)
```

```python
import functools
import math

import jax
import jax.numpy as jnp
import numpy as np
from jax import lax
from jax.experimental import pallas as pl
from jax.experimental.pallas import tpu as pltpu

F32 = jnp.float32
BF16 = jnp.bfloat16
HI = lax.Precision.HIGHEST

GRID_W = 64
HEAD_DIM = 64
NA_KH = 8
NA_KW = 16
POOL_WINDOWS = (2, 4, 8, 16)
R_W = 32
R_A = 32
R_G = 64
DECAY_SCALE = math.exp(-0.5)
GN_EPS = 64e-5
N_GROUPS = 4
EXPERTS_PER_GROUP = 8
N_EXPERTS = N_GROUPS * EXPERTS_PER_GROUP
TOP_K = 2
EXPERT_BLOCK = 512
LN_EPS = 1e-5
NEG_INF = -1e30

NA_ROWS_PER_STEP = 4
SCAN_CHUNK = 64
SUBLANES = 8
HALO = 8
LANES = 128
VMEM_LIMIT = 52 * 1024 * 1024


def _ln(x):
    mu = jnp.mean(x, axis=-1, keepdims=True)
    xc = x - mu
    var = jnp.mean(xc * xc, axis=-1, keepdims=True)
    return xc * lax.rsqrt(var + LN_EPS)


def _sigmoid(x):
    return 1.0 / (1.0 + jnp.exp(-x))


def _split_bf16(x):
    hi = x.astype(BF16)
    return hi, (x - hi.astype(F32)).astype(BF16)


def _dot_split(x, w_exact):
    hi, lo = _split_bf16(x)
    return jnp.dot(hi, w_exact, preferred_element_type=F32) + jnp.dot(lo, w_exact, preferred_element_type=F32)


def _pack_bf16_pairs(x):
    h = x.shape[1] // 2
    lo = lax.bitcast_convert_type(x[:, :h].astype(BF16).astype(F32), jnp.uint32)
    hi = lax.bitcast_convert_type(x[:, h:].astype(BF16).astype(F32), jnp.uint32)
    return (lo >> 16) | hi


def _unpack_bf16_pairs(w):
    lo = lax.bitcast_convert_type(w << 16, F32).astype(BF16)
    hi = lax.bitcast_convert_type(w & jnp.uint32(0xFFFF0000), F32).astype(BF16)
    return jnp.concatenate([lo, hi], axis=1)


def _cparams(n_axes, semantics="parallel"):
    return pltpu.CompilerParams(dimension_semantics=(semantics,) * n_axes, vmem_limit_bytes=VMEM_LIMIT)


def _mod_kernel(c_ref, w_ref, b_ref, o_ref):
    c = c_ref[...]
    s = c * _sigmoid(c)
    o_ref[0] = jnp.dot(s, w_ref[0], precision=HI, preferred_element_type=F32) + b_ref[0]


def _modulation(c, w_mod, b_mod):
    n_layers, d, d6 = w_mod.shape
    b = c.shape[0]
    bp = -(-b // 8) * 8
    cp = jnp.zeros((bp, d), F32).at[:b].set(c)
    out = pl.pallas_call(
        _mod_kernel,
        grid=(n_layers, d6 // d),
        in_specs=[pl.BlockSpec((bp, d), lambda l, j: (0, 0)),
                  pl.BlockSpec((1, d, d), lambda l, j: (l, 0, j)),
                  pl.BlockSpec((1, 1, d), lambda l, j: (l, 0, j))],
        out_specs=pl.BlockSpec((1, bp, d), lambda l, j: (l, 0, j)),
        out_shape=jax.ShapeDtypeStruct((n_layers, bp, d6), F32),
        compiler_params=_cparams(2),
        name="modulation",
    )(cp, w_mod, b_mod.reshape(n_layers, 1, d6))
    return out[:, :b].reshape(n_layers, b, d6 // d, d)


def _inproj_kernel(x_ref, mod_ref, w_ref, qkv_ref, rkv_ref, lr_ref, pool_ref, *, a3, b3, lr_w):
    m = mod_ref[0]
    u = _ln(x_ref[...]) * (1.0 + m[1:2]) + m[0:1]
    h = jnp.dot(u.astype(BF16), w_ref[...], preferred_element_type=F32)
    qkv_ref[...] = h[:, :a3].astype(BF16)
    rkv_ref[...] = h[:, a3:a3 + b3]
    lr_ref[...] = h[:, a3 + b3:a3 + b3 + lr_w]
    pool_ref[...] = h[:, a3 + b3 + lr_w:]


def _inproj(x2, modl, w_in_bf, seq, tm, a3, b3, lr_w, c_w):
    t, d = x2.shape
    tpb = seq // tm
    kern = functools.partial(_inproj_kernel, a3=a3, b3=b3, lr_w=lr_w)
    return pl.pallas_call(
        kern,
        grid=(t // tm,),
        in_specs=[pl.BlockSpec((tm, d), lambda i: (i, 0)),
                  pl.BlockSpec((1,) + modl.shape[1:], lambda i: (i // tpb, 0, 0)),
                  pl.BlockSpec(w_in_bf.shape, lambda i: (0, 0))],
        out_specs=[pl.BlockSpec((tm, a3), lambda i: (i, 0)),
                   pl.BlockSpec((tm, b3), lambda i: (i, 0)),
                   pl.BlockSpec((tm, lr_w), lambda i: (i, 0)),
                   pl.BlockSpec((tm, c_w), lambda i: (i, 0))],
        out_shape=[jax.ShapeDtypeStruct((t, a3), BF16),
                   jax.ShapeDtypeStruct((t, b3), F32),
                   jax.ShapeDtypeStruct((t, lr_w), F32),
                   jax.ShapeDtypeStruct((t, c_w), F32)],
        compiler_params=_cparams(1),
        name="inproj",
    )(x2, modl, w_in_bf)


def _na_bias_table(rpb):
    col = np.arange(GRID_W)
    cstart = np.clip(col - NA_KW // 2, 0, GRID_W - NA_KW)
    in_win = (col[None, :] >= cstart[:, None]) & (col[None, :] < cstart[:, None] + NA_KW)
    dc = np.clip(col[None, :] - col[:, None], -(NA_KW - 1), NA_KW - 1) + (NA_KW - 1)
    pick = (dc[None] == np.arange(2 * NA_KW - 1)[:, None, None]).astype(np.float32)
    cols = jnp.einsum("hrc,cqk->hrqk", rpb.astype(F32), pick, precision=HI)
    cols = jnp.where(in_win, cols, NEG_INF)
    b = jnp.stack([cols[:, NA_KH - 1 - o:2 * NA_KH - 1 - o] for o in range(NA_KH)])
    h = rpb.shape[0]
    return jnp.transpose(b, (0, 1, 3, 2, 4)).reshape(NA_KH, h * GRID_W, NA_KH * GRID_W)


def _natten_kernel(q_ref, k_ref, v_ref, bias_ref, o_ref, *, rows, heads):
    width = q_ref.shape[1]
    nk = NA_KH * GRID_W
    head_of_lane = lax.broadcasted_iota(jnp.int32, (heads * GRID_W, width), 1) // HEAD_DIM
    head_of_row = lax.broadcasted_iota(jnp.int32, (heads * GRID_W, width), 0) // GRID_W
    own = head_of_lane == head_of_row
    for j in range(NA_ROWS_PER_STEP):
        r = pl.program_id(1) * NA_ROWS_PER_STEP + j
        rstart = jnp.clip(r - NA_KH // 2, 0, rows - NA_KH)
        off = r - rstart
        start = pl.multiple_of(rstart * GRID_W, GRID_W)
        kw = k_ref[pl.ds(start, nk), :]
        vw = v_ref[pl.ds(start, nk), :]
        q = q_ref[j * GRID_W:(j + 1) * GRID_W, :]
        qs = jnp.where(own, jnp.concatenate([q] * heads, axis=0), jnp.zeros((), q.dtype))
        s = lax.dot_general(qs, kw, (((1,), (1,)), ((), ())), preferred_element_type=F32) * (HEAD_DIM ** -0.5)
        s = s + bias_ref[off]
        mx = jnp.max(s, axis=-1, keepdims=True)
        p = jnp.exp(s - mx)
        den = jnp.sum(p, axis=-1, keepdims=True)
        o = jnp.where(own, jnp.dot(p.astype(BF16), vw, preferred_element_type=F32) / den, 0.0)
        acc = o[0:GRID_W]
        for h in range(1, heads):
            acc = acc + o[h * GRID_W:(h + 1) * GRID_W]
        o_ref[j * GRID_W:(j + 1) * GRID_W, :] = acc.astype(o_ref.dtype)


def _natten(qkv, bias_tab, batch, seq, width):
    rows = seq // GRID_W
    assert rows >= NA_KH
    heads = width // HEAD_DIM
    steps = rows // NA_ROWS_PER_STEP
    assert steps * NA_ROWS_PER_STEP == rows
    tq = NA_ROWS_PER_STEP * GRID_W
    kern = functools.partial(_natten_kernel, rows=rows, heads=heads)
    return pl.pallas_call(
        kern,
        grid=(batch, steps),
        in_specs=[pl.BlockSpec((tq, width), lambda b, r: (b * steps + r, 0)),
                  pl.BlockSpec((seq, width), lambda b, r: (b, 1)),
                  pl.BlockSpec((seq, width), lambda b, r: (b, 2)),
                  pl.BlockSpec(bias_tab.shape, lambda b, r: (0, 0, 0))],
        out_specs=pl.BlockSpec((tq, width), lambda b, r: (b * steps + r, 0)),
        out_shape=jax.ShapeDtypeStruct((batch * seq, width), BF16),
        compiler_params=_cparams(2),
        name="natten",
    )(qkv, qkv, qkv, bias_tab)


def _rwkv_prep_kernel(z_ref, zp_ref, zn_ref, lr_ref, cw_ref, w0_ref, wup_ref, a0_ref, aup_ref, gup_ref,
                      kk_ref, ka_ref, rk_ref, ones_ref,
                      r_o, v_o, nkk_o, lw_o, b_o, kd_o, bonus_o, g_o, *, tiles_per_batch, width):
    i = pl.program_id(0)
    tb = i % tiles_per_batch
    z = z_ref[...]
    tm = z.shape[0]
    prev = jnp.where(tb == 0, 0.0, zp_ref[HALO - 1:HALO, :])
    nxt = jnp.where(tb == tiles_per_batch - 1, 0.0, zn_ref[0:1, :])
    row = lax.broadcasted_iota(jnp.int32, z.shape, 0)
    zm1 = jnp.where(row == 0, prev, pltpu.roll(z, 1, 0))
    zp1 = jnp.where(row == tm - 1, nxt, pltpu.roll(z, tm - 1, 0))
    rkv = zm1 * cw_ref[0:1, :] + z * cw_ref[1:2, :] + zp1 * cw_ref[2:3, :]
    r = rkv[:, :width]
    k = rkv[:, width:2 * width]
    v = rkv[:, 2 * width:]
    lr = lr_ref[...]
    th = jnp.tanh(lr)
    sg = _sigmoid(lr)
    ones = ones_ref[...]

    def headsum(x):
        return _dot_split(x, ones)

    kk = k * kk_ref[...]
    kk = kk * lax.rsqrt(jnp.maximum(headsum(kk * kk), 1e-24))
    g_o[...] = jnp.dot(sg.astype(BF16), gup_ref[...], preferred_element_type=F32)
    r_o[...] = r
    v_o[...] = v
    nkk_o[...] = -kk
    bonus = jnp.zeros_like(r)
    th_b = th.astype(BF16)
    lr_b = lr.astype(BF16)
    for d in range(2):
        wl = jnp.dot(th_b, wup_ref[d], preferred_element_type=F32) + w0_ref[d:d + 1, :]
        lw_o[d] = -DECAY_SCALE * _sigmoid(wl)
        a = _sigmoid(jnp.dot(lr_b, aup_ref[d], preferred_element_type=F32) + a0_ref[d:d + 1, :])
        kd = k * (1.0 + (a - 1.0) * ka_ref[...])
        kd_o[d] = kd
        b_o[d] = kk * a
        bonus = bonus + headsum(r * kd * rk_ref[...]) * v
    bonus_o[...] = bonus


def _rwkv_prep(rkv_raw, lr, p, seq, tm):
    t, w3 = rkv_raw.shape
    width = w3 // 3
    tpb = seq // tm
    hb = tm // HALO
    nhb = t // HALO
    kern = functools.partial(_rwkv_prep_kernel, tiles_per_batch=tpb, width=width)
    tok = lambda i: (i, 0)
    dtok = lambda i: (0, i, 0)
    full2 = lambda i: (0, 0)
    full3 = lambda i: (0, 0, 0)
    tw = jax.ShapeDtypeStruct((t, width), F32)
    dtw = jax.ShapeDtypeStruct((2, t, width), F32)
    return pl.pallas_call(
        kern,
        grid=(t // tm,),
        in_specs=[pl.BlockSpec((tm, w3), tok),
                  pl.BlockSpec((HALO, w3), lambda i: (jnp.maximum(i * hb - 1, 0), 0)),
                  pl.BlockSpec((HALO, w3), lambda i: (jnp.minimum((i + 1) * hb, nhb - 1), 0)),
                  pl.BlockSpec((tm, lr.shape[1]), tok),
                  pl.BlockSpec(p["conv"].shape, full2),
                  pl.BlockSpec(p["w0"].shape, full2),
                  pl.BlockSpec(p["w_up"].shape, full3),
                  pl.BlockSpec(p["a0"].shape, full2),
                  pl.BlockSpec(p["a_up"].shape, full3),
                  pl.BlockSpec(p["g_up"].shape, full2),
                  pl.BlockSpec(p["k_k"].shape, full2),
                  pl.BlockSpec(p["k_a"].shape, full2),
                  pl.BlockSpec(p["r_k"].shape, full2),
                  pl.BlockSpec(p["ones"].shape, full2)],
        out_specs=[pl.BlockSpec((tm, width), tok), pl.BlockSpec((tm, width), tok), pl.BlockSpec((tm, width), tok),
                   pl.BlockSpec((2, tm, width), dtok), pl.BlockSpec((2, tm, width), dtok),
                   pl.BlockSpec((2, tm, width), dtok),
                   pl.BlockSpec((tm, width), tok), pl.BlockSpec((tm, width), tok)],
        out_shape=[tw, tw, tw, dtw, dtw, dtw, tw, tw],
        compiler_params=_cparams(1),
        name="rwkv_prep",
    )(rkv_raw, rkv_raw, rkv_raw, lr, p["conv"], p["w0"], p["w_up"], p["a0"], p["a_up"], p["g_up"],
      p["k_k"], p["k_a"], p["r_k"], p["ones"])


def _dot_nt(a, b):
    return lax.dot_general(a, b, (((1,), (1,)), ((), ())), preferred_element_type=F32)


def _dot_tn(a, b):
    return lax.dot_general(a, b, (((0,), (0,)), ((), ())), preferred_element_type=F32)


def _mm(a, b):
    return jnp.dot(a.astype(BF16), b.astype(BF16), preferred_element_type=F32)


def _rwkv_scan_kernel(rf_ref, vf_ref, nf_ref, rb_ref, vb_ref, nb_ref, lwf_ref, bf_ref, kf_ref, lwb_ref, bb_ref, kb_ref,
                      yf_ref, yb_ref, s_ref, *, heads, batch):
    @pl.when(pl.program_id(0) == 0)
    def _():
        s_ref[...] = jnp.zeros_like(s_ref)

    n = SCAN_CHUNK
    pair_w = 2 * HEAD_DIM
    row = lax.broadcasted_iota(jnp.int32, (n, pair_w), 0)
    lane = lax.broadcasted_iota(jnp.int32, (n, pair_w), 1)
    col = lane & (HEAD_DIM - 1)
    even = lane < HEAD_DIM
    levels = n.bit_length()
    same = [(row >> k) == (col >> k) for k in range(levels)]
    eye = same[0].astype(F32)
    level_masks = [same[sh + 1] & jnp.logical_not(same[sh]) for sh in range(1, levels - 1)]

    def blockdiag(x2):
        xb = x2.astype(BF16)
        zero = jnp.zeros((), BF16)
        return jnp.concatenate([jnp.where(even, xb, zero), jnp.where(even, zero, xb)], axis=0)

    def mm(x2, y2):
        return jnp.dot(x2.astype(BF16), blockdiag(y2), preferred_element_type=F32)

    def mm_nt(x2, y2):
        return _dot_nt(x2.astype(BF16), blockdiag(y2))

    dirs = ((rf_ref, vf_ref, nf_ref, lwf_ref, bf_ref, kf_ref, yf_ref),
            (rb_ref, vb_ref, nb_ref, lwb_ref, bb_ref, kb_ref, yb_ref))
    chains = []
    for d, (r_ref, v_ref, n_ref, lw_ref, b_ref, k_ref, y_ref) in enumerate(dirs):
        order = row - col if d == 0 else col - row
        strict = order > 0
        incl = order >= 0
        incl_b = jnp.where(incl[:, :n], 1.0, 0.0).astype(BF16)
        for bi in range(batch):
            lw = lw_ref[0, bi]
            lw_hi, lw_mid = _split_bf16(lw)
            lw_lo = (lw - lw_hi.astype(F32) - lw_mid.astype(F32)).astype(BF16)
            g_inc = ((jnp.dot(incl_b, lw_lo, preferred_element_type=F32)
                      + jnp.dot(incl_b, lw_mid, preferred_element_type=F32))
                     + jnp.dot(incl_b, lw_hi, preferred_element_type=F32))
            g_tot = jnp.sum(lw, axis=0, keepdims=True)
            e_neg = jnp.exp(-g_inc)
            e_end = jnp.exp(g_tot - g_inc)
            decay = jnp.exp(g_tot)
            a_t = n_ref[bi] * jnp.exp(g_inc - lw)
            r_t = r_ref[bi] * jnp.exp(g_inc)
            bb = b_ref[0, bi]
            kd = k_ref[0, bi]
            b_t = bb * e_neg
            k_t = kd * e_neg
            ar_t = jnp.concatenate([a_t, r_t], axis=0).astype(BF16)
            bk_h = jnp.concatenate([bb * e_end, kd * e_end], axis=0).astype(BF16)
            v = v_ref[bi]
            for p in range(heads // 2):
                sl = slice(p * pair_w, (p + 1) * pair_w)
                chains.append(dict(strict=strict, incl=incl, sl=sl, bi=bi, y_ref=y_ref,
                                   si=(d * batch + bi) * (heads // 2) + p, decay=decay[:, sl],
                                   ar=ar_t[:, sl], b=b_t[:, sl], k=k_t[:, sl], bk_h=bk_h[:, sl], v=v[:, sl]))

    for ch in chains:
        pb = mm_nt(ch["ar"], ch["b"])
        pk = mm_nt(ch["ar"], ch["k"])
        ch["l_ab"] = jnp.where(ch["strict"], pb[:n], 0.0)
        ch["m_rb"] = jnp.where(ch["incl"], pb[n:], 0.0)
        ch["l_ak"] = jnp.where(ch["strict"], pk[:n], 0.0)
        ch["m_rk"] = jnp.where(ch["incl"], pk[n:], 0.0)
        ch["t"] = eye + jnp.where(same[1], ch["l_ab"], 0.0)
    for mask in level_masks:
        for ch in chains:
            ch["tc"] = mm(ch["t"], jnp.where(mask, ch["l_ab"], 0.0))
        for ch in chains:
            ch["t"] = ch["t"] + mm(ch["tc"], ch["t"])
    for ch in chains:
        ch["s0"] = s_ref[ch["si"]]
        ch["x"] = mm_nt(ch["ar"], ch["s0"])
    for ch in chains:
        ch["rhs"] = ch["x"][:n] + mm(ch["l_ak"], ch["v"])
    for ch in chains:
        ch["u"] = mm(ch["t"], ch["rhs"])
    for ch in chains:
        y = ch["x"][n:] + mm(ch["m_rb"], ch["u"]) + mm(ch["m_rk"], ch["v"])
        ch["y_ref"][ch["bi"], :, ch["sl"]] = y
    for ch in chains:
        uv = jnp.concatenate([ch["u"], ch["v"]], axis=0).astype(BF16)
        full = _dot_tn(uv, ch["bk_h"])
        s_ref[ch["si"]] = ch["s0"] * ch["decay"] + jnp.where(even, full[:HEAD_DIM], full[HEAD_DIM:])


def _rwkv_scan(r, v, nkk, lw, b, kd, batch, seq):
    t, width = r.shape
    heads = width // HEAD_DIM
    n = SCAN_CHUNK
    nc = seq // n
    r3, v3, n3 = (z.reshape(batch, seq, width) for z in (r, v, nkk))
    lw4, b4, k4 = (z.reshape(2, batch, seq, width) for z in (lw, b, kd))
    fwd = pl.BlockSpec((batch, n, width), lambda c: (0, c, 0))
    bwd = pl.BlockSpec((batch, n, width), lambda c: (0, nc - 1 - c, 0))
    fwd_d = pl.BlockSpec((1, batch, n, width), lambda c: (0, 0, c, 0))
    bwd_d = pl.BlockSpec((1, batch, n, width), lambda c: (1, 0, nc - 1 - c, 0))
    kern = functools.partial(_rwkv_scan_kernel, heads=heads, batch=batch)
    yf, yb = pl.pallas_call(
        kern,
        grid=(nc,),
        in_specs=[fwd, fwd, fwd, bwd, bwd, bwd, fwd_d, fwd_d, fwd_d, bwd_d, bwd_d, bwd_d],
        out_specs=[fwd, bwd],
        out_shape=[jax.ShapeDtypeStruct((batch, seq, width), F32)] * 2,
        scratch_shapes=[pltpu.VMEM((batch * heads, HEAD_DIM, 2 * HEAD_DIM), F32)],
        compiler_params=_cparams(1, "arbitrary"),
        name="rwkv_scan",
    )(r3, v3, n3, r3, v3, n3, lw4, b4, k4, lw4, b4, k4)
    return yf.reshape(t, width), yb.reshape(t, width)


def _pool_tile(p_ref, pp_ref, pn_ref, w_ref, sc_ref, ext_ref, tb, tiles_per_batch, seq):
    p = p_ref[...]
    tm, width = p.shape
    ext_ref[0:HALO, :] = jnp.where(tb == 0, 0.0, pp_ref[...])
    ext_ref[HALO:HALO + tm, :] = p
    ext_ref[HALO + tm:2 * HALO + tm, :] = jnp.where(tb == tiles_per_batch - 1, 0.0, pn_ref[...])

    def shifted(o):
        return ext_ref[HALO + o:HALO + o + tm, :]

    t = tb * tm + lax.broadcasted_iota(jnp.int32, (tm, width), 0)
    grp = lax.broadcasted_iota(jnp.int32, (tm, width), 1) // (width // len(POOL_WINDOWS))
    tot = p
    prev_half = 0
    pooled = jnp.zeros_like(p)
    for gi, win in enumerate(POOL_WINDOWS):
        half = win // 2
        for o in range(prev_half, half):
            tot = tot + shifted(-o - 1)
            if o > 0:
                tot = tot + shifted(o)
        prev_half = half
        lo = jnp.clip(t - half, 0, seq - 1)
        hi = jnp.clip(t + half - 1, 0, seq - 1)
        cnt = (hi - lo + 1).astype(F32)
        pooled = jnp.where(grp == gi, tot / cnt, pooled)
    pooled = pooled - p
    return jnp.dot(pooled, w_ref[...], preferred_element_type=F32) * sc_ref[...]


def _outproj_kernel(ya_ref, yf_ref, yb_ref, bonus_ref, g_ref, p_ref, pp_ref, pn_ref, x_ref, mod_ref,
                    wa_ref, wb_ref, wc_ref, pw_ref, psc_ref,
                    gng_ref, gnb_ref, ones_ref, l1g_ref, l1b_ref, wr_ref, br_ref,
                    x1_o, u2_o, ri_o, rw_o, cnt_o, cnt_ref, ext_ref, *, alpha, tiles_per_batch, seq):
    yc = _pool_tile(p_ref, pp_ref, pn_ref, pw_ref, psc_ref, ext_ref,
                    pl.program_id(0) % tiles_per_batch, tiles_per_batch, seq)
    m = mod_ref[0]
    ones = ones_ref[...]

    def headmean(x):
        return _dot_split(x, ones) * (1.0 / HEAD_DIM)

    ysum = yf_ref[...] + yb_ref[...]
    yc0 = ysum - headmean(ysum)
    yn = yc0 * lax.rsqrt(headmean(yc0 * yc0) + GN_EPS) * gng_ref[...] + gnb_ref[...]
    yb = (yn + bonus_ref[...]) * g_ref[...]
    mix = (jnp.dot(ya_ref[...].astype(BF16), wa_ref[...], preferred_element_type=F32)
           + jnp.dot(yb.astype(BF16), wb_ref[...], preferred_element_type=F32)
           + jnp.dot(yc.astype(BF16), wc_ref[...], preferred_element_type=F32))
    x1 = _ln(alpha * x_ref[...] + m[2:3] * mix) * l1g_ref[...] + l1b_ref[...]
    x1_o[...] = x1
    u2 = _ln(x1) * (1.0 + m[4:5]) + m[3:4]
    u2_o[...] = _pack_bf16_pairs(u2)

    u_hi, u_lo = _split_bf16(u2)
    hi_both = jnp.dot(u_hi, wr_ref[...], preferred_element_type=F32)
    lg = (hi_both[:, :LANES] + hi_both[:, LANES:]
          + jnp.dot(u_lo, wr_ref[:, :LANES], preferred_element_type=F32)) + br_ref[...]
    lane = lax.broadcasted_iota(jnp.int32, lg.shape, 1)
    big = jnp.int32(1 << 20)
    gl = jnp.where(lane < N_GROUPS, lg, -jnp.inf)
    gmax = jnp.max(gl, axis=-1, keepdims=True)
    gidx = jnp.min(jnp.where(gl == gmax, lane, big), axis=-1, keepdims=True)
    pg_sel = 1.0 / jnp.sum(jnp.exp(gl - gmax), axis=-1, keepdims=True)
    e_lo = N_GROUPS + gidx * EXPERTS_PER_GROUP
    el = jnp.where((lane >= e_lo) & (lane < e_lo + EXPERTS_PER_GROUP), lg, -jnp.inf)
    m1 = jnp.max(el, axis=-1, keepdims=True)
    i1 = jnp.min(jnp.where(el == m1, lane, big), axis=-1, keepdims=True)
    el2 = jnp.where(lane == i1, -jnp.inf, el)
    m2 = jnp.max(el2, axis=-1, keepdims=True)
    i2 = jnp.min(jnp.where(el2 == m2, lane, big), axis=-1, keepdims=True)
    e21 = jnp.exp(m2 - m1)
    p1 = 1.0 / (1.0 + e21)
    p2 = e21 / (1.0 + e21)
    rw_o[...] = jnp.where(lane == 0, pg_sel * p1, jnp.where(lane == 1, pg_sel * p2, 0.0))

    @pl.when(pl.program_id(0) == 0)
    def _():
        cnt_ref[...] = jnp.zeros_like(cnt_ref)

    tm = lg.shape[0]
    earlier = (lax.broadcasted_iota(jnp.int32, (tm, tm), 1)
               < lax.broadcasted_iota(jnp.int32, (tm, tm), 0)).astype(BF16)
    oh1 = (lane == i1).astype(F32)
    oh2 = (lane == i2).astype(F32)
    run = cnt_ref[...]
    c1 = jnp.sum(oh1, axis=0, keepdims=True)
    before1 = run + jnp.dot(earlier, oh1.astype(BF16), preferred_element_type=F32)
    before2 = run + c1 + jnp.dot(earlier, oh2.astype(BF16), preferred_element_type=F32)
    rank1 = jnp.sum(oh1 * before1, axis=-1, keepdims=True).astype(jnp.int32)
    rank2 = jnp.sum(oh2 * before2, axis=-1, keepdims=True).astype(jnp.int32)
    total = run + c1 + jnp.sum(oh2, axis=0, keepdims=True)
    cnt_ref[...] = total
    cnt_o[...] = total
    ri_o[...] = jnp.where(lane == 0, i1 - N_GROUPS, jnp.where(lane == 1, i2 - N_GROUPS,
                          jnp.where(lane == 2, rank1, jnp.where(lane == 3, rank2, 0))))


def _outproj(ya, yf, yb, bonus, g, praw, x2, modl, p, seq, tm, alpha):
    t, d = x2.shape
    tpb = seq // tm
    aw, bw, cw = ya.shape[1], bonus.shape[1], praw.shape[1]
    hb = tm // HALO
    nhb = t // HALO
    tok = lambda i: (i, 0)
    full2 = lambda i: (0, 0)
    kern = functools.partial(_outproj_kernel, alpha=alpha, tiles_per_batch=tpb, seq=seq)
    small = ["w_out_a", "w_out_b", "w_out_c", "pool_w", "pool_scale",
             "gn_gain", "gn_bias", "ones", "ln1_gain", "ln1_bias", "w_router", "b_router"]
    return pl.pallas_call(
        kern,
        grid=(t // tm,),
        in_specs=[pl.BlockSpec((tm, aw), tok),
                  pl.BlockSpec((tm, bw), tok), pl.BlockSpec((tm, bw), tok),
                  pl.BlockSpec((tm, bw), tok), pl.BlockSpec((tm, bw), tok),
                  pl.BlockSpec((tm, cw), tok),
                  pl.BlockSpec((HALO, cw), lambda i: (jnp.maximum(i * hb - 1, 0), 0)),
                  pl.BlockSpec((HALO, cw), lambda i: (jnp.minimum((i + 1) * hb, nhb - 1), 0)),
                  pl.BlockSpec((tm, d), tok),
                  pl.BlockSpec((1,) + modl.shape[1:], lambda i: (i // tpb, 0, 0))]
                 + [pl.BlockSpec(p[k].shape, functools.partial(lambda nd, i: (0,) * nd, p[k].ndim)) for k in small],
        out_specs=[pl.BlockSpec((tm, d), tok), pl.BlockSpec((tm, d // 2), tok),
                   pl.BlockSpec((tm, LANES), tok), pl.BlockSpec((tm, LANES), tok),
                   pl.BlockSpec((1, LANES), full2)],
        out_shape=[jax.ShapeDtypeStruct((t, d), F32), jax.ShapeDtypeStruct((t, d // 2), jnp.uint32),
                   jax.ShapeDtypeStruct((t, LANES), jnp.int32), jax.ShapeDtypeStruct((t, LANES), F32),
                   jax.ShapeDtypeStruct((1, LANES), F32)],
        scratch_shapes=[pltpu.VMEM((1, LANES), F32), pltpu.VMEM((tm + 2 * HALO, cw), F32)],
        compiler_params=_cparams(1, "arbitrary"),
        name="outproj",
    )(ya, yf, yb, bonus, g, praw, praw, praw, x2, modl, *[p[k] for k in small])


def _dispatch(route_i, counts_lanes, n_blocks):
    counts = counts_lanes[0, N_GROUPS:N_GROUPS + N_EXPERTS].astype(jnp.int32)
    padded = ((counts + EXPERT_BLOCK - 1) // EXPERT_BLOCK) * EXPERT_BLOCK
    pends = jnp.cumsum(padded)
    pstarts = pends - padded
    e = route_i[:, :TOP_K]
    rank = route_i[:, TOP_K:2 * TOP_K]
    ids = jnp.arange(N_EXPERTS, dtype=jnp.int32)
    dest = jnp.sum(jnp.where(e[..., None] == ids, pstarts, 0), axis=-1) + rank
    block_start = jnp.arange(n_blocks, dtype=jnp.int32) * EXPERT_BLOCK
    block_e = jnp.minimum(jnp.sum((pends[None, :] <= block_start[:, None]).astype(jnp.int32), axis=1), N_EXPERTS - 1)
    meta = jnp.concatenate([block_e, (pends[-1] // EXPERT_BLOCK)[None]]).astype(jnp.int32)
    return dest, meta, pends.astype(jnp.int32)


def _scatter_rows_kernel(pends_ref, dest_ref, u_ref, xs_ref, zeros_ref, sem, zsem):
    tm = u_ref.shape[0] * SUBLANES

    @pl.when(pl.program_id(0) == 0)
    def _():
        zeros_ref[...] = jnp.zeros_like(zeros_ref)

        def tail_copy(e):
            tail = pl.ds(pl.multiple_of(pends_ref[e] - EXPERT_BLOCK, EXPERT_BLOCK), EXPERT_BLOCK)
            return pltpu.make_async_copy(zeros_ref, xs_ref.at[tail], zsem)

        def has_rows(e):
            return pends_ref[e] > (pends_ref[e - 1] if e > 0 else 0)

        def unused_copy(j):
            return pltpu.make_async_copy(zeros_ref, xs_ref.at[pl.ds(j * EXPERT_BLOCK, EXPERT_BLOCK)], zsem)

        def is_unused(j):
            return j * EXPERT_BLOCK >= pends_ref[N_EXPERTS - 1]

        n_blocks = xs_ref.shape[0] // EXPERT_BLOCK
        for e in range(N_EXPERTS):
            pl.when(has_rows(e))(lambda e=e: tail_copy(e).start())
        for j in range(n_blocks):
            pl.when(is_unused(j))(lambda j=j: unused_copy(j).start())
        for e in range(N_EXPERTS):
            pl.when(has_rows(e))(lambda e=e: tail_copy(e).wait())
        for j in range(n_blocks):
            pl.when(is_unused(j))(lambda j=j: unused_copy(j).wait())

    def issue(grp, carry):
        for j in range(SUBLANES):
            for k in range(TOP_K):
                dst = dest_ref[0, 0, TOP_K * SUBLANES * grp + TOP_K * j + k]
                pltpu.make_async_copy(u_ref.at[grp, pl.ds(j, 1)], xs_ref.at[pl.ds(dst, 1)], sem).start()
        return carry

    lax.fori_loop(0, tm // SUBLANES, issue, 0)
    rows = pl.ds(0, TOP_K * tm)
    pltpu.make_async_copy(xs_ref.at[rows], xs_ref.at[rows], sem).wait()


def _scatter_rows(pends, u2, dest3, total, tm):
    t, d = u2.shape
    grid_spec = pltpu.PrefetchScalarGridSpec(
        num_scalar_prefetch=1,
        grid=(t // tm,),
        in_specs=[pl.BlockSpec((1, 1, TOP_K * tm), lambda i, p: (i, 0, 0), memory_space=pltpu.SMEM),
                  pl.BlockSpec((tm // SUBLANES, SUBLANES, d), lambda i, p: (i, 0, 0))],
        out_specs=pl.BlockSpec(memory_space=pl.ANY),
        scratch_shapes=[pltpu.VMEM((EXPERT_BLOCK, d), u2.dtype), pltpu.SemaphoreType.DMA(()),
                        pltpu.SemaphoreType.DMA(())],
    )
    return pl.pallas_call(
        _scatter_rows_kernel,
        grid_spec=grid_spec,
        out_shape=jax.ShapeDtypeStruct((total, d), u2.dtype),
        compiler_params=_cparams(1, "arbitrary"),
        name="scatter_rows",
    )(pends, dest3, u2.reshape(t // SUBLANES, SUBLANES, d))


def _experts_kernel(meta_ref, xs_ref, wg_ref, wu_ref, wd_ref, o_ref, wg_b, wu_b, wd_b):
    i = pl.program_id(0)
    n_used = meta_ref[pl.num_programs(0)]

    @pl.when((i == 0) | (meta_ref[i] != meta_ref[jnp.maximum(i - 1, 0)]))
    def _():
        wg_b[...] = wg_ref[0, 0].astype(BF16)
        wu_b[...] = wu_ref[0, 0].astype(BF16)
        wd_b[...] = wd_ref[0, 0].astype(BF16)

    @pl.when(i < n_used)
    def _():
        xb = _unpack_bf16_pairs(xs_ref[...])
        gate = jnp.dot(xb, wg_b[...], preferred_element_type=F32)
        up = jnp.dot(xb, wu_b[...], preferred_element_type=F32)
        hb = gate * _sigmoid(gate) * up
        o_ref[...] = _pack_bf16_pairs(jnp.dot(hb.astype(BF16), wd_b[...], preferred_element_type=F32))

    @pl.when(i >= n_used)
    def _():
        o_ref[...] = jnp.zeros_like(o_ref)


def _experts(meta, xs, wg, wu, wd, layer):
    total, dp = xs.shape
    nb = total // EXPERT_BLOCK
    d, de = wg.shape[2:]
    grid_spec = pltpu.PrefetchScalarGridSpec(
        num_scalar_prefetch=1,
        grid=(nb,),
        in_specs=[pl.BlockSpec((EXPERT_BLOCK, dp), lambda i, m: (jnp.minimum(i, m[nb] - 1), 0)),
                  pl.BlockSpec((1, 1, d, de), lambda i, m: (layer, m[i], 0, 0)),
                  pl.BlockSpec((1, 1, d, de), lambda i, m: (layer, m[i], 0, 0)),
                  pl.BlockSpec((1, 1, de, d), lambda i, m: (layer, m[i], 0, 0))],
        out_specs=pl.BlockSpec((EXPERT_BLOCK, dp), lambda i, m: (i, 0)),
        scratch_shapes=[pltpu.VMEM((d, de), BF16), pltpu.VMEM((d, de), BF16), pltpu.VMEM((de, d), BF16)],
    )
    return pl.pallas_call(
        _experts_kernel,
        grid_spec=grid_spec,
        out_shape=jax.ShapeDtypeStruct((total, dp), xs.dtype),
        compiler_params=_cparams(1, "arbitrary"),
        name="experts",
    )(meta, xs, wg, wu, wd)


def _final_kernel(dcur_ref, dnext_ref, x1_ref, rw_ref, mod_ref, g_ref, b_ref, ys_ref, o_ref, ybuf, sem, *, alpha):
    i = pl.program_id(0)
    tm = x1_ref.shape[0]
    slot = i % 2

    def gather(d_ref, s):
        def issue(grp, carry):
            for j in range(SUBLANES):
                for k in range(TOP_K):
                    src = d_ref[0, 0, TOP_K * SUBLANES * grp + TOP_K * j + k]
                    pltpu.make_async_copy(ys_ref.at[pl.ds(src, 1)], ybuf.at[s, k, grp, pl.ds(j, 1)],
                                          sem.at[s]).start()
            return carry

        lax.fori_loop(0, tm // SUBLANES, issue, 0)

    @pl.when(i == 0)
    def _():
        gather(dcur_ref, 0)

    @pl.when(i + 1 < pl.num_programs(0))
    def _():
        gather(dnext_ref, 1 - slot)

    pltpu.make_async_copy(ybuf.at[slot], ybuf.at[slot], sem.at[slot]).wait()
    m = mod_ref[0]
    rw = rw_ref[...]
    dp = ybuf.shape[-1]
    y1 = _unpack_bf16_pairs(ybuf[slot, 0].reshape(tm, dp)).astype(F32)
    y2 = _unpack_bf16_pairs(ybuf[slot, 1].reshape(tm, dp)).astype(F32)
    f = rw[:, 0:1] * y1 + rw[:, 1:2] * y2
    o_ref[...] = _ln(alpha * x1_ref[...] + m[5:6] * f) * g_ref[...] + b_ref[...]


def _final(x1, ysorted, dest3, rw, modl, gain, bias, seq, tm, alpha):
    t, d = x1.shape
    tpb = seq // tm
    n_tiles = t // tm
    tok = lambda i: (i, 0)
    kern = functools.partial(_final_kernel, alpha=alpha)
    dspec = lambda f: pl.BlockSpec((1, 1, TOP_K * tm), f, memory_space=pltpu.SMEM)
    return pl.pallas_call(
        kern,
        grid=(n_tiles,),
        in_specs=[dspec(lambda i: (i, 0, 0)), dspec(lambda i: (jnp.minimum(i + 1, n_tiles - 1), 0, 0)),
                  pl.BlockSpec((tm, d), tok), pl.BlockSpec((tm, LANES), tok),
                  pl.BlockSpec((1,) + modl.shape[1:], lambda i: (i // tpb, 0, 0)),
                  pl.BlockSpec(gain.shape, lambda i: (0, 0)), pl.BlockSpec(bias.shape, lambda i: (0, 0)),
                  pl.BlockSpec(memory_space=pl.ANY)],
        out_specs=pl.BlockSpec((tm, d), tok),
        out_shape=jax.ShapeDtypeStruct((t, d), F32),
        scratch_shapes=[pltpu.VMEM((2, TOP_K, tm // SUBLANES, SUBLANES, ysorted.shape[1]), ysorted.dtype),
                        pltpu.SemaphoreType.DMA((2,))],
        compiler_params=_cparams(1, "arbitrary"),
        name="final_ln",
    )(dest3, dest3, x1, rw, modl, gain, bias, ysorted)


def _block_diag(blocks):
    n, a, b = blocks.shape
    out = jnp.zeros((n * a, n * b), blocks.dtype)
    for i in range(n):
        out = out.at[i * a:(i + 1) * a, i * b:(i + 1) * b].set(blocks[i])
    return out


def _pad_rows(w, lo, total):
    return jnp.zeros((total, w.shape[-1]), w.dtype).at[lo:lo + w.shape[0]].set(w)


def kernel(x, c, w_mod, b_mod, w_in, na_rpb, rw_conv, rw_w0, rw_w_up, rw_a0, rw_a_up, rw_g_up, rw_k_k, rw_k_a, rw_r_k, rw_gn_gain, rw_gn_bias, pool_w, pool_scale, w_out, ln1_gain, ln1_bias, ln2_gain, ln2_bias, moe_w_group, moe_b_group, moe_w_expert, moe_b_expert, moe_w_gate, moe_w_up, moe_w_down):
    batch, seq, d = x.shape
    depth = w_mod.shape[0]
    t = batch * seq
    a_w = na_rpb.shape[1] * HEAD_DIM
    b_w = rw_w0.shape[-1]
    c_w = pool_scale.shape[-1]
    lr_w = R_W + R_A + R_G
    alpha = (2 * depth) ** 0.25
    tm = min(512, seq)
    tm_prep = min(256, seq)
    assert seq % tm == 0 and seq % SCAN_CHUNK == 0 and seq % GRID_W == 0 and lr_w == LANES

    mod = _modulation(c, w_mod, b_mod)
    ones_blk = _block_diag(jnp.ones((b_w // HEAD_DIM, HEAD_DIM, HEAD_DIM), BF16))
    row = lambda v: v.reshape(1, -1)

    x2 = x.reshape(t, d)
    for l in range(depth):
        modl = mod[l]
        qkv, rkv_raw, lr, praw = _inproj(x2, modl, w_in[l].astype(BF16), seq, tm, 3 * a_w, 3 * b_w, lr_w, c_w)
        ya = _natten(qkv, _na_bias_table(na_rpb[l]), batch, seq, a_w)
        prep_params = {
            "conv": rw_conv[l], "w0": rw_w0[l], "a0": rw_a0[l],
            "w_up": jnp.stack([_pad_rows(rw_w_up[l, dd], 0, lr_w) for dd in range(2)]).astype(BF16),
            "a_up": jnp.stack([_pad_rows(rw_a_up[l, dd], R_W, lr_w) for dd in range(2)]).astype(BF16),
            "g_up": _pad_rows(rw_g_up[l], R_W + R_A, lr_w).astype(BF16),
            "k_k": row(rw_k_k[l]), "k_a": row(rw_k_a[l]), "r_k": row(rw_r_k[l]), "ones": ones_blk,
        }
        r, v, nkk, lw, bb, kd, bonus, g = _rwkv_prep(rkv_raw, lr, prep_params, seq, tm_prep)
        yf, yb = _rwkv_scan(r, v, nkk, lw, bb, kd, batch, seq)
        w_router = jnp.zeros((d, LANES), F32).at[:, :N_GROUPS].set(moe_w_group[l])
        w_router = w_router.at[:, N_GROUPS:N_GROUPS + N_EXPERTS].set(moe_w_expert[l])
        b_router = jnp.zeros((1, LANES), F32).at[0, :N_GROUPS].set(moe_b_group[l])
        b_router = b_router.at[0, N_GROUPS:N_GROUPS + N_EXPERTS].set(moe_b_expert[l])
        wo = w_out[l].astype(BF16)
        out_params = {
            "w_out_a": wo[:a_w], "w_out_b": wo[a_w:a_w + b_w], "w_out_c": wo[a_w + b_w:],
            "pool_w": _block_diag(pool_w[l]), "pool_scale": row(pool_scale[l]),
            "gn_gain": row(rw_gn_gain[l]), "gn_bias": row(rw_gn_bias[l]), "ones": ones_blk,
            "ln1_gain": row(ln1_gain[l]), "ln1_bias": row(ln1_bias[l]),
            "w_router": jnp.concatenate(_split_bf16(w_router), axis=1), "b_router": b_router,
        }
        x1, u2, route_i, route_w, counts = _outproj(ya, yf, yb, bonus, g, praw, x2, modl, out_params, seq, tm, alpha)
        n_blocks = -(-(t * TOP_K) // EXPERT_BLOCK) + N_EXPERTS
        dest, meta, pends = _dispatch(route_i, counts, n_blocks)
        dest3 = dest.reshape(t // tm, 1, TOP_K * tm)
        xs = _scatter_rows(pends, u2, dest3, n_blocks * EXPERT_BLOCK, tm)
        ysorted = _experts(meta, xs, moe_w_gate, moe_w_up, moe_w_down, l)
        x2 = _final(x1, ysorted, dest3, route_w, modl, row(ln2_gain[l]), row(ln2_bias[l]), seq, tm, alpha)
    return x2.reshape(batch, seq, d)
```

```python
import functools
import math

import jax
import jax.numpy as jnp
import numpy as np
from jax import lax
from jax.experimental import pallas as pl
from jax.experimental.pallas import tpu as pltpu

F32 = jnp.float32
BF16 = jnp.bfloat16
HI = lax.Precision.HIGHEST

GRID_W = 64
HEAD_DIM = 64
NA_KH = 8
NA_KW = 16
POOL_WINDOWS = (2, 4, 8, 16)
R_W = 32
R_A = 32
R_G = 64
DECAY_SCALE = math.exp(-0.5)
GN_EPS = 64e-5
N_GROUPS = 4
EXPERTS_PER_GROUP = 8
N_EXPERTS = N_GROUPS * EXPERTS_PER_GROUP
TOP_K = 2
EXPERT_BLOCK = 512
LN_EPS = 1e-5
NEG_INF = -1e30

NA_ROWS_PER_STEP = 4
SCAN_CHUNK = 64
SUBLANES = 8
HALO = 8
LANES = 128
VMEM_LIMIT = 52 * 1024 * 1024


def _ln(x):
    mu = jnp.mean(x, axis=-1, keepdims=True)
    xc = x - mu
    var = jnp.mean(xc * xc, axis=-1, keepdims=True)
    return xc * lax.rsqrt(var + LN_EPS)


def _sigmoid(x):
    return 1.0 / (1.0 + jnp.exp(-x))


def _split_bf16(x):
    hi = x.astype(BF16)
    return hi, (x - hi.astype(F32)).astype(BF16)


def _dot_split(x, w_exact):
    hi, lo = _split_bf16(x)
    return jnp.dot(hi, w_exact, preferred_element_type=F32) + jnp.dot(lo, w_exact, preferred_element_type=F32)


def _pack_bf16_pairs(x):
    h = x.shape[1] // 2
    lo = lax.bitcast_convert_type(x[:, :h].astype(BF16).astype(F32), jnp.uint32)
    hi = lax.bitcast_convert_type(x[:, h:].astype(BF16).astype(F32), jnp.uint32)
    return (lo >> 16) | hi


def _unpack_bf16_pairs(w):
    lo = lax.bitcast_convert_type(w << 16, F32).astype(BF16)
    hi = lax.bitcast_convert_type(w & jnp.uint32(0xFFFF0000), F32).astype(BF16)
    return jnp.concatenate([lo, hi], axis=1)


def _cparams(n_axes, semantics="parallel"):
    return pltpu.CompilerParams(dimension_semantics=(semantics,) * n_axes, vmem_limit_bytes=VMEM_LIMIT)


def _mod_kernel(c_ref, w_ref, b_ref, o_ref):
    c = c_ref[...]
    s = c * _sigmoid(c)
    o_ref[0] = jnp.dot(s, w_ref[0], precision=HI, preferred_element_type=F32) + b_ref[0]


def _modulation(c, w_mod, b_mod):
    n_layers, d, d6 = w_mod.shape
    b = c.shape[0]
    bp = -(-b // 8) * 8
    cp = jnp.zeros((bp, d), F32).at[:b].set(c)
    out = pl.pallas_call(
        _mod_kernel,
        grid=(n_layers, d6 // d),
        in_specs=[pl.BlockSpec((bp, d), lambda l, j: (0, 0)),
                  pl.BlockSpec((1, d, d), lambda l, j: (l, 0, j)),
                  pl.BlockSpec((1, 1, d), lambda l, j: (l, 0, j))],
        out_specs=pl.BlockSpec((1, bp, d), lambda l, j: (l, 0, j)),
        out_shape=jax.ShapeDtypeStruct((n_layers, bp, d6), F32),
        compiler_params=_cparams(2),
        name="modulation",
    )(cp, w_mod, b_mod.reshape(n_layers, 1, d6))
    return out[:, :b].reshape(n_layers, b, d6 // d, d)


def _inproj_kernel(x_ref, xp_ref, xn_ref, mod_ref, w_ref, cw_ref, w0_ref, wup_ref, a0_ref, aup_ref, gup_ref,
                   kk_ref, ka_ref, rk_ref, ones_ref,
                   qkv_o, pool_o, r_o, v_o, nkk_o, lw_o, b_o, kd_o, bonus_o, g_o,
                   *, a3, b3, lr_w, tiles_per_batch):
    m = mod_ref[0]
    tm = x_ref.shape[0]
    tb = pl.program_id(0) % tiles_per_batch
    xe = jnp.concatenate([xp_ref[...], x_ref[...], xn_ref[...]], axis=0)
    u = _ln(xe) * (1.0 + m[1:2]) + m[0:1]
    h = jnp.dot(u.astype(BF16), w_ref[...], preferred_element_type=F32)
    hm = h[HALO:HALO + tm]
    qkv_o[...] = hm[:, :a3].astype(BF16)
    pool_o[...] = hm[:, a3 + b3 + lr_w:]
    prev = jnp.where(tb == 0, 0.0, h[HALO - 1:HALO, a3:a3 + b3])
    nxt = jnp.where(tb == tiles_per_batch - 1, 0.0, h[HALO + tm:HALO + tm + 1, a3:a3 + b3])
    _rwkv_prep_tile(hm[:, a3:a3 + b3], prev, nxt, hm[:, a3 + b3:a3 + b3 + lr_w],
                    cw_ref, w0_ref, wup_ref, a0_ref, aup_ref, gup_ref, kk_ref, ka_ref, rk_ref, ones_ref,
                    r_o, v_o, nkk_o, lw_o, b_o, kd_o, bonus_o, g_o)


def _inproj(x2, modl, w_in_bf, p, seq, tm, a3, b3, lr_w, c_w):
    t, d = x2.shape
    tpb = seq // tm
    hb = tm // HALO
    nhb = t // HALO
    width = b3 // 3
    kern = functools.partial(_inproj_kernel, a3=a3, b3=b3, lr_w=lr_w, tiles_per_batch=tpb)
    tok = lambda i: (i, 0)
    dtok = lambda i: (0, i, 0)
    names = ["conv", "w0", "w_up", "a0", "a_up", "g_up", "k_k", "k_a", "r_k", "ones"]
    tw = jax.ShapeDtypeStruct((t, width), F32)
    dtw = jax.ShapeDtypeStruct((2, t, width), F32)
    return pl.pallas_call(
        kern,
        grid=(t // tm,),
        in_specs=[pl.BlockSpec((tm, d), tok),
                  pl.BlockSpec((HALO, d), lambda i: (jnp.maximum(i * hb - 1, 0), 0)),
                  pl.BlockSpec((HALO, d), lambda i: (jnp.minimum((i + 1) * hb, nhb - 1), 0)),
                  pl.BlockSpec((1,) + modl.shape[1:], lambda i: (i // tpb, 0, 0)),
                  pl.BlockSpec(w_in_bf.shape, lambda i: (0, 0))]
                 + [pl.BlockSpec(p[k].shape, functools.partial(lambda nd, i: (0,) * nd, p[k].ndim)) for k in names],
        out_specs=[pl.BlockSpec((tm, a3), tok), pl.BlockSpec((tm, c_w), tok),
                   pl.BlockSpec((tm, width), tok), pl.BlockSpec((tm, width), tok), pl.BlockSpec((tm, width), tok),
                   pl.BlockSpec((2, tm, width), dtok), pl.BlockSpec((2, tm, width), dtok),
                   pl.BlockSpec((2, tm, width), dtok),
                   pl.BlockSpec((tm, width), tok), pl.BlockSpec((tm, width), tok)],
        out_shape=[jax.ShapeDtypeStruct((t, a3), BF16), jax.ShapeDtypeStruct((t, c_w), F32),
                   tw, tw, tw, dtw, dtw, dtw, tw, tw],
        compiler_params=_cparams(1),
        name="inproj",
    )(x2, x2, x2, modl, w_in_bf, *[p[k] for k in names])


def _na_bias_table(rpb):
    col = np.arange(GRID_W)
    cstart = np.clip(col - NA_KW // 2, 0, GRID_W - NA_KW)
    in_win = (col[None, :] >= cstart[:, None]) & (col[None, :] < cstart[:, None] + NA_KW)
    dc = np.clip(col[None, :] - col[:, None], -(NA_KW - 1), NA_KW - 1) + (NA_KW - 1)
    pick = (dc[None] == np.arange(2 * NA_KW - 1)[:, None, None]).astype(np.float32)
    cols = jnp.einsum("hrc,cqk->hrqk", rpb.astype(F32), pick, precision=HI)
    cols = jnp.where(in_win, cols, NEG_INF)
    b = jnp.stack([cols[:, NA_KH - 1 - o:2 * NA_KH - 1 - o] for o in range(NA_KH)])
    h = rpb.shape[0]
    return jnp.transpose(b, (0, 1, 3, 2, 4)).reshape(NA_KH, h * GRID_W, NA_KH * GRID_W)


def _natten_kernel(q_ref, k_ref, v_ref, bias_ref, o_ref, *, rows, heads):
    width = q_ref.shape[1]
    nk = NA_KH * GRID_W
    head_of_lane = lax.broadcasted_iota(jnp.int32, (heads * GRID_W, width), 1) // HEAD_DIM
    head_of_row = lax.broadcasted_iota(jnp.int32, (heads * GRID_W, width), 0) // GRID_W
    own = head_of_lane == head_of_row
    for j in range(NA_ROWS_PER_STEP):
        r = pl.program_id(1) * NA_ROWS_PER_STEP + j
        rstart = jnp.clip(r - NA_KH // 2, 0, rows - NA_KH)
        off = r - rstart
        start = pl.multiple_of(rstart * GRID_W, GRID_W)
        kw = k_ref[pl.ds(start, nk), :]
        vw = v_ref[pl.ds(start, nk), :]
        q = q_ref[j * GRID_W:(j + 1) * GRID_W, :]
        qs = jnp.where(own, jnp.concatenate([q] * heads, axis=0), jnp.zeros((), q.dtype))
        s = lax.dot_general(qs, kw, (((1,), (1,)), ((), ())), preferred_element_type=F32) * (HEAD_DIM ** -0.5)
        s = s + bias_ref[off]
        mx = jnp.max(s, axis=-1, keepdims=True)
        p = jnp.exp(s - mx)
        den = jnp.sum(p, axis=-1, keepdims=True)
        o = jnp.where(own, jnp.dot(p.astype(BF16), vw, preferred_element_type=F32) / den, 0.0)
        acc = o[0:GRID_W]
        for h in range(1, heads):
            acc = acc + o[h * GRID_W:(h + 1) * GRID_W]
        o_ref[j * GRID_W:(j + 1) * GRID_W, :] = acc.astype(o_ref.dtype)


def _natten(qkv, bias_tab, batch, seq, width):
    rows = seq // GRID_W
    assert rows >= NA_KH
    heads = width // HEAD_DIM
    steps = rows // NA_ROWS_PER_STEP
    assert steps * NA_ROWS_PER_STEP == rows
    tq = NA_ROWS_PER_STEP * GRID_W
    kern = functools.partial(_natten_kernel, rows=rows, heads=heads)
    return pl.pallas_call(
        kern,
        grid=(batch, steps),
        in_specs=[pl.BlockSpec((tq, width), lambda b, r: (b * steps + r, 0)),
                  pl.BlockSpec((seq, width), lambda b, r: (b, 1)),
                  pl.BlockSpec((seq, width), lambda b, r: (b, 2)),
                  pl.BlockSpec(bias_tab.shape, lambda b, r: (0, 0, 0))],
        out_specs=pl.BlockSpec((tq, width), lambda b, r: (b * steps + r, 0)),
        out_shape=jax.ShapeDtypeStruct((batch * seq, width), BF16),
        compiler_params=_cparams(2),
        name="natten",
    )(qkv, qkv, qkv, bias_tab)


def _rwkv_prep_tile(z, prev, nxt, lr, cw_ref, w0_ref, wup_ref, a0_ref, aup_ref, gup_ref, kk_ref, ka_ref, rk_ref, ones_ref,
                    r_o, v_o, nkk_o, lw_o, b_o, kd_o, bonus_o, g_o):
    tm = z.shape[0]
    width = z.shape[1] // 3
    row = lax.broadcasted_iota(jnp.int32, z.shape, 0)
    zm1 = jnp.where(row == 0, prev, pltpu.roll(z, 1, 0))
    zp1 = jnp.where(row == tm - 1, nxt, pltpu.roll(z, tm - 1, 0))
    rkv = zm1 * cw_ref[0:1, :] + z * cw_ref[1:2, :] + zp1 * cw_ref[2:3, :]
    r = rkv[:, :width]
    k = rkv[:, width:2 * width]
    v = rkv[:, 2 * width:]
    th = jnp.tanh(lr)
    sg = _sigmoid(lr)
    ones = ones_ref[...]

    def headsum(x):
        return _dot_split(x, ones)

    kk = k * kk_ref[...]
    kk = kk * lax.rsqrt(jnp.maximum(headsum(kk * kk), 1e-24))
    g_o[...] = jnp.dot(sg.astype(BF16), gup_ref[...], preferred_element_type=F32)
    r_o[...] = r
    v_o[...] = v
    nkk_o[...] = -kk
    bonus = jnp.zeros_like(r)
    th_b = th.astype(BF16)
    lr_b = lr.astype(BF16)
    for d in range(2):
        wl = jnp.dot(th_b, wup_ref[d], preferred_element_type=F32) + w0_ref[d:d + 1, :]
        lw_o[d] = -DECAY_SCALE * _sigmoid(wl)
        a = _sigmoid(jnp.dot(lr_b, aup_ref[d], preferred_element_type=F32) + a0_ref[d:d + 1, :])
        kd = k * (1.0 + (a - 1.0) * ka_ref[...])
        kd_o[d] = kd
        b_o[d] = kk * a
        bonus = bonus + headsum(r * kd * rk_ref[...]) * v
    bonus_o[...] = bonus


def _dot_nt(a, b):
    return lax.dot_general(a, b, (((1,), (1,)), ((), ())), preferred_element_type=F32)


def _dot_tn(a, b):
    return lax.dot_general(a, b, (((0,), (0,)), ((), ())), preferred_element_type=F32)


def _mm(a, b):
    return jnp.dot(a.astype(BF16), b.astype(BF16), preferred_element_type=F32)


def _rwkv_scan_kernel(rf_ref, vf_ref, nf_ref, rb_ref, vb_ref, nb_ref, lwf_ref, bf_ref, kf_ref, lwb_ref, bb_ref, kb_ref,
                      yf_ref, yb_ref, s_ref, *, heads, batch):
    @pl.when(pl.program_id(0) == 0)
    def _():
        s_ref[...] = jnp.zeros_like(s_ref)

    n = SCAN_CHUNK
    pair_w = 2 * HEAD_DIM
    row = lax.broadcasted_iota(jnp.int32, (n, pair_w), 0)
    lane = lax.broadcasted_iota(jnp.int32, (n, pair_w), 1)
    col = lane & (HEAD_DIM - 1)
    even = lane < HEAD_DIM
    levels = n.bit_length()
    same = [(row >> k) == (col >> k) for k in range(levels)]
    eye = same[0].astype(F32)
    level_masks = [same[sh + 1] & jnp.logical_not(same[sh]) for sh in range(1, levels - 1)]

    def blockdiag(x2):
        xb = x2.astype(BF16)
        zero = jnp.zeros((), BF16)
        return jnp.concatenate([jnp.where(even, xb, zero), jnp.where(even, zero, xb)], axis=0)

    def mm(x2, y2):
        return jnp.dot(x2.astype(BF16), blockdiag(y2), preferred_element_type=F32)

    def mm_nt(x2, y2):
        return _dot_nt(x2.astype(BF16), blockdiag(y2))

    dirs = ((rf_ref, vf_ref, nf_ref, lwf_ref, bf_ref, kf_ref, yf_ref),
            (rb_ref, vb_ref, nb_ref, lwb_ref, bb_ref, kb_ref, yb_ref))
    chains = []
    for d, (r_ref, v_ref, n_ref, lw_ref, b_ref, k_ref, y_ref) in enumerate(dirs):
        order = row - col if d == 0 else col - row
        strict = order > 0
        incl = order >= 0
        incl_b = jnp.where(incl[:, :n], 1.0, 0.0).astype(BF16)
        for bi in range(batch):
            lw = lw_ref[0, bi]
            lw_hi, lw_mid = _split_bf16(lw)
            lw_lo = (lw - lw_hi.astype(F32) - lw_mid.astype(F32)).astype(BF16)
            g_inc = ((jnp.dot(incl_b, lw_lo, preferred_element_type=F32)
                      + jnp.dot(incl_b, lw_mid, preferred_element_type=F32))
                     + jnp.dot(incl_b, lw_hi, preferred_element_type=F32))
            g_tot = jnp.sum(lw, axis=0, keepdims=True)
            e_neg = jnp.exp(-g_inc)
            e_end = jnp.exp(g_tot - g_inc)
            decay = jnp.exp(g_tot)
            a_t = n_ref[bi] * jnp.exp(g_inc - lw)
            r_t = r_ref[bi] * jnp.exp(g_inc)
            bb = b_ref[0, bi]
            kd = k_ref[0, bi]
            b_t = bb * e_neg
            k_t = kd * e_neg
            ar_t = jnp.concatenate([a_t, r_t], axis=0).astype(BF16)
            bk_h = jnp.concatenate([bb * e_end, kd * e_end], axis=0).astype(BF16)
            v = v_ref[bi]
            for p in range(heads // 2):
                sl = slice(p * pair_w, (p + 1) * pair_w)
                chains.append(dict(strict=strict, incl=incl, sl=sl, bi=bi, y_ref=y_ref,
                                   si=(d * batch + bi) * (heads // 2) + p, decay=decay[:, sl],
                                   ar=ar_t[:, sl], b=b_t[:, sl], k=k_t[:, sl], bk_h=bk_h[:, sl], v=v[:, sl]))

    for ch in chains:
        pb = mm_nt(ch["ar"], ch["b"])
        pk = mm_nt(ch["ar"], ch["k"])
        ch["l_ab"] = jnp.where(ch["strict"], pb[:n], 0.0)
        ch["m_rb"] = jnp.where(ch["incl"], pb[n:], 0.0)
        ch["l_ak"] = jnp.where(ch["strict"], pk[:n], 0.0)
        ch["m_rk"] = jnp.where(ch["incl"], pk[n:], 0.0)
        ch["t"] = eye + jnp.where(same[1], ch["l_ab"], 0.0)
    for mask in level_masks:
        for ch in chains:
            ch["tc"] = mm(ch["t"], jnp.where(mask, ch["l_ab"], 0.0))
        for ch in chains:
            ch["t"] = ch["t"] + mm(ch["tc"], ch["t"])
    for ch in chains:
        ch["s0"] = s_ref[ch["si"]]
        ch["x"] = mm_nt(ch["ar"], ch["s0"])
    for ch in chains:
        ch["kv"] = mm(jnp.concatenate([ch["l_ak"], ch["m_rk"]], axis=0), ch["v"])
    for ch in chains:
        ch["u"] = mm(ch["t"], ch["x"][:n] + ch["kv"][:n])
    for ch in chains:
        y = ch["x"][n:] + mm(ch["m_rb"], ch["u"]) + ch["kv"][n:]
        ch["y_ref"][ch["bi"], :, ch["sl"]] = y
    for ch in chains:
        uv = jnp.concatenate([ch["u"], ch["v"]], axis=0).astype(BF16)
        full = _dot_tn(uv, ch["bk_h"])
        s_ref[ch["si"]] = ch["s0"] * ch["decay"] + jnp.where(even, full[:HEAD_DIM], full[HEAD_DIM:])


def _rwkv_scan(r, v, nkk, lw, b, kd, batch, seq):
    t, width = r.shape
    heads = width // HEAD_DIM
    n = SCAN_CHUNK
    nc = seq // n
    r3, v3, n3 = (z.reshape(batch, seq, width) for z in (r, v, nkk))
    lw4, b4, k4 = (z.reshape(2, batch, seq, width) for z in (lw, b, kd))
    fwd = pl.BlockSpec((batch, n, width), lambda c: (0, c, 0))
    bwd = pl.BlockSpec((batch, n, width), lambda c: (0, nc - 1 - c, 0))
    fwd_d = pl.BlockSpec((1, batch, n, width), lambda c: (0, 0, c, 0))
    bwd_d = pl.BlockSpec((1, batch, n, width), lambda c: (1, 0, nc - 1 - c, 0))
    kern = functools.partial(_rwkv_scan_kernel, heads=heads, batch=batch)
    yf, yb = pl.pallas_call(
        kern,
        grid=(nc,),
        in_specs=[fwd, fwd, fwd, bwd, bwd, bwd, fwd_d, fwd_d, fwd_d, bwd_d, bwd_d, bwd_d],
        out_specs=[fwd, bwd],
        out_shape=[jax.ShapeDtypeStruct((batch, seq, width), F32)] * 2,
        scratch_shapes=[pltpu.VMEM((batch * heads, HEAD_DIM, 2 * HEAD_DIM), F32)],
        compiler_params=_cparams(1, "arbitrary"),
        name="rwkv_scan",
    )(r3, v3, n3, r3, v3, n3, lw4, b4, k4, lw4, b4, k4)
    return yf.reshape(t, width), yb.reshape(t, width)


def _pool_tile(p_ref, pp_ref, pn_ref, w_ref, sc_ref, ext_ref, tb, tiles_per_batch, seq):
    p = p_ref[...]
    tm, width = p.shape
    ext_ref[0:HALO, :] = jnp.where(tb == 0, 0.0, pp_ref[...])
    ext_ref[HALO:HALO + tm, :] = p
    ext_ref[HALO + tm:2 * HALO + tm, :] = jnp.where(tb == tiles_per_batch - 1, 0.0, pn_ref[...])

    def shifted(o):
        return ext_ref[HALO + o:HALO + o + tm, :]

    t = tb * tm + lax.broadcasted_iota(jnp.int32, (tm, width), 0)
    grp = lax.broadcasted_iota(jnp.int32, (tm, width), 1) // (width // len(POOL_WINDOWS))
    tot = p
    prev_half = 0
    pooled = jnp.zeros_like(p)
    for gi, win in enumerate(POOL_WINDOWS):
        half = win // 2
        for o in range(prev_half, half):
            tot = tot + shifted(-o - 1)
            if o > 0:
                tot = tot + shifted(o)
        prev_half = half
        lo = jnp.clip(t - half, 0, seq - 1)
        hi = jnp.clip(t + half - 1, 0, seq - 1)
        cnt = (hi - lo + 1).astype(F32)
        pooled = jnp.where(grp == gi, tot / cnt, pooled)
    pooled = pooled - p
    return jnp.dot(pooled, w_ref[...], preferred_element_type=F32) * sc_ref[...]


def _outproj_kernel(ya_ref, yf_ref, yb_ref, bonus_ref, g_ref, p_ref, pp_ref, pn_ref, x_ref, mod_ref,
                    wa_ref, wb_ref, wc_ref, pw_ref, psc_ref,
                    gng_ref, gnb_ref, ones_ref, l1g_ref, l1b_ref, wr_ref, br_ref,
                    x1_o, u2_o, ri_o, rw_o, cnt_o, cnt_ref, ext_ref, *, alpha, tiles_per_batch, seq):
    yc = _pool_tile(p_ref, pp_ref, pn_ref, pw_ref, psc_ref, ext_ref,
                    pl.program_id(0) % tiles_per_batch, tiles_per_batch, seq)
    m = mod_ref[0]
    ones = ones_ref[...]

    def headmean(x):
        return _dot_split(x, ones) * (1.0 / HEAD_DIM)

    ysum = yf_ref[...] + yb_ref[...]
    yc0 = ysum - headmean(ysum)
    yn = yc0 * lax.rsqrt(headmean(yc0 * yc0) + GN_EPS) * gng_ref[...] + gnb_ref[...]
    yb = (yn + bonus_ref[...]) * g_ref[...]
    mix = (jnp.dot(ya_ref[...].astype(BF16), wa_ref[...], preferred_element_type=F32)
           + jnp.dot(yb.astype(BF16), wb_ref[...], preferred_element_type=F32)
           + jnp.dot(yc.astype(BF16), wc_ref[...], preferred_element_type=F32))
    x1 = _ln(alpha * x_ref[...] + m[2:3] * mix) * l1g_ref[...] + l1b_ref[...]
    x1_o[...] = x1
    u2 = _ln(x1) * (1.0 + m[4:5]) + m[3:4]
    u2_o[...] = _pack_bf16_pairs(u2)

    u_hi, u_lo = _split_bf16(u2)
    hi_both = jnp.dot(u_hi, wr_ref[...], preferred_element_type=F32)
    lg = (hi_both[:, :LANES] + hi_both[:, LANES:]
          + jnp.dot(u_lo, wr_ref[:, :LANES], preferred_element_type=F32)) + br_ref[...]
    lane = lax.broadcasted_iota(jnp.int32, lg.shape, 1)
    big = jnp.int32(1 << 20)
    gl = jnp.where(lane < N_GROUPS, lg, -jnp.inf)
    gmax = jnp.max(gl, axis=-1, keepdims=True)
    gidx = jnp.min(jnp.where(gl == gmax, lane, big), axis=-1, keepdims=True)
    pg_sel = 1.0 / jnp.sum(jnp.exp(gl - gmax), axis=-1, keepdims=True)
    e_lo = N_GROUPS + gidx * EXPERTS_PER_GROUP
    el = jnp.where((lane >= e_lo) & (lane < e_lo + EXPERTS_PER_GROUP), lg, -jnp.inf)
    m1 = jnp.max(el, axis=-1, keepdims=True)
    i1 = jnp.min(jnp.where(el == m1, lane, big), axis=-1, keepdims=True)
    el2 = jnp.where(lane == i1, -jnp.inf, el)
    m2 = jnp.max(el2, axis=-1, keepdims=True)
    i2 = jnp.min(jnp.where(el2 == m2, lane, big), axis=-1, keepdims=True)
    e21 = jnp.exp(m2 - m1)
    p1 = 1.0 / (1.0 + e21)
    p2 = e21 / (1.0 + e21)
    rw_o[...] = jnp.where(lane == 0, pg_sel * p1, jnp.where(lane == 1, pg_sel * p2, 0.0))

    @pl.when(pl.program_id(0) == 0)
    def _():
        cnt_ref[...] = jnp.zeros_like(cnt_ref)

    tm = lg.shape[0]
    earlier = (lax.broadcasted_iota(jnp.int32, (tm, tm), 1)
               < lax.broadcasted_iota(jnp.int32, (tm, tm), 0)).astype(BF16)
    oh1 = (lane == i1).astype(F32)
    oh2 = (lane == i2).astype(F32)
    run = cnt_ref[...]
    c1 = jnp.sum(oh1, axis=0, keepdims=True)
    before1 = run + jnp.dot(earlier, oh1.astype(BF16), preferred_element_type=F32)
    before2 = run + c1 + jnp.dot(earlier, oh2.astype(BF16), preferred_element_type=F32)
    rank1 = jnp.sum(oh1 * before1, axis=-1, keepdims=True).astype(jnp.int32)
    rank2 = jnp.sum(oh2 * before2, axis=-1, keepdims=True).astype(jnp.int32)
    total = run + c1 + jnp.sum(oh2, axis=0, keepdims=True)
    cnt_ref[...] = total
    cnt_o[...] = total
    ri_o[...] = jnp.where(lane == 0, i1 - N_GROUPS, jnp.where(lane == 1, i2 - N_GROUPS,
                          jnp.where(lane == 2, rank1, jnp.where(lane == 3, rank2, 0))))


def _outproj(ya, yf, yb, bonus, g, praw, x2, modl, p, seq, tm, alpha):
    t, d = x2.shape
    tpb = seq // tm
    aw, bw, cw = ya.shape[1], bonus.shape[1], praw.shape[1]
    hb = tm // HALO
    nhb = t // HALO
    tok = lambda i: (i, 0)
    full2 = lambda i: (0, 0)
    kern = functools.partial(_outproj_kernel, alpha=alpha, tiles_per_batch=tpb, seq=seq)
    small = ["w_out_a", "w_out_b", "w_out_c", "pool_w", "pool_scale",
             "gn_gain", "gn_bias", "ones", "ln1_gain", "ln1_bias", "w_router", "b_router"]
    return pl.pallas_call(
        kern,
        grid=(t // tm,),
        in_specs=[pl.BlockSpec((tm, aw), tok),
                  pl.BlockSpec((tm, bw), tok), pl.BlockSpec((tm, bw), tok),
                  pl.BlockSpec((tm, bw), tok), pl.BlockSpec((tm, bw), tok),
                  pl.BlockSpec((tm, cw), tok),
                  pl.BlockSpec((HALO, cw), lambda i: (jnp.maximum(i * hb - 1, 0), 0)),
                  pl.BlockSpec((HALO, cw), lambda i: (jnp.minimum((i + 1) * hb, nhb - 1), 0)),
                  pl.BlockSpec((tm, d), tok),
                  pl.BlockSpec((1,) + modl.shape[1:], lambda i: (i // tpb, 0, 0))]
                 + [pl.BlockSpec(p[k].shape, functools.partial(lambda nd, i: (0,) * nd, p[k].ndim)) for k in small],
        out_specs=[pl.BlockSpec((tm, d), tok), pl.BlockSpec((tm, d // 2), tok),
                   pl.BlockSpec((tm, LANES), tok), pl.BlockSpec((tm, LANES), tok),
                   pl.BlockSpec((1, LANES), full2)],
        out_shape=[jax.ShapeDtypeStruct((t, d), F32), jax.ShapeDtypeStruct((t, d // 2), jnp.uint32),
                   jax.ShapeDtypeStruct((t, LANES), jnp.int32), jax.ShapeDtypeStruct((t, LANES), F32),
                   jax.ShapeDtypeStruct((1, LANES), F32)],
        scratch_shapes=[pltpu.VMEM((1, LANES), F32), pltpu.VMEM((tm + 2 * HALO, cw), F32)],
        compiler_params=_cparams(1, "arbitrary"),
        name="outproj",
    )(ya, yf, yb, bonus, g, praw, praw, praw, x2, modl, *[p[k] for k in small])


def _dispatch(route_i, counts_lanes, n_blocks):
    counts = counts_lanes[0, N_GROUPS:N_GROUPS + N_EXPERTS].astype(jnp.int32)
    padded = ((counts + EXPERT_BLOCK - 1) // EXPERT_BLOCK) * EXPERT_BLOCK
    pends = jnp.cumsum(padded)
    pstarts = pends - padded
    e = route_i[:, :TOP_K]
    rank = route_i[:, TOP_K:2 * TOP_K]
    ids = jnp.arange(N_EXPERTS, dtype=jnp.int32)
    dest = jnp.sum(jnp.where(e[..., None] == ids, pstarts, 0), axis=-1) + rank
    block_start = jnp.arange(n_blocks, dtype=jnp.int32) * EXPERT_BLOCK
    block_e = jnp.minimum(jnp.sum((pends[None, :] <= block_start[:, None]).astype(jnp.int32), axis=1), N_EXPERTS - 1)
    meta = jnp.concatenate([block_e, (pends[-1] // EXPERT_BLOCK)[None]]).astype(jnp.int32)
    return dest, meta, pends.astype(jnp.int32)


def _scatter_rows_kernel(pends_ref, dest_ref, u_ref, xs_ref, zeros_ref, sem, zsem):
    tm = u_ref.shape[0] * SUBLANES

    @pl.when(pl.program_id(0) == 0)
    def _():
        zeros_ref[...] = jnp.zeros_like(zeros_ref)

        def tail_copy(e):
            tail = pl.ds(pl.multiple_of(pends_ref[e] - EXPERT_BLOCK, EXPERT_BLOCK), EXPERT_BLOCK)
            return pltpu.make_async_copy(zeros_ref, xs_ref.at[tail], zsem)

        def has_rows(e):
            return pends_ref[e] > (pends_ref[e - 1] if e > 0 else 0)

        def unused_copy(j):
            return pltpu.make_async_copy(zeros_ref, xs_ref.at[pl.ds(j * EXPERT_BLOCK, EXPERT_BLOCK)], zsem)

        def is_unused(j):
            return j * EXPERT_BLOCK >= pends_ref[N_EXPERTS - 1]

        n_blocks = xs_ref.shape[0] // EXPERT_BLOCK
        for e in range(N_EXPERTS):
            pl.when(has_rows(e))(lambda e=e: tail_copy(e).start())
        for j in range(n_blocks):
            pl.when(is_unused(j))(lambda j=j: unused_copy(j).start())
        for e in range(N_EXPERTS):
            pl.when(has_rows(e))(lambda e=e: tail_copy(e).wait())
        for j in range(n_blocks):
            pl.when(is_unused(j))(lambda j=j: unused_copy(j).wait())

    def issue(grp, carry):
        for j in range(SUBLANES):
            for k in range(TOP_K):
                dst = dest_ref[0, 0, TOP_K * SUBLANES * grp + TOP_K * j + k]
                pltpu.make_async_copy(u_ref.at[grp, pl.ds(j, 1)], xs_ref.at[pl.ds(dst, 1)], sem).start()
        return carry

    lax.fori_loop(0, tm // SUBLANES, issue, 0)
    rows = pl.ds(0, TOP_K * tm)
    pltpu.make_async_copy(xs_ref.at[rows], xs_ref.at[rows], sem).wait()


def _scatter_rows(pends, u2, dest3, total, tm):
    t, d = u2.shape
    grid_spec = pltpu.PrefetchScalarGridSpec(
        num_scalar_prefetch=1,
        grid=(t // tm,),
        in_specs=[pl.BlockSpec((1, 1, TOP_K * tm), lambda i, p: (i, 0, 0), memory_space=pltpu.SMEM),
                  pl.BlockSpec((tm // SUBLANES, SUBLANES, d), lambda i, p: (i, 0, 0))],
        out_specs=pl.BlockSpec(memory_space=pl.ANY),
        scratch_shapes=[pltpu.VMEM((EXPERT_BLOCK, d), u2.dtype), pltpu.SemaphoreType.DMA(()),
                        pltpu.SemaphoreType.DMA(())],
    )
    return pl.pallas_call(
        _scatter_rows_kernel,
        grid_spec=grid_spec,
        out_shape=jax.ShapeDtypeStruct((total, d), u2.dtype),
        compiler_params=_cparams(1, "arbitrary"),
        name="scatter_rows",
    )(pends, dest3, u2.reshape(t // SUBLANES, SUBLANES, d))


def _experts_kernel(meta_ref, xs_ref, wg_ref, wu_ref, wd_ref, o_ref, wg_b, wu_b, wd_b):
    i = pl.program_id(0)
    n_used = meta_ref[pl.num_programs(0)]

    @pl.when((i == 0) | (meta_ref[i] != meta_ref[jnp.maximum(i - 1, 0)]))
    def _():
        wg_b[...] = wg_ref[0, 0].astype(BF16)
        wu_b[...] = wu_ref[0, 0].astype(BF16)
        wd_b[...] = wd_ref[0, 0].astype(BF16)

    @pl.when(i < n_used)
    def _():
        xb = _unpack_bf16_pairs(xs_ref[...])
        gate = jnp.dot(xb, wg_b[...], preferred_element_type=F32)
        up = jnp.dot(xb, wu_b[...], preferred_element_type=F32)
        hb = gate * _sigmoid(gate) * up
        o_ref[...] = _pack_bf16_pairs(jnp.dot(hb.astype(BF16), wd_b[...], preferred_element_type=F32))

    @pl.when(i >= n_used)
    def _():
        o_ref[...] = jnp.zeros_like(o_ref)


def _experts(meta, xs, wg, wu, wd, layer):
    total, dp = xs.shape
    nb = total // EXPERT_BLOCK
    d, de = wg.shape[2:]
    grid_spec = pltpu.PrefetchScalarGridSpec(
        num_scalar_prefetch=1,
        grid=(nb,),
        in_specs=[pl.BlockSpec((EXPERT_BLOCK, dp), lambda i, m: (jnp.minimum(i, m[nb] - 1), 0)),
                  pl.BlockSpec((1, 1, d, de), lambda i, m: (layer, m[i], 0, 0)),
                  pl.BlockSpec((1, 1, d, de), lambda i, m: (layer, m[i], 0, 0)),
                  pl.BlockSpec((1, 1, de, d), lambda i, m: (layer, m[i], 0, 0))],
        out_specs=pl.BlockSpec((EXPERT_BLOCK, dp), lambda i, m: (i, 0)),
        scratch_shapes=[pltpu.VMEM((d, de), BF16), pltpu.VMEM((d, de), BF16), pltpu.VMEM((de, d), BF16)],
    )
    return pl.pallas_call(
        _experts_kernel,
        grid_spec=grid_spec,
        out_shape=jax.ShapeDtypeStruct((total, dp), xs.dtype),
        compiler_params=_cparams(1, "arbitrary"),
        name="experts",
    )(meta, xs, wg, wu, wd)


def _final_kernel(dcur_ref, dnext_ref, x1_ref, rw_ref, mod_ref, g_ref, b_ref, ys_ref, o_ref, ybuf, sem, *, alpha):
    i = pl.program_id(0)
    tm = x1_ref.shape[0]
    slot = i % 2

    def gather(d_ref, s):
        def issue(grp, carry):
            for j in range(SUBLANES):
                for k in range(TOP_K):
                    src = d_ref[0, 0, TOP_K * SUBLANES * grp + TOP_K * j + k]
                    pltpu.make_async_copy(ys_ref.at[pl.ds(src, 1)], ybuf.at[s, k, grp, pl.ds(j, 1)],
                                          sem.at[s]).start()
            return carry

        lax.fori_loop(0, tm // SUBLANES, issue, 0)

    @pl.when(i == 0)
    def _():
        gather(dcur_ref, 0)

    @pl.when(i + 1 < pl.num_programs(0))
    def _():
        gather(dnext_ref, 1 - slot)

    pltpu.make_async_copy(ybuf.at[slot], ybuf.at[slot], sem.at[slot]).wait()
    m = mod_ref[0]
    rw = rw_ref[...]
    dp = ybuf.shape[-1]
    y1 = _unpack_bf16_pairs(ybuf[slot, 0].reshape(tm, dp)).astype(F32)
    y2 = _unpack_bf16_pairs(ybuf[slot, 1].reshape(tm, dp)).astype(F32)
    f = rw[:, 0:1] * y1 + rw[:, 1:2] * y2
    o_ref[...] = _ln(alpha * x1_ref[...] + m[5:6] * f) * g_ref[...] + b_ref[...]


def _final(x1, ysorted, dest3, rw, modl, gain, bias, seq, tm, alpha):
    t, d = x1.shape
    tpb = seq // tm
    n_tiles = t // tm
    tok = lambda i: (i, 0)
    kern = functools.partial(_final_kernel, alpha=alpha)
    dspec = lambda f: pl.BlockSpec((1, 1, TOP_K * tm), f, memory_space=pltpu.SMEM)
    return pl.pallas_call(
        kern,
        grid=(n_tiles,),
        in_specs=[dspec(lambda i: (i, 0, 0)), dspec(lambda i: (jnp.minimum(i + 1, n_tiles - 1), 0, 0)),
                  pl.BlockSpec((tm, d), tok), pl.BlockSpec((tm, LANES), tok),
                  pl.BlockSpec((1,) + modl.shape[1:], lambda i: (i // tpb, 0, 0)),
                  pl.BlockSpec(gain.shape, lambda i: (0, 0)), pl.BlockSpec(bias.shape, lambda i: (0, 0)),
                  pl.BlockSpec(memory_space=pl.ANY)],
        out_specs=pl.BlockSpec((tm, d), tok),
        out_shape=jax.ShapeDtypeStruct((t, d), F32),
        scratch_shapes=[pltpu.VMEM((2, TOP_K, tm // SUBLANES, SUBLANES, ysorted.shape[1]), ysorted.dtype),
                        pltpu.SemaphoreType.DMA((2,))],
        compiler_params=_cparams(1, "arbitrary"),
        name="final_ln",
    )(dest3, dest3, x1, rw, modl, gain, bias, ysorted)


def _block_diag(blocks):
    n, a, b = blocks.shape
    out = jnp.zeros((n * a, n * b), blocks.dtype)
    for i in range(n):
        out = out.at[i * a:(i + 1) * a, i * b:(i + 1) * b].set(blocks[i])
    return out


def _pad_rows(w, lo, total):
    return jnp.zeros((total, w.shape[-1]), w.dtype).at[lo:lo + w.shape[0]].set(w)


def kernel(x, c, w_mod, b_mod, w_in, na_rpb, rw_conv, rw_w0, rw_w_up, rw_a0, rw_a_up, rw_g_up, rw_k_k, rw_k_a, rw_r_k, rw_gn_gain, rw_gn_bias, pool_w, pool_scale, w_out, ln1_gain, ln1_bias, ln2_gain, ln2_bias, moe_w_group, moe_b_group, moe_w_expert, moe_b_expert, moe_w_gate, moe_w_up, moe_w_down):
    batch, seq, d = x.shape
    depth = w_mod.shape[0]
    t = batch * seq
    a_w = na_rpb.shape[1] * HEAD_DIM
    b_w = rw_w0.shape[-1]
    c_w = pool_scale.shape[-1]
    lr_w = R_W + R_A + R_G
    alpha = (2 * depth) ** 0.25
    tm = min(512, seq)
    tm_in = min(512, seq)
    assert seq % tm == 0 and seq % SCAN_CHUNK == 0 and seq % GRID_W == 0 and lr_w == LANES

    mod = _modulation(c, w_mod, b_mod)
    ones_blk = _block_diag(jnp.ones((b_w // HEAD_DIM, HEAD_DIM, HEAD_DIM), BF16))
    row = lambda v: v.reshape(1, -1)

    x2 = x.reshape(t, d)
    for l in range(depth):
        modl = mod[l]
        prep_params = {
            "conv": rw_conv[l], "w0": rw_w0[l], "a0": rw_a0[l],
            "w_up": jnp.stack([_pad_rows(rw_w_up[l, dd], 0, lr_w) for dd in range(2)]).astype(BF16),
            "a_up": jnp.stack([_pad_rows(rw_a_up[l, dd], R_W, lr_w) for dd in range(2)]).astype(BF16),
            "g_up": _pad_rows(rw_g_up[l], R_W + R_A, lr_w).astype(BF16),
            "k_k": row(rw_k_k[l]), "k_a": row(rw_k_a[l]), "r_k": row(rw_r_k[l]), "ones": ones_blk,
        }
        qkv, praw, r, v, nkk, lw, bb, kd, bonus, g = _inproj(x2, modl, w_in[l].astype(BF16), prep_params, seq, tm_in,
                                                             3 * a_w, 3 * b_w, lr_w, c_w)
        ya = _natten(qkv, _na_bias_table(na_rpb[l]), batch, seq, a_w)
        yf, yb = _rwkv_scan(r, v, nkk, lw, bb, kd, batch, seq)
        w_router = jnp.zeros((d, LANES), F32).at[:, :N_GROUPS].set(moe_w_group[l])
        w_router = w_router.at[:, N_GROUPS:N_GROUPS + N_EXPERTS].set(moe_w_expert[l])
        b_router = jnp.zeros((1, LANES), F32).at[0, :N_GROUPS].set(moe_b_group[l])
        b_router = b_router.at[0, N_GROUPS:N_GROUPS + N_EXPERTS].set(moe_b_expert[l])
        wo = w_out[l].astype(BF16)
        out_params = {
            "w_out_a": wo[:a_w], "w_out_b": wo[a_w:a_w + b_w], "w_out_c": wo[a_w + b_w:],
            "pool_w": _block_diag(pool_w[l]), "pool_scale": row(pool_scale[l]),
            "gn_gain": row(rw_gn_gain[l]), "gn_bias": row(rw_gn_bias[l]), "ones": ones_blk,
            "ln1_gain": row(ln1_gain[l]), "ln1_bias": row(ln1_bias[l]),
            "w_router": jnp.concatenate(_split_bf16(w_router), axis=1), "b_router": b_router,
        }
        x1, u2, route_i, route_w, counts = _outproj(ya, yf, yb, bonus, g, praw, x2, modl, out_params, seq, tm, alpha)
        n_blocks = -(-(t * TOP_K) // EXPERT_BLOCK) + N_EXPERTS
        dest, meta, pends = _dispatch(route_i, counts, n_blocks)
        dest3 = dest.reshape(t // tm, 1, TOP_K * tm)
        xs = _scatter_rows(pends, u2, dest3, n_blocks * EXPERT_BLOCK, tm)
        ysorted = _experts(meta, xs, moe_w_gate, moe_w_up, moe_w_down, l)
        x2 = _final(x1, ysorted, dest3, route_w, modl, row(ln2_gain[l]), row(ln2_bias[l]), seq, tm, alpha)
    return x2.reshape(batch, seq, d)
```

```python
import functools
import math

import jax
import jax.numpy as jnp
import numpy as np
from jax import lax
from jax.experimental import pallas as pl
from jax.experimental.pallas import tpu as pltpu

F32 = jnp.float32
BF16 = jnp.bfloat16
HI = lax.Precision.HIGHEST

GRID_W = 64
HEAD_DIM = 64
NA_KH = 8
NA_KW = 16
POOL_WINDOWS = (2, 4, 8, 16)
R_W = 32
R_A = 32
R_G = 64
DECAY_SCALE = math.exp(-0.5)
GN_EPS = 64e-5
N_GROUPS = 4
EXPERTS_PER_GROUP = 8
N_EXPERTS = N_GROUPS * EXPERTS_PER_GROUP
TOP_K = 2
EXPERT_BLOCK = 512
LN_EPS = 1e-5
NEG_INF = -1e30

NA_ROWS_PER_STEP = 4
SCAN_CHUNK = 64
SUBLANES = 8
HALO = 8
LANES = 128
VMEM_LIMIT = 52 * 1024 * 1024


def _ln(x):
    mu = jnp.mean(x, axis=-1, keepdims=True)
    xc = x - mu
    var = jnp.mean(xc * xc, axis=-1, keepdims=True)
    return xc * lax.rsqrt(var + LN_EPS)


def _sigmoid(x):
    return 1.0 / (1.0 + jnp.exp(-x))


def _split_bf16(x):
    hi = x.astype(BF16)
    return hi, (x - hi.astype(F32)).astype(BF16)


def _dot_split(x, w_exact):
    hi, lo = _split_bf16(x)
    return jnp.dot(hi, w_exact, preferred_element_type=F32) + jnp.dot(lo, w_exact, preferred_element_type=F32)


def _pack_bf16_pairs(x):
    h = x.shape[1] // 2
    lo = lax.bitcast_convert_type(x[:, :h].astype(BF16).astype(F32), jnp.uint32)
    hi = lax.bitcast_convert_type(x[:, h:].astype(BF16).astype(F32), jnp.uint32)
    return (lo >> 16) | hi


def _unpack_bf16_pairs(w):
    lo = lax.bitcast_convert_type(w << 16, F32).astype(BF16)
    hi = lax.bitcast_convert_type(w & jnp.uint32(0xFFFF0000), F32).astype(BF16)
    return jnp.concatenate([lo, hi], axis=1)


def _cparams(n_axes, semantics="parallel"):
    return pltpu.CompilerParams(dimension_semantics=(semantics,) * n_axes, vmem_limit_bytes=VMEM_LIMIT)


def _mod_kernel(c_ref, w_ref, b_ref, o_ref):
    c = c_ref[...]
    s = c * _sigmoid(c)
    o_ref[0] = jnp.dot(s, w_ref[0], precision=HI, preferred_element_type=F32) + b_ref[0]


def _modulation(c, w_mod, b_mod):
    n_layers, d, d6 = w_mod.shape
    b = c.shape[0]
    bp = -(-b // 8) * 8
    cp = jnp.zeros((bp, d), F32).at[:b].set(c)
    out = pl.pallas_call(
        _mod_kernel,
        grid=(n_layers, d6 // d),
        in_specs=[pl.BlockSpec((bp, d), lambda l, j: (0, 0)),
                  pl.BlockSpec((1, d, d), lambda l, j: (l, 0, j)),
                  pl.BlockSpec((1, 1, d), lambda l, j: (l, 0, j))],
        out_specs=pl.BlockSpec((1, bp, d), lambda l, j: (l, 0, j)),
        out_shape=jax.ShapeDtypeStruct((n_layers, bp, d6), F32),
        compiler_params=_cparams(2),
        name="modulation",
    )(cp, w_mod, b_mod.reshape(n_layers, 1, d6))
    return out[:, :b].reshape(n_layers, b, d6 // d, d)


def _inproj_kernel(x_ref, xp_ref, xn_ref, mod_ref, wr_ref, wo_ref, cw_ref, w0_ref, wup_ref, a0_ref, aup_ref, gup_ref,
                   kk_ref, ka_ref, rk_ref, ones_ref,
                   qkv_o, pool_o, r_o, v_o, nkk_o, lw_o, b_o, kd_o, bonus_o, g_o,
                   *, a3, b3, tiles_per_batch):
    m = mod_ref[0]
    tm = x_ref.shape[0]
    tb = pl.program_id(0) % tiles_per_batch
    xe = jnp.concatenate([xp_ref[...], x_ref[...], xn_ref[...]], axis=0)
    u = (_ln(xe) * (1.0 + m[1:2]) + m[0:1]).astype(BF16)
    h = jnp.dot(u, wr_ref[...], preferred_element_type=F32)
    ho = jnp.dot(u[HALO:HALO + tm], wo_ref[...], preferred_element_type=F32)
    hm = h[HALO:HALO + tm]
    prev = jnp.where(tb == 0, 0.0, h[HALO - 1:HALO, :b3])
    nxt = jnp.where(tb == tiles_per_batch - 1, 0.0, h[HALO + tm:HALO + tm + 1, :b3])
    _rwkv_prep_tile(hm[:, :b3], prev, nxt, hm[:, b3:],
                    cw_ref, w0_ref, wup_ref, a0_ref, aup_ref, gup_ref, kk_ref, ka_ref, rk_ref, ones_ref,
                    r_o, v_o, nkk_o, lw_o, b_o, kd_o, bonus_o, g_o)
    qkv_o[...] = ho[:, :a3].astype(BF16)
    pool_o[...] = ho[:, a3:]


def _inproj(x2, modl, w_in_bf, p, seq, tm, a3, b3, lr_w, c_w):
    t, d = x2.shape
    tpb = seq // tm
    hb = tm // HALO
    nhb = t // HALO
    width = b3 // 3
    kern = functools.partial(_inproj_kernel, a3=a3, b3=b3, tiles_per_batch=tpb)
    w_rwkv = w_in_bf[:, a3:a3 + b3 + lr_w]
    w_other = jnp.concatenate([w_in_bf[:, :a3], w_in_bf[:, a3 + b3 + lr_w:]], axis=1)
    tok = lambda i: (i, 0)
    dtok = lambda i: (0, i, 0)
    names = ["conv", "w0", "w_up", "a0", "a_up", "g_up", "k_k", "k_a", "r_k", "ones"]
    tw = jax.ShapeDtypeStruct((t, width), F32)
    dtw = jax.ShapeDtypeStruct((2, t, width), F32)
    return pl.pallas_call(
        kern,
        grid=(t // tm,),
        in_specs=[pl.BlockSpec((tm, d), tok),
                  pl.BlockSpec((HALO, d), lambda i: (jnp.maximum(i * hb - 1, 0), 0)),
                  pl.BlockSpec((HALO, d), lambda i: (jnp.minimum((i + 1) * hb, nhb - 1), 0)),
                  pl.BlockSpec((1,) + modl.shape[1:], lambda i: (i // tpb, 0, 0)),
                  pl.BlockSpec(w_rwkv.shape, lambda i: (0, 0)),
                  pl.BlockSpec(w_other.shape, lambda i: (0, 0))]
                 + [pl.BlockSpec(p[k].shape, functools.partial(lambda nd, i: (0,) * nd, p[k].ndim)) for k in names],
        out_specs=[pl.BlockSpec((tm, a3), tok), pl.BlockSpec((tm, c_w), tok),
                   pl.BlockSpec((tm, width), tok), pl.BlockSpec((tm, width), tok), pl.BlockSpec((tm, width), tok),
                   pl.BlockSpec((2, tm, width), dtok), pl.BlockSpec((2, tm, width), dtok),
                   pl.BlockSpec((2, tm, width), dtok),
                   pl.BlockSpec((tm, width), tok), pl.BlockSpec((tm, width), tok)],
        out_shape=[jax.ShapeDtypeStruct((t, a3), BF16), jax.ShapeDtypeStruct((t, c_w), F32),
                   tw, tw, tw, dtw, dtw, dtw, tw, tw],
        compiler_params=_cparams(1),
        name="inproj",
    )(x2, x2, x2, modl, w_rwkv, w_other, *[p[k] for k in names])


def _na_bias_table(rpb):
    col = np.arange(GRID_W)
    cstart = np.clip(col - NA_KW // 2, 0, GRID_W - NA_KW)
    in_win = (col[None, :] >= cstart[:, None]) & (col[None, :] < cstart[:, None] + NA_KW)
    dc = np.clip(col[None, :] - col[:, None], -(NA_KW - 1), NA_KW - 1) + (NA_KW - 1)
    pick = (dc[None] == np.arange(2 * NA_KW - 1)[:, None, None]).astype(np.float32)
    cols = jnp.einsum("hrc,cqk->hrqk", rpb.astype(F32), pick, precision=HI)
    cols = jnp.where(in_win, cols, NEG_INF)
    b = jnp.stack([cols[:, NA_KH - 1 - o:2 * NA_KH - 1 - o] for o in range(NA_KH)])
    h = rpb.shape[0]
    return jnp.transpose(b, (0, 1, 3, 2, 4)).reshape(NA_KH, h * GRID_W, NA_KH * GRID_W)


def _natten_kernel(q_ref, k_ref, v_ref, bias_ref, o_ref, *, rows, heads):
    width = q_ref.shape[1]
    nk = NA_KH * GRID_W
    head_of_lane = lax.broadcasted_iota(jnp.int32, (heads * GRID_W, width), 1) // HEAD_DIM
    head_of_row = lax.broadcasted_iota(jnp.int32, (heads * GRID_W, width), 0) // GRID_W
    own = head_of_lane == head_of_row
    for j in range(NA_ROWS_PER_STEP):
        r = pl.program_id(1) * NA_ROWS_PER_STEP + j
        rstart = jnp.clip(r - NA_KH // 2, 0, rows - NA_KH)
        off = r - rstart
        start = pl.multiple_of(rstart * GRID_W, GRID_W)
        kw = k_ref[pl.ds(start, nk), :]
        vw = v_ref[pl.ds(start, nk), :]
        q = q_ref[j * GRID_W:(j + 1) * GRID_W, :]
        qs = jnp.where(own, jnp.concatenate([q] * heads, axis=0), jnp.zeros((), q.dtype))
        s = lax.dot_general(qs, kw, (((1,), (1,)), ((), ())), preferred_element_type=F32) * (HEAD_DIM ** -0.5)
        s = s + bias_ref[off]
        mx = jnp.max(s, axis=-1, keepdims=True)
        p = jnp.exp(s - mx)
        den = jnp.sum(p, axis=-1, keepdims=True)
        o = jnp.where(own, jnp.dot(p.astype(BF16), vw, preferred_element_type=F32) / den, 0.0)
        acc = o[0:GRID_W]
        for h in range(1, heads):
            acc = acc + o[h * GRID_W:(h + 1) * GRID_W]
        o_ref[j * GRID_W:(j + 1) * GRID_W, :] = acc.astype(o_ref.dtype)


def _natten(qkv, bias_tab, batch, seq, width):
    rows = seq // GRID_W
    assert rows >= NA_KH
    heads = width // HEAD_DIM
    steps = rows // NA_ROWS_PER_STEP
    assert steps * NA_ROWS_PER_STEP == rows
    tq = NA_ROWS_PER_STEP * GRID_W
    kern = functools.partial(_natten_kernel, rows=rows, heads=heads)
    return pl.pallas_call(
        kern,
        grid=(batch, steps),
        in_specs=[pl.BlockSpec((tq, width), lambda b, r: (b * steps + r, 0)),
                  pl.BlockSpec((seq, width), lambda b, r: (b, 1)),
                  pl.BlockSpec((seq, width), lambda b, r: (b, 2)),
                  pl.BlockSpec(bias_tab.shape, lambda b, r: (0, 0, 0))],
        out_specs=pl.BlockSpec((tq, width), lambda b, r: (b * steps + r, 0)),
        out_shape=jax.ShapeDtypeStruct((batch * seq, width), BF16),
        compiler_params=_cparams(2),
        name="natten",
    )(qkv, qkv, qkv, bias_tab)


def _rwkv_prep_tile(z, prev, nxt, lr, cw_ref, w0_ref, wup_ref, a0_ref, aup_ref, gup_ref, kk_ref, ka_ref, rk_ref, ones_ref,
                    r_o, v_o, nkk_o, lw_o, b_o, kd_o, bonus_o, g_o):
    tm = z.shape[0]
    width = z.shape[1] // 3
    row = lax.broadcasted_iota(jnp.int32, z.shape, 0)
    zm1 = jnp.where(row == 0, prev, pltpu.roll(z, 1, 0))
    zp1 = jnp.where(row == tm - 1, nxt, pltpu.roll(z, tm - 1, 0))
    rkv = zm1 * cw_ref[0:1, :] + z * cw_ref[1:2, :] + zp1 * cw_ref[2:3, :]
    r = rkv[:, :width]
    k = rkv[:, width:2 * width]
    v = rkv[:, 2 * width:]
    th = jnp.tanh(lr)
    sg = _sigmoid(lr)
    ones = ones_ref[...]

    def headsum(x):
        return _dot_split(x, ones)

    kk = k * kk_ref[...]
    kk = kk * lax.rsqrt(jnp.maximum(headsum(kk * kk), 1e-24))
    g_o[...] = jnp.dot(sg.astype(BF16), gup_ref[...], preferred_element_type=F32)
    r_o[...] = r
    v_o[...] = v
    nkk_o[...] = -kk
    kd_sum = jnp.zeros_like(r)
    th_b = th.astype(BF16)
    lr_b = lr.astype(BF16)
    for d in range(2):
        wl = jnp.dot(th_b, wup_ref[d], preferred_element_type=F32) + w0_ref[d:d + 1, :]
        lw_o[d] = -DECAY_SCALE * _sigmoid(wl)
        a = _sigmoid(jnp.dot(lr_b, aup_ref[d], preferred_element_type=F32) + a0_ref[d:d + 1, :])
        kd = k * (1.0 + (a - 1.0) * ka_ref[...])
        kd_o[d] = kd
        b_o[d] = kk * a
        kd_sum = kd_sum + kd
    bonus_o[...] = headsum(r * kd_sum * rk_ref[...]) * v


def _dot_nt(a, b):
    return lax.dot_general(a, b, (((1,), (1,)), ((), ())), preferred_element_type=F32)


def _dot_tn(a, b):
    return lax.dot_general(a, b, (((0,), (0,)), ((), ())), preferred_element_type=F32)


def _mm(a, b):
    return jnp.dot(a.astype(BF16), b.astype(BF16), preferred_element_type=F32)


def _rwkv_scan_kernel(rf_ref, vf_ref, nf_ref, rb_ref, vb_ref, nb_ref, lwf_ref, bf_ref, kf_ref, lwb_ref, bb_ref, kb_ref,
                      yf_ref, yb_ref, s_ref, *, heads, batch):
    @pl.when(pl.program_id(0) == 0)
    def _():
        s_ref[...] = jnp.zeros_like(s_ref)

    n = SCAN_CHUNK
    pair_w = 2 * HEAD_DIM
    row = lax.broadcasted_iota(jnp.int32, (n, pair_w), 0)
    lane = lax.broadcasted_iota(jnp.int32, (n, pair_w), 1)
    col = lane & (HEAD_DIM - 1)
    even = lane < HEAD_DIM
    levels = n.bit_length()
    same = [(row >> k) == (col >> k) for k in range(levels)]
    eye = same[0].astype(F32)
    level_masks = [same[sh + 1] & jnp.logical_not(same[sh]) for sh in range(1, levels - 1)]

    def blockdiag(x2):
        xb = x2.astype(BF16)
        zero = jnp.zeros((), BF16)
        return jnp.concatenate([jnp.where(even, xb, zero), jnp.where(even, zero, xb)], axis=0)

    def mm(x2, y2):
        return jnp.dot(x2.astype(BF16), blockdiag(y2), preferred_element_type=F32)

    def mm_nt(x2, y2):
        return _dot_nt(x2.astype(BF16), blockdiag(y2))

    dirs = ((rf_ref, vf_ref, nf_ref, lwf_ref, bf_ref, kf_ref, yf_ref),
            (rb_ref, vb_ref, nb_ref, lwb_ref, bb_ref, kb_ref, yb_ref))
    chains = []
    for d, (r_ref, v_ref, n_ref, lw_ref, b_ref, k_ref, y_ref) in enumerate(dirs):
        order = row - col if d == 0 else col - row
        strict = order > 0
        incl = order >= 0
        incl_b = jnp.where(incl[:, :n], 1.0, 0.0).astype(BF16)
        for bi in range(batch):
            lw = lw_ref[0, bi]
            lw_hi, lw_mid = _split_bf16(lw)
            lw_lo = (lw - lw_hi.astype(F32) - lw_mid.astype(F32)).astype(BF16)
            g_inc = ((jnp.dot(incl_b, lw_lo, preferred_element_type=F32)
                      + jnp.dot(incl_b, lw_mid, preferred_element_type=F32))
                     + jnp.dot(incl_b, lw_hi, preferred_element_type=F32))
            g_tot = jnp.sum(lw, axis=0, keepdims=True)
            e_neg = jnp.exp(-g_inc)
            e_end = jnp.exp(g_tot - g_inc)
            decay = jnp.exp(g_tot)
            a_t = n_ref[bi] * jnp.exp(g_inc - lw)
            r_t = r_ref[bi] * jnp.exp(g_inc)
            bb = b_ref[0, bi]
            kd = k_ref[0, bi]
            b_t = bb * e_neg
            k_t = kd * e_neg
            ar_t = jnp.concatenate([a_t, r_t], axis=0).astype(BF16)
            bk_h = jnp.concatenate([bb * e_end, kd * e_end], axis=0).astype(BF16)
            v = v_ref[bi]
            for p in range(heads // 2):
                sl = slice(p * pair_w, (p + 1) * pair_w)
                chains.append(dict(strict=strict, incl=incl, sl=sl, bi=bi, y_ref=y_ref,
                                   si=(d * batch + bi) * (heads // 2) + p, decay=decay[:, sl],
                                   ar=ar_t[:, sl], b=b_t[:, sl], k=k_t[:, sl], bk_h=bk_h[:, sl], v=v[:, sl]))

    for ch in chains:
        pb = mm_nt(ch["ar"], ch["b"])
        pk = mm_nt(ch["ar"], ch["k"])
        ch["l_ab"] = jnp.where(ch["strict"], pb[:n], 0.0)
        ch["m_rb"] = jnp.where(ch["incl"], pb[n:], 0.0)
        ch["l_ak"] = jnp.where(ch["strict"], pk[:n], 0.0)
        ch["m_rk"] = jnp.where(ch["incl"], pk[n:], 0.0)
        ch["t"] = eye + jnp.where(same[1], ch["l_ab"], 0.0)
    for mask in level_masks:
        for ch in chains:
            ch["tc"] = mm(ch["t"], jnp.where(mask, ch["l_ab"], 0.0))
        for ch in chains:
            ch["t"] = ch["t"] + mm(ch["tc"], ch["t"])
    for ch in chains:
        ch["s0"] = s_ref[ch["si"]]
        ch["x"] = mm_nt(ch["ar"], ch["s0"])
    for ch in chains:
        ch["kv"] = mm(jnp.concatenate([ch["l_ak"], ch["m_rk"]], axis=0), ch["v"])
    for ch in chains:
        ch["u"] = mm(ch["t"], ch["x"][:n] + ch["kv"][:n])
    for ch in chains:
        y = ch["x"][n:] + mm(ch["m_rb"], ch["u"]) + ch["kv"][n:]
        ch["y_ref"][ch["bi"], :, ch["sl"]] = y
    for ch in chains:
        uv = jnp.concatenate([ch["u"], ch["v"]], axis=0).astype(BF16)
        full = _dot_tn(uv, ch["bk_h"])
        s_ref[ch["si"]] = ch["s0"] * ch["decay"] + jnp.where(even, full[:HEAD_DIM], full[HEAD_DIM:])


def _rwkv_scan(r, v, nkk, lw, b, kd, batch, seq):
    t, width = r.shape
    heads = width // HEAD_DIM
    n = SCAN_CHUNK
    nc = seq // n
    r3, v3, n3 = (z.reshape(batch, seq, width) for z in (r, v, nkk))
    lw4, b4, k4 = (z.reshape(2, batch, seq, width) for z in (lw, b, kd))
    fwd = pl.BlockSpec((batch, n, width), lambda c: (0, c, 0))
    bwd = pl.BlockSpec((batch, n, width), lambda c: (0, nc - 1 - c, 0))
    fwd_d = pl.BlockSpec((1, batch, n, width), lambda c: (0, 0, c, 0))
    bwd_d = pl.BlockSpec((1, batch, n, width), lambda c: (1, 0, nc - 1 - c, 0))
    kern = functools.partial(_rwkv_scan_kernel, heads=heads, batch=batch)
    yf, yb = pl.pallas_call(
        kern,
        grid=(nc,),
        in_specs=[fwd, fwd, fwd, bwd, bwd, bwd, fwd_d, fwd_d, fwd_d, bwd_d, bwd_d, bwd_d],
        out_specs=[fwd, bwd],
        out_shape=[jax.ShapeDtypeStruct((batch, seq, width), F32)] * 2,
        scratch_shapes=[pltpu.VMEM((batch * heads, HEAD_DIM, 2 * HEAD_DIM), F32)],
        compiler_params=_cparams(1, "arbitrary"),
        name="rwkv_scan",
    )(r3, v3, n3, r3, v3, n3, lw4, b4, k4, lw4, b4, k4)
    return yf.reshape(t, width), yb.reshape(t, width)


def _pool_tile(p_ref, pp_ref, pn_ref, w_ref, sc_ref, ext_ref, tb, tiles_per_batch, seq):
    p = p_ref[...]
    tm, width = p.shape
    ext_ref[0:HALO, :] = jnp.where(tb == 0, 0.0, pp_ref[...])
    ext_ref[HALO:HALO + tm, :] = p
    ext_ref[HALO + tm:2 * HALO + tm, :] = jnp.where(tb == tiles_per_batch - 1, 0.0, pn_ref[...])

    def shifted(o):
        return ext_ref[HALO + o:HALO + o + tm, :]

    t = tb * tm + lax.broadcasted_iota(jnp.int32, (tm, width), 0)
    grp = lax.broadcasted_iota(jnp.int32, (tm, width), 1) // (width // len(POOL_WINDOWS))
    tot = p
    prev_half = 0
    pooled = jnp.zeros_like(p)
    for gi, win in enumerate(POOL_WINDOWS):
        half = win // 2
        for o in range(prev_half, half):
            tot = tot + shifted(-o - 1)
            if o > 0:
                tot = tot + shifted(o)
        prev_half = half
        lo = jnp.clip(t - half, 0, seq - 1)
        hi = jnp.clip(t + half - 1, 0, seq - 1)
        cnt = (hi - lo + 1).astype(F32)
        pooled = jnp.where(grp == gi, tot / cnt, pooled)
    pooled = pooled - p
    return jnp.dot(pooled, w_ref[...], preferred_element_type=F32) * sc_ref[...]


def _outproj_kernel(ya_ref, yf_ref, yb_ref, bonus_ref, g_ref, p_ref, pp_ref, pn_ref, x_ref, mod_ref,
                    wa_ref, wb_ref, wc_ref, pw_ref, psc_ref,
                    gng_ref, gnb_ref, ones_ref, l1g_ref, l1b_ref, wr_ref, br_ref,
                    x1_o, u2_o, ri_o, rw_o, cnt_o, cnt_ref, ext_ref, *, alpha, tiles_per_batch, seq):
    yc = _pool_tile(p_ref, pp_ref, pn_ref, pw_ref, psc_ref, ext_ref,
                    pl.program_id(0) % tiles_per_batch, tiles_per_batch, seq)
    m = mod_ref[0]
    ones = ones_ref[...]

    def headmean(x):
        return _dot_split(x, ones) * (1.0 / HEAD_DIM)

    ysum = yf_ref[...] + yb_ref[...]
    yc0 = ysum - headmean(ysum)
    yn = yc0 * lax.rsqrt(headmean(yc0 * yc0) + GN_EPS) * gng_ref[...] + gnb_ref[...]
    yb = (yn + bonus_ref[...]) * g_ref[...]
    mix = (jnp.dot(ya_ref[...].astype(BF16), wa_ref[...], preferred_element_type=F32)
           + jnp.dot(yb.astype(BF16), wb_ref[...], preferred_element_type=F32)
           + jnp.dot(yc.astype(BF16), wc_ref[...], preferred_element_type=F32))
    x1 = _ln(alpha * x_ref[...] + m[2:3] * mix) * l1g_ref[...] + l1b_ref[...]
    x1_o[...] = x1
    u2 = _ln(x1) * (1.0 + m[4:5]) + m[3:4]
    u2_o[...] = _pack_bf16_pairs(u2)

    u_hi, u_lo = _split_bf16(u2)
    hi_both = jnp.dot(u_hi, wr_ref[...], preferred_element_type=F32)
    lg = (hi_both[:, :LANES] + hi_both[:, LANES:]
          + jnp.dot(u_lo, wr_ref[:, :LANES], preferred_element_type=F32)) + br_ref[...]
    lane = lax.broadcasted_iota(jnp.int32, lg.shape, 1)
    big = jnp.int32(1 << 20)
    gl = jnp.where(lane < N_GROUPS, lg, -jnp.inf)
    gmax = jnp.max(gl, axis=-1, keepdims=True)
    gidx = jnp.min(jnp.where(gl == gmax, lane, big), axis=-1, keepdims=True)
    pg_sel = 1.0 / jnp.sum(jnp.exp(gl - gmax), axis=-1, keepdims=True)
    e_lo = N_GROUPS + gidx * EXPERTS_PER_GROUP
    el = jnp.where((lane >= e_lo) & (lane < e_lo + EXPERTS_PER_GROUP), lg, -jnp.inf)
    m1 = jnp.max(el, axis=-1, keepdims=True)
    i1 = jnp.min(jnp.where(el == m1, lane, big), axis=-1, keepdims=True)
    el2 = jnp.where(lane == i1, -jnp.inf, el)
    m2 = jnp.max(el2, axis=-1, keepdims=True)
    i2 = jnp.min(jnp.where(el2 == m2, lane, big), axis=-1, keepdims=True)
    e21 = jnp.exp(m2 - m1)
    p1 = 1.0 / (1.0 + e21)
    p2 = e21 / (1.0 + e21)
    rw_o[...] = jnp.where(lane == 0, pg_sel * p1, jnp.where(lane == 1, pg_sel * p2, 0.0))

    @pl.when(pl.program_id(0) == 0)
    def _():
        cnt_ref[...] = jnp.zeros_like(cnt_ref)

    tm = lg.shape[0]
    earlier = (lax.broadcasted_iota(jnp.int32, (tm, tm), 1)
               < lax.broadcasted_iota(jnp.int32, (tm, tm), 0)).astype(BF16)
    oh1 = (lane == i1).astype(F32)
    oh2 = (lane == i2).astype(F32)
    run = cnt_ref[...]
    c1 = jnp.sum(oh1, axis=0, keepdims=True)
    before1 = run + jnp.dot(earlier, oh1.astype(BF16), preferred_element_type=F32)
    before2 = run + c1 + jnp.dot(earlier, oh2.astype(BF16), preferred_element_type=F32)
    rank1 = jnp.sum(oh1 * before1, axis=-1, keepdims=True).astype(jnp.int32)
    rank2 = jnp.sum(oh2 * before2, axis=-1, keepdims=True).astype(jnp.int32)
    total = run + c1 + jnp.sum(oh2, axis=0, keepdims=True)
    cnt_ref[...] = total
    cnt_o[...] = total
    ri_o[...] = jnp.where(lane == 0, i1 - N_GROUPS, jnp.where(lane == 1, i2 - N_GROUPS,
                          jnp.where(lane == 2, rank1, jnp.where(lane == 3, rank2, 0))))


def _outproj(ya, yf, yb, bonus, g, praw, x2, modl, p, seq, tm, alpha):
    t, d = x2.shape
    tpb = seq // tm
    aw, bw, cw = ya.shape[1], bonus.shape[1], praw.shape[1]
    hb = tm // HALO
    nhb = t // HALO
    tok = lambda i: (i, 0)
    full2 = lambda i: (0, 0)
    kern = functools.partial(_outproj_kernel, alpha=alpha, tiles_per_batch=tpb, seq=seq)
    small = ["w_out_a", "w_out_b", "w_out_c", "pool_w", "pool_scale",
             "gn_gain", "gn_bias", "ones", "ln1_gain", "ln1_bias", "w_router", "b_router"]
    return pl.pallas_call(
        kern,
        grid=(t // tm,),
        in_specs=[pl.BlockSpec((tm, aw), tok),
                  pl.BlockSpec((tm, bw), tok), pl.BlockSpec((tm, bw), tok),
                  pl.BlockSpec((tm, bw), tok), pl.BlockSpec((tm, bw), tok),
                  pl.BlockSpec((tm, cw), tok),
                  pl.BlockSpec((HALO, cw), lambda i: (jnp.maximum(i * hb - 1, 0), 0)),
                  pl.BlockSpec((HALO, cw), lambda i: (jnp.minimum((i + 1) * hb, nhb - 1), 0)),
                  pl.BlockSpec((tm, d), tok),
                  pl.BlockSpec((1,) + modl.shape[1:], lambda i: (i // tpb, 0, 0))]
                 + [pl.BlockSpec(p[k].shape, functools.partial(lambda nd, i: (0,) * nd, p[k].ndim)) for k in small],
        out_specs=[pl.BlockSpec((tm, d), tok), pl.BlockSpec((tm, d // 2), tok),
                   pl.BlockSpec((tm, LANES), tok), pl.BlockSpec((tm, LANES), tok),
                   pl.BlockSpec((1, LANES), full2)],
        out_shape=[jax.ShapeDtypeStruct((t, d), F32), jax.ShapeDtypeStruct((t, d // 2), jnp.uint32),
                   jax.ShapeDtypeStruct((t, LANES), jnp.int32), jax.ShapeDtypeStruct((t, LANES), F32),
                   jax.ShapeDtypeStruct((1, LANES), F32)],
        scratch_shapes=[pltpu.VMEM((1, LANES), F32), pltpu.VMEM((tm + 2 * HALO, cw), F32)],
        compiler_params=_cparams(1, "arbitrary"),
        name="outproj",
    )(ya, yf, yb, bonus, g, praw, praw, praw, x2, modl, *[p[k] for k in small])


def _dispatch(route_i, counts_lanes, n_blocks):
    counts = counts_lanes[0, N_GROUPS:N_GROUPS + N_EXPERTS].astype(jnp.int32)
    padded = ((counts + EXPERT_BLOCK - 1) // EXPERT_BLOCK) * EXPERT_BLOCK
    pends = jnp.cumsum(padded)
    pstarts = pends - padded
    e = route_i[:, :TOP_K]
    rank = route_i[:, TOP_K:2 * TOP_K]
    ids = jnp.arange(N_EXPERTS, dtype=jnp.int32)
    dest = jnp.sum(jnp.where(e[..., None] == ids, pstarts, 0), axis=-1) + rank
    block_start = jnp.arange(n_blocks, dtype=jnp.int32) * EXPERT_BLOCK
    block_e = jnp.minimum(jnp.sum((pends[None, :] <= block_start[:, None]).astype(jnp.int32), axis=1), N_EXPERTS - 1)
    meta = jnp.concatenate([block_e, (pends[-1] // EXPERT_BLOCK)[None]]).astype(jnp.int32)
    return dest, meta, pends.astype(jnp.int32)


def _scatter_rows_kernel(pends_ref, dest_ref, u_ref, xs_ref, zeros_ref, sem, zsem):
    tm = u_ref.shape[0] * SUBLANES

    @pl.when(pl.program_id(0) == 0)
    def _():
        zeros_ref[...] = jnp.zeros_like(zeros_ref)

        def tail_copy(e):
            tail = pl.ds(pl.multiple_of(pends_ref[e] - EXPERT_BLOCK, EXPERT_BLOCK), EXPERT_BLOCK)
            return pltpu.make_async_copy(zeros_ref, xs_ref.at[tail], zsem)

        def has_rows(e):
            return pends_ref[e] > (pends_ref[e - 1] if e > 0 else 0)

        def unused_copy(j):
            return pltpu.make_async_copy(zeros_ref, xs_ref.at[pl.ds(j * EXPERT_BLOCK, EXPERT_BLOCK)], zsem)

        def is_unused(j):
            return j * EXPERT_BLOCK >= pends_ref[N_EXPERTS - 1]

        n_blocks = xs_ref.shape[0] // EXPERT_BLOCK
        for e in range(N_EXPERTS):
            pl.when(has_rows(e))(lambda e=e: tail_copy(e).start())
        for j in range(n_blocks):
            pl.when(is_unused(j))(lambda j=j: unused_copy(j).start())
        for e in range(N_EXPERTS):
            pl.when(has_rows(e))(lambda e=e: tail_copy(e).wait())
        for j in range(n_blocks):
            pl.when(is_unused(j))(lambda j=j: unused_copy(j).wait())

    def issue(grp, carry):
        for j in range(SUBLANES):
            for k in range(TOP_K):
                dst = dest_ref[0, 0, TOP_K * SUBLANES * grp + TOP_K * j + k]
                pltpu.make_async_copy(u_ref.at[grp, pl.ds(j, 1)], xs_ref.at[pl.ds(dst, 1)], sem).start()
        return carry

    lax.fori_loop(0, tm // SUBLANES, issue, 0)
    rows = pl.ds(0, TOP_K * tm)
    pltpu.make_async_copy(xs_ref.at[rows], xs_ref.at[rows], sem).wait()


def _scatter_rows(pends, u2, dest3, total, tm):
    t, d = u2.shape
    grid_spec = pltpu.PrefetchScalarGridSpec(
        num_scalar_prefetch=1,
        grid=(t // tm,),
        in_specs=[pl.BlockSpec((1, 1, TOP_K * tm), lambda i, p: (i, 0, 0), memory_space=pltpu.SMEM),
                  pl.BlockSpec((tm // SUBLANES, SUBLANES, d), lambda i, p: (i, 0, 0))],
        out_specs=pl.BlockSpec(memory_space=pl.ANY),
        scratch_shapes=[pltpu.VMEM((EXPERT_BLOCK, d), u2.dtype), pltpu.SemaphoreType.DMA(()),
                        pltpu.SemaphoreType.DMA(())],
    )
    return pl.pallas_call(
        _scatter_rows_kernel,
        grid_spec=grid_spec,
        out_shape=jax.ShapeDtypeStruct((total, d), u2.dtype),
        compiler_params=_cparams(1, "arbitrary"),
        name="scatter_rows",
    )(pends, dest3, u2.reshape(t // SUBLANES, SUBLANES, d))


def _experts_kernel(meta_ref, xs_ref, wg_ref, wu_ref, wd_ref, o_ref, wg_b, wu_b, wd_b):
    i = pl.program_id(0)
    n_used = meta_ref[pl.num_programs(0)]

    @pl.when((i == 0) | (meta_ref[i] != meta_ref[jnp.maximum(i - 1, 0)]))
    def _():
        wg_b[...] = wg_ref[0, 0].astype(BF16)
        wu_b[...] = wu_ref[0, 0].astype(BF16)
        wd_b[...] = wd_ref[0, 0].astype(BF16)

    @pl.when(i < n_used)
    def _():
        xb = _unpack_bf16_pairs(xs_ref[...])
        gate = jnp.dot(xb, wg_b[...], preferred_element_type=F32)
        up = jnp.dot(xb, wu_b[...], preferred_element_type=F32)
        hb = gate * _sigmoid(gate) * up
        o_ref[...] = _pack_bf16_pairs(jnp.dot(hb.astype(BF16), wd_b[...], preferred_element_type=F32))

    @pl.when(i >= n_used)
    def _():
        o_ref[...] = jnp.zeros_like(o_ref)


def _experts(meta, xs, wg, wu, wd, layer):
    total, dp = xs.shape
    nb = total // EXPERT_BLOCK
    d, de = wg.shape[2:]
    grid_spec = pltpu.PrefetchScalarGridSpec(
        num_scalar_prefetch=1,
        grid=(nb,),
        in_specs=[pl.BlockSpec((EXPERT_BLOCK, dp), lambda i, m: (jnp.minimum(i, m[nb] - 1), 0)),
                  pl.BlockSpec((1, 1, d, de), lambda i, m: (layer, m[i], 0, 0)),
                  pl.BlockSpec((1, 1, d, de), lambda i, m: (layer, m[i], 0, 0)),
                  pl.BlockSpec((1, 1, de, d), lambda i, m: (layer, m[i], 0, 0))],
        out_specs=pl.BlockSpec((EXPERT_BLOCK, dp), lambda i, m: (i, 0)),
        scratch_shapes=[pltpu.VMEM((d, de), BF16), pltpu.VMEM((d, de), BF16), pltpu.VMEM((de, d), BF16)],
    )
    return pl.pallas_call(
        _experts_kernel,
        grid_spec=grid_spec,
        out_shape=jax.ShapeDtypeStruct((total, dp), xs.dtype),
        compiler_params=_cparams(1, "arbitrary"),
        name="experts",
    )(meta, xs, wg, wu, wd)


def _final_kernel(dcur_ref, dnext_ref, x1_ref, rw_ref, mod_ref, g_ref, b_ref, ys_ref, o_ref, ybuf, sem, *, alpha):
    i = pl.program_id(0)
    tm = x1_ref.shape[0]
    slot = i % 2

    def gather(d_ref, s):
        def issue(grp, carry):
            for j in range(SUBLANES):
                for k in range(TOP_K):
                    src = d_ref[0, 0, TOP_K * SUBLANES * grp + TOP_K * j + k]
                    pltpu.make_async_copy(ys_ref.at[pl.ds(src, 1)], ybuf.at[s, k, grp, pl.ds(j, 1)],
                                          sem.at[s]).start()
            return carry

        lax.fori_loop(0, tm // SUBLANES, issue, 0)

    @pl.when(i == 0)
    def _():
        gather(dcur_ref, 0)

    @pl.when(i + 1 < pl.num_programs(0))
    def _():
        gather(dnext_ref, 1 - slot)

    pltpu.make_async_copy(ybuf.at[slot], ybuf.at[slot], sem.at[slot]).wait()
    m = mod_ref[0]
    rw = rw_ref[...]
    dp = ybuf.shape[-1]
    y1 = _unpack_bf16_pairs(ybuf[slot, 0].reshape(tm, dp)).astype(F32)
    y2 = _unpack_bf16_pairs(ybuf[slot, 1].reshape(tm, dp)).astype(F32)
    f = rw[:, 0:1] * y1 + rw[:, 1:2] * y2
    o_ref[...] = _ln(alpha * x1_ref[...] + m[5:6] * f) * g_ref[...] + b_ref[...]


def _final(x1, ysorted, dest3, rw, modl, gain, bias, seq, tm, alpha):
    t, d = x1.shape
    tpb = seq // tm
    n_tiles = t // tm
    tok = lambda i: (i, 0)
    kern = functools.partial(_final_kernel, alpha=alpha)
    dspec = lambda f: pl.BlockSpec((1, 1, TOP_K * tm), f, memory_space=pltpu.SMEM)
    return pl.pallas_call(
        kern,
        grid=(n_tiles,),
        in_specs=[dspec(lambda i: (i, 0, 0)), dspec(lambda i: (jnp.minimum(i + 1, n_tiles - 1), 0, 0)),
                  pl.BlockSpec((tm, d), tok), pl.BlockSpec((tm, LANES), tok),
                  pl.BlockSpec((1,) + modl.shape[1:], lambda i: (i // tpb, 0, 0)),
                  pl.BlockSpec(gain.shape, lambda i: (0, 0)), pl.BlockSpec(bias.shape, lambda i: (0, 0)),
                  pl.BlockSpec(memory_space=pl.ANY)],
        out_specs=pl.BlockSpec((tm, d), tok),
        out_shape=jax.ShapeDtypeStruct((t, d), F32),
        scratch_shapes=[pltpu.VMEM((2, TOP_K, tm // SUBLANES, SUBLANES, ysorted.shape[1]), ysorted.dtype),
                        pltpu.SemaphoreType.DMA((2,))],
        compiler_params=_cparams(1, "arbitrary"),
        name="final_ln",
    )(dest3, dest3, x1, rw, modl, gain, bias, ysorted)


def _block_diag(blocks):
    n, a, b = blocks.shape
    out = jnp.zeros((n * a, n * b), blocks.dtype)
    for i in range(n):
        out = out.at[i * a:(i + 1) * a, i * b:(i + 1) * b].set(blocks[i])
    return out


def _pad_rows(w, lo, total):
    return jnp.zeros((total, w.shape[-1]), w.dtype).at[lo:lo + w.shape[0]].set(w)


def kernel(x, c, w_mod, b_mod, w_in, na_rpb, rw_conv, rw_w0, rw_w_up, rw_a0, rw_a_up, rw_g_up, rw_k_k, rw_k_a, rw_r_k, rw_gn_gain, rw_gn_bias, pool_w, pool_scale, w_out, ln1_gain, ln1_bias, ln2_gain, ln2_bias, moe_w_group, moe_b_group, moe_w_expert, moe_b_expert, moe_w_gate, moe_w_up, moe_w_down):
    batch, seq, d = x.shape
    depth = w_mod.shape[0]
    t = batch * seq
    a_w = na_rpb.shape[1] * HEAD_DIM
    b_w = rw_w0.shape[-1]
    c_w = pool_scale.shape[-1]
    lr_w = R_W + R_A + R_G
    alpha = (2 * depth) ** 0.25
    tm = min(512, seq)
    tm_in = min(512, seq)
    assert seq % tm == 0 and seq % SCAN_CHUNK == 0 and seq % GRID_W == 0 and lr_w == LANES

    mod = _modulation(c, w_mod, b_mod)
    ones_blk = _block_diag(jnp.ones((b_w // HEAD_DIM, HEAD_DIM, HEAD_DIM), BF16))
    row = lambda v: v.reshape(1, -1)

    x2 = x.reshape(t, d)
    for l in range(depth):
        modl = mod[l]
        prep_params = {
            "conv": rw_conv[l], "w0": rw_w0[l], "a0": rw_a0[l],
            "w_up": jnp.stack([_pad_rows(rw_w_up[l, dd], 0, lr_w) for dd in range(2)]).astype(BF16),
            "a_up": jnp.stack([_pad_rows(rw_a_up[l, dd], R_W, lr_w) for dd in range(2)]).astype(BF16),
            "g_up": _pad_rows(rw_g_up[l], R_W + R_A, lr_w).astype(BF16),
            "k_k": row(rw_k_k[l]), "k_a": row(rw_k_a[l]), "r_k": row(rw_r_k[l]), "ones": ones_blk,
        }
        qkv, praw, r, v, nkk, lw, bb, kd, bonus, g = _inproj(x2, modl, w_in[l].astype(BF16), prep_params, seq, tm_in,
                                                             3 * a_w, 3 * b_w, lr_w, c_w)
        ya = _natten(qkv, _na_bias_table(na_rpb[l]), batch, seq, a_w)
        yf, yb = _rwkv_scan(r, v, nkk, lw, bb, kd, batch, seq)
        w_router = jnp.zeros((d, LANES), F32).at[:, :N_GROUPS].set(moe_w_group[l])
        w_router = w_router.at[:, N_GROUPS:N_GROUPS + N_EXPERTS].set(moe_w_expert[l])
        b_router = jnp.zeros((1, LANES), F32).at[0, :N_GROUPS].set(moe_b_group[l])
        b_router = b_router.at[0, N_GROUPS:N_GROUPS + N_EXPERTS].set(moe_b_expert[l])
        wo = w_out[l].astype(BF16)
        out_params = {
            "w_out_a": wo[:a_w], "w_out_b": wo[a_w:a_w + b_w], "w_out_c": wo[a_w + b_w:],
            "pool_w": _block_diag(pool_w[l]), "pool_scale": row(pool_scale[l]),
            "gn_gain": row(rw_gn_gain[l]), "gn_bias": row(rw_gn_bias[l]), "ones": ones_blk,
            "ln1_gain": row(ln1_gain[l]), "ln1_bias": row(ln1_bias[l]),
            "w_router": jnp.concatenate(_split_bf16(w_router), axis=1), "b_router": b_router,
        }
        x1, u2, route_i, route_w, counts = _outproj(ya, yf, yb, bonus, g, praw, x2, modl, out_params, seq, tm, alpha)
        n_blocks = -(-(t * TOP_K) // EXPERT_BLOCK) + N_EXPERTS
        dest, meta, pends = _dispatch(route_i, counts, n_blocks)
        dest3 = dest.reshape(t // tm, 1, TOP_K * tm)
        xs = _scatter_rows(pends, u2, dest3, n_blocks * EXPERT_BLOCK, tm)
        ysorted = _experts(meta, xs, moe_w_gate, moe_w_up, moe_w_down, l)
        x2 = _final(x1, ysorted, dest3, route_w, modl, row(ln2_gain[l]), row(ln2_bias[l]), seq, tm, alpha)
    return x2.reshape(batch, seq, d)
```

```python
import functools
import math

import jax
import jax.numpy as jnp
import numpy as np
from jax import lax
from jax.experimental import pallas as pl
from jax.experimental.pallas import tpu as pltpu

F32 = jnp.float32
BF16 = jnp.bfloat16
HI = lax.Precision.HIGHEST

GRID_W = 64
HEAD_DIM = 64
NA_KH = 8
NA_KW = 16
POOL_WINDOWS = (2, 4, 8, 16)
R_W = 32
R_A = 32
R_G = 64
DECAY_SCALE = math.exp(-0.5)
GN_EPS = 64e-5
N_GROUPS = 4
EXPERTS_PER_GROUP = 8
N_EXPERTS = N_GROUPS * EXPERTS_PER_GROUP
TOP_K = 2
EXPERT_BLOCK = 512
LN_EPS = 1e-5
NEG_INF = -1e30

NA_ROWS_PER_STEP = 8
SCAN_CHUNK = 64
SUBLANES = 8
HALO = 8
LANES = 128
VMEM_LIMIT = 52 * 1024 * 1024


def _ln(x):
    mu = jnp.mean(x, axis=-1, keepdims=True)
    xc = x - mu
    var = jnp.mean(xc * xc, axis=-1, keepdims=True)
    return xc * lax.rsqrt(var + LN_EPS)


def _sigmoid(x):
    return 1.0 / (1.0 + jnp.exp(-x))


def _split_bf16(x):
    hi = x.astype(BF16)
    return hi, (x - hi.astype(F32)).astype(BF16)


def _dot_split(x, w_exact):
    hi, lo = _split_bf16(x)
    return jnp.dot(hi, w_exact, preferred_element_type=F32) + jnp.dot(lo, w_exact, preferred_element_type=F32)


def _pack_bf16_pairs(x):
    h = x.shape[1] // 2
    lo = lax.bitcast_convert_type(x[:, :h].astype(BF16).astype(F32), jnp.uint32)
    hi = lax.bitcast_convert_type(x[:, h:].astype(BF16).astype(F32), jnp.uint32)
    return (lo >> 16) | hi


def _unpack_bf16_pairs(w):
    lo = lax.bitcast_convert_type(w << 16, F32).astype(BF16)
    hi = lax.bitcast_convert_type(w & jnp.uint32(0xFFFF0000), F32).astype(BF16)
    return jnp.concatenate([lo, hi], axis=1)


def _cparams(n_axes, semantics="parallel"):
    return pltpu.CompilerParams(dimension_semantics=(semantics,) * n_axes, vmem_limit_bytes=VMEM_LIMIT)


def _mod_kernel(c_ref, w_ref, b_ref, o_ref):
    c = c_ref[...]
    s = c * _sigmoid(c)
    o_ref[0] = jnp.dot(s, w_ref[0], precision=HI, preferred_element_type=F32) + b_ref[0]


def _modulation(c, w_mod, b_mod):
    n_layers, d, d6 = w_mod.shape
    b = c.shape[0]
    bp = -(-b // 8) * 8
    cp = jnp.zeros((bp, d), F32).at[:b].set(c)
    out = pl.pallas_call(
        _mod_kernel,
        grid=(n_layers, d6 // d),
        in_specs=[pl.BlockSpec((bp, d), lambda l, j: (0, 0)),
                  pl.BlockSpec((1, d, d), lambda l, j: (l, 0, j)),
                  pl.BlockSpec((1, 1, d), lambda l, j: (l, 0, j))],
        out_specs=pl.BlockSpec((1, bp, d), lambda l, j: (l, 0, j)),
        out_shape=jax.ShapeDtypeStruct((n_layers, bp, d6), F32),
        compiler_params=_cparams(2),
        name="modulation",
    )(cp, w_mod, b_mod.reshape(n_layers, 1, d6))
    return out[:, :b].reshape(n_layers, b, d6 // d, d)


def _inproj_kernel(x_ref, xp_ref, xn_ref, mod_ref, wr_ref, wo_ref, cw_ref, w0_ref, wup_ref, a0_ref, aup_ref, gup_ref,
                   kk_ref, ka_ref, rk_ref, ones_ref,
                   qkv_o, pool_o, r_o, v_o, nkk_o, lw_o, b_o, kd_o, bonus_o, g_o,
                   *, a3, b3, tiles_per_batch):
    m = mod_ref[0]
    tm = x_ref.shape[0]
    tb = pl.program_id(0) % tiles_per_batch
    xe = jnp.concatenate([xp_ref[...], x_ref[...], xn_ref[...]], axis=0)
    u = (_ln(xe) * (1.0 + m[1:2]) + m[0:1]).astype(BF16)
    h = jnp.dot(u, wr_ref[...], preferred_element_type=F32)
    ho = jnp.dot(u[HALO:HALO + tm], wo_ref[...], preferred_element_type=F32)
    hm = h[HALO:HALO + tm]
    prev = jnp.where(tb == 0, 0.0, h[HALO - 1:HALO, :b3])
    nxt = jnp.where(tb == tiles_per_batch - 1, 0.0, h[HALO + tm:HALO + tm + 1, :b3])
    _rwkv_prep_tile(hm[:, :b3], prev, nxt, hm[:, b3:],
                    cw_ref, w0_ref, wup_ref, a0_ref, aup_ref, gup_ref, kk_ref, ka_ref, rk_ref, ones_ref,
                    r_o, v_o, nkk_o, lw_o, b_o, kd_o, bonus_o, g_o)
    qkv_o[...] = ho[:, :a3].astype(BF16)
    pool_o[...] = ho[:, a3:]


def _inproj(x2, modl, w_in_bf, p, seq, tm, a3, b3, lr_w, c_w):
    t, d = x2.shape
    tpb = seq // tm
    hb = tm // HALO
    nhb = t // HALO
    width = b3 // 3
    kern = functools.partial(_inproj_kernel, a3=a3, b3=b3, tiles_per_batch=tpb)
    w_rwkv = w_in_bf[:, a3:a3 + b3 + lr_w]
    w_other = jnp.concatenate([w_in_bf[:, :a3], w_in_bf[:, a3 + b3 + lr_w:]], axis=1)
    tok = lambda i: (i, 0)
    dtok = lambda i: (0, i, 0)
    names = ["conv", "w0", "w_up", "a0", "a_up", "g_up", "k_k", "k_a", "r_k", "ones"]
    tw = jax.ShapeDtypeStruct((t, width), F32)
    dtw = jax.ShapeDtypeStruct((2, t, width), F32)
    return pl.pallas_call(
        kern,
        grid=(t // tm,),
        in_specs=[pl.BlockSpec((tm, d), tok),
                  pl.BlockSpec((HALO, d), lambda i: (jnp.maximum(i * hb - 1, 0), 0)),
                  pl.BlockSpec((HALO, d), lambda i: (jnp.minimum((i + 1) * hb, nhb - 1), 0)),
                  pl.BlockSpec((1,) + modl.shape[1:], lambda i: (i // tpb, 0, 0)),
                  pl.BlockSpec(w_rwkv.shape, lambda i: (0, 0)),
                  pl.BlockSpec(w_other.shape, lambda i: (0, 0))]
                 + [pl.BlockSpec(p[k].shape, functools.partial(lambda nd, i: (0,) * nd, p[k].ndim)) for k in names],
        out_specs=[pl.BlockSpec((tm, a3), tok), pl.BlockSpec((tm, c_w), tok),
                   pl.BlockSpec((tm, width), tok), pl.BlockSpec((tm, width), tok), pl.BlockSpec((tm, width), tok),
                   pl.BlockSpec((2, tm, width), dtok), pl.BlockSpec((2, tm, width), dtok),
                   pl.BlockSpec((2, tm, width), dtok),
                   pl.BlockSpec((tm, width), tok), pl.BlockSpec((tm, width), tok)],
        out_shape=[jax.ShapeDtypeStruct((t, a3), BF16), jax.ShapeDtypeStruct((t, c_w), F32),
                   tw, tw, tw, dtw, dtw, dtw, tw, tw],
        compiler_params=_cparams(1),
        name="inproj",
    )(x2, x2, x2, modl, w_rwkv, w_other, *[p[k] for k in names])


def _na_bias_table(rpb):
    col = np.arange(GRID_W)
    cstart = np.clip(col - NA_KW // 2, 0, GRID_W - NA_KW)
    in_win = (col[None, :] >= cstart[:, None]) & (col[None, :] < cstart[:, None] + NA_KW)
    dc = np.clip(col[None, :] - col[:, None], -(NA_KW - 1), NA_KW - 1) + (NA_KW - 1)
    pick = (dc[None] == np.arange(2 * NA_KW - 1)[:, None, None]).astype(np.float32)
    cols = jnp.einsum("hrc,cqk->hqrk", rpb.astype(F32), pick, precision=HI)
    cols = jnp.where(in_win[:, None, :], cols, NEG_INF)
    h = rpb.shape[0]
    return jnp.stack([cols[:, :, NA_KH - 1 - o:2 * NA_KH - 1 - o].reshape(h * GRID_W, NA_KH * GRID_W)
                      for o in range(NA_KH)])


def _natten_kernel(q_ref, k_ref, v_ref, bias_ref, o_ref, *, rows, heads):
    width = q_ref.shape[1]
    nk = NA_KH * GRID_W
    head_of_lane = lax.broadcasted_iota(jnp.int32, (heads * GRID_W, width), 1) // HEAD_DIM
    head_of_row = lax.broadcasted_iota(jnp.int32, (heads * GRID_W, width), 0) // GRID_W
    own = head_of_lane == head_of_row
    for j in range(NA_ROWS_PER_STEP):
        r = pl.program_id(1) * NA_ROWS_PER_STEP + j
        rstart = jnp.clip(r - NA_KH // 2, 0, rows - NA_KH)
        off = r - rstart
        start = pl.multiple_of(rstart * GRID_W, GRID_W)
        kw = k_ref[pl.ds(start, nk), :]
        vw = v_ref[pl.ds(start, nk), :]
        q = q_ref[j * GRID_W:(j + 1) * GRID_W, :]
        qs = jnp.where(own, jnp.concatenate([q] * heads, axis=0), jnp.zeros((), q.dtype))
        s = lax.dot_general(qs, kw, (((1,), (1,)), ((), ())), preferred_element_type=F32) * (HEAD_DIM ** -0.5)
        s = s + bias_ref[off]
        mx = jnp.max(s, axis=-1, keepdims=True)
        p = jnp.exp(s - mx)
        den = jnp.sum(p, axis=-1, keepdims=True)
        o = jnp.where(own, jnp.dot(p.astype(BF16), vw, preferred_element_type=F32) / den, 0.0)
        acc = o[0:GRID_W]
        for h in range(1, heads):
            acc = acc + o[h * GRID_W:(h + 1) * GRID_W]
        o_ref[j * GRID_W:(j + 1) * GRID_W, :] = acc.astype(o_ref.dtype)


def _natten(qkv, bias_tab, batch, seq, width):
    rows = seq // GRID_W
    assert rows >= NA_KH
    heads = width // HEAD_DIM
    steps = rows // NA_ROWS_PER_STEP
    assert steps * NA_ROWS_PER_STEP == rows
    tq = NA_ROWS_PER_STEP * GRID_W
    kern = functools.partial(_natten_kernel, rows=rows, heads=heads)
    return pl.pallas_call(
        kern,
        grid=(batch, steps),
        in_specs=[pl.BlockSpec((tq, width), lambda b, r: (b * steps + r, 0)),
                  pl.BlockSpec((seq, width), lambda b, r: (b, 1)),
                  pl.BlockSpec((seq, width), lambda b, r: (b, 2)),
                  pl.BlockSpec(bias_tab.shape, lambda b, r: (0, 0, 0))],
        out_specs=pl.BlockSpec((tq, width), lambda b, r: (b * steps + r, 0)),
        out_shape=jax.ShapeDtypeStruct((batch * seq, width), BF16),
        compiler_params=_cparams(2),
        name="natten",
    )(qkv, qkv, qkv, bias_tab)


def _rwkv_prep_tile(z, prev, nxt, lr, cw_ref, w0_ref, wup_ref, a0_ref, aup_ref, gup_ref, kk_ref, ka_ref, rk_ref, ones_ref,
                    r_o, v_o, nkk_o, lw_o, b_o, kd_o, bonus_o, g_o):
    tm = z.shape[0]
    width = z.shape[1] // 3
    row = lax.broadcasted_iota(jnp.int32, z.shape, 0)
    zm1 = jnp.where(row == 0, prev, pltpu.roll(z, 1, 0))
    zp1 = jnp.where(row == tm - 1, nxt, pltpu.roll(z, tm - 1, 0))
    rkv = zm1 * cw_ref[0:1, :] + z * cw_ref[1:2, :] + zp1 * cw_ref[2:3, :]
    r = rkv[:, :width]
    k = rkv[:, width:2 * width]
    v = rkv[:, 2 * width:]
    th = jnp.tanh(lr)
    sg = _sigmoid(lr)
    ones = ones_ref[...]

    def headsum(x):
        return _dot_split(x, ones)

    kk = k * kk_ref[...]
    kk = kk * lax.rsqrt(jnp.maximum(headsum(kk * kk), 1e-24))
    g_o[...] = jnp.dot(sg.astype(BF16), gup_ref[...], preferred_element_type=F32)
    r_o[...] = r
    v_o[...] = v
    nkk_o[...] = -kk
    kd_sum = jnp.zeros_like(r)
    th_b = th.astype(BF16)
    lr_b = lr.astype(BF16)
    for d in range(2):
        wl = jnp.dot(th_b, wup_ref[d], preferred_element_type=F32) + w0_ref[d:d + 1, :]
        lw_o[d] = -DECAY_SCALE * _sigmoid(wl)
        a = _sigmoid(jnp.dot(lr_b, aup_ref[d], preferred_element_type=F32) + a0_ref[d:d + 1, :])
        kd = k * (1.0 + (a - 1.0) * ka_ref[...])
        kd_o[d] = kd
        b_o[d] = kk * a
        kd_sum = kd_sum + kd
    bonus_o[...] = headsum(r * kd_sum * rk_ref[...]) * v


def _dot_nt(a, b):
    return lax.dot_general(a, b, (((1,), (1,)), ((), ())), preferred_element_type=F32)


def _dot_tn(a, b):
    return lax.dot_general(a, b, (((0,), (0,)), ((), ())), preferred_element_type=F32)


def _mm(a, b):
    return jnp.dot(a.astype(BF16), b.astype(BF16), preferred_element_type=F32)


def _rwkv_scan_kernel(rf_ref, vf_ref, nf_ref, rb_ref, vb_ref, nb_ref, lwf_ref, bf_ref, kf_ref, lwb_ref, bb_ref, kb_ref,
                      yf_ref, yb_ref, s_ref, *, heads, batch):
    @pl.when(pl.program_id(0) == 0)
    def _():
        s_ref[...] = jnp.zeros_like(s_ref)

    n = SCAN_CHUNK
    pair_w = 2 * HEAD_DIM
    row = lax.broadcasted_iota(jnp.int32, (n, pair_w), 0)
    lane = lax.broadcasted_iota(jnp.int32, (n, pair_w), 1)
    col = lane & (HEAD_DIM - 1)
    even = lane < HEAD_DIM
    levels = n.bit_length()
    same = [(row >> k) == (col >> k) for k in range(levels)]
    eye = same[0].astype(F32)
    level_masks = [same[sh + 1] & jnp.logical_not(same[sh]) for sh in range(1, levels - 1)]

    def blockdiag(x2):
        xb = x2.astype(BF16)
        zero = jnp.zeros((), BF16)
        return jnp.concatenate([jnp.where(even, xb, zero), jnp.where(even, zero, xb)], axis=0)

    def mm(x2, y2):
        return jnp.dot(x2.astype(BF16), blockdiag(y2), preferred_element_type=F32)

    def mm_nt(x2, y2):
        return _dot_nt(x2.astype(BF16), blockdiag(y2))

    dirs = ((rf_ref, vf_ref, nf_ref, lwf_ref, bf_ref, kf_ref, yf_ref),
            (rb_ref, vb_ref, nb_ref, lwb_ref, bb_ref, kb_ref, yb_ref))
    chains = []
    for d, (r_ref, v_ref, n_ref, lw_ref, b_ref, k_ref, y_ref) in enumerate(dirs):
        order = row - col if d == 0 else col - row
        strict = order > 0
        incl = order >= 0
        incl_b = jnp.where(incl[:, :n], 1.0, 0.0).astype(BF16)
        for bi in range(batch):
            lw = lw_ref[0, bi]
            lw_hi, lw_mid = _split_bf16(lw)
            lw_lo = (lw - lw_hi.astype(F32) - lw_mid.astype(F32)).astype(BF16)
            g_inc = ((jnp.dot(incl_b, lw_lo, preferred_element_type=F32)
                      + jnp.dot(incl_b, lw_mid, preferred_element_type=F32))
                     + jnp.dot(incl_b, lw_hi, preferred_element_type=F32))
            g_tot = jnp.sum(lw, axis=0, keepdims=True)
            e_neg = jnp.exp(-g_inc)
            e_end = jnp.exp(g_tot - g_inc)
            decay = jnp.exp(g_tot)
            a_t = n_ref[bi] * jnp.exp(g_inc - lw)
            r_t = r_ref[bi] * jnp.exp(g_inc)
            bb = b_ref[0, bi]
            kd = k_ref[0, bi]
            b_t = bb * e_neg
            k_t = kd * e_neg
            ar_t = jnp.concatenate([a_t, r_t], axis=0).astype(BF16)
            bk_h = jnp.concatenate([bb * e_end, kd * e_end], axis=0).astype(BF16)
            v = v_ref[bi]
            for p in range(heads // 2):
                sl = slice(p * pair_w, (p + 1) * pair_w)
                chains.append(dict(strict=strict, incl=incl, sl=sl, bi=bi, y_ref=y_ref,
                                   si=(d * batch + bi) * (heads // 2) + p, decay=decay[:, sl],
                                   ar=ar_t[:, sl], b=b_t[:, sl], k=k_t[:, sl], bk_h=bk_h[:, sl], v=v[:, sl]))

    for ch in chains:
        pb = mm_nt(ch["ar"], ch["b"])
        pk = mm_nt(ch["ar"], ch["k"])
        ch["l_ab"] = jnp.where(ch["strict"], pb[:n], 0.0)
        ch["m_rb"] = jnp.where(ch["incl"], pb[n:], 0.0)
        ch["l_ak"] = jnp.where(ch["strict"], pk[:n], 0.0)
        ch["m_rk"] = jnp.where(ch["incl"], pk[n:], 0.0)
        ch["t"] = eye + jnp.where(same[1], ch["l_ab"], 0.0)
    for mask in level_masks:
        for ch in chains:
            ch["tc"] = mm(ch["t"], jnp.where(mask, ch["l_ab"], 0.0))
        for ch in chains:
            ch["t"] = ch["t"] + mm(ch["tc"], ch["t"])
    for ch in chains:
        ch["s0"] = s_ref[ch["si"]]
        ch["x"] = mm_nt(ch["ar"], ch["s0"])
    for ch in chains:
        ch["kv"] = mm(jnp.concatenate([ch["l_ak"], ch["m_rk"]], axis=0), ch["v"])
    for ch in chains:
        ch["u"] = mm(ch["t"], ch["x"][:n] + ch["kv"][:n])
    for ch in chains:
        y = ch["x"][n:] + mm(ch["m_rb"], ch["u"]) + ch["kv"][n:]
        ch["y_ref"][ch["bi"], :, ch["sl"]] = y
    for ch in chains:
        uv = jnp.concatenate([ch["u"], ch["v"]], axis=0).astype(BF16)
        full = _dot_tn(uv, ch["bk_h"])
        s_ref[ch["si"]] = ch["s0"] * ch["decay"] + jnp.where(even, full[:HEAD_DIM], full[HEAD_DIM:])


def _rwkv_scan(r, v, nkk, lw, b, kd, batch, seq):
    t, width = r.shape
    heads = width // HEAD_DIM
    n = SCAN_CHUNK
    nc = seq // n
    r3, v3, n3 = (z.reshape(batch, seq, width) for z in (r, v, nkk))
    lw4, b4, k4 = (z.reshape(2, batch, seq, width) for z in (lw, b, kd))
    fwd = pl.BlockSpec((batch, n, width), lambda c: (0, c, 0))
    bwd = pl.BlockSpec((batch, n, width), lambda c: (0, nc - 1 - c, 0))
    fwd_d = pl.BlockSpec((1, batch, n, width), lambda c: (0, 0, c, 0))
    bwd_d = pl.BlockSpec((1, batch, n, width), lambda c: (1, 0, nc - 1 - c, 0))
    kern = functools.partial(_rwkv_scan_kernel, heads=heads, batch=batch)
    yf, yb = pl.pallas_call(
        kern,
        grid=(nc,),
        in_specs=[fwd, fwd, fwd, bwd, bwd, bwd, fwd_d, fwd_d, fwd_d, bwd_d, bwd_d, bwd_d],
        out_specs=[fwd, bwd],
        out_shape=[jax.ShapeDtypeStruct((batch, seq, width), F32)] * 2,
        scratch_shapes=[pltpu.VMEM((batch * heads, HEAD_DIM, 2 * HEAD_DIM), F32)],
        compiler_params=_cparams(1, "arbitrary"),
        name="rwkv_scan",
    )(r3, v3, n3, r3, v3, n3, lw4, b4, k4, lw4, b4, k4)
    return yf.reshape(t, width), yb.reshape(t, width)


def _pool_tile(p_ref, pp_ref, pn_ref, w_ref, sc_ref, ext_ref, tb, tiles_per_batch, seq):
    p = p_ref[...]
    tm, width = p.shape
    assert all(w == 2 << i for i, w in enumerate(POOL_WINDOWS)) and POOL_WINDOWS[-1] <= 2 * HALO
    n = tm + 2 * HALO
    pad = jnp.zeros((HALO, width), F32)
    for k in range(len(POOL_WINDOWS)):
        ext_ref[k, 0:HALO, :] = pad
        ext_ref[k, HALO + n:2 * HALO + n, :] = pad
    ext_ref[0, HALO:2 * HALO, :] = jnp.where(tb == 0, 0.0, pp_ref[...])
    ext_ref[0, 2 * HALO:2 * HALO + tm, :] = p
    ext_ref[0, 2 * HALO + tm:HALO + n, :] = jnp.where(tb == tiles_per_batch - 1, 0.0, pn_ref[...])

    def rows(k, first, count):
        return ext_ref[k, 2 * HALO + first:2 * HALO + first + count, :]

    ext_ref[1, HALO:HALO + n, :] = rows(0, -HALO - 1, n) + rows(0, -HALO, n)
    for k in range(1, len(POOL_WINDOWS) - 1):
        q = POOL_WINDOWS[k - 1] // 2
        ext_ref[k + 1, HALO:HALO + n, :] = rows(k, -HALO - q, n) + rows(k, -HALO + q, n)
    q = POOL_WINDOWS[-2] // 2
    sums = [rows(k + 1, 0, tm) for k in range(len(POOL_WINDOWS) - 1)]
    sums.append(rows(len(POOL_WINDOWS) - 1, -q, tm) + rows(len(POOL_WINDOWS) - 1, q, tm))

    t = tb * tm + lax.broadcasted_iota(jnp.int32, (tm, width), 0)
    grp = lax.broadcasted_iota(jnp.int32, (tm, width), 1) // (width // len(POOL_WINDOWS))
    pooled = jnp.zeros_like(p)
    for gi, win in enumerate(POOL_WINDOWS):
        half = win // 2
        lo = jnp.clip(t - half, 0, seq - 1)
        hi = jnp.clip(t + half - 1, 0, seq - 1)
        cnt = (hi - lo + 1).astype(F32)
        pooled = jnp.where(grp == gi, sums[gi] / cnt, pooled)
    pooled = pooled - p
    return jnp.dot(pooled, w_ref[...], preferred_element_type=F32) * sc_ref[...]


def _outproj_kernel(ya_ref, yf_ref, yb_ref, bonus_ref, g_ref, p_ref, pp_ref, pn_ref, x_ref, mod_ref,
                    wa_ref, wb_ref, wc_ref, pw_ref, psc_ref,
                    gng_ref, gnb_ref, ones_ref, l1g_ref, l1b_ref, wr_ref, br_ref,
                    x1_o, u2_o, ri_o, rw_o, cnt_o, cnt_ref, ext_ref, *, alpha, tiles_per_batch, seq):
    yc = _pool_tile(p_ref, pp_ref, pn_ref, pw_ref, psc_ref, ext_ref,
                    pl.program_id(0) % tiles_per_batch, tiles_per_batch, seq)
    m = mod_ref[0]
    ones = ones_ref[...]

    def headmean(x):
        return _dot_split(x, ones) * (1.0 / HEAD_DIM)

    ysum = yf_ref[...] + yb_ref[...]
    yc0 = ysum - headmean(ysum)
    yn = yc0 * lax.rsqrt(headmean(yc0 * yc0) + GN_EPS) * gng_ref[...] + gnb_ref[...]
    yb = (yn + bonus_ref[...]) * g_ref[...]
    mix = (jnp.dot(ya_ref[...].astype(BF16), wa_ref[...], preferred_element_type=F32)
           + jnp.dot(yb.astype(BF16), wb_ref[...], preferred_element_type=F32)
           + jnp.dot(yc.astype(BF16), wc_ref[...], preferred_element_type=F32))
    x1 = _ln(alpha * x_ref[...] + m[2:3] * mix) * l1g_ref[...] + l1b_ref[...]
    x1_o[...] = x1
    u2 = _ln(x1) * (1.0 + m[4:5]) + m[3:4]
    u2_o[...] = _pack_bf16_pairs(u2)

    u_hi, u_lo = _split_bf16(u2)
    hi_both = jnp.dot(u_hi, wr_ref[...], preferred_element_type=F32)
    lg = (hi_both[:, :LANES] + hi_both[:, LANES:]
          + jnp.dot(u_lo, wr_ref[:, :LANES], preferred_element_type=F32)) + br_ref[...]
    lane = lax.broadcasted_iota(jnp.int32, lg.shape, 1)
    big = jnp.int32(1 << 20)
    gl = jnp.where(lane < N_GROUPS, lg, -jnp.inf)
    gmax = jnp.max(gl, axis=-1, keepdims=True)
    gidx = jnp.min(jnp.where(gl == gmax, lane, big), axis=-1, keepdims=True)
    pg_sel = 1.0 / jnp.sum(jnp.exp(gl - gmax), axis=-1, keepdims=True)
    e_lo = N_GROUPS + gidx * EXPERTS_PER_GROUP
    el = jnp.where((lane >= e_lo) & (lane < e_lo + EXPERTS_PER_GROUP), lg, -jnp.inf)
    m1 = jnp.max(el, axis=-1, keepdims=True)
    i1 = jnp.min(jnp.where(el == m1, lane, big), axis=-1, keepdims=True)
    el2 = jnp.where(lane == i1, -jnp.inf, el)
    m2 = jnp.max(el2, axis=-1, keepdims=True)
    i2 = jnp.min(jnp.where(el2 == m2, lane, big), axis=-1, keepdims=True)
    e21 = jnp.exp(m2 - m1)
    p1 = 1.0 / (1.0 + e21)
    p2 = e21 / (1.0 + e21)
    rw_o[...] = jnp.where(lane == 0, pg_sel * p1, jnp.where(lane == 1, pg_sel * p2, 0.0))

    @pl.when(pl.program_id(0) == 0)
    def _():
        cnt_ref[...] = jnp.zeros_like(cnt_ref)

    tm = lg.shape[0]
    earlier = (lax.broadcasted_iota(jnp.int32, (tm, tm), 1)
               < lax.broadcasted_iota(jnp.int32, (tm, tm), 0)).astype(BF16)
    oh1 = (lane == i1).astype(F32)
    oh2 = (lane == i2).astype(F32)
    run = cnt_ref[...]
    c1 = jnp.sum(oh1, axis=0, keepdims=True)
    before1 = run + jnp.dot(earlier, oh1.astype(BF16), preferred_element_type=F32)
    before2 = run + c1 + jnp.dot(earlier, oh2.astype(BF16), preferred_element_type=F32)
    rank1 = jnp.sum(oh1 * before1, axis=-1, keepdims=True).astype(jnp.int32)
    rank2 = jnp.sum(oh2 * before2, axis=-1, keepdims=True).astype(jnp.int32)
    total = run + c1 + jnp.sum(oh2, axis=0, keepdims=True)
    cnt_ref[...] = total
    cnt_o[...] = total
    ri_o[...] = jnp.where(lane == 0, i1 - N_GROUPS, jnp.where(lane == 1, i2 - N_GROUPS,
                          jnp.where(lane == 2, rank1, jnp.where(lane == 3, rank2, 0))))


def _outproj(ya, yf, yb, bonus, g, praw, x2, modl, p, seq, tm, alpha):
    t, d = x2.shape
    tpb = seq // tm
    aw, bw, cw = ya.shape[1], bonus.shape[1], praw.shape[1]
    hb = tm // HALO
    nhb = t // HALO
    tok = lambda i: (i, 0)
    full2 = lambda i: (0, 0)
    kern = functools.partial(_outproj_kernel, alpha=alpha, tiles_per_batch=tpb, seq=seq)
    small = ["w_out_a", "w_out_b", "w_out_c", "pool_w", "pool_scale",
             "gn_gain", "gn_bias", "ones", "ln1_gain", "ln1_bias", "w_router", "b_router"]
    return pl.pallas_call(
        kern,
        grid=(t // tm,),
        in_specs=[pl.BlockSpec((tm, aw), tok),
                  pl.BlockSpec((tm, bw), tok), pl.BlockSpec((tm, bw), tok),
                  pl.BlockSpec((tm, bw), tok), pl.BlockSpec((tm, bw), tok),
                  pl.BlockSpec((tm, cw), tok),
                  pl.BlockSpec((HALO, cw), lambda i: (jnp.maximum(i * hb - 1, 0), 0)),
                  pl.BlockSpec((HALO, cw), lambda i: (jnp.minimum((i + 1) * hb, nhb - 1), 0)),
                  pl.BlockSpec((tm, d), tok),
                  pl.BlockSpec((1,) + modl.shape[1:], lambda i: (i // tpb, 0, 0))]
                 + [pl.BlockSpec(p[k].shape, functools.partial(lambda nd, i: (0,) * nd, p[k].ndim)) for k in small],
        out_specs=[pl.BlockSpec((tm, d), tok), pl.BlockSpec((tm, d // 2), tok),
                   pl.BlockSpec((tm, LANES), tok), pl.BlockSpec((tm, LANES), tok),
                   pl.BlockSpec((1, LANES), full2)],
        out_shape=[jax.ShapeDtypeStruct((t, d), F32), jax.ShapeDtypeStruct((t, d // 2), jnp.uint32),
                   jax.ShapeDtypeStruct((t, LANES), jnp.int32), jax.ShapeDtypeStruct((t, LANES), F32),
                   jax.ShapeDtypeStruct((1, LANES), F32)],
        scratch_shapes=[pltpu.VMEM((1, LANES), F32), pltpu.VMEM((len(POOL_WINDOWS), tm + 4 * HALO, cw), F32)],
        compiler_params=_cparams(1, "arbitrary"),
        name="outproj",
    )(ya, yf, yb, bonus, g, praw, praw, praw, x2, modl, *[p[k] for k in small])


def _dispatch(route_i, counts_lanes, n_blocks):
    counts = counts_lanes[0, N_GROUPS:N_GROUPS + N_EXPERTS].astype(jnp.int32)
    padded = ((counts + EXPERT_BLOCK - 1) // EXPERT_BLOCK) * EXPERT_BLOCK
    pends = jnp.cumsum(padded)
    pstarts = pends - padded
    e = route_i[:, :TOP_K]
    rank = route_i[:, TOP_K:2 * TOP_K]
    ids = jnp.arange(N_EXPERTS, dtype=jnp.int32)
    dest = jnp.sum(jnp.where(e[..., None] == ids, pstarts, 0), axis=-1) + rank
    block_start = jnp.arange(n_blocks, dtype=jnp.int32) * EXPERT_BLOCK
    block_e = jnp.minimum(jnp.sum((pends[None, :] <= block_start[:, None]).astype(jnp.int32), axis=1), N_EXPERTS - 1)
    meta = jnp.concatenate([block_e, (pends[-1] // EXPERT_BLOCK)[None]]).astype(jnp.int32)
    return dest, meta, pends.astype(jnp.int32)


def _scatter_rows_kernel(pends_ref, dest_ref, u_ref, xs_ref, zeros_ref, sem, zsem):
    tm = u_ref.shape[0] * SUBLANES

    @pl.when(pl.program_id(0) == 0)
    def _():
        zeros_ref[...] = jnp.zeros_like(zeros_ref)

        def tail_copy(e):
            tail = pl.ds(pl.multiple_of(pends_ref[e] - EXPERT_BLOCK, EXPERT_BLOCK), EXPERT_BLOCK)
            return pltpu.make_async_copy(zeros_ref, xs_ref.at[tail], zsem)

        def has_rows(e):
            return pends_ref[e] > (pends_ref[e - 1] if e > 0 else 0)

        def unused_copy(j):
            return pltpu.make_async_copy(zeros_ref, xs_ref.at[pl.ds(j * EXPERT_BLOCK, EXPERT_BLOCK)], zsem)

        def is_unused(j):
            return j * EXPERT_BLOCK >= pends_ref[N_EXPERTS - 1]

        n_blocks = xs_ref.shape[0] // EXPERT_BLOCK
        for e in range(N_EXPERTS):
            pl.when(has_rows(e))(lambda e=e: tail_copy(e).start())
        for j in range(n_blocks):
            pl.when(is_unused(j))(lambda j=j: unused_copy(j).start())
        for e in range(N_EXPERTS):
            pl.when(has_rows(e))(lambda e=e: tail_copy(e).wait())
        for j in range(n_blocks):
            pl.when(is_unused(j))(lambda j=j: unused_copy(j).wait())

    def issue(grp, carry):
        for j in range(SUBLANES):
            for k in range(TOP_K):
                dst = dest_ref[0, 0, TOP_K * SUBLANES * grp + TOP_K * j + k]
                pltpu.make_async_copy(u_ref.at[grp, pl.ds(j, 1)], xs_ref.at[pl.ds(dst, 1)], sem).start()
        return carry

    lax.fori_loop(0, tm // SUBLANES, issue, 0)
    rows = pl.ds(0, TOP_K * tm)
    pltpu.make_async_copy(xs_ref.at[rows], xs_ref.at[rows], sem).wait()


def _scatter_rows(pends, u2, dest3, total, tm):
    t, d = u2.shape
    grid_spec = pltpu.PrefetchScalarGridSpec(
        num_scalar_prefetch=1,
        grid=(t // tm,),
        in_specs=[pl.BlockSpec((1, 1, TOP_K * tm), lambda i, p: (i, 0, 0), memory_space=pltpu.SMEM),
                  pl.BlockSpec((tm // SUBLANES, SUBLANES, d), lambda i, p: (i, 0, 0))],
        out_specs=pl.BlockSpec(memory_space=pl.ANY),
        scratch_shapes=[pltpu.VMEM((EXPERT_BLOCK, d), u2.dtype), pltpu.SemaphoreType.DMA(()),
                        pltpu.SemaphoreType.DMA(())],
    )
    return pl.pallas_call(
        _scatter_rows_kernel,
        grid_spec=grid_spec,
        out_shape=jax.ShapeDtypeStruct((total, d), u2.dtype),
        compiler_params=_cparams(1, "arbitrary"),
        name="scatter_rows",
    )(pends, dest3, u2.reshape(t // SUBLANES, SUBLANES, d))


def _experts_kernel(meta_ref, xs_ref, wg_ref, wu_ref, wd_ref, o_ref, wg_b, wu_b, wd_b):
    i = pl.program_id(0)
    n_used = meta_ref[pl.num_programs(0)]

    @pl.when((i == 0) | (meta_ref[i] != meta_ref[jnp.maximum(i - 1, 0)]))
    def _():
        wg_b[...] = wg_ref[0, 0].astype(BF16)
        wu_b[...] = wu_ref[0, 0].astype(BF16)
        wd_b[...] = wd_ref[0, 0].astype(BF16)

    @pl.when(i < n_used)
    def _():
        xb = _unpack_bf16_pairs(xs_ref[...])
        gate = jnp.dot(xb, wg_b[...], preferred_element_type=F32)
        up = jnp.dot(xb, wu_b[...], preferred_element_type=F32)
        hb = gate * _sigmoid(gate) * up
        o_ref[...] = _pack_bf16_pairs(jnp.dot(hb.astype(BF16), wd_b[...], preferred_element_type=F32))

    @pl.when(i >= n_used)
    def _():
        o_ref[...] = jnp.zeros_like(o_ref)


def _experts(meta, xs, wg, wu, wd, layer):
    total, dp = xs.shape
    nb = total // EXPERT_BLOCK
    d, de = wg.shape[2:]
    grid_spec = pltpu.PrefetchScalarGridSpec(
        num_scalar_prefetch=1,
        grid=(nb,),
        in_specs=[pl.BlockSpec((EXPERT_BLOCK, dp), lambda i, m: (jnp.minimum(i, m[nb] - 1), 0)),
                  pl.BlockSpec((1, 1, d, de), lambda i, m: (layer, m[i], 0, 0)),
                  pl.BlockSpec((1, 1, d, de), lambda i, m: (layer, m[i], 0, 0)),
                  pl.BlockSpec((1, 1, de, d), lambda i, m: (layer, m[i], 0, 0))],
        out_specs=pl.BlockSpec((EXPERT_BLOCK, dp), lambda i, m: (i, 0)),
        scratch_shapes=[pltpu.VMEM((d, de), BF16), pltpu.VMEM((d, de), BF16), pltpu.VMEM((de, d), BF16)],
    )
    return pl.pallas_call(
        _experts_kernel,
        grid_spec=grid_spec,
        out_shape=jax.ShapeDtypeStruct((total, dp), xs.dtype),
        compiler_params=_cparams(1, "arbitrary"),
        name="experts",
    )(meta, xs, wg, wu, wd)


def _final_kernel(dcur_ref, dnext_ref, x1_ref, rw_ref, mod_ref, g_ref, b_ref, ys_ref, o_ref, ybuf, sem, *, alpha):
    i = pl.program_id(0)
    tm = x1_ref.shape[0]
    slot = i % 2

    def gather(d_ref, s):
        def issue(grp, carry):
            for j in range(SUBLANES):
                for k in range(TOP_K):
                    src = d_ref[0, 0, TOP_K * SUBLANES * grp + TOP_K * j + k]
                    pltpu.make_async_copy(ys_ref.at[pl.ds(src, 1)], ybuf.at[s, k, grp, pl.ds(j, 1)],
                                          sem.at[s]).start()
            return carry

        lax.fori_loop(0, tm // SUBLANES, issue, 0)

    @pl.when(i == 0)
    def _():
        gather(dcur_ref, 0)

    @pl.when(i + 1 < pl.num_programs(0))
    def _():
        gather(dnext_ref, 1 - slot)

    pltpu.make_async_copy(ybuf.at[slot], ybuf.at[slot], sem.at[slot]).wait()
    m = mod_ref[0]
    rw = rw_ref[...]
    dp = ybuf.shape[-1]
    y1 = _unpack_bf16_pairs(ybuf[slot, 0].reshape(tm, dp)).astype(F32)
    y2 = _unpack_bf16_pairs(ybuf[slot, 1].reshape(tm, dp)).astype(F32)
    f = rw[:, 0:1] * y1 + rw[:, 1:2] * y2
    o_ref[...] = _ln(alpha * x1_ref[...] + m[5:6] * f) * g_ref[...] + b_ref[...]


def _final(x1, ysorted, dest3, rw, modl, gain, bias, seq, tm, alpha):
    t, d = x1.shape
    tpb = seq // tm
    n_tiles = t // tm
    tok = lambda i: (i, 0)
    kern = functools.partial(_final_kernel, alpha=alpha)
    dspec = lambda f: pl.BlockSpec((1, 1, TOP_K * tm), f, memory_space=pltpu.SMEM)
    return pl.pallas_call(
        kern,
        grid=(n_tiles,),
        in_specs=[dspec(lambda i: (i, 0, 0)), dspec(lambda i: (jnp.minimum(i + 1, n_tiles - 1), 0, 0)),
                  pl.BlockSpec((tm, d), tok), pl.BlockSpec((tm, LANES), tok),
                  pl.BlockSpec((1,) + modl.shape[1:], lambda i: (i // tpb, 0, 0)),
                  pl.BlockSpec(gain.shape, lambda i: (0, 0)), pl.BlockSpec(bias.shape, lambda i: (0, 0)),
                  pl.BlockSpec(memory_space=pl.ANY)],
        out_specs=pl.BlockSpec((tm, d), tok),
        out_shape=jax.ShapeDtypeStruct((t, d), F32),
        scratch_shapes=[pltpu.VMEM((2, TOP_K, tm // SUBLANES, SUBLANES, ysorted.shape[1]), ysorted.dtype),
                        pltpu.SemaphoreType.DMA((2,))],
        compiler_params=_cparams(1, "arbitrary"),
        name="final_ln",
    )(dest3, dest3, x1, rw, modl, gain, bias, ysorted)


def _block_diag(blocks):
    n, a, b = blocks.shape
    out = jnp.zeros((n * a, n * b), blocks.dtype)
    for i in range(n):
        out = out.at[i * a:(i + 1) * a, i * b:(i + 1) * b].set(blocks[i])
    return out


def _pad_rows(w, lo, total):
    return jnp.zeros((total, w.shape[-1]), w.dtype).at[lo:lo + w.shape[0]].set(w)


def kernel(x, c, w_mod, b_mod, w_in, na_rpb, rw_conv, rw_w0, rw_w_up, rw_a0, rw_a_up, rw_g_up, rw_k_k, rw_k_a, rw_r_k, rw_gn_gain, rw_gn_bias, pool_w, pool_scale, w_out, ln1_gain, ln1_bias, ln2_gain, ln2_bias, moe_w_group, moe_b_group, moe_w_expert, moe_b_expert, moe_w_gate, moe_w_up, moe_w_down):
    batch, seq, d = x.shape
    depth = w_mod.shape[0]
    t = batch * seq
    a_w = na_rpb.shape[1] * HEAD_DIM
    b_w = rw_w0.shape[-1]
    c_w = pool_scale.shape[-1]
    lr_w = R_W + R_A + R_G
    alpha = (2 * depth) ** 0.25
    tm = min(512, seq)
    tm_in = min(512, seq)
    assert seq % tm == 0 and seq % SCAN_CHUNK == 0 and seq % GRID_W == 0 and lr_w == LANES

    mod = _modulation(c, w_mod, b_mod)
    ones_blk = _block_diag(jnp.ones((b_w // HEAD_DIM, HEAD_DIM, HEAD_DIM), BF16))
    row = lambda v: v.reshape(1, -1)

    x2 = x.reshape(t, d)
    for l in range(depth):
        modl = mod[l]
        prep_params = {
            "conv": rw_conv[l], "w0": rw_w0[l], "a0": rw_a0[l],
            "w_up": jnp.stack([_pad_rows(rw_w_up[l, dd], 0, lr_w) for dd in range(2)]).astype(BF16),
            "a_up": jnp.stack([_pad_rows(rw_a_up[l, dd], R_W, lr_w) for dd in range(2)]).astype(BF16),
            "g_up": _pad_rows(rw_g_up[l], R_W + R_A, lr_w).astype(BF16),
            "k_k": row(rw_k_k[l]), "k_a": row(rw_k_a[l]), "r_k": row(rw_r_k[l]), "ones": ones_blk,
        }
        qkv, praw, r, v, nkk, lw, bb, kd, bonus, g = _inproj(x2, modl, w_in[l].astype(BF16), prep_params, seq, tm_in,
                                                             3 * a_w, 3 * b_w, lr_w, c_w)
        ya = _natten(qkv, _na_bias_table(na_rpb[l]), batch, seq, a_w)
        yf, yb = _rwkv_scan(r, v, nkk, lw, bb, kd, batch, seq)
        w_router = jnp.zeros((d, LANES), F32).at[:, :N_GROUPS].set(moe_w_group[l])
        w_router = w_router.at[:, N_GROUPS:N_GROUPS + N_EXPERTS].set(moe_w_expert[l])
        b_router = jnp.zeros((1, LANES), F32).at[0, :N_GROUPS].set(moe_b_group[l])
        b_router = b_router.at[0, N_GROUPS:N_GROUPS + N_EXPERTS].set(moe_b_expert[l])
        wo = w_out[l].astype(BF16)
        out_params = {
            "w_out_a": wo[:a_w], "w_out_b": wo[a_w:a_w + b_w], "w_out_c": wo[a_w + b_w:],
            "pool_w": _block_diag(pool_w[l]), "pool_scale": row(pool_scale[l]),
            "gn_gain": row(rw_gn_gain[l]), "gn_bias": row(rw_gn_bias[l]), "ones": ones_blk,
            "ln1_gain": row(ln1_gain[l]), "ln1_bias": row(ln1_bias[l]),
            "w_router": jnp.concatenate(_split_bf16(w_router), axis=1), "b_router": b_router,
        }
        x1, u2, route_i, route_w, counts = _outproj(ya, yf, yb, bonus, g, praw, x2, modl, out_params, seq, tm, alpha)
        n_blocks = -(-(t * TOP_K) // EXPERT_BLOCK) + N_EXPERTS
        dest, meta, pends = _dispatch(route_i, counts, n_blocks)
        dest3 = dest.reshape(t // tm, 1, TOP_K * tm)
        xs = _scatter_rows(pends, u2, dest3, n_blocks * EXPERT_BLOCK, tm)
        ysorted = _experts(meta, xs, moe_w_gate, moe_w_up, moe_w_down, l)
        x2 = _final(x1, ysorted, dest3, route_w, modl, row(ln2_gain[l]), row(ln2_bias[l]), seq, tm, alpha)
    return x2.reshape(batch, seq, d)
```

```python
import functools
import math

import jax
import jax.numpy as jnp
import numpy as np
from jax import lax
from jax.experimental import pallas as pl
from jax.experimental.pallas import tpu as pltpu

F32 = jnp.float32
BF16 = jnp.bfloat16
HI = lax.Precision.HIGHEST

GRID_W = 64
HEAD_DIM = 64
NA_KH = 8
NA_KW = 16
POOL_WINDOWS = (2, 4, 8, 16)
R_W = 32
R_A = 32
R_G = 64
DECAY_SCALE = math.exp(-0.5)
GN_EPS = 64e-5
N_GROUPS = 4
EXPERTS_PER_GROUP = 8
N_EXPERTS = N_GROUPS * EXPERTS_PER_GROUP
TOP_K = 2
EXPERT_BLOCK = 512
LN_EPS = 1e-5
NEG_INF = -1e30

NA_ROWS_PER_STEP = 8
SCAN_CHUNK = 64
SUBLANES = 8
HALO = 8
LANES = 128
VMEM_LIMIT = 52 * 1024 * 1024


def _ln(x):
    mu = jnp.mean(x, axis=-1, keepdims=True)
    xc = x - mu
    var = jnp.mean(xc * xc, axis=-1, keepdims=True)
    return xc * lax.rsqrt(var + LN_EPS)


def _sigmoid(x):
    return 1.0 / (1.0 + jnp.exp(-x))


def _split_bf16(x):
    hi = x.astype(BF16)
    return hi, (x - hi.astype(F32)).astype(BF16)


def _dot_split(x, w_exact):
    hi, lo = _split_bf16(x)
    return jnp.dot(hi, w_exact, preferred_element_type=F32) + jnp.dot(lo, w_exact, preferred_element_type=F32)


def _pack_bf16_pairs(x):
    h = x.shape[1] // 2
    lo = lax.bitcast_convert_type(x[:, :h].astype(BF16).astype(F32), jnp.uint32)
    hi = lax.bitcast_convert_type(x[:, h:].astype(BF16).astype(F32), jnp.uint32)
    return (lo >> 16) | hi


def _unpack_bf16_pairs(w):
    lo = lax.bitcast_convert_type(w << 16, F32).astype(BF16)
    hi = lax.bitcast_convert_type(w & jnp.uint32(0xFFFF0000), F32).astype(BF16)
    return jnp.concatenate([lo, hi], axis=1)


def _cparams(n_axes, semantics="parallel"):
    return pltpu.CompilerParams(dimension_semantics=(semantics,) * n_axes, vmem_limit_bytes=VMEM_LIMIT)


def _mod_kernel(c_ref, w_ref, b_ref, o_ref):
    c = c_ref[...]
    s = c * _sigmoid(c)
    o_ref[0] = jnp.dot(s, w_ref[0], precision=HI, preferred_element_type=F32) + b_ref[0]


def _modulation(c, w_mod, b_mod):
    n_layers, d, d6 = w_mod.shape
    b = c.shape[0]
    bp = -(-b // 8) * 8
    cp = jnp.zeros((bp, d), F32).at[:b].set(c)
    out = pl.pallas_call(
        _mod_kernel,
        grid=(n_layers, d6 // d),
        in_specs=[pl.BlockSpec((bp, d), lambda l, j: (0, 0)),
                  pl.BlockSpec((1, d, d), lambda l, j: (l, 0, j)),
                  pl.BlockSpec((1, 1, d), lambda l, j: (l, 0, j))],
        out_specs=pl.BlockSpec((1, bp, d), lambda l, j: (l, 0, j)),
        out_shape=jax.ShapeDtypeStruct((n_layers, bp, d6), F32),
        compiler_params=_cparams(2),
        name="modulation",
    )(cp, w_mod, b_mod.reshape(n_layers, 1, d6))
    return out[:, :b].reshape(n_layers, b, d6 // d, d)


def _inproj_kernel(x_ref, xp_ref, xn_ref, mod_ref, wr_ref, wo_ref, cw_ref, w0_ref, wup_ref, a0_ref, aup_ref, gup_ref,
                   kk_ref, ka_ref, rk_ref, ones_ref,
                   qkv_o, pool_o, r_o, v_o, nkk_o, lw_o, b_o, kd_o, bonus_o, g_o,
                   *, a3, b3, tiles_per_batch):
    m = mod_ref[0]
    tm = x_ref.shape[0]
    tb = pl.program_id(0) % tiles_per_batch
    xe = jnp.concatenate([xp_ref[...], x_ref[...], xn_ref[...]], axis=0)
    u = (_ln(xe) * (1.0 + m[1:2]) + m[0:1]).astype(BF16)
    h = jnp.dot(u, wr_ref[...], preferred_element_type=F32)
    ho = jnp.dot(u[HALO:HALO + tm], wo_ref[...], preferred_element_type=F32)
    hm = h[HALO:HALO + tm]
    prev = jnp.where(tb == 0, 0.0, h[HALO - 1:HALO, :b3])
    nxt = jnp.where(tb == tiles_per_batch - 1, 0.0, h[HALO + tm:HALO + tm + 1, :b3])
    _rwkv_prep_tile(hm[:, :b3], prev, nxt, hm[:, b3:],
                    cw_ref, w0_ref, wup_ref, a0_ref, aup_ref, gup_ref, kk_ref, ka_ref, rk_ref, ones_ref,
                    r_o, v_o, nkk_o, lw_o, b_o, kd_o, bonus_o, g_o)
    qkv_o[...] = ho[:, :a3].astype(BF16)
    pool_o[...] = ho[:, a3:]


def _inproj(x2, modl, w_in_bf, p, seq, tm, a3, b3, lr_w, c_w):
    t, d = x2.shape
    tpb = seq // tm
    hb = tm // HALO
    nhb = t // HALO
    width = b3 // 3
    kern = functools.partial(_inproj_kernel, a3=a3, b3=b3, tiles_per_batch=tpb)
    w_rwkv = w_in_bf[:, a3:a3 + b3 + lr_w]
    w_other = jnp.concatenate([w_in_bf[:, :a3], w_in_bf[:, a3 + b3 + lr_w:]], axis=1)
    tok = lambda i: (i, 0)
    dtok = lambda i: (0, i, 0)
    names = ["conv", "w0", "w_up", "a0", "a_up", "g_up", "k_k", "k_a", "r_k", "ones"]
    tw = jax.ShapeDtypeStruct((t, width), F32)
    dtw = jax.ShapeDtypeStruct((2, t, width), F32)
    return pl.pallas_call(
        kern,
        grid=(t // tm,),
        in_specs=[pl.BlockSpec((tm, d), tok),
                  pl.BlockSpec((HALO, d), lambda i: (jnp.maximum(i * hb - 1, 0), 0)),
                  pl.BlockSpec((HALO, d), lambda i: (jnp.minimum((i + 1) * hb, nhb - 1), 0)),
                  pl.BlockSpec((1,) + modl.shape[1:], lambda i: (i // tpb, 0, 0)),
                  pl.BlockSpec(w_rwkv.shape, lambda i: (0, 0)),
                  pl.BlockSpec(w_other.shape, lambda i: (0, 0))]
                 + [pl.BlockSpec(p[k].shape, functools.partial(lambda nd, i: (0,) * nd, p[k].ndim)) for k in names],
        out_specs=[pl.BlockSpec((tm, a3), tok), pl.BlockSpec((tm, c_w), tok),
                   pl.BlockSpec((tm, width), tok), pl.BlockSpec((tm, width), tok), pl.BlockSpec((tm, width), tok),
                   pl.BlockSpec((2, tm, width), dtok), pl.BlockSpec((2, tm, width), dtok),
                   pl.BlockSpec((2, tm, width), dtok),
                   pl.BlockSpec((tm, width), tok), pl.BlockSpec((tm, width), tok)],
        out_shape=[jax.ShapeDtypeStruct((t, a3), BF16), jax.ShapeDtypeStruct((t, c_w), F32),
                   tw, tw, tw, dtw, dtw, dtw, tw, tw],
        compiler_params=_cparams(1),
        name="inproj",
    )(x2, x2, x2, modl, w_rwkv, w_other, *[p[k] for k in names])


def _na_bias_table(rpb):
    col = np.arange(GRID_W)
    cstart = np.clip(col - NA_KW // 2, 0, GRID_W - NA_KW)
    in_win = (col[None, :] >= cstart[:, None]) & (col[None, :] < cstart[:, None] + NA_KW)
    dc = np.clip(col[None, :] - col[:, None], -(NA_KW - 1), NA_KW - 1) + (NA_KW - 1)
    pick = (dc[None] == np.arange(2 * NA_KW - 1)[:, None, None]).astype(np.float32)
    cols = jnp.einsum("hrc,cqk->hrqk", rpb.astype(F32), pick, precision=HI)
    cols = jnp.where(in_win, cols, NEG_INF)
    b = jnp.stack([cols[:, NA_KH - 1 - o:2 * NA_KH - 1 - o] for o in range(NA_KH)])
    h = rpb.shape[0]
    return jnp.transpose(b, (0, 1, 3, 2, 4)).reshape(NA_KH, h * GRID_W, NA_KH * GRID_W)


def _natten_kernel(q_ref, k_ref, v_ref, bias_ref, o_ref, *, rows, heads):
    width = q_ref.shape[1]
    nk = NA_KH * GRID_W
    head_of_lane = lax.broadcasted_iota(jnp.int32, (heads * GRID_W, width), 1) // HEAD_DIM
    head_of_row = lax.broadcasted_iota(jnp.int32, (heads * GRID_W, width), 0) // GRID_W
    own = head_of_lane == head_of_row
    for j in range(NA_ROWS_PER_STEP):
        r = pl.program_id(1) * NA_ROWS_PER_STEP + j
        rstart = jnp.clip(r - NA_KH // 2, 0, rows - NA_KH)
        off = r - rstart
        start = pl.multiple_of(rstart * GRID_W, GRID_W)
        kw = k_ref[pl.ds(start, nk), :]
        vw = v_ref[pl.ds(start, nk), :]
        q = q_ref[j * GRID_W:(j + 1) * GRID_W, :]
        qs = jnp.where(own, jnp.concatenate([q] * heads, axis=0), jnp.zeros((), q.dtype))
        s = lax.dot_general(qs, kw, (((1,), (1,)), ((), ())), preferred_element_type=F32) * (HEAD_DIM ** -0.5)
        s = s + bias_ref[off]
        mx = jnp.max(s, axis=-1, keepdims=True)
        p = jnp.exp(s - mx)
        den = jnp.sum(p, axis=-1, keepdims=True)
        o = jnp.where(own, jnp.dot(p.astype(BF16), vw, preferred_element_type=F32) / den, 0.0)
        acc = o[0:GRID_W]
        for h in range(1, heads):
            acc = acc + o[h * GRID_W:(h + 1) * GRID_W]
        o_ref[j * GRID_W:(j + 1) * GRID_W, :] = acc.astype(o_ref.dtype)


def _natten(qkv, bias_tab, batch, seq, width):
    rows = seq // GRID_W
    assert rows >= NA_KH
    heads = width // HEAD_DIM
    steps = rows // NA_ROWS_PER_STEP
    assert steps * NA_ROWS_PER_STEP == rows
    tq = NA_ROWS_PER_STEP * GRID_W
    kern = functools.partial(_natten_kernel, rows=rows, heads=heads)
    return pl.pallas_call(
        kern,
        grid=(batch, steps),
        in_specs=[pl.BlockSpec((tq, width), lambda b, r: (b * steps + r, 0)),
                  pl.BlockSpec((seq, width), lambda b, r: (b, 1)),
                  pl.BlockSpec((seq, width), lambda b, r: (b, 2)),
                  pl.BlockSpec(bias_tab.shape, lambda b, r: (0, 0, 0))],
        out_specs=pl.BlockSpec((tq, width), lambda b, r: (b * steps + r, 0)),
        out_shape=jax.ShapeDtypeStruct((batch * seq, width), BF16),
        compiler_params=_cparams(2),
        name="natten",
    )(qkv, qkv, qkv, bias_tab)


def _rwkv_prep_tile(z, prev, nxt, lr, cw_ref, w0_ref, wup_ref, a0_ref, aup_ref, gup_ref, kk_ref, ka_ref, rk_ref, ones_ref,
                    r_o, v_o, nkk_o, lw_o, b_o, kd_o, bonus_o, g_o):
    tm = z.shape[0]
    width = z.shape[1] // 3
    row = lax.broadcasted_iota(jnp.int32, z.shape, 0)
    zm1 = jnp.where(row == 0, prev, pltpu.roll(z, 1, 0))
    zp1 = jnp.where(row == tm - 1, nxt, pltpu.roll(z, tm - 1, 0))
    rkv = zm1 * cw_ref[0:1, :] + z * cw_ref[1:2, :] + zp1 * cw_ref[2:3, :]
    r = rkv[:, :width]
    k = rkv[:, width:2 * width]
    v = rkv[:, 2 * width:]
    th = jnp.tanh(lr)
    sg = _sigmoid(lr)
    ones = ones_ref[...]

    def headsum(x):
        return _dot_split(x, ones)

    kk = k * kk_ref[...]
    kk = kk * lax.rsqrt(jnp.maximum(headsum(kk * kk), 1e-24))
    g_o[...] = jnp.dot(sg.astype(BF16), gup_ref[...], preferred_element_type=F32)
    r_o[...] = r
    v_o[...] = v
    nkk_o[...] = -kk
    kd_sum = jnp.zeros_like(r)
    th_b = th.astype(BF16)
    lr_b = lr.astype(BF16)
    for d in range(2):
        wl = jnp.dot(th_b, wup_ref[d], preferred_element_type=F32) + w0_ref[d:d + 1, :]
        lw_o[d] = -DECAY_SCALE * _sigmoid(wl)
        a = _sigmoid(jnp.dot(lr_b, aup_ref[d], preferred_element_type=F32) + a0_ref[d:d + 1, :])
        kd = k * (1.0 + (a - 1.0) * ka_ref[...])
        kd_o[d] = kd
        b_o[d] = kk * a
        kd_sum = kd_sum + kd
    bonus_o[...] = headsum(r * kd_sum * rk_ref[...]) * v


def _dot_nt(a, b):
    return lax.dot_general(a, b, (((1,), (1,)), ((), ())), preferred_element_type=F32)


def _dot_tn(a, b):
    return lax.dot_general(a, b, (((0,), (0,)), ((), ())), preferred_element_type=F32)


def _mm(a, b):
    return jnp.dot(a.astype(BF16), b.astype(BF16), preferred_element_type=F32)


def _rwkv_scan_kernel(rf_ref, vf_ref, nf_ref, rb_ref, vb_ref, nb_ref, lwf_ref, bf_ref, kf_ref, lwb_ref, bb_ref, kb_ref,
                      yf_ref, yb_ref, s_ref, *, heads, batch):
    @pl.when(pl.program_id(0) == 0)
    def _():
        s_ref[...] = jnp.zeros_like(s_ref)

    n = SCAN_CHUNK
    pair_w = 2 * HEAD_DIM
    row = lax.broadcasted_iota(jnp.int32, (n, pair_w), 0)
    lane = lax.broadcasted_iota(jnp.int32, (n, pair_w), 1)
    col = lane & (HEAD_DIM - 1)
    even = lane < HEAD_DIM
    levels = n.bit_length()
    same = [(row >> k) == (col >> k) for k in range(levels)]
    eye = same[0].astype(F32)
    level_masks = [same[sh + 1] & jnp.logical_not(same[sh]) for sh in range(1, levels - 1)]

    def blockdiag(x2):
        xb = x2.astype(BF16)
        zero = jnp.zeros((), BF16)
        return jnp.concatenate([jnp.where(even, xb, zero), jnp.where(even, zero, xb)], axis=0)

    def mm(x2, y2):
        return jnp.dot(x2.astype(BF16), blockdiag(y2), preferred_element_type=F32)

    def mm_nt(x2, y2):
        return _dot_nt(x2.astype(BF16), blockdiag(y2))

    dirs = ((rf_ref, vf_ref, nf_ref, lwf_ref, bf_ref, kf_ref, yf_ref),
            (rb_ref, vb_ref, nb_ref, lwb_ref, bb_ref, kb_ref, yb_ref))
    chains = []
    for d, (r_ref, v_ref, n_ref, lw_ref, b_ref, k_ref, y_ref) in enumerate(dirs):
        order = row - col if d == 0 else col - row
        strict = order > 0
        incl = order >= 0
        incl_b = jnp.where(incl[:, :n], 1.0, 0.0).astype(BF16)
        for bi in range(batch):
            lw = lw_ref[0, bi]
            lw_hi, lw_mid = _split_bf16(lw)
            lw_lo = (lw - lw_hi.astype(F32) - lw_mid.astype(F32)).astype(BF16)
            g_inc = ((jnp.dot(incl_b, lw_lo, preferred_element_type=F32)
                      + jnp.dot(incl_b, lw_mid, preferred_element_type=F32))
                     + jnp.dot(incl_b, lw_hi, preferred_element_type=F32))
            g_tot = jnp.sum(lw, axis=0, keepdims=True)
            e_neg = jnp.exp(-g_inc)
            e_end = jnp.exp(g_tot - g_inc)
            decay = jnp.exp(g_tot)
            a_t = n_ref[bi] * jnp.exp(g_inc - lw)
            r_t = r_ref[bi] * jnp.exp(g_inc)
            bb = b_ref[0, bi]
            kd = k_ref[0, bi]
            b_t = bb * e_neg
            k_t = kd * e_neg
            ar_t = jnp.concatenate([a_t, r_t], axis=0).astype(BF16)
            bk_h = jnp.concatenate([bb * e_end, kd * e_end], axis=0).astype(BF16)
            v = v_ref[bi]
            for p in range(heads // 2):
                sl = slice(p * pair_w, (p + 1) * pair_w)
                chains.append(dict(strict=strict, incl=incl, sl=sl, bi=bi, y_ref=y_ref,
                                   si=(d * batch + bi) * (heads // 2) + p, decay=decay[:, sl],
                                   ar=ar_t[:, sl], b=b_t[:, sl], k=k_t[:, sl], bk_h=bk_h[:, sl], v=v[:, sl]))

    for ch in chains:
        pb = mm_nt(ch["ar"], ch["b"])
        pk = mm_nt(ch["ar"], ch["k"])
        ch["l_ab"] = jnp.where(ch["strict"], pb[:n], 0.0)
        ch["m_rb"] = jnp.where(ch["incl"], pb[n:], 0.0)
        ch["l_ak"] = jnp.where(ch["strict"], pk[:n], 0.0)
        ch["m_rk"] = jnp.where(ch["incl"], pk[n:], 0.0)
        ch["t"] = eye + jnp.where(same[1], ch["l_ab"], 0.0)
    for mask in level_masks:
        for ch in chains:
            ch["tc"] = mm(ch["t"], jnp.where(mask, ch["l_ab"], 0.0))
        for ch in chains:
            ch["t"] = ch["t"] + mm(ch["tc"], ch["t"])
    for ch in chains:
        ch["s0"] = s_ref[ch["si"]]
        ch["x"] = mm_nt(ch["ar"], ch["s0"])
    for ch in chains:
        ch["kv"] = mm(jnp.concatenate([ch["l_ak"], ch["m_rk"]], axis=0), ch["v"])
    for ch in chains:
        ch["u"] = mm(ch["t"], ch["x"][:n] + ch["kv"][:n])
    for ch in chains:
        y = ch["x"][n:] + mm(ch["m_rb"], ch["u"]) + ch["kv"][n:]
        ch["y_ref"][ch["bi"], :, ch["sl"]] = y
    for ch in chains:
        uv = jnp.concatenate([ch["u"], ch["v"]], axis=0).astype(BF16)
        full = _dot_tn(uv, ch["bk_h"])
        s_ref[ch["si"]] = ch["s0"] * ch["decay"] + jnp.where(even, full[:HEAD_DIM], full[HEAD_DIM:])


def _rwkv_scan(r, v, nkk, lw, b, kd, batch, seq):
    t, width = r.shape
    heads = width // HEAD_DIM
    n = SCAN_CHUNK
    nc = seq // n
    r3, v3, n3 = (z.reshape(batch, seq, width) for z in (r, v, nkk))
    lw4, b4, k4 = (z.reshape(2, batch, seq, width) for z in (lw, b, kd))
    fwd = pl.BlockSpec((batch, n, width), lambda c: (0, c, 0))
    bwd = pl.BlockSpec((batch, n, width), lambda c: (0, nc - 1 - c, 0))
    fwd_d = pl.BlockSpec((1, batch, n, width), lambda c: (0, 0, c, 0))
    bwd_d = pl.BlockSpec((1, batch, n, width), lambda c: (1, 0, nc - 1 - c, 0))
    kern = functools.partial(_rwkv_scan_kernel, heads=heads, batch=batch)
    yf, yb = pl.pallas_call(
        kern,
        grid=(nc,),
        in_specs=[fwd, fwd, fwd, bwd, bwd, bwd, fwd_d, fwd_d, fwd_d, bwd_d, bwd_d, bwd_d],
        out_specs=[fwd, bwd],
        out_shape=[jax.ShapeDtypeStruct((batch, seq, width), F32)] * 2,
        scratch_shapes=[pltpu.VMEM((batch * heads, HEAD_DIM, 2 * HEAD_DIM), F32)],
        compiler_params=_cparams(1, "arbitrary"),
        name="rwkv_scan",
    )(r3, v3, n3, r3, v3, n3, lw4, b4, k4, lw4, b4, k4)
    return yf.reshape(t, width), yb.reshape(t, width)


def _pool_tile(p_ref, pp_ref, pn_ref, w_ref, sc_ref, ext_ref, tb, tiles_per_batch, seq):
    p = p_ref[...]
    tm, width = p.shape
    assert all(w == 2 << i for i, w in enumerate(POOL_WINDOWS)) and POOL_WINDOWS[-1] <= 2 * HALO
    n = tm + 2 * HALO
    pad = jnp.zeros((HALO, width), F32)
    for k in range(len(POOL_WINDOWS)):
        ext_ref[k, 0:HALO, :] = pad
        ext_ref[k, HALO + n:2 * HALO + n, :] = pad
    ext_ref[0, HALO:2 * HALO, :] = jnp.where(tb == 0, 0.0, pp_ref[...])
    ext_ref[0, 2 * HALO:2 * HALO + tm, :] = p
    ext_ref[0, 2 * HALO + tm:HALO + n, :] = jnp.where(tb == tiles_per_batch - 1, 0.0, pn_ref[...])

    def rows(k, first, count):
        return ext_ref[k, 2 * HALO + first:2 * HALO + first + count, :]

    ext_ref[1, HALO:HALO + n, :] = rows(0, -HALO - 1, n) + rows(0, -HALO, n)
    for k in range(1, len(POOL_WINDOWS) - 1):
        q = POOL_WINDOWS[k - 1] // 2
        ext_ref[k + 1, HALO:HALO + n, :] = rows(k, -HALO - q, n) + rows(k, -HALO + q, n)
    q = POOL_WINDOWS[-2] // 2
    sums = [rows(k + 1, 0, tm) for k in range(len(POOL_WINDOWS) - 1)]
    sums.append(rows(len(POOL_WINDOWS) - 1, -q, tm) + rows(len(POOL_WINDOWS) - 1, q, tm))

    t = tb * tm + lax.broadcasted_iota(jnp.int32, (tm, width), 0)
    grp = lax.broadcasted_iota(jnp.int32, (tm, width), 1) // (width // len(POOL_WINDOWS))
    pooled = jnp.zeros_like(p)
    for gi, win in enumerate(POOL_WINDOWS):
        half = win // 2
        lo = jnp.clip(t - half, 0, seq - 1)
        hi = jnp.clip(t + half - 1, 0, seq - 1)
        cnt = (hi - lo + 1).astype(F32)
        pooled = jnp.where(grp == gi, sums[gi] / cnt, pooled)
    pooled = pooled - p
    return jnp.dot(pooled, w_ref[...], preferred_element_type=F32) * sc_ref[...]


def _outproj_kernel(ya_ref, yf_ref, yb_ref, bonus_ref, g_ref, p_ref, pp_ref, pn_ref, x_ref, mod_ref,
                    wa_ref, wb_ref, wc_ref, pw_ref, psc_ref,
                    gng_ref, gnb_ref, ones_ref, l1g_ref, l1b_ref, wr_ref, br_ref,
                    x1_o, u2_o, ri_o, rw_o, cnt_o, cnt_ref, ext_ref, *, alpha, tiles_per_batch, seq):
    yc = _pool_tile(p_ref, pp_ref, pn_ref, pw_ref, psc_ref, ext_ref,
                    pl.program_id(0) % tiles_per_batch, tiles_per_batch, seq)
    m = mod_ref[0]
    ones = ones_ref[...]

    def headmean(x):
        return _dot_split(x, ones) * (1.0 / HEAD_DIM)

    ysum = yf_ref[...] + yb_ref[...]
    yc0 = ysum - headmean(ysum)
    yn = yc0 * lax.rsqrt(headmean(yc0 * yc0) + GN_EPS) * gng_ref[...] + gnb_ref[...]
    yb = (yn + bonus_ref[...]) * g_ref[...]
    mix = (jnp.dot(ya_ref[...].astype(BF16), wa_ref[...], preferred_element_type=F32)
           + jnp.dot(yb.astype(BF16), wb_ref[...], preferred_element_type=F32)
           + jnp.dot(yc.astype(BF16), wc_ref[...], preferred_element_type=F32))
    x1 = _ln(alpha * x_ref[...] + m[2:3] * mix) * l1g_ref[...] + l1b_ref[...]
    x1_o[...] = x1
    u2 = _ln(x1) * (1.0 + m[4:5]) + m[3:4]
    u2_o[...] = _pack_bf16_pairs(u2)

    u_hi, u_lo = _split_bf16(u2)
    hi_both = jnp.dot(u_hi, wr_ref[...], preferred_element_type=F32)
    lg = (hi_both[:, :LANES] + hi_both[:, LANES:]
          + jnp.dot(u_lo, wr_ref[:, :LANES], preferred_element_type=F32)) + br_ref[...]
    lane = lax.broadcasted_iota(jnp.int32, lg.shape, 1)
    big = jnp.int32(1 << 20)
    gl = jnp.where(lane < N_GROUPS, lg, -jnp.inf)
    gmax = jnp.max(gl, axis=-1, keepdims=True)
    gidx = jnp.min(jnp.where(gl == gmax, lane, big), axis=-1, keepdims=True)
    pg_sel = 1.0 / jnp.sum(jnp.exp(gl - gmax), axis=-1, keepdims=True)
    e_lo = N_GROUPS + gidx * EXPERTS_PER_GROUP
    el = jnp.where((lane >= e_lo) & (lane < e_lo + EXPERTS_PER_GROUP), lg, -jnp.inf)
    m1 = jnp.max(el, axis=-1, keepdims=True)
    i1 = jnp.min(jnp.where(el == m1, lane, big), axis=-1, keepdims=True)
    el2 = jnp.where(lane == i1, -jnp.inf, el)
    m2 = jnp.max(el2, axis=-1, keepdims=True)
    i2 = jnp.min(jnp.where(el2 == m2, lane, big), axis=-1, keepdims=True)
    e21 = jnp.exp(m2 - m1)
    p1 = 1.0 / (1.0 + e21)
    p2 = e21 / (1.0 + e21)
    rw_o[...] = jnp.where(lane == 0, pg_sel * p1, jnp.where(lane == 1, pg_sel * p2, 0.0))

    @pl.when(pl.program_id(0) == 0)
    def _():
        cnt_ref[...] = jnp.zeros_like(cnt_ref)

    tm = lg.shape[0]
    earlier = (lax.broadcasted_iota(jnp.int32, (tm, tm), 1)
               < lax.broadcasted_iota(jnp.int32, (tm, tm), 0)).astype(BF16)
    oh1 = (lane == i1).astype(F32)
    oh2 = (lane == i2).astype(F32)
    run = cnt_ref[...]
    c1 = jnp.sum(oh1, axis=0, keepdims=True)
    before1 = run + jnp.dot(earlier, oh1.astype(BF16), preferred_element_type=F32)
    before2 = run + c1 + jnp.dot(earlier, oh2.astype(BF16), preferred_element_type=F32)
    rank1 = jnp.sum(oh1 * before1, axis=-1, keepdims=True).astype(jnp.int32)
    rank2 = jnp.sum(oh2 * before2, axis=-1, keepdims=True).astype(jnp.int32)
    total = run + c1 + jnp.sum(oh2, axis=0, keepdims=True)
    cnt_ref[...] = total
    cnt_o[...] = total
    ri = jnp.where(lane == 0, i1 - N_GROUPS, jnp.where(lane == 1, i2 - N_GROUPS,
                   jnp.where(lane == 2, rank1, jnp.where(lane == 3, rank2, 0))))
    ri_o[...] = jnp.transpose(ri.astype(F32))[:SUBLANES].astype(jnp.int32)


def _outproj(ya, yf, yb, bonus, g, praw, x2, modl, p, seq, tm, alpha):
    t, d = x2.shape
    tpb = seq // tm
    aw, bw, cw = ya.shape[1], bonus.shape[1], praw.shape[1]
    hb = tm // HALO
    nhb = t // HALO
    tok = lambda i: (i, 0)
    full2 = lambda i: (0, 0)
    kern = functools.partial(_outproj_kernel, alpha=alpha, tiles_per_batch=tpb, seq=seq)
    small = ["w_out_a", "w_out_b", "w_out_c", "pool_w", "pool_scale",
             "gn_gain", "gn_bias", "ones", "ln1_gain", "ln1_bias", "w_router", "b_router"]
    return pl.pallas_call(
        kern,
        grid=(t // tm,),
        in_specs=[pl.BlockSpec((tm, aw), tok),
                  pl.BlockSpec((tm, bw), tok), pl.BlockSpec((tm, bw), tok),
                  pl.BlockSpec((tm, bw), tok), pl.BlockSpec((tm, bw), tok),
                  pl.BlockSpec((tm, cw), tok),
                  pl.BlockSpec((HALO, cw), lambda i: (jnp.maximum(i * hb - 1, 0), 0)),
                  pl.BlockSpec((HALO, cw), lambda i: (jnp.minimum((i + 1) * hb, nhb - 1), 0)),
                  pl.BlockSpec((tm, d), tok),
                  pl.BlockSpec((1,) + modl.shape[1:], lambda i: (i // tpb, 0, 0))]
                 + [pl.BlockSpec(p[k].shape, functools.partial(lambda nd, i: (0,) * nd, p[k].ndim)) for k in small],
        out_specs=[pl.BlockSpec((tm, d), tok), pl.BlockSpec((tm, d // 2), tok),
                   pl.BlockSpec((SUBLANES, tm), lambda i: (0, i)), pl.BlockSpec((tm, LANES), tok),
                   pl.BlockSpec((1, LANES), full2)],
        out_shape=[jax.ShapeDtypeStruct((t, d), F32), jax.ShapeDtypeStruct((t, d // 2), jnp.uint32),
                   jax.ShapeDtypeStruct((SUBLANES, t), jnp.int32), jax.ShapeDtypeStruct((t, LANES), F32),
                   jax.ShapeDtypeStruct((1, LANES), F32)],
        scratch_shapes=[pltpu.VMEM((1, LANES), F32), pltpu.VMEM((len(POOL_WINDOWS), tm + 4 * HALO, cw), F32)],
        compiler_params=_cparams(1, "arbitrary"),
        name="outproj",
    )(ya, yf, yb, bonus, g, praw, praw, praw, x2, modl, *[p[k] for k in small])


def _dispatch(route_t, counts_lanes, n_blocks, tm):
    counts = counts_lanes[0, N_GROUPS:N_GROUPS + N_EXPERTS].astype(jnp.int32)
    padded = ((counts + EXPERT_BLOCK - 1) // EXPERT_BLOCK) * EXPERT_BLOCK
    pends = jnp.cumsum(padded)
    pstarts = pends - padded
    e = route_t[:TOP_K]
    dest = route_t[TOP_K:2 * TOP_K]
    for x in range(N_EXPERTS):
        dest = dest + jnp.where(e == x, pstarts[x], 0)
    t = route_t.shape[1]
    dest = dest.reshape(TOP_K, t // tm, tm).transpose(1, 0, 2).reshape(t // tm, 1, TOP_K * tm)
    block_start = jnp.arange(n_blocks, dtype=jnp.int32) * EXPERT_BLOCK
    block_e = jnp.minimum(jnp.sum((pends[None, :] <= block_start[:, None]).astype(jnp.int32), axis=1), N_EXPERTS - 1)
    meta = jnp.concatenate([block_e, (pends[-1] // EXPERT_BLOCK)[None]]).astype(jnp.int32)
    return dest, meta, pends.astype(jnp.int32)


def _scatter_rows_kernel(pends_ref, dest_ref, u_ref, xs_ref, zeros_ref, sem, zsem):
    tm = u_ref.shape[0] * SUBLANES

    @pl.when(pl.program_id(0) == 0)
    def _():
        zeros_ref[...] = jnp.zeros_like(zeros_ref)

        def tail_copy(e):
            tail = pl.ds(pl.multiple_of(pends_ref[e] - EXPERT_BLOCK, EXPERT_BLOCK), EXPERT_BLOCK)
            return pltpu.make_async_copy(zeros_ref, xs_ref.at[tail], zsem)

        def has_rows(e):
            return pends_ref[e] > (pends_ref[e - 1] if e > 0 else 0)

        def unused_copy(j):
            return pltpu.make_async_copy(zeros_ref, xs_ref.at[pl.ds(j * EXPERT_BLOCK, EXPERT_BLOCK)], zsem)

        def is_unused(j):
            return j * EXPERT_BLOCK >= pends_ref[N_EXPERTS - 1]

        n_blocks = xs_ref.shape[0] // EXPERT_BLOCK
        for e in range(N_EXPERTS):
            pl.when(has_rows(e))(lambda e=e: tail_copy(e).start())
        for j in range(n_blocks):
            pl.when(is_unused(j))(lambda j=j: unused_copy(j).start())
        for e in range(N_EXPERTS):
            pl.when(has_rows(e))(lambda e=e: tail_copy(e).wait())
        for j in range(n_blocks):
            pl.when(is_unused(j))(lambda j=j: unused_copy(j).wait())

    def issue(grp, carry):
        for j in range(SUBLANES):
            for k in range(TOP_K):
                dst = dest_ref[0, 0, k * tm + SUBLANES * grp + j]
                pltpu.make_async_copy(u_ref.at[grp, pl.ds(j, 1)], xs_ref.at[pl.ds(dst, 1)], sem).start()
        return carry

    lax.fori_loop(0, tm // SUBLANES, issue, 0)
    rows = pl.ds(0, TOP_K * tm)
    pltpu.make_async_copy(xs_ref.at[rows], xs_ref.at[rows], sem).wait()


def _scatter_rows(pends, u2, dest3, total, tm):
    t, d = u2.shape
    grid_spec = pltpu.PrefetchScalarGridSpec(
        num_scalar_prefetch=1,
        grid=(t // tm,),
        in_specs=[pl.BlockSpec((1, 1, TOP_K * tm), lambda i, p: (i, 0, 0), memory_space=pltpu.SMEM),
                  pl.BlockSpec((tm // SUBLANES, SUBLANES, d), lambda i, p: (i, 0, 0))],
        out_specs=pl.BlockSpec(memory_space=pl.ANY),
        scratch_shapes=[pltpu.VMEM((EXPERT_BLOCK, d), u2.dtype), pltpu.SemaphoreType.DMA(()),
                        pltpu.SemaphoreType.DMA(())],
    )
    return pl.pallas_call(
        _scatter_rows_kernel,
        grid_spec=grid_spec,
        out_shape=jax.ShapeDtypeStruct((total, d), u2.dtype),
        compiler_params=_cparams(1, "arbitrary"),
        name="scatter_rows",
    )(pends, dest3, u2.reshape(t // SUBLANES, SUBLANES, d))


def _experts_kernel(meta_ref, xs_ref, wg_ref, wu_ref, wd_ref, o_ref, wg_b, wu_b, wd_b):
    i = pl.program_id(0)
    n_used = meta_ref[pl.num_programs(0)]

    @pl.when((i == 0) | (meta_ref[i] != meta_ref[jnp.maximum(i - 1, 0)]))
    def _():
        wg_b[...] = wg_ref[0, 0].astype(BF16)
        wu_b[...] = wu_ref[0, 0].astype(BF16)
        wd_b[...] = wd_ref[0, 0].astype(BF16)

    @pl.when(i < n_used)
    def _():
        xb = _unpack_bf16_pairs(xs_ref[...])
        gate = jnp.dot(xb, wg_b[...], preferred_element_type=F32)
        up = jnp.dot(xb, wu_b[...], preferred_element_type=F32)
        hb = gate * _sigmoid(gate) * up
        o_ref[...] = _pack_bf16_pairs(jnp.dot(hb.astype(BF16), wd_b[...], preferred_element_type=F32))

    @pl.when(i >= n_used)
    def _():
        o_ref[...] = jnp.zeros_like(o_ref)


def _experts(meta, xs, wg, wu, wd, layer):
    total, dp = xs.shape
    nb = total // EXPERT_BLOCK
    d, de = wg.shape[2:]
    grid_spec = pltpu.PrefetchScalarGridSpec(
        num_scalar_prefetch=1,
        grid=(nb,),
        in_specs=[pl.BlockSpec((EXPERT_BLOCK, dp), lambda i, m: (jnp.minimum(i, m[nb] - 1), 0)),
                  pl.BlockSpec((1, 1, d, de), lambda i, m: (layer, m[i], 0, 0)),
                  pl.BlockSpec((1, 1, d, de), lambda i, m: (layer, m[i], 0, 0)),
                  pl.BlockSpec((1, 1, de, d), lambda i, m: (layer, m[i], 0, 0))],
        out_specs=pl.BlockSpec((EXPERT_BLOCK, dp), lambda i, m: (i, 0)),
        scratch_shapes=[pltpu.VMEM((d, de), BF16), pltpu.VMEM((d, de), BF16), pltpu.VMEM((de, d), BF16)],
    )
    return pl.pallas_call(
        _experts_kernel,
        grid_spec=grid_spec,
        out_shape=jax.ShapeDtypeStruct((total, dp), xs.dtype),
        compiler_params=_cparams(1, "arbitrary"),
        name="experts",
    )(meta, xs, wg, wu, wd)


def _final_kernel(dcur_ref, dnext_ref, x1_ref, rw_ref, mod_ref, g_ref, b_ref, ys_ref, o_ref, ybuf, sem, *, alpha):
    i = pl.program_id(0)
    tm = x1_ref.shape[0]
    slot = i % 2

    def gather(d_ref, s):
        def issue(grp, carry):
            for j in range(SUBLANES):
                for k in range(TOP_K):
                    src = d_ref[0, 0, k * tm + SUBLANES * grp + j]
                    pltpu.make_async_copy(ys_ref.at[pl.ds(src, 1)], ybuf.at[s, k, grp, pl.ds(j, 1)],
                                          sem.at[s]).start()
            return carry

        lax.fori_loop(0, tm // SUBLANES, issue, 0)

    @pl.when(i == 0)
    def _():
        gather(dcur_ref, 0)

    @pl.when(i + 1 < pl.num_programs(0))
    def _():
        gather(dnext_ref, 1 - slot)

    pltpu.make_async_copy(ybuf.at[slot], ybuf.at[slot], sem.at[slot]).wait()
    m = mod_ref[0]
    rw = rw_ref[...]
    dp = ybuf.shape[-1]
    y1 = _unpack_bf16_pairs(ybuf[slot, 0].reshape(tm, dp)).astype(F32)
    y2 = _unpack_bf16_pairs(ybuf[slot, 1].reshape(tm, dp)).astype(F32)
    f = rw[:, 0:1] * y1 + rw[:, 1:2] * y2
    o_ref[...] = _ln(alpha * x1_ref[...] + m[5:6] * f) * g_ref[...] + b_ref[...]


def _final(x1, ysorted, dest3, rw, modl, gain, bias, seq, tm, alpha):
    t, d = x1.shape
    tpb = seq // tm
    n_tiles = t // tm
    tok = lambda i: (i, 0)
    kern = functools.partial(_final_kernel, alpha=alpha)
    dspec = lambda f: pl.BlockSpec((1, 1, TOP_K * tm), f, memory_space=pltpu.SMEM)
    return pl.pallas_call(
        kern,
        grid=(n_tiles,),
        in_specs=[dspec(lambda i: (i, 0, 0)), dspec(lambda i: (jnp.minimum(i + 1, n_tiles - 1), 0, 0)),
                  pl.BlockSpec((tm, d), tok), pl.BlockSpec((tm, LANES), tok),
                  pl.BlockSpec((1,) + modl.shape[1:], lambda i: (i // tpb, 0, 0)),
                  pl.BlockSpec(gain.shape, lambda i: (0, 0)), pl.BlockSpec(bias.shape, lambda i: (0, 0)),
                  pl.BlockSpec(memory_space=pl.ANY)],
        out_specs=pl.BlockSpec((tm, d), tok),
        out_shape=jax.ShapeDtypeStruct((t, d), F32),
        scratch_shapes=[pltpu.VMEM((2, TOP_K, tm // SUBLANES, SUBLANES, ysorted.shape[1]), ysorted.dtype),
                        pltpu.SemaphoreType.DMA((2,))],
        compiler_params=_cparams(1, "arbitrary"),
        name="final_ln",
    )(dest3, dest3, x1, rw, modl, gain, bias, ysorted)


def _block_diag(blocks):
    n, a, b = blocks.shape
    out = jnp.zeros((n * a, n * b), blocks.dtype)
    for i in range(n):
        out = out.at[i * a:(i + 1) * a, i * b:(i + 1) * b].set(blocks[i])
    return out


def _pad_rows(w, lo, total):
    return jnp.zeros((total, w.shape[-1]), w.dtype).at[lo:lo + w.shape[0]].set(w)


def kernel(x, c, w_mod, b_mod, w_in, na_rpb, rw_conv, rw_w0, rw_w_up, rw_a0, rw_a_up, rw_g_up, rw_k_k, rw_k_a, rw_r_k, rw_gn_gain, rw_gn_bias, pool_w, pool_scale, w_out, ln1_gain, ln1_bias, ln2_gain, ln2_bias, moe_w_group, moe_b_group, moe_w_expert, moe_b_expert, moe_w_gate, moe_w_up, moe_w_down):
    batch, seq, d = x.shape
    depth = w_mod.shape[0]
    t = batch * seq
    a_w = na_rpb.shape[1] * HEAD_DIM
    b_w = rw_w0.shape[-1]
    c_w = pool_scale.shape[-1]
    lr_w = R_W + R_A + R_G
    alpha = (2 * depth) ** 0.25
    tm = min(512, seq)
    tm_in = min(512, seq)
    assert seq % tm == 0 and seq % SCAN_CHUNK == 0 and seq % GRID_W == 0 and lr_w == LANES

    mod = _modulation(c, w_mod, b_mod)
    ones_blk = _block_diag(jnp.ones((b_w // HEAD_DIM, HEAD_DIM, HEAD_DIM), BF16))
    row = lambda v: v.reshape(1, -1)

    x2 = x.reshape(t, d)
    for l in range(depth):
        modl = mod[l]
        prep_params = {
            "conv": rw_conv[l], "w0": rw_w0[l], "a0": rw_a0[l],
            "w_up": jnp.stack([_pad_rows(rw_w_up[l, dd], 0, lr_w) for dd in range(2)]).astype(BF16),
            "a_up": jnp.stack([_pad_rows(rw_a_up[l, dd], R_W, lr_w) for dd in range(2)]).astype(BF16),
            "g_up": _pad_rows(rw_g_up[l], R_W + R_A, lr_w).astype(BF16),
            "k_k": row(rw_k_k[l]), "k_a": row(rw_k_a[l]), "r_k": row(rw_r_k[l]), "ones": ones_blk,
        }
        qkv, praw, r, v, nkk, lw, bb, kd, bonus, g = _inproj(x2, modl, w_in[l].astype(BF16), prep_params, seq, tm_in,
                                                             3 * a_w, 3 * b_w, lr_w, c_w)
        ya = _natten(qkv, _na_bias_table(na_rpb[l]), batch, seq, a_w)
        yf, yb = _rwkv_scan(r, v, nkk, lw, bb, kd, batch, seq)
        w_router = jnp.zeros((d, LANES), F32).at[:, :N_GROUPS].set(moe_w_group[l])
        w_router = w_router.at[:, N_GROUPS:N_GROUPS + N_EXPERTS].set(moe_w_expert[l])
        b_router = jnp.zeros((1, LANES), F32).at[0, :N_GROUPS].set(moe_b_group[l])
        b_router = b_router.at[0, N_GROUPS:N_GROUPS + N_EXPERTS].set(moe_b_expert[l])
        wo = w_out[l].astype(BF16)
        out_params = {
            "w_out_a": wo[:a_w], "w_out_b": wo[a_w:a_w + b_w], "w_out_c": wo[a_w + b_w:],
            "pool_w": _block_diag(pool_w[l]), "pool_scale": row(pool_scale[l]),
            "gn_gain": row(rw_gn_gain[l]), "gn_bias": row(rw_gn_bias[l]), "ones": ones_blk,
            "ln1_gain": row(ln1_gain[l]), "ln1_bias": row(ln1_bias[l]),
            "w_router": jnp.concatenate(_split_bf16(w_router), axis=1), "b_router": b_router,
        }
        x1, u2, route_i, route_w, counts = _outproj(ya, yf, yb, bonus, g, praw, x2, modl, out_params, seq, tm, alpha)
        n_blocks = -(-(t * TOP_K) // EXPERT_BLOCK) + N_EXPERTS
        dest3, meta, pends = _dispatch(route_i, counts, n_blocks, tm)
        xs = _scatter_rows(pends, u2, dest3, n_blocks * EXPERT_BLOCK, tm)
        ysorted = _experts(meta, xs, moe_w_gate, moe_w_up, moe_w_down, l)
        x2 = _final(x1, ysorted, dest3, route_w, modl, row(ln2_gain[l]), row(ln2_bias[l]), seq, tm, alpha)
    return x2.reshape(batch, seq, d)
```

```python
import functools
import math

import jax
import jax.numpy as jnp
import numpy as np
from jax import lax
from jax.experimental import pallas as pl
from jax.experimental.pallas import tpu as pltpu

F32 = jnp.float32
BF16 = jnp.bfloat16
HI = lax.Precision.HIGHEST

GRID_W = 64
HEAD_DIM = 64
NA_KH = 8
NA_KW = 16
POOL_WINDOWS = (2, 4, 8, 16)
R_W = 32
R_A = 32
R_G = 64
DECAY_SCALE = math.exp(-0.5)
GN_EPS = 64e-5
N_GROUPS = 4
EXPERTS_PER_GROUP = 8
N_EXPERTS = N_GROUPS * EXPERTS_PER_GROUP
TOP_K = 2
EXPERT_BLOCK = 512
LN_EPS = 1e-5
NEG_INF = -1e30

NA_ROWS_PER_STEP = 8
SCAN_CHUNK = 64
SCAN_CHUNKS_PER_STEP = 2
SUBLANES = 8
HALO = 8
LANES = 128
VMEM_LIMIT = 52 * 1024 * 1024


def _ln(x):
    mu = jnp.mean(x, axis=-1, keepdims=True)
    xc = x - mu
    var = jnp.mean(xc * xc, axis=-1, keepdims=True)
    return xc * lax.rsqrt(var + LN_EPS)


def _sigmoid(x):
    return 1.0 / (1.0 + jnp.exp(-x))


def _split_bf16(x):
    hi = x.astype(BF16)
    return hi, (x - hi.astype(F32)).astype(BF16)


def _dot_split(x, w_exact):
    hi, lo = _split_bf16(x)
    return jnp.dot(hi, w_exact, preferred_element_type=F32) + jnp.dot(lo, w_exact, preferred_element_type=F32)


def _pack_bf16_pairs(x):
    h = x.shape[1] // 2
    lo = lax.bitcast_convert_type(x[:, :h].astype(BF16).astype(F32), jnp.uint32)
    hi = lax.bitcast_convert_type(x[:, h:].astype(BF16).astype(F32), jnp.uint32)
    return (lo >> 16) | hi


def _unpack_bf16_pairs(w):
    lo = lax.bitcast_convert_type(w << 16, F32).astype(BF16)
    hi = lax.bitcast_convert_type(w & jnp.uint32(0xFFFF0000), F32).astype(BF16)
    return jnp.concatenate([lo, hi], axis=1)


def _cparams(n_axes, semantics="parallel"):
    return pltpu.CompilerParams(dimension_semantics=(semantics,) * n_axes, vmem_limit_bytes=VMEM_LIMIT)


def _mod_kernel(c_ref, w_ref, b_ref, o_ref):
    c = c_ref[...]
    s = c * _sigmoid(c)
    o_ref[0] = jnp.dot(s, w_ref[0], precision=HI, preferred_element_type=F32) + b_ref[0]


def _modulation(c, w_mod, b_mod):
    n_layers, d, d6 = w_mod.shape
    b = c.shape[0]
    bp = -(-b // 8) * 8
    cp = jnp.zeros((bp, d), F32).at[:b].set(c)
    out = pl.pallas_call(
        _mod_kernel,
        grid=(n_layers, d6 // d),
        in_specs=[pl.BlockSpec((bp, d), lambda l, j: (0, 0)),
                  pl.BlockSpec((1, d, d), lambda l, j: (l, 0, j)),
                  pl.BlockSpec((1, 1, d), lambda l, j: (l, 0, j))],
        out_specs=pl.BlockSpec((1, bp, d), lambda l, j: (l, 0, j)),
        out_shape=jax.ShapeDtypeStruct((n_layers, bp, d6), F32),
        compiler_params=_cparams(2),
        name="modulation",
    )(cp, w_mod, b_mod.reshape(n_layers, 1, d6))
    return out[:, :b].reshape(n_layers, b, d6 // d, d)


def _inproj_kernel(x_ref, xp_ref, xn_ref, mod_ref, wr_ref, wo_ref, cw_ref, w0_ref, wup_ref, a0_ref, aup_ref, gup_ref,
                   kk_ref, ka_ref, rk_ref, ones_ref,
                   qkv_o, pool_o, r_o, v_o, nkk_o, lw_o, b_o, kd_o, bonus_o, g_o,
                   *, a3, b3, tiles_per_batch):
    m = mod_ref[0]
    tm = x_ref.shape[0]
    tb = pl.program_id(0) % tiles_per_batch
    xe = jnp.concatenate([xp_ref[...], x_ref[...], xn_ref[...]], axis=0)
    u = (_ln(xe) * (1.0 + m[1:2]) + m[0:1]).astype(BF16)
    h = jnp.dot(u, wr_ref[...], preferred_element_type=F32)
    ho = jnp.dot(u[HALO:HALO + tm], wo_ref[...], preferred_element_type=F32)
    hm = h[HALO:HALO + tm]
    prev = jnp.where(tb == 0, 0.0, h[HALO - 1:HALO, :b3])
    nxt = jnp.where(tb == tiles_per_batch - 1, 0.0, h[HALO + tm:HALO + tm + 1, :b3])
    _rwkv_prep_tile(hm[:, :b3], prev, nxt, hm[:, b3:],
                    cw_ref, w0_ref, wup_ref, a0_ref, aup_ref, gup_ref, kk_ref, ka_ref, rk_ref, ones_ref,
                    r_o, v_o, nkk_o, lw_o, b_o, kd_o, bonus_o, g_o)
    qkv_o[...] = ho[:, :a3].astype(BF16)
    pool_o[...] = ho[:, a3:]


def _inproj(x2, modl, w_in_bf, p, seq, tm, a3, b3, lr_w, c_w):
    t, d = x2.shape
    tpb = seq // tm
    hb = tm // HALO
    nhb = t // HALO
    width = b3 // 3
    kern = functools.partial(_inproj_kernel, a3=a3, b3=b3, tiles_per_batch=tpb)
    w_rwkv = w_in_bf[:, a3:a3 + b3 + lr_w]
    w_other = jnp.concatenate([w_in_bf[:, :a3], w_in_bf[:, a3 + b3 + lr_w:]], axis=1)
    tok = lambda i: (i, 0)
    dtok = lambda i: (0, i, 0)
    names = ["conv", "w0", "w_up", "a0", "a_up", "g_up", "k_k", "k_a", "r_k", "ones"]
    tw = jax.ShapeDtypeStruct((t, width), F32)
    dtw = jax.ShapeDtypeStruct((2, t, width), F32)
    return pl.pallas_call(
        kern,
        grid=(t // tm,),
        in_specs=[pl.BlockSpec((tm, d), tok),
                  pl.BlockSpec((HALO, d), lambda i: (jnp.maximum(i * hb - 1, 0), 0)),
                  pl.BlockSpec((HALO, d), lambda i: (jnp.minimum((i + 1) * hb, nhb - 1), 0)),
                  pl.BlockSpec((1,) + modl.shape[1:], lambda i: (i // tpb, 0, 0)),
                  pl.BlockSpec(w_rwkv.shape, lambda i: (0, 0)),
                  pl.BlockSpec(w_other.shape, lambda i: (0, 0))]
                 + [pl.BlockSpec(p[k].shape, functools.partial(lambda nd, i: (0,) * nd, p[k].ndim)) for k in names],
        out_specs=[pl.BlockSpec((tm, a3), tok), pl.BlockSpec((tm, c_w), tok),
                   pl.BlockSpec((tm, width), tok), pl.BlockSpec((tm, width), tok), pl.BlockSpec((tm, width), tok),
                   pl.BlockSpec((2, tm, width), dtok), pl.BlockSpec((2, tm, width), dtok),
                   pl.BlockSpec((2, tm, width), dtok),
                   pl.BlockSpec((tm, width), tok), pl.BlockSpec((tm, width), tok)],
        out_shape=[jax.ShapeDtypeStruct((t, a3), BF16), jax.ShapeDtypeStruct((t, c_w), F32),
                   tw, tw, tw, dtw, dtw, dtw, tw, tw],
        compiler_params=_cparams(1),
        name="inproj",
    )(x2, x2, x2, modl, w_rwkv, w_other, *[p[k] for k in names])


def _na_bias_table(rpb):
    col = np.arange(GRID_W)
    cstart = np.clip(col - NA_KW // 2, 0, GRID_W - NA_KW)
    in_win = (col[None, :] >= cstart[:, None]) & (col[None, :] < cstart[:, None] + NA_KW)
    dc = np.clip(col[None, :] - col[:, None], -(NA_KW - 1), NA_KW - 1) + (NA_KW - 1)
    pick = (dc[None] == np.arange(2 * NA_KW - 1)[:, None, None]).astype(np.float32)
    cols = jnp.einsum("hrc,cqk->hrqk", rpb.astype(F32), pick, precision=HI)
    cols = jnp.where(in_win, cols, NEG_INF)
    b = jnp.stack([cols[:, NA_KH - 1 - o:2 * NA_KH - 1 - o] for o in range(NA_KH)])
    h = rpb.shape[0]
    return jnp.transpose(b, (0, 1, 3, 2, 4)).reshape(NA_KH, h * GRID_W, NA_KH * GRID_W)


def _natten_kernel(q_ref, k_ref, v_ref, bias_ref, o_ref, *, rows, heads):
    width = q_ref.shape[1]
    nk = NA_KH * GRID_W
    head_of_lane = lax.broadcasted_iota(jnp.int32, (heads * GRID_W, width), 1) // HEAD_DIM
    head_of_row = lax.broadcasted_iota(jnp.int32, (heads * GRID_W, width), 0) // GRID_W
    own = head_of_lane == head_of_row
    for j in range(NA_ROWS_PER_STEP):
        r = pl.program_id(1) * NA_ROWS_PER_STEP + j
        rstart = jnp.clip(r - NA_KH // 2, 0, rows - NA_KH)
        off = r - rstart
        start = pl.multiple_of(rstart * GRID_W, GRID_W)
        kw = k_ref[pl.ds(start, nk), :]
        vw = v_ref[pl.ds(start, nk), :]
        q = q_ref[j * GRID_W:(j + 1) * GRID_W, :]
        qs = jnp.where(own, jnp.concatenate([q] * heads, axis=0), jnp.zeros((), q.dtype))
        s = lax.dot_general(qs, kw, (((1,), (1,)), ((), ())), preferred_element_type=F32) * (HEAD_DIM ** -0.5)
        s = s + bias_ref[off]
        mx = jnp.max(s, axis=-1, keepdims=True)
        p = jnp.exp(s - mx)
        den = jnp.sum(p, axis=-1, keepdims=True)
        o = jnp.where(own, jnp.dot(p.astype(BF16), vw, preferred_element_type=F32) / den, 0.0)
        acc = o[0:GRID_W]
        for h in range(1, heads):
            acc = acc + o[h * GRID_W:(h + 1) * GRID_W]
        o_ref[j * GRID_W:(j + 1) * GRID_W, :] = acc.astype(o_ref.dtype)


def _natten(qkv, bias_tab, batch, seq, width):
    rows = seq // GRID_W
    assert rows >= NA_KH
    heads = width // HEAD_DIM
    steps = rows // NA_ROWS_PER_STEP
    assert steps * NA_ROWS_PER_STEP == rows
    tq = NA_ROWS_PER_STEP * GRID_W
    kern = functools.partial(_natten_kernel, rows=rows, heads=heads)
    return pl.pallas_call(
        kern,
        grid=(batch, steps),
        in_specs=[pl.BlockSpec((tq, width), lambda b, r: (b * steps + r, 0)),
                  pl.BlockSpec((seq, width), lambda b, r: (b, 1)),
                  pl.BlockSpec((seq, width), lambda b, r: (b, 2)),
                  pl.BlockSpec(bias_tab.shape, lambda b, r: (0, 0, 0))],
        out_specs=pl.BlockSpec((tq, width), lambda b, r: (b * steps + r, 0)),
        out_shape=jax.ShapeDtypeStruct((batch * seq, width), BF16),
        compiler_params=_cparams(2),
        name="natten",
    )(qkv, qkv, qkv, bias_tab)


def _rwkv_prep_tile(z, prev, nxt, lr, cw_ref, w0_ref, wup_ref, a0_ref, aup_ref, gup_ref, kk_ref, ka_ref, rk_ref, ones_ref,
                    r_o, v_o, nkk_o, lw_o, b_o, kd_o, bonus_o, g_o):
    tm = z.shape[0]
    width = z.shape[1] // 3
    row = lax.broadcasted_iota(jnp.int32, z.shape, 0)
    zm1 = jnp.where(row == 0, prev, pltpu.roll(z, 1, 0))
    zp1 = jnp.where(row == tm - 1, nxt, pltpu.roll(z, tm - 1, 0))
    rkv = zm1 * cw_ref[0:1, :] + z * cw_ref[1:2, :] + zp1 * cw_ref[2:3, :]
    r = rkv[:, :width]
    k = rkv[:, width:2 * width]
    v = rkv[:, 2 * width:]
    th = jnp.tanh(lr)
    sg = _sigmoid(lr)
    ones = ones_ref[...]

    def headsum(x):
        return _dot_split(x, ones)

    kk = k * kk_ref[...]
    kk = kk * lax.rsqrt(jnp.maximum(headsum(kk * kk), 1e-24))
    g_o[...] = jnp.dot(sg.astype(BF16), gup_ref[...], preferred_element_type=F32)
    r_o[...] = r
    v_o[...] = v
    nkk_o[...] = -kk
    kd_sum = jnp.zeros_like(r)
    th_b = th.astype(BF16)
    lr_b = lr.astype(BF16)
    for d in range(2):
        wl = jnp.dot(th_b, wup_ref[d], preferred_element_type=F32) + w0_ref[d:d + 1, :]
        lw_o[d] = -DECAY_SCALE * _sigmoid(wl)
        a = _sigmoid(jnp.dot(lr_b, aup_ref[d], preferred_element_type=F32) + a0_ref[d:d + 1, :])
        kd = k * (1.0 + (a - 1.0) * ka_ref[...])
        kd_o[d] = kd
        b_o[d] = kk * a
        kd_sum = kd_sum + kd
    bonus_o[...] = headsum(r * kd_sum * rk_ref[...]) * v


def _dot_nt(a, b):
    return lax.dot_general(a, b, (((1,), (1,)), ((), ())), preferred_element_type=F32)


def _dot_tn(a, b):
    return lax.dot_general(a, b, (((0,), (0,)), ((), ())), preferred_element_type=F32)


def _mm(a, b):
    return jnp.dot(a.astype(BF16), b.astype(BF16), preferred_element_type=F32)


def _rwkv_scan_kernel(rf_ref, vf_ref, nf_ref, rb_ref, vb_ref, nb_ref, lwf_ref, bf_ref, kf_ref, lwb_ref, bb_ref, kb_ref,
                      yf_ref, yb_ref, s_ref, *, heads, batch):
    @pl.when(pl.program_id(0) == 0)
    def _():
        s_ref[...] = jnp.zeros_like(s_ref)

    n = SCAN_CHUNK
    pair_w = 2 * HEAD_DIM
    row = lax.broadcasted_iota(jnp.int32, (n, pair_w), 0)
    lane = lax.broadcasted_iota(jnp.int32, (n, pair_w), 1)
    col = lane & (HEAD_DIM - 1)
    even = lane < HEAD_DIM
    levels = n.bit_length()
    same = [(row >> k) == (col >> k) for k in range(levels)]
    eye = same[0].astype(F32)
    level_masks = [same[sh + 1] & jnp.logical_not(same[sh]) for sh in range(1, levels - 1)]

    def blockdiag(x2):
        xb = x2.astype(BF16)
        zero = jnp.zeros((), BF16)
        return jnp.concatenate([jnp.where(even, xb, zero), jnp.where(even, zero, xb)], axis=0)

    def mm(x2, y2):
        return jnp.dot(x2.astype(BF16), blockdiag(y2), preferred_element_type=F32)

    def mm_nt(x2, y2):
        return _dot_nt(x2.astype(BF16), blockdiag(y2))

    dirs = ((rf_ref, vf_ref, nf_ref, lwf_ref, bf_ref, kf_ref, yf_ref),
            (rb_ref, vb_ref, nb_ref, lwb_ref, bb_ref, kb_ref, yb_ref))
    def build(q):
        chains = []
        for d, (r_ref, v_ref, n_ref, lw_ref, b_ref, k_ref, y_ref) in enumerate(dirs):
            sub = q if d == 0 else SCAN_CHUNKS_PER_STEP - 1 - q
            rs = slice(sub * n, (sub + 1) * n)
            order = row - col if d == 0 else col - row
            strict = order > 0
            incl = order >= 0
            incl_b = jnp.where(incl[:, :n], 1.0, 0.0).astype(BF16)
            for bi in range(batch):
                lw = lw_ref[0, bi, rs]
                lw_hi, lw_mid = _split_bf16(lw)
                lw_lo = (lw - lw_hi.astype(F32) - lw_mid.astype(F32)).astype(BF16)
                g_inc = ((jnp.dot(incl_b, lw_lo, preferred_element_type=F32)
                          + jnp.dot(incl_b, lw_mid, preferred_element_type=F32))
                         + jnp.dot(incl_b, lw_hi, preferred_element_type=F32))
                g_tot = jnp.sum(lw, axis=0, keepdims=True)
                e_neg = jnp.exp(-g_inc)
                e_end = jnp.exp(g_tot - g_inc)
                decay = jnp.exp(g_tot)
                a_t = n_ref[bi, rs] * jnp.exp(g_inc - lw)
                r_t = r_ref[bi, rs] * jnp.exp(g_inc)
                bb = b_ref[0, bi, rs]
                kd = k_ref[0, bi, rs]
                b_t = bb * e_neg
                k_t = kd * e_neg
                ar_t = jnp.concatenate([a_t, r_t], axis=0).astype(BF16)
                bk_h = jnp.concatenate([bb * e_end, kd * e_end], axis=0).astype(BF16)
                v = v_ref[bi, rs]
                for p in range(heads // 2):
                    sl = slice(p * pair_w, (p + 1) * pair_w)
                    chains.append(dict(strict=strict, incl=incl, sl=sl, rs=rs, bi=bi, y_ref=y_ref,
                                       si=(d * batch + bi) * (heads // 2) + p, decay=decay[:, sl],
                                       ar=ar_t[:, sl], b=b_t[:, sl], k=k_t[:, sl], bk_h=bk_h[:, sl], v=v[:, sl]))
        return chains

    def state_free_stages(chains):
        def products():
            for ch in chains:
                pb = mm_nt(ch["ar"], ch["b"])
                pk = mm_nt(ch["ar"], ch["k"])
                ch["l_ab"] = jnp.where(ch["strict"], pb[:n], 0.0)
                ch["m_rb"] = jnp.where(ch["incl"], pb[n:], 0.0)
                ch["l_ak"] = jnp.where(ch["strict"], pk[:n], 0.0)
                ch["m_rk"] = jnp.where(ch["incl"], pk[n:], 0.0)
                ch["t"] = eye + jnp.where(same[1], ch["l_ab"], 0.0)

        def level_left(mask):
            for ch in chains:
                ch["tc"] = mm(ch["t"], jnp.where(mask, ch["l_ab"], 0.0))

        def level_right():
            for ch in chains:
                ch["t"] = ch["t"] + mm(ch["tc"], ch["t"])

        def values():
            for ch in chains:
                ch["kv"] = mm(jnp.concatenate([ch["l_ak"], ch["m_rk"]], axis=0), ch["v"])

        stages = [products]
        for mask in level_masks:
            stages += [functools.partial(level_left, mask), level_right]
        return stages + [values]

    def state_stages(chains):
        def read():
            for ch in chains:
                ch["s0"] = s_ref[ch["si"]]
                ch["x"] = mm_nt(ch["ar"], ch["s0"])

        def solve():
            for ch in chains:
                ch["u"] = mm(ch["t"], ch["x"][:n] + ch["kv"][:n])

        def emit():
            for ch in chains:
                y = ch["x"][n:] + mm(ch["m_rb"], ch["u"]) + ch["kv"][n:]
                ch["y_ref"][ch["bi"], ch["rs"], ch["sl"]] = y

        def write():
            for ch in chains:
                uv = jnp.concatenate([ch["u"], ch["v"]], axis=0).astype(BF16)
                full = _dot_tn(uv, ch["bk_h"])
                s_ref[ch["si"]] = ch["s0"] * ch["decay"] + jnp.where(even, full[:HEAD_DIM], full[HEAD_DIM:])

        return [read, solve, emit, write]

    pending = []
    for q in range(SCAN_CHUNKS_PER_STEP):
        chains = build(q)
        free = state_free_stages(chains)
        if pending:
            share = -(-len(free) // len(pending))
            for i, carried in enumerate(pending):
                carried()
                for stage in free[i * share:(i + 1) * share]:
                    stage()
        else:
            for stage in free:
                stage()
        pending = state_stages(chains)
    for carried in pending:
        carried()


def _rwkv_scan(r, v, nkk, lw, b, kd, batch, seq):
    t, width = r.shape
    heads = width // HEAD_DIM
    n = SCAN_CHUNK * SCAN_CHUNKS_PER_STEP
    nc = seq // n
    assert nc * n == seq
    r3, v3, n3 = (z.reshape(batch, seq, width) for z in (r, v, nkk))
    lw4, b4, k4 = (z.reshape(2, batch, seq, width) for z in (lw, b, kd))
    fwd = pl.BlockSpec((batch, n, width), lambda c: (0, c, 0))
    bwd = pl.BlockSpec((batch, n, width), lambda c: (0, nc - 1 - c, 0))
    fwd_d = pl.BlockSpec((1, batch, n, width), lambda c: (0, 0, c, 0))
    bwd_d = pl.BlockSpec((1, batch, n, width), lambda c: (1, 0, nc - 1 - c, 0))
    kern = functools.partial(_rwkv_scan_kernel, heads=heads, batch=batch)
    yf, yb = pl.pallas_call(
        kern,
        grid=(nc,),
        in_specs=[fwd, fwd, fwd, bwd, bwd, bwd, fwd_d, fwd_d, fwd_d, bwd_d, bwd_d, bwd_d],
        out_specs=[fwd, bwd],
        out_shape=[jax.ShapeDtypeStruct((batch, seq, width), F32)] * 2,
        scratch_shapes=[pltpu.VMEM((batch * heads, HEAD_DIM, 2 * HEAD_DIM), F32)],
        compiler_params=_cparams(1, "arbitrary"),
        name="rwkv_scan",
    )(r3, v3, n3, r3, v3, n3, lw4, b4, k4, lw4, b4, k4)
    return yf.reshape(t, width), yb.reshape(t, width)


def _pool_tile(p_ref, pp_ref, pn_ref, w_ref, sc_ref, ext_ref, tb, tiles_per_batch, seq):
    p = p_ref[...]
    tm, width = p.shape
    assert all(w == 2 << i for i, w in enumerate(POOL_WINDOWS)) and POOL_WINDOWS[-1] <= 2 * HALO
    n = tm + 2 * HALO
    pad = jnp.zeros((HALO, width), F32)
    for k in range(len(POOL_WINDOWS)):
        ext_ref[k, 0:HALO, :] = pad
        ext_ref[k, HALO + n:2 * HALO + n, :] = pad
    ext_ref[0, HALO:2 * HALO, :] = jnp.where(tb == 0, 0.0, pp_ref[...])
    ext_ref[0, 2 * HALO:2 * HALO + tm, :] = p
    ext_ref[0, 2 * HALO + tm:HALO + n, :] = jnp.where(tb == tiles_per_batch - 1, 0.0, pn_ref[...])

    def rows(k, first, count):
        return ext_ref[k, 2 * HALO + first:2 * HALO + first + count, :]

    ext_ref[1, HALO:HALO + n, :] = rows(0, -HALO - 1, n) + rows(0, -HALO, n)
    for k in range(1, len(POOL_WINDOWS) - 1):
        q = POOL_WINDOWS[k - 1] // 2
        ext_ref[k + 1, HALO:HALO + n, :] = rows(k, -HALO - q, n) + rows(k, -HALO + q, n)
    q = POOL_WINDOWS[-2] // 2
    sums = [rows(k + 1, 0, tm) for k in range(len(POOL_WINDOWS) - 1)]
    sums.append(rows(len(POOL_WINDOWS) - 1, -q, tm) + rows(len(POOL_WINDOWS) - 1, q, tm))

    t = tb * tm + lax.broadcasted_iota(jnp.int32, (tm, width), 0)
    grp = lax.broadcasted_iota(jnp.int32, (tm, width), 1) // (width // len(POOL_WINDOWS))
    pooled = jnp.zeros_like(p)
    for gi, win in enumerate(POOL_WINDOWS):
        half = win // 2
        lo = jnp.clip(t - half, 0, seq - 1)
        hi = jnp.clip(t + half - 1, 0, seq - 1)
        cnt = (hi - lo + 1).astype(F32)
        pooled = jnp.where(grp == gi, sums[gi] / cnt, pooled)
    pooled = pooled - p
    return jnp.dot(pooled, w_ref[...], preferred_element_type=F32) * sc_ref[...]


def _outproj_kernel(ya_ref, yf_ref, yb_ref, bonus_ref, g_ref, p_ref, pp_ref, pn_ref, x_ref, mod_ref,
                    wa_ref, wb_ref, wc_ref, pw_ref, psc_ref,
                    gng_ref, gnb_ref, ones_ref, l1g_ref, l1b_ref, wr_ref, br_ref,
                    x1_o, u2_o, ri_o, rw_o, cnt_o, cnt_ref, ext_ref, *, alpha, tiles_per_batch, seq):
    yc = _pool_tile(p_ref, pp_ref, pn_ref, pw_ref, psc_ref, ext_ref,
                    pl.program_id(0) % tiles_per_batch, tiles_per_batch, seq)
    m = mod_ref[0]
    ones = ones_ref[...]

    def headmean(x):
        return _dot_split(x, ones) * (1.0 / HEAD_DIM)

    ysum = yf_ref[...] + yb_ref[...]
    yc0 = ysum - headmean(ysum)
    yn = yc0 * lax.rsqrt(headmean(yc0 * yc0) + GN_EPS) * gng_ref[...] + gnb_ref[...]
    yb = (yn + bonus_ref[...]) * g_ref[...]
    mix = (jnp.dot(ya_ref[...].astype(BF16), wa_ref[...], preferred_element_type=F32)
           + jnp.dot(yb.astype(BF16), wb_ref[...], preferred_element_type=F32)
           + jnp.dot(yc.astype(BF16), wc_ref[...], preferred_element_type=F32))
    x1 = _ln(alpha * x_ref[...] + m[2:3] * mix) * l1g_ref[...] + l1b_ref[...]
    x1_o[...] = x1
    u2 = _ln(x1) * (1.0 + m[4:5]) + m[3:4]
    u2_o[...] = _pack_bf16_pairs(u2)

    u_hi, u_lo = _split_bf16(u2)
    hi_both = jnp.dot(u_hi, wr_ref[...], preferred_element_type=F32)
    lg = (hi_both[:, :LANES] + hi_both[:, LANES:]
          + jnp.dot(u_lo, wr_ref[:, :LANES], preferred_element_type=F32)) + br_ref[...]
    lane = lax.broadcasted_iota(jnp.int32, lg.shape, 1)
    big = jnp.int32(1 << 20)
    gl = jnp.where(lane < N_GROUPS, lg, -jnp.inf)
    gmax = jnp.max(gl, axis=-1, keepdims=True)
    gidx = jnp.min(jnp.where(gl == gmax, lane, big), axis=-1, keepdims=True)
    pg_sel = 1.0 / jnp.sum(jnp.exp(gl - gmax), axis=-1, keepdims=True)
    e_lo = N_GROUPS + gidx * EXPERTS_PER_GROUP
    el = jnp.where((lane >= e_lo) & (lane < e_lo + EXPERTS_PER_GROUP), lg, -jnp.inf)
    m1 = jnp.max(el, axis=-1, keepdims=True)
    i1 = jnp.min(jnp.where(el == m1, lane, big), axis=-1, keepdims=True)
    el2 = jnp.where(lane == i1, -jnp.inf, el)
    m2 = jnp.max(el2, axis=-1, keepdims=True)
    i2 = jnp.min(jnp.where(el2 == m2, lane, big), axis=-1, keepdims=True)
    e21 = jnp.exp(m2 - m1)
    p1 = 1.0 / (1.0 + e21)
    p2 = e21 / (1.0 + e21)
    rw_o[...] = jnp.where(lane == 0, pg_sel * p1, jnp.where(lane == 1, pg_sel * p2, 0.0))

    @pl.when(pl.program_id(0) == 0)
    def _():
        cnt_ref[...] = jnp.zeros_like(cnt_ref)

    tm = lg.shape[0]
    earlier = (lax.broadcasted_iota(jnp.int32, (tm, tm), 1)
               < lax.broadcasted_iota(jnp.int32, (tm, tm), 0)).astype(BF16)
    oh1 = (lane == i1).astype(F32)
    oh2 = (lane == i2).astype(F32)
    run = cnt_ref[...]
    c1 = jnp.sum(oh1, axis=0, keepdims=True)
    before1 = run + jnp.dot(earlier, oh1.astype(BF16), preferred_element_type=F32)
    before2 = run + c1 + jnp.dot(earlier, oh2.astype(BF16), preferred_element_type=F32)
    rank1 = jnp.sum(oh1 * before1, axis=-1, keepdims=True).astype(jnp.int32)
    rank2 = jnp.sum(oh2 * before2, axis=-1, keepdims=True).astype(jnp.int32)
    total = run + c1 + jnp.sum(oh2, axis=0, keepdims=True)
    cnt_ref[...] = total
    cnt_o[...] = total
    ri = jnp.where(lane == 0, i1 - N_GROUPS, jnp.where(lane == 1, i2 - N_GROUPS,
                   jnp.where(lane == 2, rank1, jnp.where(lane == 3, rank2, 0))))
    ri_o[...] = jnp.transpose(ri.astype(F32))[:SUBLANES].astype(jnp.int32)


def _outproj(ya, yf, yb, bonus, g, praw, x2, modl, p, seq, tm, alpha):
    t, d = x2.shape
    tpb = seq // tm
    aw, bw, cw = ya.shape[1], bonus.shape[1], praw.shape[1]
    hb = tm // HALO
    nhb = t // HALO
    tok = lambda i: (i, 0)
    full2 = lambda i: (0, 0)
    kern = functools.partial(_outproj_kernel, alpha=alpha, tiles_per_batch=tpb, seq=seq)
    small = ["w_out_a", "w_out_b", "w_out_c", "pool_w", "pool_scale",
             "gn_gain", "gn_bias", "ones", "ln1_gain", "ln1_bias", "w_router", "b_router"]
    return pl.pallas_call(
        kern,
        grid=(t // tm,),
        in_specs=[pl.BlockSpec((tm, aw), tok),
                  pl.BlockSpec((tm, bw), tok), pl.BlockSpec((tm, bw), tok),
                  pl.BlockSpec((tm, bw), tok), pl.BlockSpec((tm, bw), tok),
                  pl.BlockSpec((tm, cw), tok),
                  pl.BlockSpec((HALO, cw), lambda i: (jnp.maximum(i * hb - 1, 0), 0)),
                  pl.BlockSpec((HALO, cw), lambda i: (jnp.minimum((i + 1) * hb, nhb - 1), 0)),
                  pl.BlockSpec((tm, d), tok),
                  pl.BlockSpec((1,) + modl.shape[1:], lambda i: (i // tpb, 0, 0))]
                 + [pl.BlockSpec(p[k].shape, functools.partial(lambda nd, i: (0,) * nd, p[k].ndim)) for k in small],
        out_specs=[pl.BlockSpec((tm, d), tok), pl.BlockSpec((tm, d // 2), tok),
                   pl.BlockSpec((SUBLANES, tm), lambda i: (0, i)), pl.BlockSpec((tm, LANES), tok),
                   pl.BlockSpec((1, LANES), full2)],
        out_shape=[jax.ShapeDtypeStruct((t, d), F32), jax.ShapeDtypeStruct((t, d // 2), jnp.uint32),
                   jax.ShapeDtypeStruct((SUBLANES, t), jnp.int32), jax.ShapeDtypeStruct((t, LANES), F32),
                   jax.ShapeDtypeStruct((1, LANES), F32)],
        scratch_shapes=[pltpu.VMEM((1, LANES), F32), pltpu.VMEM((len(POOL_WINDOWS), tm + 4 * HALO, cw), F32)],
        compiler_params=_cparams(1, "arbitrary"),
        name="outproj",
    )(ya, yf, yb, bonus, g, praw, praw, praw, x2, modl, *[p[k] for k in small])


def _dispatch(route_t, counts_lanes, n_blocks, tm):
    counts = counts_lanes[0, N_GROUPS:N_GROUPS + N_EXPERTS].astype(jnp.int32)
    padded = ((counts + EXPERT_BLOCK - 1) // EXPERT_BLOCK) * EXPERT_BLOCK
    pends = jnp.cumsum(padded)
    pstarts = pends - padded
    e = route_t[:TOP_K]
    onehot = (e[..., None] == jnp.arange(N_EXPERTS, dtype=jnp.int32)).astype(F32)
    start_of = jnp.einsum("ktx,x->kt", onehot, pstarts.astype(F32), precision=HI)
    dest = route_t[TOP_K:2 * TOP_K] + start_of.astype(jnp.int32)
    t = route_t.shape[1]
    dest = dest.reshape(TOP_K, t // tm, tm).transpose(1, 0, 2).reshape(t // tm, 1, TOP_K * tm)
    block_start = jnp.arange(n_blocks, dtype=jnp.int32) * EXPERT_BLOCK
    block_e = jnp.minimum(jnp.sum((pends[None, :] <= block_start[:, None]).astype(jnp.int32), axis=1), N_EXPERTS - 1)
    meta = jnp.concatenate([block_e, (pends[-1] // EXPERT_BLOCK)[None]]).astype(jnp.int32)
    return dest, meta, pends.astype(jnp.int32)


def _scatter_rows_kernel(pends_ref, dest_ref, u_ref, xs_ref, zeros_ref, sem, zsem):
    tm = u_ref.shape[0] * SUBLANES

    @pl.when(pl.program_id(0) == 0)
    def _():
        zeros_ref[...] = jnp.zeros_like(zeros_ref)

        def tail_copy(e):
            tail = pl.ds(pl.multiple_of(pends_ref[e] - EXPERT_BLOCK, EXPERT_BLOCK), EXPERT_BLOCK)
            return pltpu.make_async_copy(zeros_ref, xs_ref.at[tail], zsem)

        def has_rows(e):
            return pends_ref[e] > (pends_ref[e - 1] if e > 0 else 0)

        def unused_copy(j):
            return pltpu.make_async_copy(zeros_ref, xs_ref.at[pl.ds(j * EXPERT_BLOCK, EXPERT_BLOCK)], zsem)

        def is_unused(j):
            return j * EXPERT_BLOCK >= pends_ref[N_EXPERTS - 1]

        n_blocks = xs_ref.shape[0] // EXPERT_BLOCK
        for e in range(N_EXPERTS):
            pl.when(has_rows(e))(lambda e=e: tail_copy(e).start())
        for j in range(n_blocks):
            pl.when(is_unused(j))(lambda j=j: unused_copy(j).start())
        for e in range(N_EXPERTS):
            pl.when(has_rows(e))(lambda e=e: tail_copy(e).wait())
        for j in range(n_blocks):
            pl.when(is_unused(j))(lambda j=j: unused_copy(j).wait())

    def issue(grp, carry):
        for j in range(SUBLANES):
            for k in range(TOP_K):
                dst = dest_ref[0, 0, k * tm + SUBLANES * grp + j]
                pltpu.make_async_copy(u_ref.at[grp, pl.ds(j, 1)], xs_ref.at[pl.ds(dst, 1)], sem).start()
        return carry

    lax.fori_loop(0, tm // SUBLANES, issue, 0)
    rows = pl.ds(0, TOP_K * tm)
    pltpu.make_async_copy(xs_ref.at[rows], xs_ref.at[rows], sem).wait()


def _scatter_rows(pends, u2, dest3, total, tm):
    t, d = u2.shape
    grid_spec = pltpu.PrefetchScalarGridSpec(
        num_scalar_prefetch=1,
        grid=(t // tm,),
        in_specs=[pl.BlockSpec((1, 1, TOP_K * tm), lambda i, p: (i, 0, 0), memory_space=pltpu.SMEM),
                  pl.BlockSpec((tm // SUBLANES, SUBLANES, d), lambda i, p: (i, 0, 0))],
        out_specs=pl.BlockSpec(memory_space=pl.ANY),
        scratch_shapes=[pltpu.VMEM((EXPERT_BLOCK, d), u2.dtype), pltpu.SemaphoreType.DMA(()),
                        pltpu.SemaphoreType.DMA(())],
    )
    return pl.pallas_call(
        _scatter_rows_kernel,
        grid_spec=grid_spec,
        out_shape=jax.ShapeDtypeStruct((total, d), u2.dtype),
        compiler_params=_cparams(1, "arbitrary"),
        name="scatter_rows",
    )(pends, dest3, u2.reshape(t // SUBLANES, SUBLANES, d))


def _experts_kernel(meta_ref, xs_ref, wg_ref, wu_ref, wd_ref, o_ref, wg_b, wu_b, wd_b):
    i = pl.program_id(0)
    n_used = meta_ref[pl.num_programs(0)]

    @pl.when((i == 0) | (meta_ref[i] != meta_ref[jnp.maximum(i - 1, 0)]))
    def _():
        wg_b[...] = wg_ref[0, 0].astype(BF16)
        wu_b[...] = wu_ref[0, 0].astype(BF16)
        wd_b[...] = wd_ref[0, 0].astype(BF16)

    @pl.when(i < n_used)
    def _():
        xb = _unpack_bf16_pairs(xs_ref[...])
        gate = jnp.dot(xb, wg_b[...], preferred_element_type=F32)
        up = jnp.dot(xb, wu_b[...], preferred_element_type=F32)
        hb = gate * _sigmoid(gate) * up
        o_ref[...] = _pack_bf16_pairs(jnp.dot(hb.astype(BF16), wd_b[...], preferred_element_type=F32))

    @pl.when(i >= n_used)
    def _():
        o_ref[...] = jnp.zeros_like(o_ref)


def _experts(meta, xs, wg, wu, wd, layer):
    total, dp = xs.shape
    nb = total // EXPERT_BLOCK
    d, de = wg.shape[2:]
    grid_spec = pltpu.PrefetchScalarGridSpec(
        num_scalar_prefetch=1,
        grid=(nb,),
        in_specs=[pl.BlockSpec((EXPERT_BLOCK, dp), lambda i, m: (jnp.minimum(i, m[nb] - 1), 0)),
                  pl.BlockSpec((1, 1, d, de), lambda i, m: (layer, m[i], 0, 0)),
                  pl.BlockSpec((1, 1, d, de), lambda i, m: (layer, m[i], 0, 0)),
                  pl.BlockSpec((1, 1, de, d), lambda i, m: (layer, m[i], 0, 0))],
        out_specs=pl.BlockSpec((EXPERT_BLOCK, dp), lambda i, m: (i, 0)),
        scratch_shapes=[pltpu.VMEM((d, de), BF16), pltpu.VMEM((d, de), BF16), pltpu.VMEM((de, d), BF16)],
    )
    return pl.pallas_call(
        _experts_kernel,
        grid_spec=grid_spec,
        out_shape=jax.ShapeDtypeStruct((total, dp), xs.dtype),
        compiler_params=_cparams(1, "arbitrary"),
        name="experts",
    )(meta, xs, wg, wu, wd)


def _final_kernel(dcur_ref, dnext_ref, x1_ref, rw_ref, mod_ref, g_ref, b_ref, ys_ref, o_ref, ybuf, sem, *, alpha):
    i = pl.program_id(0)
    tm = x1_ref.shape[0]
    slot = i % 2

    def gather(d_ref, s):
        def issue(grp, carry):
            for j in range(SUBLANES):
                for k in range(TOP_K):
                    src = d_ref[0, 0, k * tm + SUBLANES * grp + j]
                    pltpu.make_async_copy(ys_ref.at[pl.ds(src, 1)], ybuf.at[s, k, grp, pl.ds(j, 1)],
                                          sem.at[s]).start()
            return carry

        lax.fori_loop(0, tm // SUBLANES, issue, 0)

    @pl.when(i == 0)
    def _():
        gather(dcur_ref, 0)

    @pl.when(i + 1 < pl.num_programs(0))
    def _():
        gather(dnext_ref, 1 - slot)

    pltpu.make_async_copy(ybuf.at[slot], ybuf.at[slot], sem.at[slot]).wait()
    m = mod_ref[0]
    rw = rw_ref[...]
    dp = ybuf.shape[-1]
    y1 = _unpack_bf16_pairs(ybuf[slot, 0].reshape(tm, dp)).astype(F32)
    y2 = _unpack_bf16_pairs(ybuf[slot, 1].reshape(tm, dp)).astype(F32)
    f = rw[:, 0:1] * y1 + rw[:, 1:2] * y2
    o_ref[...] = _ln(alpha * x1_ref[...] + m[5:6] * f) * g_ref[...] + b_ref[...]


def _final(x1, ysorted, dest3, rw, modl, gain, bias, seq, tm, alpha):
    t, d = x1.shape
    tpb = seq // tm
    n_tiles = t // tm
    tok = lambda i: (i, 0)
    kern = functools.partial(_final_kernel, alpha=alpha)
    dspec = lambda f: pl.BlockSpec((1, 1, TOP_K * tm), f, memory_space=pltpu.SMEM)
    return pl.pallas_call(
        kern,
        grid=(n_tiles,),
        in_specs=[dspec(lambda i: (i, 0, 0)), dspec(lambda i: (jnp.minimum(i + 1, n_tiles - 1), 0, 0)),
                  pl.BlockSpec((tm, d), tok), pl.BlockSpec((tm, LANES), tok),
                  pl.BlockSpec((1,) + modl.shape[1:], lambda i: (i // tpb, 0, 0)),
                  pl.BlockSpec(gain.shape, lambda i: (0, 0)), pl.BlockSpec(bias.shape, lambda i: (0, 0)),
                  pl.BlockSpec(memory_space=pl.ANY)],
        out_specs=pl.BlockSpec((tm, d), tok),
        out_shape=jax.ShapeDtypeStruct((t, d), F32),
        scratch_shapes=[pltpu.VMEM((2, TOP_K, tm // SUBLANES, SUBLANES, ysorted.shape[1]), ysorted.dtype),
                        pltpu.SemaphoreType.DMA((2,))],
        compiler_params=_cparams(1, "arbitrary"),
        name="final_ln",
    )(dest3, dest3, x1, rw, modl, gain, bias, ysorted)


def _block_diag(blocks):
    n, a, b = blocks.shape
    out = jnp.zeros((n * a, n * b), blocks.dtype)
    for i in range(n):
        out = out.at[i * a:(i + 1) * a, i * b:(i + 1) * b].set(blocks[i])
    return out


def _pad_rows(w, lo, total):
    return jnp.zeros((total, w.shape[-1]), w.dtype).at[lo:lo + w.shape[0]].set(w)


def kernel(x, c, w_mod, b_mod, w_in, na_rpb, rw_conv, rw_w0, rw_w_up, rw_a0, rw_a_up, rw_g_up, rw_k_k, rw_k_a, rw_r_k, rw_gn_gain, rw_gn_bias, pool_w, pool_scale, w_out, ln1_gain, ln1_bias, ln2_gain, ln2_bias, moe_w_group, moe_b_group, moe_w_expert, moe_b_expert, moe_w_gate, moe_w_up, moe_w_down):
    batch, seq, d = x.shape
    depth = w_mod.shape[0]
    t = batch * seq
    a_w = na_rpb.shape[1] * HEAD_DIM
    b_w = rw_w0.shape[-1]
    c_w = pool_scale.shape[-1]
    lr_w = R_W + R_A + R_G
    alpha = (2 * depth) ** 0.25
    tm = min(512, seq)
    tm_in = min(512, seq)
    assert seq % tm == 0 and seq % SCAN_CHUNK == 0 and seq % GRID_W == 0 and lr_w == LANES

    mod = _modulation(c, w_mod, b_mod)
    ones_blk = _block_diag(jnp.ones((b_w // HEAD_DIM, HEAD_DIM, HEAD_DIM), BF16))
    row = lambda v: v.reshape(1, -1)

    x2 = x.reshape(t, d)
    for l in range(depth):
        modl = mod[l]
        prep_params = {
            "conv": rw_conv[l], "w0": rw_w0[l], "a0": rw_a0[l],
            "w_up": jnp.stack([_pad_rows(rw_w_up[l, dd], 0, lr_w) for dd in range(2)]).astype(BF16),
            "a_up": jnp.stack([_pad_rows(rw_a_up[l, dd], R_W, lr_w) for dd in range(2)]).astype(BF16),
            "g_up": _pad_rows(rw_g_up[l], R_W + R_A, lr_w).astype(BF16),
            "k_k": row(rw_k_k[l]), "k_a": row(rw_k_a[l]), "r_k": row(rw_r_k[l]), "ones": ones_blk,
        }
        qkv, praw, r, v, nkk, lw, bb, kd, bonus, g = _inproj(x2, modl, w_in[l].astype(BF16), prep_params, seq, tm_in,
                                                             3 * a_w, 3 * b_w, lr_w, c_w)
        ya = _natten(qkv, _na_bias_table(na_rpb[l]), batch, seq, a_w)
        yf, yb = _rwkv_scan(r, v, nkk, lw, bb, kd, batch, seq)
        w_router = jnp.zeros((d, LANES), F32).at[:, :N_GROUPS].set(moe_w_group[l])
        w_router = w_router.at[:, N_GROUPS:N_GROUPS + N_EXPERTS].set(moe_w_expert[l])
        b_router = jnp.zeros((1, LANES), F32).at[0, :N_GROUPS].set(moe_b_group[l])
        b_router = b_router.at[0, N_GROUPS:N_GROUPS + N_EXPERTS].set(moe_b_expert[l])
        wo = w_out[l].astype(BF16)
        out_params = {
            "w_out_a": wo[:a_w], "w_out_b": wo[a_w:a_w + b_w], "w_out_c": wo[a_w + b_w:],
            "pool_w": _block_diag(pool_w[l]), "pool_scale": row(pool_scale[l]),
            "gn_gain": row(rw_gn_gain[l]), "gn_bias": row(rw_gn_bias[l]), "ones": ones_blk,
            "ln1_gain": row(ln1_gain[l]), "ln1_bias": row(ln1_bias[l]),
            "w_router": jnp.concatenate(_split_bf16(w_router), axis=1), "b_router": b_router,
        }
        x1, u2, route_i, route_w, counts = _outproj(ya, yf, yb, bonus, g, praw, x2, modl, out_params, seq, tm, alpha)
        n_blocks = -(-(t * TOP_K) // EXPERT_BLOCK) + N_EXPERTS
        dest3, meta, pends = _dispatch(route_i, counts, n_blocks, tm)
        xs = _scatter_rows(pends, u2, dest3, n_blocks * EXPERT_BLOCK, tm)
        ysorted = _experts(meta, xs, moe_w_gate, moe_w_up, moe_w_down, l)
        x2 = _final(x1, ysorted, dest3, route_w, modl, row(ln2_gain[l]), row(ln2_bias[l]), seq, tm, alpha)
    return x2.reshape(batch, seq, d)
```

```python
import functools
import math

import jax
import jax.numpy as jnp
import numpy as np
from jax import lax
from jax.experimental import pallas as pl
from jax.experimental.pallas import tpu as pltpu

F32 = jnp.float32
BF16 = jnp.bfloat16
HI = lax.Precision.HIGHEST

GRID_W = 64
HEAD_DIM = 64
NA_KH = 8
NA_KW = 16
POOL_WINDOWS = (2, 4, 8, 16)
R_W = 32
R_A = 32
R_G = 64
DECAY_SCALE = math.exp(-0.5)
GN_EPS = 64e-5
N_GROUPS = 4
EXPERTS_PER_GROUP = 8
N_EXPERTS = N_GROUPS * EXPERTS_PER_GROUP
TOP_K = 2
ROUTER_ROWS = -(-(N_GROUPS + N_EXPERTS) // 8) * 8
EXPERT_BLOCK = 512
LN_EPS = 1e-5
NEG_INF = -1e30

NA_ROWS_PER_STEP = 8
SCAN_CHUNK = 64
SCAN_CHUNKS_PER_STEP = 4
SUBLANES = 8
HALO = 8
LANES = 128
VMEM_LIMIT = 52 * 1024 * 1024


def _ln(x):
    mu = jnp.mean(x, axis=-1, keepdims=True)
    xc = x - mu
    var = jnp.mean(xc * xc, axis=-1, keepdims=True)
    return xc * lax.rsqrt(var + LN_EPS)


def _sigmoid(x):
    return 1.0 / (1.0 + jnp.exp(-x))


def _split_bf16(x):
    hi = x.astype(BF16)
    return hi, (x - hi.astype(F32)).astype(BF16)


def _dot_split(x, w_exact):
    hi, lo = _split_bf16(x)
    return jnp.dot(hi, w_exact, preferred_element_type=F32) + jnp.dot(lo, w_exact, preferred_element_type=F32)


def _pack_bf16_pairs(x):
    h = x.shape[1] // 2
    lo = lax.bitcast_convert_type(x[:, :h].astype(BF16).astype(F32), jnp.uint32)
    hi = lax.bitcast_convert_type(x[:, h:].astype(BF16).astype(F32), jnp.uint32)
    return (lo >> 16) | hi


def _unpack_bf16_pairs(w):
    lo = lax.bitcast_convert_type(w << 16, F32).astype(BF16)
    hi = lax.bitcast_convert_type(w & jnp.uint32(0xFFFF0000), F32).astype(BF16)
    return jnp.concatenate([lo, hi], axis=1)


def _cparams(n_axes, semantics="parallel"):
    return pltpu.CompilerParams(dimension_semantics=(semantics,) * n_axes, vmem_limit_bytes=VMEM_LIMIT)


def _mod_kernel(c_ref, w_ref, b_ref, o_ref):
    c = c_ref[...]
    s = c * _sigmoid(c)
    o_ref[0] = jnp.dot(s, w_ref[0], precision=HI, preferred_element_type=F32) + b_ref[0]


def _modulation(c, w_mod, b_mod):
    n_layers, d, d6 = w_mod.shape
    b = c.shape[0]
    bp = -(-b // 8) * 8
    cp = jnp.zeros((bp, d), F32).at[:b].set(c)
    out = pl.pallas_call(
        _mod_kernel,
        grid=(n_layers, d6 // d),
        in_specs=[pl.BlockSpec((bp, d), lambda l, j: (0, 0)),
                  pl.BlockSpec((1, d, d), lambda l, j: (l, 0, j)),
                  pl.BlockSpec((1, 1, d), lambda l, j: (l, 0, j))],
        out_specs=pl.BlockSpec((1, bp, d), lambda l, j: (l, 0, j)),
        out_shape=jax.ShapeDtypeStruct((n_layers, bp, d6), F32),
        compiler_params=_cparams(2),
        name="modulation",
    )(cp, w_mod, b_mod.reshape(n_layers, 1, d6))
    return out[:, :b].reshape(n_layers, b, d6 // d, d)


def _inproj_kernel(x_ref, xp_ref, xn_ref, mod_ref, wr_ref, wo_ref, cw_ref, w0_ref, wup_ref, a0_ref, aup_ref, gup_ref,
                   kk_ref, ka_ref, rk_ref, ones_ref,
                   qkv_o, pool_o, r_o, v_o, nkk_o, lw_o, b_o, kd_o, bonus_o, g_o,
                   *, a3, b3, tiles_per_batch):
    m = mod_ref[0]
    tm = x_ref.shape[0]
    tb = pl.program_id(0) % tiles_per_batch
    xe = jnp.concatenate([xp_ref[...], x_ref[...], xn_ref[...]], axis=0)
    u = (_ln(xe) * (1.0 + m[1:2]) + m[0:1]).astype(BF16)
    h = jnp.dot(u, wr_ref[...], preferred_element_type=F32)
    ho = jnp.dot(u[HALO:HALO + tm], wo_ref[...], preferred_element_type=F32)
    hm = h[HALO:HALO + tm]
    prev = jnp.where(tb == 0, 0.0, h[HALO - 1:HALO, :b3])
    nxt = jnp.where(tb == tiles_per_batch - 1, 0.0, h[HALO + tm:HALO + tm + 1, :b3])
    _rwkv_prep_tile(hm[:, :b3], prev, nxt, hm[:, b3:],
                    cw_ref, w0_ref, wup_ref, a0_ref, aup_ref, gup_ref, kk_ref, ka_ref, rk_ref, ones_ref,
                    r_o, v_o, nkk_o, lw_o, b_o, kd_o, bonus_o, g_o)
    qkv_o[...] = ho[:, :a3].astype(BF16)
    pool_o[...] = ho[:, a3:]


def _inproj(x2, modl, w_in_bf, p, seq, tm, a3, b3, lr_w, c_w):
    t, d = x2.shape
    tpb = seq // tm
    hb = tm // HALO
    nhb = t // HALO
    width = b3 // 3
    kern = functools.partial(_inproj_kernel, a3=a3, b3=b3, tiles_per_batch=tpb)
    w_rwkv = w_in_bf[:, a3:a3 + b3 + lr_w]
    w_other = jnp.concatenate([w_in_bf[:, :a3], w_in_bf[:, a3 + b3 + lr_w:]], axis=1)
    tok = lambda i: (i, 0)
    dtok = lambda i: (0, i, 0)
    names = ["conv", "w0", "w_up", "a0", "a_up", "g_up", "k_k", "k_a", "r_k", "ones"]
    tw = jax.ShapeDtypeStruct((t, width), F32)
    dtw = jax.ShapeDtypeStruct((2, t, width), F32)
    return pl.pallas_call(
        kern,
        grid=(t // tm,),
        in_specs=[pl.BlockSpec((tm, d), tok),
                  pl.BlockSpec((HALO, d), lambda i: (jnp.maximum(i * hb - 1, 0), 0)),
                  pl.BlockSpec((HALO, d), lambda i: (jnp.minimum((i + 1) * hb, nhb - 1), 0)),
                  pl.BlockSpec((1,) + modl.shape[1:], lambda i: (i // tpb, 0, 0)),
                  pl.BlockSpec(w_rwkv.shape, lambda i: (0, 0)),
                  pl.BlockSpec(w_other.shape, lambda i: (0, 0))]
                 + [pl.BlockSpec(p[k].shape, functools.partial(lambda nd, i: (0,) * nd, p[k].ndim)) for k in names],
        out_specs=[pl.BlockSpec((tm, a3), tok), pl.BlockSpec((tm, c_w), tok),
                   pl.BlockSpec((tm, width), tok), pl.BlockSpec((tm, width), tok), pl.BlockSpec((tm, width), tok),
                   pl.BlockSpec((2, tm, width), dtok), pl.BlockSpec((2, tm, width), dtok),
                   pl.BlockSpec((2, tm, width), dtok),
                   pl.BlockSpec((tm, width), tok), pl.BlockSpec((tm, width), tok)],
        out_shape=[jax.ShapeDtypeStruct((t, a3), BF16), jax.ShapeDtypeStruct((t, c_w), F32),
                   tw, tw, tw, dtw, dtw, dtw, tw, tw],
        compiler_params=_cparams(1),
        name="inproj",
    )(x2, x2, x2, modl, w_rwkv, w_other, *[p[k] for k in names])


def _na_bias_table(rpb):
    col = np.arange(GRID_W)
    cstart = np.clip(col - NA_KW // 2, 0, GRID_W - NA_KW)
    in_win = (col[None, :] >= cstart[:, None]) & (col[None, :] < cstart[:, None] + NA_KW)
    dc = np.clip(col[None, :] - col[:, None], -(NA_KW - 1), NA_KW - 1) + (NA_KW - 1)
    pick = (dc[None] == np.arange(2 * NA_KW - 1)[:, None, None]).astype(np.float32)
    cols = jnp.einsum("hrc,cqk->hrqk", rpb.astype(F32), pick, precision=HI)
    cols = jnp.where(in_win, cols, NEG_INF)
    b = jnp.stack([cols[:, NA_KH - 1 - o:2 * NA_KH - 1 - o] for o in range(NA_KH)])
    h = rpb.shape[0]
    return jnp.transpose(b, (0, 1, 3, 2, 4)).reshape(NA_KH, h * GRID_W, NA_KH * GRID_W)


def _natten_kernel(q_ref, k_ref, v_ref, bias_ref, o_ref, *, rows, heads):
    width = q_ref.shape[1]
    nk = NA_KH * GRID_W
    head_of_lane = lax.broadcasted_iota(jnp.int32, (heads * GRID_W, width), 1) // HEAD_DIM
    head_of_row = lax.broadcasted_iota(jnp.int32, (heads * GRID_W, width), 0) // GRID_W
    own = head_of_lane == head_of_row
    for j in range(NA_ROWS_PER_STEP):
        r = pl.program_id(1) * NA_ROWS_PER_STEP + j
        rstart = jnp.clip(r - NA_KH // 2, 0, rows - NA_KH)
        off = r - rstart
        start = pl.multiple_of(rstart * GRID_W, GRID_W)
        kw = k_ref[pl.ds(start, nk), :]
        vw = v_ref[pl.ds(start, nk), :]
        q = q_ref[j * GRID_W:(j + 1) * GRID_W, :]
        qs = jnp.where(own, jnp.concatenate([q] * heads, axis=0), jnp.zeros((), q.dtype))
        s = lax.dot_general(qs, kw, (((1,), (1,)), ((), ())), preferred_element_type=F32) * (HEAD_DIM ** -0.5)
        s = s + bias_ref[off]
        mx = jnp.max(s, axis=-1, keepdims=True)
        p = jnp.exp(s - mx)
        den = jnp.sum(p, axis=-1, keepdims=True)
        o = jnp.where(own, jnp.dot(p.astype(BF16), vw, preferred_element_type=F32) / den, 0.0)
        acc = o[0:GRID_W]
        for h in range(1, heads):
            acc = acc + o[h * GRID_W:(h + 1) * GRID_W]
        o_ref[j * GRID_W:(j + 1) * GRID_W, :] = acc.astype(o_ref.dtype)


def _natten(qkv, bias_tab, batch, seq, width):
    rows = seq // GRID_W
    assert rows >= NA_KH
    heads = width // HEAD_DIM
    steps = rows // NA_ROWS_PER_STEP
    assert steps * NA_ROWS_PER_STEP == rows
    tq = NA_ROWS_PER_STEP * GRID_W
    kern = functools.partial(_natten_kernel, rows=rows, heads=heads)
    return pl.pallas_call(
        kern,
        grid=(batch, steps),
        in_specs=[pl.BlockSpec((tq, width), lambda b, r: (b * steps + r, 0)),
                  pl.BlockSpec((seq, width), lambda b, r: (b, 1)),
                  pl.BlockSpec((seq, width), lambda b, r: (b, 2)),
                  pl.BlockSpec(bias_tab.shape, lambda b, r: (0, 0, 0))],
        out_specs=pl.BlockSpec((tq, width), lambda b, r: (b * steps + r, 0)),
        out_shape=jax.ShapeDtypeStruct((batch * seq, width), BF16),
        compiler_params=_cparams(2),
        name="natten",
    )(qkv, qkv, qkv, bias_tab)


def _rwkv_prep_tile(z, prev, nxt, lr, cw_ref, w0_ref, wup_ref, a0_ref, aup_ref, gup_ref, kk_ref, ka_ref, rk_ref, ones_ref,
                    r_o, v_o, nkk_o, lw_o, b_o, kd_o, bonus_o, g_o):
    tm = z.shape[0]
    width = z.shape[1] // 3
    row = lax.broadcasted_iota(jnp.int32, z.shape, 0)
    zm1 = jnp.where(row == 0, prev, pltpu.roll(z, 1, 0))
    zp1 = jnp.where(row == tm - 1, nxt, pltpu.roll(z, tm - 1, 0))
    rkv = zm1 * cw_ref[0:1, :] + z * cw_ref[1:2, :] + zp1 * cw_ref[2:3, :]
    r = rkv[:, :width]
    k = rkv[:, width:2 * width]
    v = rkv[:, 2 * width:]
    th = jnp.tanh(lr)
    sg = _sigmoid(lr)
    ones = ones_ref[...]

    def headsum(x):
        return _dot_split(x, ones)

    kk = k * kk_ref[...]
    kk = kk * lax.rsqrt(jnp.maximum(headsum(kk * kk), 1e-24))
    g_o[...] = jnp.dot(sg.astype(BF16), gup_ref[...], preferred_element_type=F32)
    r_o[...] = r
    v_o[...] = v
    nkk_o[...] = -kk
    kd_sum = jnp.zeros_like(r)
    th_b = th.astype(BF16)
    lr_b = lr.astype(BF16)
    for d in range(2):
        wl = jnp.dot(th_b, wup_ref[d], preferred_element_type=F32) + w0_ref[d:d + 1, :]
        lw_o[d] = -DECAY_SCALE * _sigmoid(wl)
        a = _sigmoid(jnp.dot(lr_b, aup_ref[d], preferred_element_type=F32) + a0_ref[d:d + 1, :])
        kd = k * (1.0 + (a - 1.0) * ka_ref[...])
        kd_o[d] = kd
        b_o[d] = kk * a
        kd_sum = kd_sum + kd
    bonus_o[...] = headsum(r * kd_sum * rk_ref[...]) * v


def _dot_nt(a, b):
    return lax.dot_general(a, b, (((1,), (1,)), ((), ())), preferred_element_type=F32)


def _dot_tn(a, b):
    return lax.dot_general(a, b, (((0,), (0,)), ((), ())), preferred_element_type=F32)


def _mm(a, b):
    return jnp.dot(a.astype(BF16), b.astype(BF16), preferred_element_type=F32)


def _rwkv_scan_kernel(rf_ref, vf_ref, nf_ref, rb_ref, vb_ref, nb_ref, lwf_ref, bf_ref, kf_ref, lwb_ref, bb_ref, kb_ref,
                      yf_ref, yb_ref, s_ref, *, heads, batch):
    @pl.when(pl.program_id(0) == 0)
    def _():
        s_ref[...] = jnp.zeros_like(s_ref)

    n = SCAN_CHUNK
    pair_w = 2 * HEAD_DIM
    row = lax.broadcasted_iota(jnp.int32, (n, pair_w), 0)
    lane = lax.broadcasted_iota(jnp.int32, (n, pair_w), 1)
    col = lane & (HEAD_DIM - 1)
    even = lane < HEAD_DIM
    levels = n.bit_length()
    same = [(row >> k) == (col >> k) for k in range(levels)]
    eye = same[0].astype(F32)
    level_masks = [same[sh + 1] & jnp.logical_not(same[sh]) for sh in range(1, levels - 1)]

    def blockdiag(x2):
        xb = x2.astype(BF16)
        zero = jnp.zeros((), BF16)
        return jnp.concatenate([jnp.where(even, xb, zero), jnp.where(even, zero, xb)], axis=0)

    def mm(x2, y2):
        return jnp.dot(x2.astype(BF16), blockdiag(y2), preferred_element_type=F32)

    def mm_nt(x2, y2):
        return _dot_nt(x2.astype(BF16), blockdiag(y2))

    dirs = ((rf_ref, vf_ref, nf_ref, lwf_ref, bf_ref, kf_ref, yf_ref),
            (rb_ref, vb_ref, nb_ref, lwb_ref, bb_ref, kb_ref, yb_ref))
    def build(q):
        chains = []
        for d, (r_ref, v_ref, n_ref, lw_ref, b_ref, k_ref, y_ref) in enumerate(dirs):
            sub = q if d == 0 else SCAN_CHUNKS_PER_STEP - 1 - q
            rs = slice(sub * n, (sub + 1) * n)
            order = row - col if d == 0 else col - row
            strict = order > 0
            incl = order >= 0
            incl_b = jnp.where(incl[:, :n], 1.0, 0.0).astype(BF16)
            for bi in range(batch):
                lw = lw_ref[0, bi, rs]
                lw_hi, lw_mid = _split_bf16(lw)
                lw_lo = (lw - lw_hi.astype(F32) - lw_mid.astype(F32)).astype(BF16)
                g_inc = ((jnp.dot(incl_b, lw_lo, preferred_element_type=F32)
                          + jnp.dot(incl_b, lw_mid, preferred_element_type=F32))
                         + jnp.dot(incl_b, lw_hi, preferred_element_type=F32))
                g_tot = jnp.sum(lw, axis=0, keepdims=True)
                e_neg = jnp.exp(-g_inc)
                e_end = jnp.exp(g_tot - g_inc)
                decay = jnp.exp(g_tot)
                a_t = n_ref[bi, rs] * jnp.exp(g_inc - lw)
                r_t = r_ref[bi, rs] * jnp.exp(g_inc)
                bb = b_ref[0, bi, rs]
                kd = k_ref[0, bi, rs]
                b_t = bb * e_neg
                k_t = kd * e_neg
                ar_t = jnp.concatenate([a_t, r_t], axis=0).astype(BF16)
                bk_h = jnp.concatenate([bb * e_end, kd * e_end], axis=0).astype(BF16)
                v = v_ref[bi, rs]
                for p in range(heads // 2):
                    sl = slice(p * pair_w, (p + 1) * pair_w)
                    chains.append(dict(strict=strict, incl=incl, sl=sl, rs=rs, bi=bi, y_ref=y_ref,
                                       si=(d * batch + bi) * (heads // 2) + p, decay=decay[:, sl],
                                       ar=ar_t[:, sl], b=b_t[:, sl], k=k_t[:, sl], bk_h=bk_h[:, sl], v=v[:, sl]))
        return chains

    def state_free_stages(chains):
        def products():
            for ch in chains:
                pb = mm_nt(ch["ar"], ch["b"])
                pk = mm_nt(ch["ar"], ch["k"])
                ch["l_ab"] = jnp.where(ch["strict"], pb[:n], 0.0)
                ch["m_rb"] = jnp.where(ch["incl"], pb[n:], 0.0)
                ch["l_ak"] = jnp.where(ch["strict"], pk[:n], 0.0)
                ch["m_rk"] = jnp.where(ch["incl"], pk[n:], 0.0)
                ch["t"] = eye + jnp.where(same[1], ch["l_ab"], 0.0)

        def level_left(mask):
            for ch in chains:
                ch["tc"] = mm(ch["t"], jnp.where(mask, ch["l_ab"], 0.0))

        def level_right():
            for ch in chains:
                ch["t"] = ch["t"] + mm(ch["tc"], ch["t"])

        def values():
            for ch in chains:
                ch["kv"] = mm(jnp.concatenate([ch["l_ak"], ch["m_rk"]], axis=0), ch["v"])

        stages = [products]
        for mask in level_masks:
            stages += [functools.partial(level_left, mask), level_right]
        return stages + [values]

    def state_stages(chains):
        def read():
            for ch in chains:
                ch["s0"] = s_ref[ch["si"]]
                ch["x"] = mm_nt(ch["ar"], ch["s0"])

        def solve():
            for ch in chains:
                ch["u"] = mm(ch["t"], ch["x"][:n] + ch["kv"][:n])

        def emit():
            for ch in chains:
                y = ch["x"][n:] + mm(ch["m_rb"], ch["u"]) + ch["kv"][n:]
                ch["y_ref"][ch["bi"], ch["rs"], ch["sl"]] = y

        def write():
            for ch in chains:
                uv = jnp.concatenate([ch["u"], ch["v"]], axis=0).astype(BF16)
                full = _dot_tn(uv, ch["bk_h"])
                s_ref[ch["si"]] = ch["s0"] * ch["decay"] + jnp.where(even, full[:HEAD_DIM], full[HEAD_DIM:])

        return [read, solve, emit, write]

    pending = []
    for q in range(SCAN_CHUNKS_PER_STEP):
        chains = build(q)
        free = state_free_stages(chains)
        if pending:
            share = -(-len(free) // len(pending))
            for i, carried in enumerate(pending):
                carried()
                for stage in free[i * share:(i + 1) * share]:
                    stage()
        else:
            for stage in free:
                stage()
        pending = state_stages(chains)
    for carried in pending:
        carried()


def _rwkv_scan(r, v, nkk, lw, b, kd, batch, seq):
    t, width = r.shape
    heads = width // HEAD_DIM
    n = SCAN_CHUNK * SCAN_CHUNKS_PER_STEP
    nc = seq // n
    assert nc * n == seq
    r3, v3, n3 = (z.reshape(batch, seq, width) for z in (r, v, nkk))
    lw4, b4, k4 = (z.reshape(2, batch, seq, width) for z in (lw, b, kd))
    fwd = pl.BlockSpec((batch, n, width), lambda c: (0, c, 0))
    bwd = pl.BlockSpec((batch, n, width), lambda c: (0, nc - 1 - c, 0))
    fwd_d = pl.BlockSpec((1, batch, n, width), lambda c: (0, 0, c, 0))
    bwd_d = pl.BlockSpec((1, batch, n, width), lambda c: (1, 0, nc - 1 - c, 0))
    kern = functools.partial(_rwkv_scan_kernel, heads=heads, batch=batch)
    yf, yb = pl.pallas_call(
        kern,
        grid=(nc,),
        in_specs=[fwd, fwd, fwd, bwd, bwd, bwd, fwd_d, fwd_d, fwd_d, bwd_d, bwd_d, bwd_d],
        out_specs=[fwd, bwd],
        out_shape=[jax.ShapeDtypeStruct((batch, seq, width), F32)] * 2,
        scratch_shapes=[pltpu.VMEM((batch * heads, HEAD_DIM, 2 * HEAD_DIM), F32)],
        compiler_params=_cparams(1, "arbitrary"),
        name="rwkv_scan",
    )(r3, v3, n3, r3, v3, n3, lw4, b4, k4, lw4, b4, k4)
    return yf.reshape(t, width), yb.reshape(t, width)


def _pool_tile(p_ref, pp_ref, pn_ref, w_ref, sc_ref, ext_ref, tb, tiles_per_batch, seq):
    p = p_ref[...]
    tm, width = p.shape
    assert all(w == 2 << i for i, w in enumerate(POOL_WINDOWS)) and POOL_WINDOWS[-1] <= 2 * HALO
    n = tm + 2 * HALO
    pad = jnp.zeros((HALO, width), F32)
    for k in range(len(POOL_WINDOWS)):
        ext_ref[k, 0:HALO, :] = pad
        ext_ref[k, HALO + n:2 * HALO + n, :] = pad
    ext_ref[0, HALO:2 * HALO, :] = jnp.where(tb == 0, 0.0, pp_ref[...])
    ext_ref[0, 2 * HALO:2 * HALO + tm, :] = p
    ext_ref[0, 2 * HALO + tm:HALO + n, :] = jnp.where(tb == tiles_per_batch - 1, 0.0, pn_ref[...])

    def rows(k, first, count):
        return ext_ref[k, 2 * HALO + first:2 * HALO + first + count, :]

    ext_ref[1, HALO:HALO + n, :] = rows(0, -HALO - 1, n) + rows(0, -HALO, n)
    for k in range(1, len(POOL_WINDOWS) - 1):
        q = POOL_WINDOWS[k - 1] // 2
        ext_ref[k + 1, HALO:HALO + n, :] = rows(k, -HALO - q, n) + rows(k, -HALO + q, n)
    q = POOL_WINDOWS[-2] // 2
    sums = [rows(k + 1, 0, tm) for k in range(len(POOL_WINDOWS) - 1)]
    sums.append(rows(len(POOL_WINDOWS) - 1, -q, tm) + rows(len(POOL_WINDOWS) - 1, q, tm))

    t = tb * tm + lax.broadcasted_iota(jnp.int32, (tm, width), 0)
    grp = lax.broadcasted_iota(jnp.int32, (tm, width), 1) // (width // len(POOL_WINDOWS))
    pooled = jnp.zeros_like(p)
    for gi, win in enumerate(POOL_WINDOWS):
        half = win // 2
        lo = jnp.clip(t - half, 0, seq - 1)
        hi = jnp.clip(t + half - 1, 0, seq - 1)
        cnt = (hi - lo + 1).astype(F32)
        pooled = jnp.where(grp == gi, sums[gi] / cnt, pooled)
    pooled = pooled - p
    return jnp.dot(pooled, w_ref[...], preferred_element_type=F32) * sc_ref[...]


def _outproj_kernel(ya_ref, yf_ref, yb_ref, bonus_ref, g_ref, p_ref, pp_ref, pn_ref, x_ref, mod_ref,
                    wa_ref, wb_ref, wc_ref, pw_ref, psc_ref,
                    gng_ref, gnb_ref, ones_ref, l1g_ref, l1b_ref, wr_ref, br_ref,
                    x1_o, u2_o, ri_o, rw_o, cnt_o, cnt_ref, ext_ref, *, alpha, tiles_per_batch, seq):
    yc = _pool_tile(p_ref, pp_ref, pn_ref, pw_ref, psc_ref, ext_ref,
                    pl.program_id(0) % tiles_per_batch, tiles_per_batch, seq)
    m = mod_ref[0]
    ones = ones_ref[...]

    def headmean(x):
        return _dot_split(x, ones) * (1.0 / HEAD_DIM)

    ysum = yf_ref[...] + yb_ref[...]
    yc0 = ysum - headmean(ysum)
    yn = yc0 * lax.rsqrt(headmean(yc0 * yc0) + GN_EPS) * gng_ref[...] + gnb_ref[...]
    yb = (yn + bonus_ref[...]) * g_ref[...]
    mix = (jnp.dot(ya_ref[...].astype(BF16), wa_ref[...], preferred_element_type=F32)
           + jnp.dot(yb.astype(BF16), wb_ref[...], preferred_element_type=F32)
           + jnp.dot(yc.astype(BF16), wc_ref[...], preferred_element_type=F32))
    x1 = _ln(alpha * x_ref[...] + m[2:3] * mix) * l1g_ref[...] + l1b_ref[...]
    x1_o[...] = x1
    u2 = _ln(x1) * (1.0 + m[4:5]) + m[3:4]
    u2_o[...] = _pack_bf16_pairs(u2)

    u_hi, u_lo = _split_bf16(u2)
    hi_both = jnp.dot(u_hi, wr_ref[...], preferred_element_type=F32)
    lg = (hi_both[:, :LANES] + hi_both[:, LANES:]
          + jnp.dot(u_lo, wr_ref[:, :LANES], preferred_element_type=F32)) + br_ref[...]
    tm = lg.shape[0]
    lt = jnp.transpose(lg)[:ROUTER_ROWS]
    rowi = lax.broadcasted_iota(jnp.int32, lt.shape, 0)
    big = jnp.int32(1 << 20)
    gl = jnp.where(rowi < N_GROUPS, lt, -jnp.inf)
    gmax = jnp.max(gl, axis=0, keepdims=True)
    gidx = jnp.min(jnp.where(gl == gmax, rowi, big), axis=0, keepdims=True)
    pg_sel = 1.0 / jnp.sum(jnp.exp(gl - gmax), axis=0, keepdims=True)
    e_lo = N_GROUPS + gidx * EXPERTS_PER_GROUP
    el = jnp.where((rowi >= e_lo) & (rowi < e_lo + EXPERTS_PER_GROUP), lt, -jnp.inf)
    m1 = jnp.max(el, axis=0, keepdims=True)
    i1 = jnp.min(jnp.where(el == m1, rowi, big), axis=0, keepdims=True)
    el2 = jnp.where(rowi == i1, -jnp.inf, el)
    m2 = jnp.max(el2, axis=0, keepdims=True)
    i2 = jnp.min(jnp.where(el2 == m2, rowi, big), axis=0, keepdims=True)
    e21 = jnp.exp(m2 - m1)
    gate1 = pg_sel / (1.0 + e21)
    gate2 = pg_sel * e21 / (1.0 + e21)
    row128 = lax.broadcasted_iota(jnp.int32, (LANES, tm), 0)
    rw_o[...] = jnp.transpose(jnp.where(row128 == 0, gate1, jnp.where(row128 == 1, gate2, 0.0)))

    @pl.when(pl.program_id(0) == 0)
    def _():
        cnt_ref[...] = jnp.zeros_like(cnt_ref)

    earlier = (lax.broadcasted_iota(jnp.int32, (tm, tm), 0)
               < lax.broadcasted_iota(jnp.int32, (tm, tm), 1)).astype(BF16)
    oh1 = (rowi == i1).astype(F32)
    oh2 = (rowi == i2).astype(F32)
    run = cnt_ref[...]
    c1 = jnp.sum(oh1, axis=1, keepdims=True)
    before1 = run + jnp.dot(oh1.astype(BF16), earlier, preferred_element_type=F32)
    before2 = run + c1 + jnp.dot(oh2.astype(BF16), earlier, preferred_element_type=F32)
    rank1 = jnp.sum(oh1 * before1, axis=0, keepdims=True).astype(jnp.int32)
    rank2 = jnp.sum(oh2 * before2, axis=0, keepdims=True).astype(jnp.int32)
    total = run + c1 + jnp.sum(oh2, axis=1, keepdims=True)
    cnt_ref[...] = total
    cnt_o[...] = total
    row8 = lax.broadcasted_iota(jnp.int32, (SUBLANES, tm), 0)
    ri_o[...] = jnp.where(row8 == 0, i1 - N_GROUPS, jnp.where(row8 == 1, i2 - N_GROUPS,
                          jnp.where(row8 == 2, rank1, jnp.where(row8 == 3, rank2, 0))))


def _outproj(ya, yf, yb, bonus, g, praw, x2, modl, p, seq, tm, alpha):
    t, d = x2.shape
    tpb = seq // tm
    aw, bw, cw = ya.shape[1], bonus.shape[1], praw.shape[1]
    hb = tm // HALO
    nhb = t // HALO
    tok = lambda i: (i, 0)
    full2 = lambda i: (0, 0)
    kern = functools.partial(_outproj_kernel, alpha=alpha, tiles_per_batch=tpb, seq=seq)
    small = ["w_out_a", "w_out_b", "w_out_c", "pool_w", "pool_scale",
             "gn_gain", "gn_bias", "ones", "ln1_gain", "ln1_bias", "w_router", "b_router"]
    return pl.pallas_call(
        kern,
        grid=(t // tm,),
        in_specs=[pl.BlockSpec((tm, aw), tok),
                  pl.BlockSpec((tm, bw), tok), pl.BlockSpec((tm, bw), tok),
                  pl.BlockSpec((tm, bw), tok), pl.BlockSpec((tm, bw), tok),
                  pl.BlockSpec((tm, cw), tok),
                  pl.BlockSpec((HALO, cw), lambda i: (jnp.maximum(i * hb - 1, 0), 0)),
                  pl.BlockSpec((HALO, cw), lambda i: (jnp.minimum((i + 1) * hb, nhb - 1), 0)),
                  pl.BlockSpec((tm, d), tok),
                  pl.BlockSpec((1,) + modl.shape[1:], lambda i: (i // tpb, 0, 0))]
                 + [pl.BlockSpec(p[k].shape, functools.partial(lambda nd, i: (0,) * nd, p[k].ndim)) for k in small],
        out_specs=[pl.BlockSpec((tm, d), tok), pl.BlockSpec((tm, d // 2), tok),
                   pl.BlockSpec((SUBLANES, tm), lambda i: (0, i)), pl.BlockSpec((tm, LANES), tok),
                   pl.BlockSpec((ROUTER_ROWS, 1), full2)],
        out_shape=[jax.ShapeDtypeStruct((t, d), F32), jax.ShapeDtypeStruct((t, d // 2), jnp.uint32),
                   jax.ShapeDtypeStruct((SUBLANES, t), jnp.int32), jax.ShapeDtypeStruct((t, LANES), F32),
                   jax.ShapeDtypeStruct((ROUTER_ROWS, 1), F32)],
        scratch_shapes=[pltpu.VMEM((ROUTER_ROWS, 1), F32),pltpu.VMEM((len(POOL_WINDOWS), tm + 4 * HALO, cw), F32)],
        compiler_params=_cparams(1, "arbitrary"),
        name="outproj",
    )(ya, yf, yb, bonus, g, praw, praw, praw, x2, modl, *[p[k] for k in small])


def _dispatch(route_t, counts_rows, n_blocks, tm):
    counts = counts_rows[N_GROUPS:N_GROUPS + N_EXPERTS, 0].astype(jnp.int32)
    padded = ((counts + EXPERT_BLOCK - 1) // EXPERT_BLOCK) * EXPERT_BLOCK
    pends = jnp.cumsum(padded)
    pstarts = pends - padded
    e = route_t[:TOP_K]
    onehot = (e[..., None] == jnp.arange(N_EXPERTS, dtype=jnp.int32)).astype(F32)
    start_of = jnp.einsum("ktx,x->kt", onehot, pstarts.astype(F32), precision=HI)
    dest = route_t[TOP_K:2 * TOP_K] + start_of.astype(jnp.int32)
    t = route_t.shape[1]
    dest = dest.reshape(TOP_K, t // tm, tm).transpose(1, 0, 2).reshape(t // tm, 1, TOP_K * tm)
    block_start = jnp.arange(n_blocks, dtype=jnp.int32) * EXPERT_BLOCK
    block_e = jnp.minimum(jnp.sum((pends[None, :] <= block_start[:, None]).astype(jnp.int32), axis=1), N_EXPERTS - 1)
    meta = jnp.concatenate([block_e, (pends[-1] // EXPERT_BLOCK)[None]]).astype(jnp.int32)
    return dest, meta, pends.astype(jnp.int32)


def _scatter_rows_kernel(pends_ref, dest_ref, u_ref, xs_ref, zeros_ref, sem, zsem):
    tm = u_ref.shape[0] * SUBLANES

    @pl.when(pl.program_id(0) == 0)
    def _():
        zeros_ref[...] = jnp.zeros_like(zeros_ref)

        def tail_copy(e):
            tail = pl.ds(pl.multiple_of(pends_ref[e] - EXPERT_BLOCK, EXPERT_BLOCK), EXPERT_BLOCK)
            return pltpu.make_async_copy(zeros_ref, xs_ref.at[tail], zsem)

        def has_rows(e):
            return pends_ref[e] > (pends_ref[e - 1] if e > 0 else 0)

        def unused_copy(j):
            return pltpu.make_async_copy(zeros_ref, xs_ref.at[pl.ds(j * EXPERT_BLOCK, EXPERT_BLOCK)], zsem)

        def is_unused(j):
            return j * EXPERT_BLOCK >= pends_ref[N_EXPERTS - 1]

        n_blocks = xs_ref.shape[0] // EXPERT_BLOCK
        for e in range(N_EXPERTS):
            pl.when(has_rows(e))(lambda e=e: tail_copy(e).start())
        for j in range(n_blocks):
            pl.when(is_unused(j))(lambda j=j: unused_copy(j).start())
        for e in range(N_EXPERTS):
            pl.when(has_rows(e))(lambda e=e: tail_copy(e).wait())
        for j in range(n_blocks):
            pl.when(is_unused(j))(lambda j=j: unused_copy(j).wait())

    def issue(grp, carry):
        for j in range(SUBLANES):
            for k in range(TOP_K):
                dst = dest_ref[0, 0, k * tm + SUBLANES * grp + j]
                pltpu.make_async_copy(u_ref.at[grp, pl.ds(j, 1)], xs_ref.at[pl.ds(dst, 1)], sem).start()
        return carry

    lax.fori_loop(0, tm // SUBLANES, issue, 0)
    rows = pl.ds(0, TOP_K * tm)
    pltpu.make_async_copy(xs_ref.at[rows], xs_ref.at[rows], sem).wait()


def _scatter_rows(pends, u2, dest3, total, tm):
    t, d = u2.shape
    grid_spec = pltpu.PrefetchScalarGridSpec(
        num_scalar_prefetch=1,
        grid=(t // tm,),
        in_specs=[pl.BlockSpec((1, 1, TOP_K * tm), lambda i, p: (i, 0, 0), memory_space=pltpu.SMEM),
                  pl.BlockSpec((tm // SUBLANES, SUBLANES, d), lambda i, p: (i, 0, 0))],
        out_specs=pl.BlockSpec(memory_space=pl.ANY),
        scratch_shapes=[pltpu.VMEM((EXPERT_BLOCK, d), u2.dtype), pltpu.SemaphoreType.DMA(()),
                        pltpu.SemaphoreType.DMA(())],
    )
    return pl.pallas_call(
        _scatter_rows_kernel,
        grid_spec=grid_spec,
        out_shape=jax.ShapeDtypeStruct((total, d), u2.dtype),
        compiler_params=_cparams(1, "arbitrary"),
        name="scatter_rows",
    )(pends, dest3, u2.reshape(t // SUBLANES, SUBLANES, d))


def _experts_kernel(meta_ref, xs_ref, wg_ref, wu_ref, wd_ref, o_ref, wg_b, wu_b, wd_b):
    i = pl.program_id(0)
    n_used = meta_ref[pl.num_programs(0)]

    @pl.when((i == 0) | (meta_ref[i] != meta_ref[jnp.maximum(i - 1, 0)]))
    def _():
        wg_b[...] = wg_ref[0, 0].astype(BF16)
        wu_b[...] = wu_ref[0, 0].astype(BF16)
        wd_b[...] = wd_ref[0, 0].astype(BF16)

    @pl.when(i < n_used)
    def _():
        xb = _unpack_bf16_pairs(xs_ref[...])
        gate = jnp.dot(xb, wg_b[...], preferred_element_type=F32)
        up = jnp.dot(xb, wu_b[...], preferred_element_type=F32)
        hb = gate * _sigmoid(gate) * up
        o_ref[...] = _pack_bf16_pairs(jnp.dot(hb.astype(BF16), wd_b[...], preferred_element_type=F32))

    @pl.when(i >= n_used)
    def _():
        o_ref[...] = jnp.zeros_like(o_ref)


def _experts(meta, xs, wg, wu, wd, layer):
    total, dp = xs.shape
    nb = total // EXPERT_BLOCK
    d, de = wg.shape[2:]
    grid_spec = pltpu.PrefetchScalarGridSpec(
        num_scalar_prefetch=1,
        grid=(nb,),
        in_specs=[pl.BlockSpec((EXPERT_BLOCK, dp), lambda i, m: (jnp.minimum(i, m[nb] - 1), 0)),
                  pl.BlockSpec((1, 1, d, de), lambda i, m: (layer, m[i], 0, 0)),
                  pl.BlockSpec((1, 1, d, de), lambda i, m: (layer, m[i], 0, 0)),
                  pl.BlockSpec((1, 1, de, d), lambda i, m: (layer, m[i], 0, 0))],
        out_specs=pl.BlockSpec((EXPERT_BLOCK, dp), lambda i, m: (i, 0)),
        scratch_shapes=[pltpu.VMEM((d, de), BF16), pltpu.VMEM((d, de), BF16), pltpu.VMEM((de, d), BF16)],
    )
    return pl.pallas_call(
        _experts_kernel,
        grid_spec=grid_spec,
        out_shape=jax.ShapeDtypeStruct((total, dp), xs.dtype),
        compiler_params=_cparams(1, "arbitrary"),
        name="experts",
    )(meta, xs, wg, wu, wd)


def _final_kernel(dcur_ref, dnext_ref, x1_ref, rw_ref, mod_ref, g_ref, b_ref, ys_ref, o_ref, ybuf, sem, *, alpha):
    i = pl.program_id(0)
    tm = x1_ref.shape[0]
    slot = i % 2

    def gather(d_ref, s):
        def issue(grp, carry):
            for j in range(SUBLANES):
                for k in range(TOP_K):
                    src = d_ref[0, 0, k * tm + SUBLANES * grp + j]
                    pltpu.make_async_copy(ys_ref.at[pl.ds(src, 1)], ybuf.at[s, k, grp, pl.ds(j, 1)],
                                          sem.at[s]).start()
            return carry

        lax.fori_loop(0, tm // SUBLANES, issue, 0)

    @pl.when(i == 0)
    def _():
        gather(dcur_ref, 0)

    @pl.when(i + 1 < pl.num_programs(0))
    def _():
        gather(dnext_ref, 1 - slot)

    pltpu.make_async_copy(ybuf.at[slot], ybuf.at[slot], sem.at[slot]).wait()
    m = mod_ref[0]
    rw = rw_ref[...]
    dp = ybuf.shape[-1]
    y1 = _unpack_bf16_pairs(ybuf[slot, 0].reshape(tm, dp)).astype(F32)
    y2 = _unpack_bf16_pairs(ybuf[slot, 1].reshape(tm, dp)).astype(F32)
    f = rw[:, 0:1] * y1 + rw[:, 1:2] * y2
    o_ref[...] = _ln(alpha * x1_ref[...] + m[5:6] * f) * g_ref[...] + b_ref[...]


def _final(x1, ysorted, dest3, rw, modl, gain, bias, seq, tm, alpha):
    t, d = x1.shape
    tpb = seq // tm
    n_tiles = t // tm
    tok = lambda i: (i, 0)
    kern = functools.partial(_final_kernel, alpha=alpha)
    dspec = lambda f: pl.BlockSpec((1, 1, TOP_K * tm), f, memory_space=pltpu.SMEM)
    return pl.pallas_call(
        kern,
        grid=(n_tiles,),
        in_specs=[dspec(lambda i: (i, 0, 0)), dspec(lambda i: (jnp.minimum(i + 1, n_tiles - 1), 0, 0)),
                  pl.BlockSpec((tm, d), tok), pl.BlockSpec((tm, LANES), tok),
                  pl.BlockSpec((1,) + modl.shape[1:], lambda i: (i // tpb, 0, 0)),
                  pl.BlockSpec(gain.shape, lambda i: (0, 0)), pl.BlockSpec(bias.shape, lambda i: (0, 0)),
                  pl.BlockSpec(memory_space=pl.ANY)],
        out_specs=pl.BlockSpec((tm, d), tok),
        out_shape=jax.ShapeDtypeStruct((t, d), F32),
        scratch_shapes=[pltpu.VMEM((2, TOP_K, tm // SUBLANES, SUBLANES, ysorted.shape[1]), ysorted.dtype),
                        pltpu.SemaphoreType.DMA((2,))],
        compiler_params=_cparams(1, "arbitrary"),
        name="final_ln",
    )(dest3, dest3, x1, rw, modl, gain, bias, ysorted)


def _block_diag(blocks):
    n, a, b = blocks.shape
    out = jnp.zeros((n * a, n * b), blocks.dtype)
    for i in range(n):
        out = out.at[i * a:(i + 1) * a, i * b:(i + 1) * b].set(blocks[i])
    return out


def _pad_rows(w, lo, total):
    return jnp.zeros((total, w.shape[-1]), w.dtype).at[lo:lo + w.shape[0]].set(w)


def kernel(x, c, w_mod, b_mod, w_in, na_rpb, rw_conv, rw_w0, rw_w_up, rw_a0, rw_a_up, rw_g_up, rw_k_k, rw_k_a, rw_r_k, rw_gn_gain, rw_gn_bias, pool_w, pool_scale, w_out, ln1_gain, ln1_bias, ln2_gain, ln2_bias, moe_w_group, moe_b_group, moe_w_expert, moe_b_expert, moe_w_gate, moe_w_up, moe_w_down):
    batch, seq, d = x.shape
    depth = w_mod.shape[0]
    t = batch * seq
    a_w = na_rpb.shape[1] * HEAD_DIM
    b_w = rw_w0.shape[-1]
    c_w = pool_scale.shape[-1]
    lr_w = R_W + R_A + R_G
    alpha = (2 * depth) ** 0.25
    tm = min(512, seq)
    tm_in = min(512, seq)
    assert seq % tm == 0 and seq % SCAN_CHUNK == 0 and seq % GRID_W == 0 and lr_w == LANES

    mod = _modulation(c, w_mod, b_mod)
    ones_blk = _block_diag(jnp.ones((b_w // HEAD_DIM, HEAD_DIM, HEAD_DIM), BF16))
    row = lambda v: v.reshape(1, -1)

    x2 = x.reshape(t, d)
    for l in range(depth):
        modl = mod[l]
        prep_params = {
            "conv": rw_conv[l], "w0": rw_w0[l], "a0": rw_a0[l],
            "w_up": jnp.stack([_pad_rows(rw_w_up[l, dd], 0, lr_w) for dd in range(2)]).astype(BF16),
            "a_up": jnp.stack([_pad_rows(rw_a_up[l, dd], R_W, lr_w) for dd in range(2)]).astype(BF16),
            "g_up": _pad_rows(rw_g_up[l], R_W + R_A, lr_w).astype(BF16),
            "k_k": row(rw_k_k[l]), "k_a": row(rw_k_a[l]), "r_k": row(rw_r_k[l]), "ones": ones_blk,
        }
        qkv, praw, r, v, nkk, lw, bb, kd, bonus, g = _inproj(x2, modl, w_in[l].astype(BF16), prep_params, seq, tm_in,
                                                             3 * a_w, 3 * b_w, lr_w, c_w)
        ya = _natten(qkv, _na_bias_table(na_rpb[l]), batch, seq, a_w)
        yf, yb = _rwkv_scan(r, v, nkk, lw, bb, kd, batch, seq)
        w_router = jnp.zeros((d, LANES), F32).at[:, :N_GROUPS].set(moe_w_group[l])
        w_router = w_router.at[:, N_GROUPS:N_GROUPS + N_EXPERTS].set(moe_w_expert[l])
        b_router = jnp.zeros((1, LANES), F32).at[0, :N_GROUPS].set(moe_b_group[l])
        b_router = b_router.at[0, N_GROUPS:N_GROUPS + N_EXPERTS].set(moe_b_expert[l])
        wo = w_out[l].astype(BF16)
        out_params = {
            "w_out_a": wo[:a_w], "w_out_b": wo[a_w:a_w + b_w], "w_out_c": wo[a_w + b_w:],
            "pool_w": _block_diag(pool_w[l]), "pool_scale": row(pool_scale[l]),
            "gn_gain": row(rw_gn_gain[l]), "gn_bias": row(rw_gn_bias[l]), "ones": ones_blk,
            "ln1_gain": row(ln1_gain[l]), "ln1_bias": row(ln1_bias[l]),
            "w_router": jnp.concatenate(_split_bf16(w_router), axis=1), "b_router": b_router,
        }
        x1, u2, route_i, route_w, counts = _outproj(ya, yf, yb, bonus, g, praw, x2, modl, out_params, seq, tm, alpha)
        n_blocks = -(-(t * TOP_K) // EXPERT_BLOCK) + N_EXPERTS
        dest3, meta, pends = _dispatch(route_i, counts, n_blocks, tm)
        xs = _scatter_rows(pends, u2, dest3, n_blocks * EXPERT_BLOCK, tm)
        ysorted = _experts(meta, xs, moe_w_gate, moe_w_up, moe_w_down, l)
        x2 = _final(x1, ysorted, dest3, route_w, modl, row(ln2_gain[l]), row(ln2_bias[l]), seq, tm, alpha)
    return x2.reshape(batch, seq, d)
```

```python
import functools
import math

import jax
import jax.numpy as jnp
import numpy as np
from jax import lax
from jax.experimental import pallas as pl
from jax.experimental.pallas import tpu as pltpu

F32 = jnp.float32
BF16 = jnp.bfloat16
HI = lax.Precision.HIGHEST

GRID_W = 64
HEAD_DIM = 64
NA_KH = 8
NA_KW = 16
POOL_WINDOWS = (2, 4, 8, 16)
R_W = 32
R_A = 32
R_G = 64
DECAY_SCALE = math.exp(-0.5)
GN_EPS = 64e-5
N_GROUPS = 4
EXPERTS_PER_GROUP = 8
N_EXPERTS = N_GROUPS * EXPERTS_PER_GROUP
TOP_K = 2
ROUTER_ROWS = -(-(N_GROUPS + N_EXPERTS) // 8) * 8
EXPERT_BLOCK = 512
LN_EPS = 1e-5
NEG_INF = -1e30

TOKEN_TILE = 512
NA_ROWS_PER_STEP = 8
SCAN_CHUNK = 64
SCAN_CHUNKS_PER_STEP = 4
SUBLANES = 8
HALO = 8
LANES = 128
VMEM_LIMIT = 52 * 1024 * 1024


def _ln(x):
    mu = jnp.mean(x, axis=-1, keepdims=True)
    xc = x - mu
    var = jnp.mean(xc * xc, axis=-1, keepdims=True)
    return xc * lax.rsqrt(var + LN_EPS)


def _sigmoid(x):
    return 1.0 / (1.0 + jnp.exp(-x))


def _split_bf16(x):
    hi = x.astype(BF16)
    return hi, (x - hi.astype(F32)).astype(BF16)


def _dot_split(x, w_exact):
    hi, lo = _split_bf16(x)
    return jnp.dot(hi, w_exact, preferred_element_type=F32) + jnp.dot(lo, w_exact, preferred_element_type=F32)


def _pack_bf16_pairs(x):
    h = x.shape[1] // 2
    lo = lax.bitcast_convert_type(x[:, :h].astype(BF16).astype(F32), jnp.uint32)
    hi = lax.bitcast_convert_type(x[:, h:].astype(BF16).astype(F32), jnp.uint32)
    return (lo >> 16) | hi


def _unpack_bf16_pairs(w):
    lo = lax.bitcast_convert_type(w << 16, F32).astype(BF16)
    hi = lax.bitcast_convert_type(w & jnp.uint32(0xFFFF0000), F32).astype(BF16)
    return jnp.concatenate([lo, hi], axis=1)


def _cparams(n_axes, semantics="parallel"):
    return pltpu.CompilerParams(dimension_semantics=(semantics,) * n_axes, vmem_limit_bytes=VMEM_LIMIT)


def _mod_kernel(c_ref, w_ref, b_ref, o_ref):
    c = c_ref[...]
    s = c * _sigmoid(c)
    o_ref[0] = jnp.dot(s, w_ref[0], precision=HI, preferred_element_type=F32) + b_ref[0]


def _modulation(c, w_mod, b_mod):
    n_layers, d, d6 = w_mod.shape
    b = c.shape[0]
    bp = -(-b // 8) * 8
    cp = jnp.zeros((bp, d), F32).at[:b].set(c)
    out = pl.pallas_call(
        _mod_kernel,
        grid=(n_layers, d6 // d),
        in_specs=[pl.BlockSpec((bp, d), lambda l, j: (0, 0)),
                  pl.BlockSpec((1, d, d), lambda l, j: (l, 0, j)),
                  pl.BlockSpec((1, 1, d), lambda l, j: (l, 0, j))],
        out_specs=pl.BlockSpec((1, bp, d), lambda l, j: (l, 0, j)),
        out_shape=jax.ShapeDtypeStruct((n_layers, bp, d6), F32),
        compiler_params=_cparams(2),
        name="modulation",
    )(cp, w_mod, b_mod.reshape(n_layers, 1, d6))
    return out[:, :b].reshape(n_layers, b, d6 // d, d)


def _inproj_kernel(x_ref, xp_ref, xn_ref, mod_ref, wr_ref, wo_ref, cw_ref, w0_ref, wup_ref, a0_ref, aup_ref, gup_ref,
                   kk_ref, ka_ref, rk_ref, ones_ref,
                   qkv_o, pool_o, r_o, v_o, nkk_o, lw_o, b_o, kd_o, bonus_o, g_o,
                   *, a3, b3, tiles_per_batch):
    m = mod_ref[0]
    tm = x_ref.shape[0]
    tb = pl.program_id(0) % tiles_per_batch
    xe = jnp.concatenate([xp_ref[...], x_ref[...], xn_ref[...]], axis=0)
    u = (_ln(xe) * (1.0 + m[1:2]) + m[0:1]).astype(BF16)
    h = jnp.dot(u, wr_ref[...], preferred_element_type=F32)
    ho = jnp.dot(u[HALO:HALO + tm], wo_ref[...], preferred_element_type=F32)
    hm = h[HALO:HALO + tm]
    prev = jnp.where(tb == 0, 0.0, h[HALO - 1:HALO, :b3])
    nxt = jnp.where(tb == tiles_per_batch - 1, 0.0, h[HALO + tm:HALO + tm + 1, :b3])
    _rwkv_prep_tile(hm[:, :b3], prev, nxt, hm[:, b3:],
                    cw_ref, w0_ref, wup_ref, a0_ref, aup_ref, gup_ref, kk_ref, ka_ref, rk_ref, ones_ref,
                    r_o, v_o, nkk_o, lw_o, b_o, kd_o, bonus_o, g_o)
    qkv_o[...] = ho[:, :a3].astype(BF16)
    pool_o[...] = ho[:, a3:]


def _inproj(x2, modl, w_in_bf, p, seq, tm, a3, b3, lr_w, c_w):
    t, d = x2.shape
    tpb = seq // tm
    hb = tm // HALO
    nhb = t // HALO
    width = b3 // 3
    kern = functools.partial(_inproj_kernel, a3=a3, b3=b3, tiles_per_batch=tpb)
    w_rwkv = w_in_bf[:, a3:a3 + b3 + lr_w]
    w_other = jnp.concatenate([w_in_bf[:, :a3], w_in_bf[:, a3 + b3 + lr_w:]], axis=1)
    tok = lambda i: (i, 0)
    dtok = lambda i: (0, i, 0)
    names = ["conv", "w0", "w_up", "a0", "a_up", "g_up", "k_k", "k_a", "r_k", "ones"]
    tw = jax.ShapeDtypeStruct((t, width), F32)
    dtw = jax.ShapeDtypeStruct((2, t, width), F32)
    return pl.pallas_call(
        kern,
        grid=(t // tm,),
        in_specs=[pl.BlockSpec((tm, d), tok),
                  pl.BlockSpec((HALO, d), lambda i: (jnp.maximum(i * hb - 1, 0), 0)),
                  pl.BlockSpec((HALO, d), lambda i: (jnp.minimum((i + 1) * hb, nhb - 1), 0)),
                  pl.BlockSpec((1,) + modl.shape[1:], lambda i: (i // tpb, 0, 0)),
                  pl.BlockSpec(w_rwkv.shape, lambda i: (0, 0)),
                  pl.BlockSpec(w_other.shape, lambda i: (0, 0))]
                 + [pl.BlockSpec(p[k].shape, functools.partial(lambda nd, i: (0,) * nd, p[k].ndim)) for k in names],
        out_specs=[pl.BlockSpec((tm, a3), tok), pl.BlockSpec((tm, c_w), tok),
                   pl.BlockSpec((tm, width), tok), pl.BlockSpec((tm, width), tok), pl.BlockSpec((tm, width), tok),
                   pl.BlockSpec((2, tm, width), dtok), pl.BlockSpec((2, tm, width), dtok),
                   pl.BlockSpec((2, tm, width), dtok),
                   pl.BlockSpec((tm, width), tok), pl.BlockSpec((tm, width), tok)],
        out_shape=[jax.ShapeDtypeStruct((t, a3), BF16), jax.ShapeDtypeStruct((t, c_w), F32),
                   tw, tw, tw, dtw, dtw, dtw, tw, tw],
        compiler_params=_cparams(1),
        name="inproj",
    )(x2, x2, x2, modl, w_rwkv, w_other, *[p[k] for k in names])


def _na_bias_table(rpb):
    col = np.arange(GRID_W)
    cstart = np.clip(col - NA_KW // 2, 0, GRID_W - NA_KW)
    in_win = (col[None, :] >= cstart[:, None]) & (col[None, :] < cstart[:, None] + NA_KW)
    dc = np.clip(col[None, :] - col[:, None], -(NA_KW - 1), NA_KW - 1) + (NA_KW - 1)
    pick = (dc[None] == np.arange(2 * NA_KW - 1)[:, None, None]).astype(np.float32)
    cols = jnp.einsum("hrc,cqk->hrqk", rpb.astype(F32), pick, precision=HI)
    cols = jnp.where(in_win, cols, NEG_INF)
    b = jnp.stack([cols[:, NA_KH - 1 - o:2 * NA_KH - 1 - o] for o in range(NA_KH)])
    h = rpb.shape[0]
    return jnp.transpose(b, (0, 1, 3, 2, 4)).reshape(NA_KH, h * GRID_W, NA_KH * GRID_W)


def _natten_kernel(q_ref, k_ref, v_ref, bias_ref, o_ref, *, rows, heads):
    width = q_ref.shape[1]
    nk = NA_KH * GRID_W
    head_of_lane = lax.broadcasted_iota(jnp.int32, (heads * GRID_W, width), 1) // HEAD_DIM
    head_of_row = lax.broadcasted_iota(jnp.int32, (heads * GRID_W, width), 0) // GRID_W
    own = head_of_lane == head_of_row
    for j in range(NA_ROWS_PER_STEP):
        r = pl.program_id(1) * NA_ROWS_PER_STEP + j
        rstart = jnp.clip(r - NA_KH // 2, 0, rows - NA_KH)
        off = r - rstart
        start = pl.multiple_of(rstart * GRID_W, GRID_W)
        kw = k_ref[pl.ds(start, nk), :]
        vw = v_ref[pl.ds(start, nk), :]
        q = q_ref[j * GRID_W:(j + 1) * GRID_W, :]
        qs = jnp.where(own, jnp.concatenate([q] * heads, axis=0), jnp.zeros((), q.dtype))
        s = lax.dot_general(qs, kw, (((1,), (1,)), ((), ())), preferred_element_type=F32) * (HEAD_DIM ** -0.5)
        s = s + bias_ref[off]
        mx = jnp.max(s, axis=-1, keepdims=True)
        p = jnp.exp(s - mx)
        den = jnp.sum(p, axis=-1, keepdims=True)
        o = jnp.where(own, jnp.dot(p.astype(BF16), vw, preferred_element_type=F32) / den, 0.0)
        acc = o[0:GRID_W]
        for h in range(1, heads):
            acc = acc + o[h * GRID_W:(h + 1) * GRID_W]
        o_ref[j * GRID_W:(j + 1) * GRID_W, :] = acc.astype(o_ref.dtype)


def _natten(qkv, bias_tab, batch, seq, width):
    rows = seq // GRID_W
    assert rows >= NA_KH
    heads = width // HEAD_DIM
    steps = rows // NA_ROWS_PER_STEP
    assert steps * NA_ROWS_PER_STEP == rows
    tq = NA_ROWS_PER_STEP * GRID_W
    kern = functools.partial(_natten_kernel, rows=rows, heads=heads)
    return pl.pallas_call(
        kern,
        grid=(batch, steps),
        in_specs=[pl.BlockSpec((tq, width), lambda b, r: (b * steps + r, 0)),
                  pl.BlockSpec((seq, width), lambda b, r: (b, 1)),
                  pl.BlockSpec((seq, width), lambda b, r: (b, 2)),
                  pl.BlockSpec(bias_tab.shape, lambda b, r: (0, 0, 0))],
        out_specs=pl.BlockSpec((tq, width), lambda b, r: (b * steps + r, 0)),
        out_shape=jax.ShapeDtypeStruct((batch * seq, width), BF16),
        compiler_params=_cparams(2),
        name="natten",
    )(qkv, qkv, qkv, bias_tab)


def _rwkv_prep_tile(z, prev, nxt, lr, cw_ref, w0_ref, wup_ref, a0_ref, aup_ref, gup_ref, kk_ref, ka_ref, rk_ref, ones_ref,
                    r_o, v_o, nkk_o, lw_o, b_o, kd_o, bonus_o, g_o):
    tm = z.shape[0]
    width = z.shape[1] // 3
    row = lax.broadcasted_iota(jnp.int32, z.shape, 0)
    zm1 = jnp.where(row == 0, prev, pltpu.roll(z, 1, 0))
    zp1 = jnp.where(row == tm - 1, nxt, pltpu.roll(z, tm - 1, 0))
    rkv = zm1 * cw_ref[0:1, :] + z * cw_ref[1:2, :] + zp1 * cw_ref[2:3, :]
    r = rkv[:, :width]
    k = rkv[:, width:2 * width]
    v = rkv[:, 2 * width:]
    th = jnp.tanh(lr)
    sg = _sigmoid(lr)
    ones = ones_ref[...]

    def headsum(x):
        return _dot_split(x, ones)

    kk = k * kk_ref[...]
    kk = kk * lax.rsqrt(jnp.maximum(headsum(kk * kk), 1e-24))
    g_o[...] = jnp.dot(sg.astype(BF16), gup_ref[...], preferred_element_type=F32)
    r_o[...] = r
    v_o[...] = v
    nkk_o[...] = -kk
    kd_sum = jnp.zeros_like(r)
    th_b = th.astype(BF16)
    lr_b = lr.astype(BF16)
    for d in range(2):
        wl = jnp.dot(th_b, wup_ref[d], preferred_element_type=F32) + w0_ref[d:d + 1, :]
        lw_o[d] = -DECAY_SCALE * _sigmoid(wl)
        a = _sigmoid(jnp.dot(lr_b, aup_ref[d], preferred_element_type=F32) + a0_ref[d:d + 1, :])
        kd = k * (1.0 + (a - 1.0) * ka_ref[...])
        kd_o[d] = kd
        b_o[d] = kk * a
        kd_sum = kd_sum + kd
    bonus_o[...] = headsum(r * kd_sum * rk_ref[...]) * v


def _dot_nt(a, b):
    return lax.dot_general(a, b, (((1,), (1,)), ((), ())), preferred_element_type=F32)


def _dot_tn(a, b):
    return lax.dot_general(a, b, (((0,), (0,)), ((), ())), preferred_element_type=F32)


def _rwkv_scan_kernel(rf_ref, vf_ref, nf_ref, rb_ref, vb_ref, nb_ref, lwf_ref, bf_ref, kf_ref, lwb_ref, bb_ref, kb_ref,
                      yf_ref, yb_ref, s_ref, *, heads, batch):
    @pl.when(pl.program_id(0) == 0)
    def _():
        s_ref[...] = jnp.zeros_like(s_ref)

    n = SCAN_CHUNK
    pair_w = 2 * HEAD_DIM
    row = lax.broadcasted_iota(jnp.int32, (n, pair_w), 0)
    lane = lax.broadcasted_iota(jnp.int32, (n, pair_w), 1)
    col = lane & (HEAD_DIM - 1)
    even = lane < HEAD_DIM
    levels = n.bit_length()
    same = [(row >> k) == (col >> k) for k in range(levels)]
    eye = same[0].astype(F32)
    level_masks = [same[sh + 1] & jnp.logical_not(same[sh]) for sh in range(1, levels - 1)]

    def blockdiag(x2):
        xb = x2.astype(BF16)
        zero = jnp.zeros((), BF16)
        return jnp.concatenate([jnp.where(even, xb, zero), jnp.where(even, zero, xb)], axis=0)

    def mm(x2, y2):
        return jnp.dot(x2.astype(BF16), blockdiag(y2), preferred_element_type=F32)

    def mm_nt(x2, y2):
        return _dot_nt(x2.astype(BF16), blockdiag(y2))

    dirs = ((rf_ref, vf_ref, nf_ref, lwf_ref, bf_ref, kf_ref, yf_ref),
            (rb_ref, vb_ref, nb_ref, lwb_ref, bb_ref, kb_ref, yb_ref))
    def build(q):
        chains = []
        for d, (r_ref, v_ref, n_ref, lw_ref, b_ref, k_ref, y_ref) in enumerate(dirs):
            sub = q if d == 0 else SCAN_CHUNKS_PER_STEP - 1 - q
            rs = slice(sub * n, (sub + 1) * n)
            order = row - col if d == 0 else col - row
            strict = order > 0
            incl = order >= 0
            incl_b = jnp.where(incl[:, :n], 1.0, 0.0).astype(BF16)
            for bi in range(batch):
                lw = lw_ref[0, bi, rs]
                lw_hi, lw_mid = _split_bf16(lw)
                lw_lo = (lw - lw_hi.astype(F32) - lw_mid.astype(F32)).astype(BF16)
                g_inc = ((jnp.dot(incl_b, lw_lo, preferred_element_type=F32)
                          + jnp.dot(incl_b, lw_mid, preferred_element_type=F32))
                         + jnp.dot(incl_b, lw_hi, preferred_element_type=F32))
                g_tot = jnp.sum(lw, axis=0, keepdims=True)
                e_neg = jnp.exp(-g_inc)
                e_end = jnp.exp(g_tot - g_inc)
                decay = jnp.exp(g_tot)
                a_t = n_ref[bi, rs] * jnp.exp(g_inc - lw)
                r_t = r_ref[bi, rs] * jnp.exp(g_inc)
                bb = b_ref[0, bi, rs]
                kd = k_ref[0, bi, rs]
                b_t = bb * e_neg
                k_t = kd * e_neg
                ar_t = jnp.concatenate([a_t, r_t], axis=0).astype(BF16)
                bk_h = jnp.concatenate([bb * e_end, kd * e_end], axis=0).astype(BF16)
                v = v_ref[bi, rs]
                for p in range(heads // 2):
                    sl = slice(p * pair_w, (p + 1) * pair_w)
                    chains.append(dict(strict=strict, incl=incl, sl=sl, rs=rs, bi=bi, y_ref=y_ref,
                                       si=(d * batch + bi) * (heads // 2) + p, decay=decay[:, sl],
                                       ar=ar_t[:, sl], b=b_t[:, sl], k=k_t[:, sl], bk_h=bk_h[:, sl], v=v[:, sl]))
        return chains

    def state_free_stages(chains):
        def products():
            for ch in chains:
                pb = mm_nt(ch["ar"], ch["b"])
                pk = mm_nt(ch["ar"], ch["k"])
                ch["l_ab"] = jnp.where(ch["strict"], pb[:n], 0.0)
                ch["m_rb"] = jnp.where(ch["incl"], pb[n:], 0.0)
                ch["l_ak"] = jnp.where(ch["strict"], pk[:n], 0.0)
                ch["m_rk"] = jnp.where(ch["incl"], pk[n:], 0.0)
                ch["t"] = eye + jnp.where(same[1], ch["l_ab"], 0.0)

        def level_left(mask):
            for ch in chains:
                ch["tc"] = mm(ch["t"], jnp.where(mask, ch["l_ab"], 0.0))

        def level_right():
            for ch in chains:
                ch["t"] = ch["t"] + mm(ch["tc"], ch["t"])

        def values():
            for ch in chains:
                ch["kv"] = mm(jnp.concatenate([ch["l_ak"], ch["m_rk"]], axis=0), ch["v"])

        stages = [products]
        for mask in level_masks:
            stages += [functools.partial(level_left, mask), level_right]
        return stages + [values]

    def state_stages(chains):
        def read():
            for ch in chains:
                ch["s0"] = s_ref[ch["si"]]
                ch["x"] = mm_nt(ch["ar"], ch["s0"])

        def solve():
            for ch in chains:
                ch["u"] = mm(ch["t"], ch["x"][:n] + ch["kv"][:n])

        def emit():
            for ch in chains:
                y = ch["x"][n:] + mm(ch["m_rb"], ch["u"]) + ch["kv"][n:]
                ch["y_ref"][ch["bi"], ch["rs"], ch["sl"]] = y

        def write():
            for ch in chains:
                uv = jnp.concatenate([ch["u"], ch["v"]], axis=0).astype(BF16)
                full = _dot_tn(uv, ch["bk_h"])
                s_ref[ch["si"]] = ch["s0"] * ch["decay"] + jnp.where(even, full[:HEAD_DIM], full[HEAD_DIM:])

        return [read, solve, emit, write]

    pending = []
    for q in range(SCAN_CHUNKS_PER_STEP):
        chains = build(q)
        free = state_free_stages(chains)
        if pending:
            share = -(-len(free) // len(pending))
            for i, carried in enumerate(pending):
                carried()
                for stage in free[i * share:(i + 1) * share]:
                    stage()
        else:
            for stage in free:
                stage()
        pending = state_stages(chains)
    for carried in pending:
        carried()


def _rwkv_scan(r, v, nkk, lw, b, kd, batch, seq):
    t, width = r.shape
    heads = width // HEAD_DIM
    n = SCAN_CHUNK * SCAN_CHUNKS_PER_STEP
    nc = seq // n
    assert nc * n == seq
    r3, v3, n3 = (z.reshape(batch, seq, width) for z in (r, v, nkk))
    lw4, b4, k4 = (z.reshape(2, batch, seq, width) for z in (lw, b, kd))
    fwd = pl.BlockSpec((batch, n, width), lambda c: (0, c, 0))
    bwd = pl.BlockSpec((batch, n, width), lambda c: (0, nc - 1 - c, 0))
    fwd_d = pl.BlockSpec((1, batch, n, width), lambda c: (0, 0, c, 0))
    bwd_d = pl.BlockSpec((1, batch, n, width), lambda c: (1, 0, nc - 1 - c, 0))
    kern = functools.partial(_rwkv_scan_kernel, heads=heads, batch=batch)
    yf, yb = pl.pallas_call(
        kern,
        grid=(nc,),
        in_specs=[fwd, fwd, fwd, bwd, bwd, bwd, fwd_d, fwd_d, fwd_d, bwd_d, bwd_d, bwd_d],
        out_specs=[fwd, bwd],
        out_shape=[jax.ShapeDtypeStruct((batch, seq, width), F32)] * 2,
        scratch_shapes=[pltpu.VMEM((batch * heads, HEAD_DIM, 2 * HEAD_DIM), F32)],
        compiler_params=_cparams(1, "arbitrary"),
        name="rwkv_scan",
    )(r3, v3, n3, r3, v3, n3, lw4, b4, k4, lw4, b4, k4)
    return yf.reshape(t, width), yb.reshape(t, width)


def _pool_tile(p_ref, pp_ref, pn_ref, w_ref, sc_ref, ext_ref, tb, tiles_per_batch, seq):
    p = p_ref[...]
    tm, width = p.shape
    assert all(w == 2 << i for i, w in enumerate(POOL_WINDOWS)) and POOL_WINDOWS[-1] <= 2 * HALO
    n = tm + 2 * HALO
    pad = jnp.zeros((HALO, width), F32)
    for k in range(len(POOL_WINDOWS)):
        ext_ref[k, 0:HALO, :] = pad
        ext_ref[k, HALO + n:2 * HALO + n, :] = pad
    ext_ref[0, HALO:2 * HALO, :] = jnp.where(tb == 0, 0.0, pp_ref[...])
    ext_ref[0, 2 * HALO:2 * HALO + tm, :] = p
    ext_ref[0, 2 * HALO + tm:HALO + n, :] = jnp.where(tb == tiles_per_batch - 1, 0.0, pn_ref[...])

    def rows(k, first, count):
        return ext_ref[k, 2 * HALO + first:2 * HALO + first + count, :]

    ext_ref[1, HALO:HALO + n, :] = rows(0, -HALO - 1, n) + rows(0, -HALO, n)
    for k in range(1, len(POOL_WINDOWS) - 1):
        q = POOL_WINDOWS[k - 1] // 2
        ext_ref[k + 1, HALO:HALO + n, :] = rows(k, -HALO - q, n) + rows(k, -HALO + q, n)
    q = POOL_WINDOWS[-2] // 2
    sums = [rows(k + 1, 0, tm) for k in range(len(POOL_WINDOWS) - 1)]
    sums.append(rows(len(POOL_WINDOWS) - 1, -q, tm) + rows(len(POOL_WINDOWS) - 1, q, tm))

    t = tb * tm + lax.broadcasted_iota(jnp.int32, (tm, width), 0)
    grp = lax.broadcasted_iota(jnp.int32, (tm, width), 1) // (width // len(POOL_WINDOWS))
    pooled = jnp.zeros_like(p)
    for gi, win in enumerate(POOL_WINDOWS):
        half = win // 2
        lo = jnp.clip(t - half, 0, seq - 1)
        hi = jnp.clip(t + half - 1, 0, seq - 1)
        cnt = (hi - lo + 1).astype(F32)
        pooled = jnp.where(grp == gi, sums[gi] / cnt, pooled)
    pooled = pooled - p
    return jnp.dot(pooled, w_ref[...], preferred_element_type=F32) * sc_ref[...]


def _outproj_kernel(ya_ref, yf_ref, yb_ref, bonus_ref, g_ref, p_ref, pp_ref, pn_ref, x_ref, mod_ref,
                    wa_ref, wb_ref, wc_ref, pw_ref, psc_ref,
                    gng_ref, gnb_ref, ones_ref, l1g_ref, l1b_ref, wr_ref, br_ref,
                    x1_o, u2_o, ri_o, rw_o, cnt_o, cnt_ref, ext_ref, *, alpha, tiles_per_batch, seq):
    yc = _pool_tile(p_ref, pp_ref, pn_ref, pw_ref, psc_ref, ext_ref,
                    pl.program_id(0) % tiles_per_batch, tiles_per_batch, seq)
    m = mod_ref[0]
    ones = ones_ref[...]

    def headmean(x):
        return _dot_split(x, ones) * (1.0 / HEAD_DIM)

    ysum = yf_ref[...] + yb_ref[...]
    yc0 = ysum - headmean(ysum)
    yn = yc0 * lax.rsqrt(headmean(yc0 * yc0) + GN_EPS) * gng_ref[...] + gnb_ref[...]
    yb = (yn + bonus_ref[...]) * g_ref[...]
    mix = (jnp.dot(ya_ref[...].astype(BF16), wa_ref[...], preferred_element_type=F32)
           + jnp.dot(yb.astype(BF16), wb_ref[...], preferred_element_type=F32)
           + jnp.dot(yc.astype(BF16), wc_ref[...], preferred_element_type=F32))
    x1 = _ln(alpha * x_ref[...] + m[2:3] * mix) * l1g_ref[...] + l1b_ref[...]
    x1_o[...] = x1
    u2 = _ln(x1) * (1.0 + m[4:5]) + m[3:4]
    u2_o[...] = _pack_bf16_pairs(u2)

    u_hi, u_lo = _split_bf16(u2)
    hi_both = jnp.dot(u_hi, wr_ref[...], preferred_element_type=F32)
    lg = (hi_both[:, :LANES] + hi_both[:, LANES:]
          + jnp.dot(u_lo, wr_ref[:, :LANES], preferred_element_type=F32)) + br_ref[...]
    tm = lg.shape[0]
    lt = jnp.transpose(lg)[:ROUTER_ROWS]
    rowi = lax.broadcasted_iota(jnp.int32, lt.shape, 0)
    big = jnp.int32(1 << 20)
    gl = jnp.where(rowi < N_GROUPS, lt, -jnp.inf)
    gmax = jnp.max(gl, axis=0, keepdims=True)
    gidx = jnp.min(jnp.where(gl == gmax, rowi, big), axis=0, keepdims=True)
    pg_sel = 1.0 / jnp.sum(jnp.exp(gl - gmax), axis=0, keepdims=True)
    e_lo = N_GROUPS + gidx * EXPERTS_PER_GROUP
    el = jnp.where((rowi >= e_lo) & (rowi < e_lo + EXPERTS_PER_GROUP), lt, -jnp.inf)
    m1 = jnp.max(el, axis=0, keepdims=True)
    i1 = jnp.min(jnp.where(el == m1, rowi, big), axis=0, keepdims=True)
    el2 = jnp.where(rowi == i1, -jnp.inf, el)
    m2 = jnp.max(el2, axis=0, keepdims=True)
    i2 = jnp.min(jnp.where(el2 == m2, rowi, big), axis=0, keepdims=True)
    e21 = jnp.exp(m2 - m1)
    gate1 = pg_sel / (1.0 + e21)
    gate2 = pg_sel * e21 / (1.0 + e21)
    row128 = lax.broadcasted_iota(jnp.int32, (LANES, tm), 0)
    rw_o[...] = jnp.transpose(jnp.where(row128 == 0, gate1, jnp.where(row128 == 1, gate2, 0.0)))

    @pl.when(pl.program_id(0) == 0)
    def _():
        cnt_ref[...] = jnp.zeros_like(cnt_ref)

    earlier = (lax.broadcasted_iota(jnp.int32, (tm, tm), 0)
               < lax.broadcasted_iota(jnp.int32, (tm, tm), 1)).astype(BF16)
    oh1 = (rowi == i1).astype(F32)
    oh2 = (rowi == i2).astype(F32)
    run = cnt_ref[...]
    c1 = jnp.sum(oh1, axis=1, keepdims=True)
    before1 = run + jnp.dot(oh1.astype(BF16), earlier, preferred_element_type=F32)
    before2 = run + c1 + jnp.dot(oh2.astype(BF16), earlier, preferred_element_type=F32)
    rank1 = jnp.sum(oh1 * before1, axis=0, keepdims=True).astype(jnp.int32)
    rank2 = jnp.sum(oh2 * before2, axis=0, keepdims=True).astype(jnp.int32)
    total = run + c1 + jnp.sum(oh2, axis=1, keepdims=True)
    cnt_ref[...] = total
    cnt_o[...] = total
    row8 = lax.broadcasted_iota(jnp.int32, (SUBLANES, tm), 0)
    ri_o[...] = jnp.where(row8 == 0, i1 - N_GROUPS, jnp.where(row8 == 1, i2 - N_GROUPS,
                          jnp.where(row8 == 2, rank1, jnp.where(row8 == 3, rank2, 0))))


def _outproj(ya, yf, yb, bonus, g, praw, x2, modl, p, seq, tm, alpha):
    t, d = x2.shape
    tpb = seq // tm
    aw, bw, cw = ya.shape[1], bonus.shape[1], praw.shape[1]
    hb = tm // HALO
    nhb = t // HALO
    tok = lambda i: (i, 0)
    full2 = lambda i: (0, 0)
    kern = functools.partial(_outproj_kernel, alpha=alpha, tiles_per_batch=tpb, seq=seq)
    small = ["w_out_a", "w_out_b", "w_out_c", "pool_w", "pool_scale",
             "gn_gain", "gn_bias", "ones", "ln1_gain", "ln1_bias", "w_router", "b_router"]
    return pl.pallas_call(
        kern,
        grid=(t // tm,),
        in_specs=[pl.BlockSpec((tm, aw), tok),
                  pl.BlockSpec((tm, bw), tok), pl.BlockSpec((tm, bw), tok),
                  pl.BlockSpec((tm, bw), tok), pl.BlockSpec((tm, bw), tok),
                  pl.BlockSpec((tm, cw), tok),
                  pl.BlockSpec((HALO, cw), lambda i: (jnp.maximum(i * hb - 1, 0), 0)),
                  pl.BlockSpec((HALO, cw), lambda i: (jnp.minimum((i + 1) * hb, nhb - 1), 0)),
                  pl.BlockSpec((tm, d), tok),
                  pl.BlockSpec((1,) + modl.shape[1:], lambda i: (i // tpb, 0, 0))]
                 + [pl.BlockSpec(p[k].shape, functools.partial(lambda nd, i: (0,) * nd, p[k].ndim)) for k in small],
        out_specs=[pl.BlockSpec((tm, d), tok), pl.BlockSpec((tm, d // 2), tok),
                   pl.BlockSpec((SUBLANES, tm), lambda i: (0, i)), pl.BlockSpec((tm, LANES), tok),
                   pl.BlockSpec((ROUTER_ROWS, 1), full2)],
        out_shape=[jax.ShapeDtypeStruct((t, d), F32), jax.ShapeDtypeStruct((t, d // 2), jnp.uint32),
                   jax.ShapeDtypeStruct((SUBLANES, t), jnp.int32), jax.ShapeDtypeStruct((t, LANES), F32),
                   jax.ShapeDtypeStruct((ROUTER_ROWS, 1), F32)],
        scratch_shapes=[pltpu.VMEM((ROUTER_ROWS, 1), F32),pltpu.VMEM((len(POOL_WINDOWS), tm + 4 * HALO, cw), F32)],
        compiler_params=_cparams(1, "arbitrary"),
        name="outproj",
    )(ya, yf, yb, bonus, g, praw, praw, praw, x2, modl, *[p[k] for k in small])


def _dispatch(route_t, counts_rows, n_blocks, tm):
    counts = counts_rows[N_GROUPS:N_GROUPS + N_EXPERTS, 0].astype(jnp.int32)
    padded = ((counts + EXPERT_BLOCK - 1) // EXPERT_BLOCK) * EXPERT_BLOCK
    pends = jnp.cumsum(padded)
    pstarts = pends - padded
    e = route_t[:TOP_K]
    onehot = (e[..., None] == jnp.arange(N_EXPERTS, dtype=jnp.int32)).astype(F32)
    start_of = jnp.einsum("ktx,x->kt", onehot, pstarts.astype(F32), precision=HI)
    dest = route_t[TOP_K:2 * TOP_K] + start_of.astype(jnp.int32)
    t = route_t.shape[1]
    dest = dest.reshape(TOP_K, t // tm, tm).transpose(1, 0, 2).reshape(t // tm, 1, TOP_K * tm)
    block_start = jnp.arange(n_blocks, dtype=jnp.int32) * EXPERT_BLOCK
    block_e = jnp.minimum(jnp.sum((pends[None, :] <= block_start[:, None]).astype(jnp.int32), axis=1), N_EXPERTS - 1)
    meta = jnp.concatenate([block_e, (pends[-1] // EXPERT_BLOCK)[None]]).astype(jnp.int32)
    return dest, meta, pends.astype(jnp.int32)


def _scatter_rows_kernel(pends_ref, dest_ref, u_ref, xs_ref, zeros_ref, sem, zsem):
    tm = u_ref.shape[0] * SUBLANES

    @pl.when(pl.program_id(0) == 0)
    def _():
        zeros_ref[...] = jnp.zeros_like(zeros_ref)

        def tail_copy(e):
            tail = pl.ds(pl.multiple_of(pends_ref[e] - EXPERT_BLOCK, EXPERT_BLOCK), EXPERT_BLOCK)
            return pltpu.make_async_copy(zeros_ref, xs_ref.at[tail], zsem)

        def has_rows(e):
            return pends_ref[e] > (pends_ref[e - 1] if e > 0 else 0)

        def unused_copy(j):
            return pltpu.make_async_copy(zeros_ref, xs_ref.at[pl.ds(j * EXPERT_BLOCK, EXPERT_BLOCK)], zsem)

        def is_unused(j):
            return j * EXPERT_BLOCK >= pends_ref[N_EXPERTS - 1]

        n_blocks = xs_ref.shape[0] // EXPERT_BLOCK
        for e in range(N_EXPERTS):
            pl.when(has_rows(e))(lambda e=e: tail_copy(e).start())
        for j in range(n_blocks):
            pl.when(is_unused(j))(lambda j=j: unused_copy(j).start())
        for e in range(N_EXPERTS):
            pl.when(has_rows(e))(lambda e=e: tail_copy(e).wait())
        for j in range(n_blocks):
            pl.when(is_unused(j))(lambda j=j: unused_copy(j).wait())

    def issue(grp, carry):
        for j in range(SUBLANES):
            for k in range(TOP_K):
                dst = dest_ref[0, 0, k * tm + SUBLANES * grp + j]
                pltpu.make_async_copy(u_ref.at[grp, pl.ds(j, 1)], xs_ref.at[pl.ds(dst, 1)], sem).start()
        return carry

    lax.fori_loop(0, tm // SUBLANES, issue, 0)
    rows = pl.ds(0, TOP_K * tm)
    pltpu.make_async_copy(xs_ref.at[rows], xs_ref.at[rows], sem).wait()


def _scatter_rows(pends, u2, dest3, total, tm):
    t, d = u2.shape
    grid_spec = pltpu.PrefetchScalarGridSpec(
        num_scalar_prefetch=1,
        grid=(t // tm,),
        in_specs=[pl.BlockSpec((1, 1, TOP_K * tm), lambda i, p: (i, 0, 0), memory_space=pltpu.SMEM),
                  pl.BlockSpec((tm // SUBLANES, SUBLANES, d), lambda i, p: (i, 0, 0))],
        out_specs=pl.BlockSpec(memory_space=pl.ANY),
        scratch_shapes=[pltpu.VMEM((EXPERT_BLOCK, d), u2.dtype), pltpu.SemaphoreType.DMA(()),
                        pltpu.SemaphoreType.DMA(())],
    )
    return pl.pallas_call(
        _scatter_rows_kernel,
        grid_spec=grid_spec,
        out_shape=jax.ShapeDtypeStruct((total, d), u2.dtype),
        compiler_params=_cparams(1, "arbitrary"),
        name="scatter_rows",
    )(pends, dest3, u2.reshape(t // SUBLANES, SUBLANES, d))


def _experts_kernel(meta_ref, xs_ref, wg_ref, wu_ref, wd_ref, o_ref, wg_b, wu_b, wd_b):
    i = pl.program_id(0)
    n_used = meta_ref[pl.num_programs(0)]

    @pl.when((i == 0) | (meta_ref[i] != meta_ref[jnp.maximum(i - 1, 0)]))
    def _():
        wg_b[...] = wg_ref[0, 0].astype(BF16)
        wu_b[...] = wu_ref[0, 0].astype(BF16)
        wd_b[...] = wd_ref[0, 0].astype(BF16)

    @pl.when(i < n_used)
    def _():
        xb = _unpack_bf16_pairs(xs_ref[...])
        gate = jnp.dot(xb, wg_b[...], preferred_element_type=F32)
        up = jnp.dot(xb, wu_b[...], preferred_element_type=F32)
        hb = gate * _sigmoid(gate) * up
        o_ref[...] = _pack_bf16_pairs(jnp.dot(hb.astype(BF16), wd_b[...], preferred_element_type=F32))

    @pl.when(i >= n_used)
    def _():
        o_ref[...] = jnp.zeros_like(o_ref)


def _experts(meta, xs, wg, wu, wd, layer):
    total, dp = xs.shape
    nb = total // EXPERT_BLOCK
    d, de = wg.shape[2:]
    grid_spec = pltpu.PrefetchScalarGridSpec(
        num_scalar_prefetch=1,
        grid=(nb,),
        in_specs=[pl.BlockSpec((EXPERT_BLOCK, dp), lambda i, m: (jnp.minimum(i, m[nb] - 1), 0)),
                  pl.BlockSpec((1, 1, d, de), lambda i, m: (layer, m[i], 0, 0)),
                  pl.BlockSpec((1, 1, d, de), lambda i, m: (layer, m[i], 0, 0)),
                  pl.BlockSpec((1, 1, de, d), lambda i, m: (layer, m[i], 0, 0))],
        out_specs=pl.BlockSpec((EXPERT_BLOCK, dp), lambda i, m: (i, 0)),
        scratch_shapes=[pltpu.VMEM((d, de), BF16), pltpu.VMEM((d, de), BF16), pltpu.VMEM((de, d), BF16)],
    )
    return pl.pallas_call(
        _experts_kernel,
        grid_spec=grid_spec,
        out_shape=jax.ShapeDtypeStruct((total, dp), xs.dtype),
        compiler_params=_cparams(1, "arbitrary"),
        name="experts",
    )(meta, xs, wg, wu, wd)


def _final_kernel(dcur_ref, dnext_ref, x1_ref, rw_ref, mod_ref, g_ref, b_ref, ys_ref, o_ref, ybuf, sem, *, alpha):
    i = pl.program_id(0)
    tm = x1_ref.shape[0]
    slot = i % 2

    def gather(d_ref, s):
        def issue(grp, carry):
            for j in range(SUBLANES):
                for k in range(TOP_K):
                    src = d_ref[0, 0, k * tm + SUBLANES * grp + j]
                    pltpu.make_async_copy(ys_ref.at[pl.ds(src, 1)], ybuf.at[s, k, grp, pl.ds(j, 1)],
                                          sem.at[s]).start()
            return carry

        lax.fori_loop(0, tm // SUBLANES, issue, 0)

    @pl.when(i == 0)
    def _():
        gather(dcur_ref, 0)

    @pl.when(i + 1 < pl.num_programs(0))
    def _():
        gather(dnext_ref, 1 - slot)

    pltpu.make_async_copy(ybuf.at[slot], ybuf.at[slot], sem.at[slot]).wait()
    m = mod_ref[0]
    rw = rw_ref[...]
    dp = ybuf.shape[-1]
    y1 = _unpack_bf16_pairs(ybuf[slot, 0].reshape(tm, dp)).astype(F32)
    y2 = _unpack_bf16_pairs(ybuf[slot, 1].reshape(tm, dp)).astype(F32)
    f = rw[:, 0:1] * y1 + rw[:, 1:2] * y2
    o_ref[...] = _ln(alpha * x1_ref[...] + m[5:6] * f) * g_ref[...] + b_ref[...]


def _final(x1, ysorted, dest3, rw, modl, gain, bias, seq, tm, alpha):
    t, d = x1.shape
    tpb = seq // tm
    n_tiles = t // tm
    tok = lambda i: (i, 0)
    kern = functools.partial(_final_kernel, alpha=alpha)
    dspec = lambda f: pl.BlockSpec((1, 1, TOP_K * tm), f, memory_space=pltpu.SMEM)
    return pl.pallas_call(
        kern,
        grid=(n_tiles,),
        in_specs=[dspec(lambda i: (i, 0, 0)), dspec(lambda i: (jnp.minimum(i + 1, n_tiles - 1), 0, 0)),
                  pl.BlockSpec((tm, d), tok), pl.BlockSpec((tm, LANES), tok),
                  pl.BlockSpec((1,) + modl.shape[1:], lambda i: (i // tpb, 0, 0)),
                  pl.BlockSpec(gain.shape, lambda i: (0, 0)), pl.BlockSpec(bias.shape, lambda i: (0, 0)),
                  pl.BlockSpec(memory_space=pl.ANY)],
        out_specs=pl.BlockSpec((tm, d), tok),
        out_shape=jax.ShapeDtypeStruct((t, d), F32),
        scratch_shapes=[pltpu.VMEM((2, TOP_K, tm // SUBLANES, SUBLANES, ysorted.shape[1]), ysorted.dtype),
                        pltpu.SemaphoreType.DMA((2,))],
        compiler_params=_cparams(1, "arbitrary"),
        name="final_ln",
    )(dest3, dest3, x1, rw, modl, gain, bias, ysorted)


def _block_diag(blocks):
    n, a, b = blocks.shape
    out = jnp.zeros((n * a, n * b), blocks.dtype)
    for i in range(n):
        out = out.at[i * a:(i + 1) * a, i * b:(i + 1) * b].set(blocks[i])
    return out


def _pad_rows(w, lo, total):
    return jnp.pad(w, ((lo, total - lo - w.shape[0]), (0, 0)))


def kernel(x, c, w_mod, b_mod, w_in, na_rpb, rw_conv, rw_w0, rw_w_up, rw_a0, rw_a_up, rw_g_up, rw_k_k, rw_k_a, rw_r_k, rw_gn_gain, rw_gn_bias, pool_w, pool_scale, w_out, ln1_gain, ln1_bias, ln2_gain, ln2_bias, moe_w_group, moe_b_group, moe_w_expert, moe_b_expert, moe_w_gate, moe_w_up, moe_w_down):
    batch, seq, d = x.shape
    depth = w_mod.shape[0]
    t = batch * seq
    a_w = na_rpb.shape[1] * HEAD_DIM
    b_w = rw_w0.shape[-1]
    c_w = pool_scale.shape[-1]
    lr_w = R_W + R_A + R_G
    alpha = (2 * depth) ** 0.25
    tm = tm_in = min(TOKEN_TILE, seq)
    assert seq % tm == 0 and seq % SCAN_CHUNK == 0 and seq % GRID_W == 0 and lr_w == LANES

    mod = _modulation(c, w_mod, b_mod)
    ones_blk = _block_diag(jnp.ones((b_w // HEAD_DIM, HEAD_DIM, HEAD_DIM), BF16))
    row = lambda v: v.reshape(1, -1)

    x2 = x.reshape(t, d)
    for l in range(depth):
        modl = mod[l]
        prep_params = {
            "conv": rw_conv[l], "w0": rw_w0[l], "a0": rw_a0[l],
            "w_up": jnp.stack([_pad_rows(rw_w_up[l, dd], 0, lr_w) for dd in range(2)]).astype(BF16),
            "a_up": jnp.stack([_pad_rows(rw_a_up[l, dd], R_W, lr_w) for dd in range(2)]).astype(BF16),
            "g_up": _pad_rows(rw_g_up[l], R_W + R_A, lr_w).astype(BF16),
            "k_k": row(rw_k_k[l]), "k_a": row(rw_k_a[l]), "r_k": row(rw_r_k[l]), "ones": ones_blk,
        }
        qkv, praw, r, v, nkk, lw, bb, kd, bonus, g = _inproj(x2, modl, w_in[l].astype(BF16), prep_params, seq, tm_in,
                                                             3 * a_w, 3 * b_w, lr_w, c_w)
        ya = _natten(qkv, _na_bias_table(na_rpb[l]), batch, seq, a_w)
        yf, yb = _rwkv_scan(r, v, nkk, lw, bb, kd, batch, seq)
        lane_pad = LANES - N_GROUPS - N_EXPERTS
        w_router = jnp.concatenate([moe_w_group[l], moe_w_expert[l], jnp.zeros((d, lane_pad), F32)], axis=1)
        b_router = jnp.concatenate([moe_b_group[l], moe_b_expert[l], jnp.zeros((lane_pad,), F32)]).reshape(1, LANES)
        wo = w_out[l].astype(BF16)
        out_params = {
            "w_out_a": wo[:a_w], "w_out_b": wo[a_w:a_w + b_w], "w_out_c": wo[a_w + b_w:],
            "pool_w": _block_diag(pool_w[l]), "pool_scale": row(pool_scale[l]),
            "gn_gain": row(rw_gn_gain[l]), "gn_bias": row(rw_gn_bias[l]), "ones": ones_blk,
            "ln1_gain": row(ln1_gain[l]), "ln1_bias": row(ln1_bias[l]),
            "w_router": jnp.concatenate(_split_bf16(w_router), axis=1), "b_router": b_router,
        }
        x1, u2, route_i, route_w, counts = _outproj(ya, yf, yb, bonus, g, praw, x2, modl, out_params, seq, tm, alpha)
        n_blocks = -(-(t * TOP_K) // EXPERT_BLOCK) + N_EXPERTS
        dest3, meta, pends = _dispatch(route_i, counts, n_blocks, tm)
        xs = _scatter_rows(pends, u2, dest3, n_blocks * EXPERT_BLOCK, tm)
        ysorted = _experts(meta, xs, moe_w_gate, moe_w_up, moe_w_down, l)
        x2 = _final(x1, ysorted, dest3, route_w, modl, row(ln2_gain[l]), row(ln2_bias[l]), seq, tm, alpha)
    return x2.reshape(batch, seq, d)
```

```python
import functools
import math

import jax
import jax.numpy as jnp
import numpy as np
from jax import lax
from jax.experimental import pallas as pl
from jax.experimental.pallas import tpu as pltpu

F32 = jnp.float32
BF16 = jnp.bfloat16
HI = lax.Precision.HIGHEST

GRID_W = 64
HEAD_DIM = 64
NA_KH = 8
NA_KW = 16
POOL_WINDOWS = (2, 4, 8, 16)
R_W = 32
R_A = 32
R_G = 64
DECAY_SCALE = math.exp(-0.5)
GN_EPS = 64e-5
N_GROUPS = 4
EXPERTS_PER_GROUP = 8
N_EXPERTS = N_GROUPS * EXPERTS_PER_GROUP
TOP_K = 2
ROUTER_ROWS = -(-(N_GROUPS + N_EXPERTS) // 8) * 8
EXPERT_BLOCK = 512
LN_EPS = 1e-5
NEG_INF = -1e30

TOKEN_TILE = 512
NA_ROWS_PER_STEP = 8
NA_SCORES_AHEAD = 1
SCAN_CHUNK = 64
SCAN_CHUNKS_PER_STEP = 4
SUBLANES = 8
HALO = 8
LANES = 128
VMEM_LIMIT = 52 * 1024 * 1024


def _ln(x):
    mu = jnp.mean(x, axis=-1, keepdims=True)
    xc = x - mu
    var = jnp.mean(xc * xc, axis=-1, keepdims=True)
    return xc * lax.rsqrt(var + LN_EPS)


def _sigmoid(x):
    return 1.0 / (1.0 + jnp.exp(-x))


def _split_bf16(x):
    hi = x.astype(BF16)
    return hi, (x - hi.astype(F32)).astype(BF16)


def _dot_split(x, w_exact):
    hi, lo = _split_bf16(x)
    return jnp.dot(hi, w_exact, preferred_element_type=F32) + jnp.dot(lo, w_exact, preferred_element_type=F32)


def _pack_bf16_pairs(x):
    h = x.shape[1] // 2
    lo = lax.bitcast_convert_type(x[:, :h].astype(BF16).astype(F32), jnp.uint32)
    hi = lax.bitcast_convert_type(x[:, h:].astype(BF16).astype(F32), jnp.uint32)
    return (lo >> 16) | hi


def _unpack_bf16_pairs(w):
    lo = lax.bitcast_convert_type(w << 16, F32).astype(BF16)
    hi = lax.bitcast_convert_type(w & jnp.uint32(0xFFFF0000), F32).astype(BF16)
    return jnp.concatenate([lo, hi], axis=1)


def _cparams(n_axes, semantics="parallel"):
    return pltpu.CompilerParams(dimension_semantics=(semantics,) * n_axes, vmem_limit_bytes=VMEM_LIMIT)


def _mod_kernel(c_ref, w_ref, b_ref, o_ref):
    c = c_ref[...]
    s = c * _sigmoid(c)
    o_ref[0] = jnp.dot(s, w_ref[0], precision=HI, preferred_element_type=F32) + b_ref[0]


def _modulation(c, w_mod, b_mod):
    n_layers, d, d6 = w_mod.shape
    b = c.shape[0]
    bp = -(-b // 8) * 8
    cp = jnp.zeros((bp, d), F32).at[:b].set(c)
    out = pl.pallas_call(
        _mod_kernel,
        grid=(n_layers, d6 // d),
        in_specs=[pl.BlockSpec((bp, d), lambda l, j: (0, 0)),
                  pl.BlockSpec((1, d, d), lambda l, j: (l, 0, j)),
                  pl.BlockSpec((1, 1, d), lambda l, j: (l, 0, j))],
        out_specs=pl.BlockSpec((1, bp, d), lambda l, j: (l, 0, j)),
        out_shape=jax.ShapeDtypeStruct((n_layers, bp, d6), F32),
        compiler_params=_cparams(2),
        name="modulation",
    )(cp, w_mod, b_mod.reshape(n_layers, 1, d6))
    return out[:, :b].reshape(n_layers, b, d6 // d, d)


def _inproj_kernel(x_ref, xp_ref, xn_ref, mod_ref, wr_ref, wo_ref, cw_ref, w0_ref, wup_ref, a0_ref, aup_ref, gup_ref,
                   kk_ref, ka_ref, rk_ref, ones_ref,
                   qkv_o, pool_o, r_o, v_o, nkk_o, lw_o, b_o, kd_o, bonus_o, g_o,
                   *, a3, b3, tiles_per_batch):
    m = mod_ref[0]
    tm = x_ref.shape[0]
    tb = pl.program_id(0) % tiles_per_batch
    xe = jnp.concatenate([xp_ref[...], x_ref[...], xn_ref[...]], axis=0)
    u = (_ln(xe) * (1.0 + m[1:2]) + m[0:1]).astype(BF16)
    h = jnp.dot(u, wr_ref[...], preferred_element_type=F32)
    ho = jnp.dot(u[HALO:HALO + tm], wo_ref[...], preferred_element_type=F32)
    hm = h[HALO:HALO + tm]
    prev = jnp.where(tb == 0, 0.0, h[HALO - 1:HALO, :b3])
    nxt = jnp.where(tb == tiles_per_batch - 1, 0.0, h[HALO + tm:HALO + tm + 1, :b3])
    _rwkv_prep_tile(hm[:, :b3], prev, nxt, hm[:, b3:],
                    cw_ref, w0_ref, wup_ref, a0_ref, aup_ref, gup_ref, kk_ref, ka_ref, rk_ref, ones_ref,
                    r_o, v_o, nkk_o, lw_o, b_o, kd_o, bonus_o, g_o)
    qkv_o[...] = ho[:, :a3].astype(BF16)
    pool_o[...] = ho[:, a3:]


def _inproj(x2, modl, w_in_bf, p, seq, tm, a3, b3, lr_w, c_w):
    t, d = x2.shape
    tpb = seq // tm
    hb = tm // HALO
    nhb = t // HALO
    width = b3 // 3
    kern = functools.partial(_inproj_kernel, a3=a3, b3=b3, tiles_per_batch=tpb)
    w_rwkv = w_in_bf[:, a3:a3 + b3 + lr_w]
    w_other = jnp.concatenate([w_in_bf[:, :a3], w_in_bf[:, a3 + b3 + lr_w:]], axis=1)
    tok = lambda i: (i, 0)
    dtok = lambda i: (0, i, 0)
    names = ["conv", "w0", "w_up", "a0", "a_up", "g_up", "k_k", "k_a", "r_k", "ones"]
    tw = jax.ShapeDtypeStruct((t, width), F32)
    dtw = jax.ShapeDtypeStruct((2, t, width), F32)
    return pl.pallas_call(
        kern,
        grid=(t // tm,),
        in_specs=[pl.BlockSpec((tm, d), tok),
                  pl.BlockSpec((HALO, d), lambda i: (jnp.maximum(i * hb - 1, 0), 0)),
                  pl.BlockSpec((HALO, d), lambda i: (jnp.minimum((i + 1) * hb, nhb - 1), 0)),
                  pl.BlockSpec((1,) + modl.shape[1:], lambda i: (i // tpb, 0, 0)),
                  pl.BlockSpec(w_rwkv.shape, lambda i: (0, 0)),
                  pl.BlockSpec(w_other.shape, lambda i: (0, 0))]
                 + [pl.BlockSpec(p[k].shape, functools.partial(lambda nd, i: (0,) * nd, p[k].ndim)) for k in names],
        out_specs=[pl.BlockSpec((tm, a3), tok), pl.BlockSpec((tm, c_w), tok),
                   pl.BlockSpec((tm, width), tok), pl.BlockSpec((tm, width), tok), pl.BlockSpec((tm, width), tok),
                   pl.BlockSpec((2, tm, width), dtok), pl.BlockSpec((2, tm, width), dtok),
                   pl.BlockSpec((2, tm, width), dtok),
                   pl.BlockSpec((tm, width), tok), pl.BlockSpec((tm, width), tok)],
        out_shape=[jax.ShapeDtypeStruct((t, a3), BF16), jax.ShapeDtypeStruct((t, c_w), F32),
                   tw, tw, tw, dtw, dtw, dtw, tw, tw],
        compiler_params=_cparams(1),
        name="inproj",
    )(x2, x2, x2, modl, w_rwkv, w_other, *[p[k] for k in names])


def _na_bias_table(rpb):
    col = np.arange(GRID_W)
    cstart = np.clip(col - NA_KW // 2, 0, GRID_W - NA_KW)
    in_win = (col[None, :] >= cstart[:, None]) & (col[None, :] < cstart[:, None] + NA_KW)
    dc = np.clip(col[None, :] - col[:, None], -(NA_KW - 1), NA_KW - 1) + (NA_KW - 1)
    pick = (dc[None] == np.arange(2 * NA_KW - 1)[:, None, None]).astype(np.float32)
    cols = jnp.einsum("hrc,cqk->hrqk", rpb.astype(F32), pick, precision=HI)
    cols = jnp.where(in_win, cols, NEG_INF)
    b = jnp.stack([cols[:, NA_KH - 1 - o:2 * NA_KH - 1 - o] for o in range(NA_KH)])
    h = rpb.shape[0]
    return jnp.transpose(b, (0, 1, 3, 2, 4)).reshape(NA_KH, h * GRID_W, NA_KH * GRID_W)


def _natten_kernel(q_ref, k_ref, v_ref, bias_ref, o_ref, *, rows, heads):
    width = q_ref.shape[1]
    nk = NA_KH * GRID_W
    head_of_lane = lax.broadcasted_iota(jnp.int32, (heads * GRID_W, width), 1) // HEAD_DIM
    head_of_row = lax.broadcasted_iota(jnp.int32, (heads * GRID_W, width), 0) // GRID_W
    own = head_of_lane == head_of_row
    def scores(j):
        r = pl.program_id(1) * NA_ROWS_PER_STEP + j
        rstart = jnp.clip(r - NA_KH // 2, 0, rows - NA_KH)
        start = pl.multiple_of(rstart * GRID_W, GRID_W)
        q = q_ref[j * GRID_W:(j + 1) * GRID_W, :]
        qs = jnp.where(own, jnp.concatenate([q] * heads, axis=0), jnp.zeros((), q.dtype))
        s = lax.dot_general(qs, k_ref[pl.ds(start, nk), :], (((1,), (1,)), ((), ())), preferred_element_type=F32)
        return s * (HEAD_DIM ** -0.5) + bias_ref[r - rstart], start

    ahead = [scores(j) for j in range(min(NA_SCORES_AHEAD, NA_ROWS_PER_STEP))]
    for j in range(NA_ROWS_PER_STEP):
        s, start = ahead.pop(0)
        if j + NA_SCORES_AHEAD < NA_ROWS_PER_STEP:
            ahead.append(scores(j + NA_SCORES_AHEAD))
        mx = jnp.max(s, axis=-1, keepdims=True)
        p = jnp.exp(s - mx)
        den = jnp.sum(p, axis=-1, keepdims=True)
        pv = jnp.dot(p.astype(BF16), v_ref[pl.ds(start, nk), :], preferred_element_type=F32)
        o = jnp.where(own, pv / den, 0.0)
        acc = o[0:GRID_W]
        for h in range(1, heads):
            acc = acc + o[h * GRID_W:(h + 1) * GRID_W]
        o_ref[j * GRID_W:(j + 1) * GRID_W, :] = acc.astype(o_ref.dtype)


def _natten(qkv, bias_tab, batch, seq, width):
    rows = seq // GRID_W
    assert rows >= NA_KH
    heads = width // HEAD_DIM
    steps = rows // NA_ROWS_PER_STEP
    assert steps * NA_ROWS_PER_STEP == rows
    tq = NA_ROWS_PER_STEP * GRID_W
    kern = functools.partial(_natten_kernel, rows=rows, heads=heads)
    return pl.pallas_call(
        kern,
        grid=(batch, steps),
        in_specs=[pl.BlockSpec((tq, width), lambda b, r: (b * steps + r, 0)),
                  pl.BlockSpec((seq, width), lambda b, r: (b, 1)),
                  pl.BlockSpec((seq, width), lambda b, r: (b, 2)),
                  pl.BlockSpec(bias_tab.shape, lambda b, r: (0, 0, 0))],
        out_specs=pl.BlockSpec((tq, width), lambda b, r: (b * steps + r, 0)),
        out_shape=jax.ShapeDtypeStruct((batch * seq, width), BF16),
        compiler_params=_cparams(2),
        name="natten",
    )(qkv, qkv, qkv, bias_tab)


def _rwkv_prep_tile(z, prev, nxt, lr, cw_ref, w0_ref, wup_ref, a0_ref, aup_ref, gup_ref, kk_ref, ka_ref, rk_ref, ones_ref,
                    r_o, v_o, nkk_o, lw_o, b_o, kd_o, bonus_o, g_o):
    tm = z.shape[0]
    width = z.shape[1] // 3
    row = lax.broadcasted_iota(jnp.int32, z.shape, 0)
    zm1 = jnp.where(row == 0, prev, pltpu.roll(z, 1, 0))
    zp1 = jnp.where(row == tm - 1, nxt, pltpu.roll(z, tm - 1, 0))
    rkv = zm1 * cw_ref[0:1, :] + z * cw_ref[1:2, :] + zp1 * cw_ref[2:3, :]
    r = rkv[:, :width]
    k = rkv[:, width:2 * width]
    v = rkv[:, 2 * width:]
    th = jnp.tanh(lr)
    sg = _sigmoid(lr)
    ones = ones_ref[...]

    def headsum(x):
        return _dot_split(x, ones)

    kk = k * kk_ref[...]
    kk = kk * lax.rsqrt(jnp.maximum(headsum(kk * kk), 1e-24))
    g_o[...] = jnp.dot(sg.astype(BF16), gup_ref[...], preferred_element_type=F32)
    r_o[...] = r
    v_o[...] = v
    nkk_o[...] = -kk
    kd_sum = jnp.zeros_like(r)
    th_b = th.astype(BF16)
    lr_b = lr.astype(BF16)
    for d in range(2):
        wl = jnp.dot(th_b, wup_ref[d], preferred_element_type=F32) + w0_ref[d:d + 1, :]
        lw_o[d] = -DECAY_SCALE * _sigmoid(wl)
        a = _sigmoid(jnp.dot(lr_b, aup_ref[d], preferred_element_type=F32) + a0_ref[d:d + 1, :])
        kd = k * (1.0 + (a - 1.0) * ka_ref[...])
        kd_o[d] = kd
        b_o[d] = kk * a
        kd_sum = kd_sum + kd
    bonus_o[...] = headsum(r * kd_sum * rk_ref[...]) * v


def _dot_nt(a, b):
    return lax.dot_general(a, b, (((1,), (1,)), ((), ())), preferred_element_type=F32)


def _dot_tn(a, b):
    return lax.dot_general(a, b, (((0,), (0,)), ((), ())), preferred_element_type=F32)


def _rwkv_scan_kernel(rf_ref, vf_ref, nf_ref, rb_ref, vb_ref, nb_ref, lwf_ref, bf_ref, kf_ref, lwb_ref, bb_ref, kb_ref,
                      yf_ref, yb_ref, s_ref, *, heads, batch):
    @pl.when(pl.program_id(0) == 0)
    def _():
        s_ref[...] = jnp.zeros_like(s_ref)

    n = SCAN_CHUNK
    pair_w = 2 * HEAD_DIM
    row = lax.broadcasted_iota(jnp.int32, (n, pair_w), 0)
    lane = lax.broadcasted_iota(jnp.int32, (n, pair_w), 1)
    col = lane & (HEAD_DIM - 1)
    even = lane < HEAD_DIM
    levels = n.bit_length()
    same = [(row >> k) == (col >> k) for k in range(levels)]
    eye = same[0].astype(F32)
    level_masks = [same[sh + 1] & jnp.logical_not(same[sh]) for sh in range(1, levels - 1)]

    def blockdiag(x2):
        xb = x2.astype(BF16)
        zero = jnp.zeros((), BF16)
        return jnp.concatenate([jnp.where(even, xb, zero), jnp.where(even, zero, xb)], axis=0)

    def mm(x2, y2):
        return jnp.dot(x2.astype(BF16), blockdiag(y2), preferred_element_type=F32)

    def mm_nt(x2, y2):
        return _dot_nt(x2.astype(BF16), blockdiag(y2))

    dirs = ((rf_ref, vf_ref, nf_ref, lwf_ref, bf_ref, kf_ref, yf_ref),
            (rb_ref, vb_ref, nb_ref, lwb_ref, bb_ref, kb_ref, yb_ref))
    def build(q):
        chains = []
        for d, (r_ref, v_ref, n_ref, lw_ref, b_ref, k_ref, y_ref) in enumerate(dirs):
            sub = q if d == 0 else SCAN_CHUNKS_PER_STEP - 1 - q
            rs = slice(sub * n, (sub + 1) * n)
            order = row - col if d == 0 else col - row
            strict = order > 0
            incl = order >= 0
            incl_b = jnp.where(incl[:, :n], 1.0, 0.0).astype(BF16)
            for bi in range(batch):
                lw = lw_ref[0, bi, rs]
                lw_hi, lw_mid = _split_bf16(lw)
                lw_lo = (lw - lw_hi.astype(F32) - lw_mid.astype(F32)).astype(BF16)
                g_inc = ((jnp.dot(incl_b, lw_lo, preferred_element_type=F32)
                          + jnp.dot(incl_b, lw_mid, preferred_element_type=F32))
                         + jnp.dot(incl_b, lw_hi, preferred_element_type=F32))
                g_tot = jnp.sum(lw, axis=0, keepdims=True)
                e_neg = jnp.exp(-g_inc)
                e_end = jnp.exp(g_tot - g_inc)
                decay = jnp.exp(g_tot)
                a_t = n_ref[bi, rs] * jnp.exp(g_inc - lw)
                r_t = r_ref[bi, rs] * jnp.exp(g_inc)
                bb = b_ref[0, bi, rs]
                kd = k_ref[0, bi, rs]
                b_t = bb * e_neg
                k_t = kd * e_neg
                ar_t = jnp.concatenate([a_t, r_t], axis=0).astype(BF16)
                bk_h = jnp.concatenate([bb * e_end, kd * e_end], axis=0).astype(BF16)
                v = v_ref[bi, rs]
                for p in range(heads // 2):
                    sl = slice(p * pair_w, (p + 1) * pair_w)
                    chains.append(dict(strict=strict, incl=incl, sl=sl, rs=rs, bi=bi, y_ref=y_ref,
                                       si=(d * batch + bi) * (heads // 2) + p, decay=decay[:, sl],
                                       ar=ar_t[:, sl], b=b_t[:, sl], k=k_t[:, sl], bk_h=bk_h[:, sl], v=v[:, sl]))
        return chains

    def state_free_stages(chains):
        def products():
            for ch in chains:
                pb = mm_nt(ch["ar"], ch["b"])
                pk = mm_nt(ch["ar"], ch["k"])
                ch["l_ab"] = jnp.where(ch["strict"], pb[:n], 0.0)
                ch["m_rb"] = jnp.where(ch["incl"], pb[n:], 0.0)
                ch["l_ak"] = jnp.where(ch["strict"], pk[:n], 0.0)
                ch["m_rk"] = jnp.where(ch["incl"], pk[n:], 0.0)
                ch["t"] = eye + jnp.where(same[1], ch["l_ab"], 0.0)

        def level_left(mask):
            for ch in chains:
                ch["tc"] = mm(ch["t"], jnp.where(mask, ch["l_ab"], 0.0))

        def level_right():
            for ch in chains:
                ch["t"] = ch["t"] + mm(ch["tc"], ch["t"])

        def values():
            for ch in chains:
                ch["kv"] = mm(jnp.concatenate([ch["l_ak"], ch["m_rk"]], axis=0), ch["v"])

        stages = [products]
        for mask in level_masks:
            stages += [functools.partial(level_left, mask), level_right]
        return stages + [values]

    def state_stages(chains):
        def read():
            for ch in chains:
                ch["s0"] = s_ref[ch["si"]]
                ch["x"] = mm_nt(ch["ar"], ch["s0"])

        def solve():
            for ch in chains:
                ch["u"] = mm(ch["t"], ch["x"][:n] + ch["kv"][:n])

        def emit():
            for ch in chains:
                y = ch["x"][n:] + mm(ch["m_rb"], ch["u"]) + ch["kv"][n:]
                ch["y_ref"][ch["bi"], ch["rs"], ch["sl"]] = y

        def write():
            for ch in chains:
                uv = jnp.concatenate([ch["u"], ch["v"]], axis=0).astype(BF16)
                full = _dot_tn(uv, ch["bk_h"])
                s_ref[ch["si"]] = ch["s0"] * ch["decay"] + jnp.where(even, full[:HEAD_DIM], full[HEAD_DIM:])

        return [read, solve, emit, write]

    pending = []
    for q in range(SCAN_CHUNKS_PER_STEP):
        chains = build(q)
        free = state_free_stages(chains)
        if pending:
            share = -(-len(free) // len(pending))
            for i, carried in enumerate(pending):
                carried()
                for stage in free[i * share:(i + 1) * share]:
                    stage()
        else:
            for stage in free:
                stage()
        pending = state_stages(chains)
    for carried in pending:
        carried()


def _rwkv_scan(r, v, nkk, lw, b, kd, batch, seq):
    t, width = r.shape
    heads = width // HEAD_DIM
    n = SCAN_CHUNK * SCAN_CHUNKS_PER_STEP
    nc = seq // n
    assert nc * n == seq
    r3, v3, n3 = (z.reshape(batch, seq, width) for z in (r, v, nkk))
    lw4, b4, k4 = (z.reshape(2, batch, seq, width) for z in (lw, b, kd))
    fwd = pl.BlockSpec((batch, n, width), lambda c: (0, c, 0))
    bwd = pl.BlockSpec((batch, n, width), lambda c: (0, nc - 1 - c, 0))
    fwd_d = pl.BlockSpec((1, batch, n, width), lambda c: (0, 0, c, 0))
    bwd_d = pl.BlockSpec((1, batch, n, width), lambda c: (1, 0, nc - 1 - c, 0))
    kern = functools.partial(_rwkv_scan_kernel, heads=heads, batch=batch)
    yf, yb = pl.pallas_call(
        kern,
        grid=(nc,),
        in_specs=[fwd, fwd, fwd, bwd, bwd, bwd, fwd_d, fwd_d, fwd_d, bwd_d, bwd_d, bwd_d],
        out_specs=[fwd, bwd],
        out_shape=[jax.ShapeDtypeStruct((batch, seq, width), F32)] * 2,
        scratch_shapes=[pltpu.VMEM((batch * heads, HEAD_DIM, 2 * HEAD_DIM), F32)],
        compiler_params=_cparams(1, "arbitrary"),
        name="rwkv_scan",
    )(r3, v3, n3, r3, v3, n3, lw4, b4, k4, lw4, b4, k4)
    return yf.reshape(t, width), yb.reshape(t, width)


def _pool_tile(p_ref, pp_ref, pn_ref, w_ref, sc_ref, ext_ref, tb, tiles_per_batch, seq):
    p = p_ref[...]
    tm, width = p.shape
    assert all(w == 2 << i for i, w in enumerate(POOL_WINDOWS)) and POOL_WINDOWS[-1] <= 2 * HALO
    n = tm + 2 * HALO
    pad = jnp.zeros((HALO, width), F32)
    for k in range(len(POOL_WINDOWS)):
        ext_ref[k, 0:HALO, :] = pad
        ext_ref[k, HALO + n:2 * HALO + n, :] = pad
    ext_ref[0, HALO:2 * HALO, :] = jnp.where(tb == 0, 0.0, pp_ref[...])
    ext_ref[0, 2 * HALO:2 * HALO + tm, :] = p
    ext_ref[0, 2 * HALO + tm:HALO + n, :] = jnp.where(tb == tiles_per_batch - 1, 0.0, pn_ref[...])

    def rows(k, first, count):
        return ext_ref[k, 2 * HALO + first:2 * HALO + first + count, :]

    ext_ref[1, HALO:HALO + n, :] = rows(0, -HALO - 1, n) + rows(0, -HALO, n)
    for k in range(1, len(POOL_WINDOWS) - 1):
        q = POOL_WINDOWS[k - 1] // 2
        ext_ref[k + 1, HALO:HALO + n, :] = rows(k, -HALO - q, n) + rows(k, -HALO + q, n)
    q = POOL_WINDOWS[-2] // 2
    sums = [rows(k + 1, 0, tm) for k in range(len(POOL_WINDOWS) - 1)]
    sums.append(rows(len(POOL_WINDOWS) - 1, -q, tm) + rows(len(POOL_WINDOWS) - 1, q, tm))

    t = tb * tm + lax.broadcasted_iota(jnp.int32, (tm, width), 0)
    grp = lax.broadcasted_iota(jnp.int32, (tm, width), 1) // (width // len(POOL_WINDOWS))
    pooled = jnp.zeros_like(p)
    for gi, win in enumerate(POOL_WINDOWS):
        half = win // 2
        lo = jnp.clip(t - half, 0, seq - 1)
        hi = jnp.clip(t + half - 1, 0, seq - 1)
        cnt = (hi - lo + 1).astype(F32)
        pooled = jnp.where(grp == gi, sums[gi] / cnt, pooled)
    pooled = pooled - p
    return jnp.dot(pooled, w_ref[...], preferred_element_type=F32) * sc_ref[...]


def _outproj_kernel(ya_ref, yf_ref, yb_ref, bonus_ref, g_ref, p_ref, pp_ref, pn_ref, x_ref, mod_ref,
                    wa_ref, wb_ref, wc_ref, pw_ref, psc_ref,
                    gng_ref, gnb_ref, ones_ref, l1g_ref, l1b_ref, wr_ref, br_ref,
                    x1_o, u2_o, ri_o, rw_o, cnt_o, cnt_ref, ext_ref, *, alpha, tiles_per_batch, seq):
    yc = _pool_tile(p_ref, pp_ref, pn_ref, pw_ref, psc_ref, ext_ref,
                    pl.program_id(0) % tiles_per_batch, tiles_per_batch, seq)
    m = mod_ref[0]
    ones = ones_ref[...]

    def headmean(x):
        return _dot_split(x, ones) * (1.0 / HEAD_DIM)

    ysum = yf_ref[...] + yb_ref[...]
    yc0 = ysum - headmean(ysum)
    yn = yc0 * lax.rsqrt(headmean(yc0 * yc0) + GN_EPS) * gng_ref[...] + gnb_ref[...]
    yb = (yn + bonus_ref[...]) * g_ref[...]
    mix = (jnp.dot(ya_ref[...].astype(BF16), wa_ref[...], preferred_element_type=F32)
           + jnp.dot(yb.astype(BF16), wb_ref[...], preferred_element_type=F32)
           + jnp.dot(yc.astype(BF16), wc_ref[...], preferred_element_type=F32))
    x1 = _ln(alpha * x_ref[...] + m[2:3] * mix) * l1g_ref[...] + l1b_ref[...]
    x1_o[...] = x1
    u2 = _ln(x1) * (1.0 + m[4:5]) + m[3:4]
    u2_o[...] = _pack_bf16_pairs(u2)

    u_hi, u_lo = _split_bf16(u2)
    hi_both = jnp.dot(u_hi, wr_ref[...], preferred_element_type=F32)
    lg = (hi_both[:, :LANES] + hi_both[:, LANES:]
          + jnp.dot(u_lo, wr_ref[:, :LANES], preferred_element_type=F32)) + br_ref[...]
    tm = lg.shape[0]
    lt = jnp.transpose(lg)[:ROUTER_ROWS]
    rowi = lax.broadcasted_iota(jnp.int32, lt.shape, 0)
    big = jnp.int32(1 << 20)
    gl = jnp.where(rowi < N_GROUPS, lt, -jnp.inf)
    gmax = jnp.max(gl, axis=0, keepdims=True)
    gidx = jnp.min(jnp.where(gl == gmax, rowi, big), axis=0, keepdims=True)
    pg_sel = 1.0 / jnp.sum(jnp.exp(gl - gmax), axis=0, keepdims=True)
    e_lo = N_GROUPS + gidx * EXPERTS_PER_GROUP
    el = jnp.where((rowi >= e_lo) & (rowi < e_lo + EXPERTS_PER_GROUP), lt, -jnp.inf)
    m1 = jnp.max(el, axis=0, keepdims=True)
    i1 = jnp.min(jnp.where(el == m1, rowi, big), axis=0, keepdims=True)
    el2 = jnp.where(rowi == i1, -jnp.inf, el)
    m2 = jnp.max(el2, axis=0, keepdims=True)
    i2 = jnp.min(jnp.where(el2 == m2, rowi, big), axis=0, keepdims=True)
    e21 = jnp.exp(m2 - m1)
    gate1 = pg_sel / (1.0 + e21)
    gate2 = pg_sel * e21 / (1.0 + e21)
    row128 = lax.broadcasted_iota(jnp.int32, (LANES, tm), 0)
    rw_o[...] = jnp.transpose(jnp.where(row128 == 0, gate1, jnp.where(row128 == 1, gate2, 0.0)))

    @pl.when(pl.program_id(0) == 0)
    def _():
        cnt_ref[...] = jnp.zeros_like(cnt_ref)

    earlier = (lax.broadcasted_iota(jnp.int32, (tm, tm), 0)
               < lax.broadcasted_iota(jnp.int32, (tm, tm), 1)).astype(BF16)
    oh1 = (rowi == i1).astype(F32)
    oh2 = (rowi == i2).astype(F32)
    run = cnt_ref[...]
    c1 = jnp.sum(oh1, axis=1, keepdims=True)
    before1 = run + jnp.dot(oh1.astype(BF16), earlier, preferred_element_type=F32)
    before2 = run + c1 + jnp.dot(oh2.astype(BF16), earlier, preferred_element_type=F32)
    rank1 = jnp.sum(oh1 * before1, axis=0, keepdims=True).astype(jnp.int32)
    rank2 = jnp.sum(oh2 * before2, axis=0, keepdims=True).astype(jnp.int32)
    total = run + c1 + jnp.sum(oh2, axis=1, keepdims=True)
    cnt_ref[...] = total
    cnt_o[...] = total
    row8 = lax.broadcasted_iota(jnp.int32, (SUBLANES, tm), 0)
    ri_o[...] = jnp.where(row8 == 0, i1 - N_GROUPS, jnp.where(row8 == 1, i2 - N_GROUPS,
                          jnp.where(row8 == 2, rank1, jnp.where(row8 == 3, rank2, 0))))


def _outproj(ya, yf, yb, bonus, g, praw, x2, modl, p, seq, tm, alpha):
    t, d = x2.shape
    tpb = seq // tm
    aw, bw, cw = ya.shape[1], bonus.shape[1], praw.shape[1]
    hb = tm // HALO
    nhb = t // HALO
    tok = lambda i: (i, 0)
    full2 = lambda i: (0, 0)
    kern = functools.partial(_outproj_kernel, alpha=alpha, tiles_per_batch=tpb, seq=seq)
    small = ["w_out_a", "w_out_b", "w_out_c", "pool_w", "pool_scale",
             "gn_gain", "gn_bias", "ones", "ln1_gain", "ln1_bias", "w_router", "b_router"]
    return pl.pallas_call(
        kern,
        grid=(t // tm,),
        in_specs=[pl.BlockSpec((tm, aw), tok),
                  pl.BlockSpec((tm, bw), tok), pl.BlockSpec((tm, bw), tok),
                  pl.BlockSpec((tm, bw), tok), pl.BlockSpec((tm, bw), tok),
                  pl.BlockSpec((tm, cw), tok),
                  pl.BlockSpec((HALO, cw), lambda i: (jnp.maximum(i * hb - 1, 0), 0)),
                  pl.BlockSpec((HALO, cw), lambda i: (jnp.minimum((i + 1) * hb, nhb - 1), 0)),
                  pl.BlockSpec((tm, d), tok),
                  pl.BlockSpec((1,) + modl.shape[1:], lambda i: (i // tpb, 0, 0))]
                 + [pl.BlockSpec(p[k].shape, functools.partial(lambda nd, i: (0,) * nd, p[k].ndim)) for k in small],
        out_specs=[pl.BlockSpec((tm, d), tok), pl.BlockSpec((tm, d // 2), tok),
                   pl.BlockSpec((SUBLANES, tm), lambda i: (0, i)), pl.BlockSpec((tm, LANES), tok),
                   pl.BlockSpec((ROUTER_ROWS, 1), full2)],
        out_shape=[jax.ShapeDtypeStruct((t, d), F32), jax.ShapeDtypeStruct((t, d // 2), jnp.uint32),
                   jax.ShapeDtypeStruct((SUBLANES, t), jnp.int32), jax.ShapeDtypeStruct((t, LANES), F32),
                   jax.ShapeDtypeStruct((ROUTER_ROWS, 1), F32)],
        scratch_shapes=[pltpu.VMEM((ROUTER_ROWS, 1), F32),pltpu.VMEM((len(POOL_WINDOWS), tm + 4 * HALO, cw), F32)],
        compiler_params=_cparams(1, "arbitrary"),
        name="outproj",
    )(ya, yf, yb, bonus, g, praw, praw, praw, x2, modl, *[p[k] for k in small])


def _dispatch(route_t, counts_rows, n_blocks, tm):
    counts = counts_rows[N_GROUPS:N_GROUPS + N_EXPERTS, 0].astype(jnp.int32)
    padded = ((counts + EXPERT_BLOCK - 1) // EXPERT_BLOCK) * EXPERT_BLOCK
    pends = jnp.cumsum(padded)
    pstarts = pends - padded
    e = route_t[:TOP_K]
    onehot = (e[..., None] == jnp.arange(N_EXPERTS, dtype=jnp.int32)).astype(F32)
    start_of = jnp.einsum("ktx,x->kt", onehot, pstarts.astype(F32), precision=HI)
    dest = route_t[TOP_K:2 * TOP_K] + start_of.astype(jnp.int32)
    t = route_t.shape[1]
    dest = dest.reshape(TOP_K, t // tm, tm).transpose(1, 0, 2).reshape(t // tm, 1, TOP_K * tm)
    block_start = jnp.arange(n_blocks, dtype=jnp.int32) * EXPERT_BLOCK
    block_e = jnp.minimum(jnp.sum((pends[None, :] <= block_start[:, None]).astype(jnp.int32), axis=1), N_EXPERTS - 1)
    meta = jnp.concatenate([block_e, (pends[-1] // EXPERT_BLOCK)[None]]).astype(jnp.int32)
    return dest, meta, pends.astype(jnp.int32)


def _scatter_rows_kernel(pends_ref, dest_ref, u_ref, xs_ref, zeros_ref, sem, zsem):
    tm = u_ref.shape[0] * SUBLANES

    @pl.when(pl.program_id(0) == 0)
    def _():
        zeros_ref[...] = jnp.zeros_like(zeros_ref)

        def tail_copy(e):
            tail = pl.ds(pl.multiple_of(pends_ref[e] - EXPERT_BLOCK, EXPERT_BLOCK), EXPERT_BLOCK)
            return pltpu.make_async_copy(zeros_ref, xs_ref.at[tail], zsem)

        def has_rows(e):
            return pends_ref[e] > (pends_ref[e - 1] if e > 0 else 0)

        def unused_copy(j):
            return pltpu.make_async_copy(zeros_ref, xs_ref.at[pl.ds(j * EXPERT_BLOCK, EXPERT_BLOCK)], zsem)

        def is_unused(j):
            return j * EXPERT_BLOCK >= pends_ref[N_EXPERTS - 1]

        n_blocks = xs_ref.shape[0] // EXPERT_BLOCK
        for e in range(N_EXPERTS):
            pl.when(has_rows(e))(lambda e=e: tail_copy(e).start())
        for j in range(n_blocks):
            pl.when(is_unused(j))(lambda j=j: unused_copy(j).start())
        for e in range(N_EXPERTS):
            pl.when(has_rows(e))(lambda e=e: tail_copy(e).wait())
        for j in range(n_blocks):
            pl.when(is_unused(j))(lambda j=j: unused_copy(j).wait())

    def issue(grp, carry):
        for j in range(SUBLANES):
            for k in range(TOP_K):
                dst = dest_ref[0, 0, k * tm + SUBLANES * grp + j]
                pltpu.make_async_copy(u_ref.at[grp, pl.ds(j, 1)], xs_ref.at[pl.ds(dst, 1)], sem).start()
        return carry

    lax.fori_loop(0, tm // SUBLANES, issue, 0)
    rows = pl.ds(0, TOP_K * tm)
    pltpu.make_async_copy(xs_ref.at[rows], xs_ref.at[rows], sem).wait()


def _scatter_rows(pends, u2, dest3, total, tm):
    t, d = u2.shape
    grid_spec = pltpu.PrefetchScalarGridSpec(
        num_scalar_prefetch=1,
        grid=(t // tm,),
        in_specs=[pl.BlockSpec((1, 1, TOP_K * tm), lambda i, p: (i, 0, 0), memory_space=pltpu.SMEM),
                  pl.BlockSpec((tm // SUBLANES, SUBLANES, d), lambda i, p: (i, 0, 0))],
        out_specs=pl.BlockSpec(memory_space=pl.ANY),
        scratch_shapes=[pltpu.VMEM((EXPERT_BLOCK, d), u2.dtype), pltpu.SemaphoreType.DMA(()),
                        pltpu.SemaphoreType.DMA(())],
    )
    return pl.pallas_call(
        _scatter_rows_kernel,
        grid_spec=grid_spec,
        out_shape=jax.ShapeDtypeStruct((total, d), u2.dtype),
        compiler_params=_cparams(1, "arbitrary"),
        name="scatter_rows",
    )(pends, dest3, u2.reshape(t // SUBLANES, SUBLANES, d))


def _experts_kernel(meta_ref, xs_ref, wg_ref, wu_ref, wd_ref, o_ref, wg_b, wu_b, wd_b):
    i = pl.program_id(0)
    n_used = meta_ref[pl.num_programs(0)]

    @pl.when((i == 0) | (meta_ref[i] != meta_ref[jnp.maximum(i - 1, 0)]))
    def _():
        wg_b[...] = wg_ref[0, 0].astype(BF16)
        wu_b[...] = wu_ref[0, 0].astype(BF16)
        wd_b[...] = wd_ref[0, 0].astype(BF16)

    @pl.when(i < n_used)
    def _():
        half = EXPERT_BLOCK // 2
        gate_up = []
        for r0 in (0, half):
            xb = _unpack_bf16_pairs(xs_ref[r0:r0 + half, :])
            gate_up.append((jnp.dot(xb, wg_b[...], preferred_element_type=F32),
                            jnp.dot(xb, wu_b[...], preferred_element_type=F32)))
        for r0, (gate, up) in zip((0, half), gate_up):
            hb = gate * _sigmoid(gate) * up
            y = jnp.dot(hb.astype(BF16), wd_b[...], preferred_element_type=F32)
            o_ref[r0:r0 + half, :] = _pack_bf16_pairs(y)

    @pl.when(i >= n_used)
    def _():
        o_ref[...] = jnp.zeros_like(o_ref)


def _experts(meta, xs, wg, wu, wd, layer):
    total, dp = xs.shape
    nb = total // EXPERT_BLOCK
    d, de = wg.shape[2:]
    grid_spec = pltpu.PrefetchScalarGridSpec(
        num_scalar_prefetch=1,
        grid=(nb,),
        in_specs=[pl.BlockSpec((EXPERT_BLOCK, dp), lambda i, m: (jnp.minimum(i, m[nb] - 1), 0)),
                  pl.BlockSpec((1, 1, d, de), lambda i, m: (layer, m[i], 0, 0)),
                  pl.BlockSpec((1, 1, d, de), lambda i, m: (layer, m[i], 0, 0)),
                  pl.BlockSpec((1, 1, de, d), lambda i, m: (layer, m[i], 0, 0))],
        out_specs=pl.BlockSpec((EXPERT_BLOCK, dp), lambda i, m: (i, 0)),
        scratch_shapes=[pltpu.VMEM((d, de), BF16), pltpu.VMEM((d, de), BF16), pltpu.VMEM((de, d), BF16)],
    )
    return pl.pallas_call(
        _experts_kernel,
        grid_spec=grid_spec,
        out_shape=jax.ShapeDtypeStruct((total, dp), xs.dtype),
        compiler_params=_cparams(1, "arbitrary"),
        name="experts",
    )(meta, xs, wg, wu, wd)


def _final_kernel(dcur_ref, dnext_ref, x1_ref, rw_ref, mod_ref, g_ref, b_ref, ys_ref, o_ref, ybuf, sem, *, alpha):
    i = pl.program_id(0)
    tm = x1_ref.shape[0]
    slot = i % 2

    def gather(d_ref, s):
        def issue(grp, carry):
            for j in range(SUBLANES):
                for k in range(TOP_K):
                    src = d_ref[0, 0, k * tm + SUBLANES * grp + j]
                    pltpu.make_async_copy(ys_ref.at[pl.ds(src, 1)], ybuf.at[s, k, grp, pl.ds(j, 1)],
                                          sem.at[s]).start()
            return carry

        lax.fori_loop(0, tm // SUBLANES, issue, 0)

    @pl.when(i == 0)
    def _():
        gather(dcur_ref, 0)

    @pl.when(i + 1 < pl.num_programs(0))
    def _():
        gather(dnext_ref, 1 - slot)

    pltpu.make_async_copy(ybuf.at[slot], ybuf.at[slot], sem.at[slot]).wait()
    m = mod_ref[0]
    rw = rw_ref[...]
    dp = ybuf.shape[-1]
    y1 = _unpack_bf16_pairs(ybuf[slot, 0].reshape(tm, dp)).astype(F32)
    y2 = _unpack_bf16_pairs(ybuf[slot, 1].reshape(tm, dp)).astype(F32)
    f = rw[:, 0:1] * y1 + rw[:, 1:2] * y2
    o_ref[...] = _ln(alpha * x1_ref[...] + m[5:6] * f) * g_ref[...] + b_ref[...]


def _final(x1, ysorted, dest3, rw, modl, gain, bias, seq, tm, alpha):
    t, d = x1.shape
    tpb = seq // tm
    n_tiles = t // tm
    tok = lambda i: (i, 0)
    kern = functools.partial(_final_kernel, alpha=alpha)
    dspec = lambda f: pl.BlockSpec((1, 1, TOP_K * tm), f, memory_space=pltpu.SMEM)
    return pl.pallas_call(
        kern,
        grid=(n_tiles,),
        in_specs=[dspec(lambda i: (i, 0, 0)), dspec(lambda i: (jnp.minimum(i + 1, n_tiles - 1), 0, 0)),
                  pl.BlockSpec((tm, d), tok), pl.BlockSpec((tm, LANES), tok),
                  pl.BlockSpec((1,) + modl.shape[1:], lambda i: (i // tpb, 0, 0)),
                  pl.BlockSpec(gain.shape, lambda i: (0, 0)), pl.BlockSpec(bias.shape, lambda i: (0, 0)),
                  pl.BlockSpec(memory_space=pl.ANY)],
        out_specs=pl.BlockSpec((tm, d), tok),
        out_shape=jax.ShapeDtypeStruct((t, d), F32),
        scratch_shapes=[pltpu.VMEM((2, TOP_K, tm // SUBLANES, SUBLANES, ysorted.shape[1]), ysorted.dtype),
                        pltpu.SemaphoreType.DMA((2,))],
        compiler_params=_cparams(1, "arbitrary"),
        name="final_ln",
    )(dest3, dest3, x1, rw, modl, gain, bias, ysorted)


def _block_diag(blocks):
    n, a, b = blocks.shape
    out = jnp.zeros((n * a, n * b), blocks.dtype)
    for i in range(n):
        out = out.at[i * a:(i + 1) * a, i * b:(i + 1) * b].set(blocks[i])
    return out


def _pad_rows(w, lo, total):
    return jnp.pad(w, ((lo, total - lo - w.shape[0]), (0, 0)))


def kernel(x, c, w_mod, b_mod, w_in, na_rpb, rw_conv, rw_w0, rw_w_up, rw_a0, rw_a_up, rw_g_up, rw_k_k, rw_k_a, rw_r_k, rw_gn_gain, rw_gn_bias, pool_w, pool_scale, w_out, ln1_gain, ln1_bias, ln2_gain, ln2_bias, moe_w_group, moe_b_group, moe_w_expert, moe_b_expert, moe_w_gate, moe_w_up, moe_w_down):
    batch, seq, d = x.shape
    depth = w_mod.shape[0]
    t = batch * seq
    a_w = na_rpb.shape[1] * HEAD_DIM
    b_w = rw_w0.shape[-1]
    c_w = pool_scale.shape[-1]
    lr_w = R_W + R_A + R_G
    alpha = (2 * depth) ** 0.25
    tm = tm_in = min(TOKEN_TILE, seq)
    assert seq % tm == 0 and seq % SCAN_CHUNK == 0 and seq % GRID_W == 0 and lr_w == LANES

    mod = _modulation(c, w_mod, b_mod)
    ones_blk = _block_diag(jnp.ones((b_w // HEAD_DIM, HEAD_DIM, HEAD_DIM), BF16))
    row = lambda v: v.reshape(1, -1)

    x2 = x.reshape(t, d)
    for l in range(depth):
        modl = mod[l]
        prep_params = {
            "conv": rw_conv[l], "w0": rw_w0[l], "a0": rw_a0[l],
            "w_up": jnp.stack([_pad_rows(rw_w_up[l, dd], 0, lr_w) for dd in range(2)]).astype(BF16),
            "a_up": jnp.stack([_pad_rows(rw_a_up[l, dd], R_W, lr_w) for dd in range(2)]).astype(BF16),
            "g_up": _pad_rows(rw_g_up[l], R_W + R_A, lr_w).astype(BF16),
            "k_k": row(rw_k_k[l]), "k_a": row(rw_k_a[l]), "r_k": row(rw_r_k[l]), "ones": ones_blk,
        }
        qkv, praw, r, v, nkk, lw, bb, kd, bonus, g = _inproj(x2, modl, w_in[l].astype(BF16), prep_params, seq, tm_in,
                                                             3 * a_w, 3 * b_w, lr_w, c_w)
        ya = _natten(qkv, _na_bias_table(na_rpb[l]), batch, seq, a_w)
        yf, yb = _rwkv_scan(r, v, nkk, lw, bb, kd, batch, seq)
        lane_pad = LANES - N_GROUPS - N_EXPERTS
        w_router = jnp.concatenate([moe_w_group[l], moe_w_expert[l], jnp.zeros((d, lane_pad), F32)], axis=1)
        b_router = jnp.concatenate([moe_b_group[l], moe_b_expert[l], jnp.zeros((lane_pad,), F32)]).reshape(1, LANES)
        wo = w_out[l].astype(BF16)
        out_params = {
            "w_out_a": wo[:a_w], "w_out_b": wo[a_w:a_w + b_w], "w_out_c": wo[a_w + b_w:],
            "pool_w": _block_diag(pool_w[l]), "pool_scale": row(pool_scale[l]),
            "gn_gain": row(rw_gn_gain[l]), "gn_bias": row(rw_gn_bias[l]), "ones": ones_blk,
            "ln1_gain": row(ln1_gain[l]), "ln1_bias": row(ln1_bias[l]),
            "w_router": jnp.concatenate(_split_bf16(w_router), axis=1), "b_router": b_router,
        }
        x1, u2, route_i, route_w, counts = _outproj(ya, yf, yb, bonus, g, praw, x2, modl, out_params, seq, tm, alpha)
        n_blocks = -(-(t * TOP_K) // EXPERT_BLOCK) + N_EXPERTS
        dest3, meta, pends = _dispatch(route_i, counts, n_blocks, tm)
        xs = _scatter_rows(pends, u2, dest3, n_blocks * EXPERT_BLOCK, tm)
        ysorted = _experts(meta, xs, moe_w_gate, moe_w_up, moe_w_down, l)
        x2 = _final(x1, ysorted, dest3, route_w, modl, row(ln2_gain[l]), row(ln2_bias[l]), seq, tm, alpha)
    return x2.reshape(batch, seq, d)
```

```python
import functools
import math

import jax
import jax.numpy as jnp
import numpy as np
from jax import lax
from jax.experimental import pallas as pl
from jax.experimental.pallas import tpu as pltpu

F32 = jnp.float32
BF16 = jnp.bfloat16
HI = lax.Precision.HIGHEST

GRID_W = 64
HEAD_DIM = 64
NA_KH = 8
NA_KW = 16
POOL_WINDOWS = (2, 4, 8, 16)
R_W = 32
R_A = 32
R_G = 64
DECAY_SCALE = math.exp(-0.5)
GN_EPS = 64e-5
N_GROUPS = 4
EXPERTS_PER_GROUP = 8
N_EXPERTS = N_GROUPS * EXPERTS_PER_GROUP
TOP_K = 2
ROUTER_ROWS = -(-(N_GROUPS + N_EXPERTS) // 8) * 8
EXPERT_BLOCK = 512
LN_EPS = 1e-5
NEG_INF = -1e30

TOKEN_TILE = 512
NA_ROWS_PER_STEP = 8
NA_SCORES_AHEAD = 1
SCAN_CHUNK = 64
SCAN_CHUNKS_PER_STEP = 4
SUBLANES = 8
HALO = 8
LANES = 128
VMEM_LIMIT = 52 * 1024 * 1024


def _ln(x):
    mu = jnp.mean(x, axis=-1, keepdims=True)
    xc = x - mu
    var = jnp.mean(xc * xc, axis=-1, keepdims=True)
    return xc * lax.rsqrt(var + LN_EPS)


def _sigmoid(x):
    return 1.0 / (1.0 + jnp.exp(-x))


def _split_bf16(x):
    hi = x.astype(BF16)
    return hi, (x - hi.astype(F32)).astype(BF16)


def _dot_split(x, w_exact):
    hi, lo = _split_bf16(x)
    return jnp.dot(hi, w_exact, preferred_element_type=F32) + jnp.dot(lo, w_exact, preferred_element_type=F32)


def _pack_bf16_pairs(x):
    h = x.shape[1] // 2
    lo = lax.bitcast_convert_type(x[:, :h].astype(BF16).astype(F32), jnp.uint32)
    hi = lax.bitcast_convert_type(x[:, h:].astype(BF16).astype(F32), jnp.uint32)
    return (lo >> 16) | hi


def _unpack_bf16_pairs(w):
    lo = lax.bitcast_convert_type(w << 16, F32).astype(BF16)
    hi = lax.bitcast_convert_type(w & jnp.uint32(0xFFFF0000), F32).astype(BF16)
    return jnp.concatenate([lo, hi], axis=1)


def _cparams(n_axes, semantics="parallel"):
    return pltpu.CompilerParams(dimension_semantics=(semantics,) * n_axes, vmem_limit_bytes=VMEM_LIMIT)


def _mod_kernel(c_ref, w_ref, b_ref, o_ref):
    c = c_ref[...]
    s = c * _sigmoid(c)
    o_ref[0] = jnp.dot(s, w_ref[0], precision=HI, preferred_element_type=F32) + b_ref[0]


def _modulation(c, w_mod, b_mod):
    n_layers, d, d6 = w_mod.shape
    b = c.shape[0]
    bp = -(-b // 8) * 8
    cp = jnp.zeros((bp, d), F32).at[:b].set(c)
    out = pl.pallas_call(
        _mod_kernel,
        grid=(n_layers, d6 // d),
        in_specs=[pl.BlockSpec((bp, d), lambda l, j: (0, 0)),
                  pl.BlockSpec((1, d, d), lambda l, j: (l, 0, j)),
                  pl.BlockSpec((1, 1, d), lambda l, j: (l, 0, j))],
        out_specs=pl.BlockSpec((1, bp, d), lambda l, j: (l, 0, j)),
        out_shape=jax.ShapeDtypeStruct((n_layers, bp, d6), F32),
        compiler_params=_cparams(2),
        name="modulation",
    )(cp, w_mod, b_mod.reshape(n_layers, 1, d6))
    return out[:, :b].reshape(n_layers, b, d6 // d, d)


def _inproj_kernel(x_ref, xp_ref, xn_ref, mod_ref, wr_ref, wo_ref, cw_ref, w0_ref, wup_ref, a0_ref, aup_ref, gup_ref,
                   kk_ref, ka_ref, rk_ref, ones_ref,
                   qkv_o, pool_o, r_o, v_o, nkk_o, lw_o, b_o, kd_o, bonus_o, g_o,
                   *, a3, b3, tiles_per_batch):
    m = mod_ref[0]
    tm = x_ref.shape[0]
    tb = pl.program_id(0) % tiles_per_batch
    xe = jnp.concatenate([xp_ref[...], x_ref[...], xn_ref[...]], axis=0)
    u = (_ln(xe) * (1.0 + m[1:2]) + m[0:1]).astype(BF16)
    h = jnp.dot(u, wr_ref[...], preferred_element_type=F32)
    ho = jnp.dot(u[HALO:HALO + tm], wo_ref[...], preferred_element_type=F32)
    hm = h[HALO:HALO + tm]
    prev = jnp.where(tb == 0, 0.0, h[HALO - 1:HALO, :b3])
    nxt = jnp.where(tb == tiles_per_batch - 1, 0.0, h[HALO + tm:HALO + tm + 1, :b3])
    _rwkv_prep_tile(hm[:, :b3], prev, nxt, hm[:, b3:],
                    cw_ref, w0_ref, wup_ref, a0_ref, aup_ref, gup_ref, kk_ref, ka_ref, rk_ref, ones_ref,
                    r_o, v_o, nkk_o, lw_o, b_o, kd_o, bonus_o, g_o)
    qkv_o[...] = ho[:, :a3].astype(BF16)
    pool_o[...] = ho[:, a3:]


def _inproj(x2, modl, w_in_bf, p, seq, tm, a3, b3, lr_w, c_w):
    t, d = x2.shape
    tpb = seq // tm
    hb = tm // HALO
    nhb = t // HALO
    width = b3 // 3
    kern = functools.partial(_inproj_kernel, a3=a3, b3=b3, tiles_per_batch=tpb)
    w_rwkv = w_in_bf[:, a3:a3 + b3 + lr_w]
    w_other = jnp.concatenate([w_in_bf[:, :a3], w_in_bf[:, a3 + b3 + lr_w:]], axis=1)
    tok = lambda i: (i, 0)
    dtok = lambda i: (0, i, 0)
    names = ["conv", "w0", "w_up", "a0", "a_up", "g_up", "k_k", "k_a", "r_k", "ones"]
    tw = jax.ShapeDtypeStruct((t, width), F32)
    dtw = jax.ShapeDtypeStruct((2, t, width), F32)
    return pl.pallas_call(
        kern,
        grid=(t // tm,),
        in_specs=[pl.BlockSpec((tm, d), tok),
                  pl.BlockSpec((HALO, d), lambda i: (jnp.maximum(i * hb - 1, 0), 0)),
                  pl.BlockSpec((HALO, d), lambda i: (jnp.minimum((i + 1) * hb, nhb - 1), 0)),
                  pl.BlockSpec((1,) + modl.shape[1:], lambda i: (i // tpb, 0, 0)),
                  pl.BlockSpec(w_rwkv.shape, lambda i: (0, 0)),
                  pl.BlockSpec(w_other.shape, lambda i: (0, 0))]
                 + [pl.BlockSpec(p[k].shape, functools.partial(lambda nd, i: (0,) * nd, p[k].ndim)) for k in names],
        out_specs=[pl.BlockSpec((tm, a3), tok), pl.BlockSpec((tm, c_w), tok),
                   pl.BlockSpec((tm, width), tok), pl.BlockSpec((tm, width), tok), pl.BlockSpec((tm, width), tok),
                   pl.BlockSpec((2, tm, width), dtok), pl.BlockSpec((2, tm, width), dtok),
                   pl.BlockSpec((2, tm, width), dtok),
                   pl.BlockSpec((tm, width), tok), pl.BlockSpec((tm, width), tok)],
        out_shape=[jax.ShapeDtypeStruct((t, a3), BF16), jax.ShapeDtypeStruct((t, c_w), F32),
                   tw, tw, tw, dtw, dtw, dtw, tw, tw],
        compiler_params=_cparams(1),
        name="inproj",
    )(x2, x2, x2, modl, w_rwkv, w_other, *[p[k] for k in names])


def _na_bias_table(rpb):
    col = np.arange(GRID_W)
    cstart = np.clip(col - NA_KW // 2, 0, GRID_W - NA_KW)
    in_win = (col[None, :] >= cstart[:, None]) & (col[None, :] < cstart[:, None] + NA_KW)
    dc = np.clip(col[None, :] - col[:, None], -(NA_KW - 1), NA_KW - 1) + (NA_KW - 1)
    pick = (dc[None] == np.arange(2 * NA_KW - 1)[:, None, None]).astype(np.float32)
    cols = jnp.einsum("hrc,cqk->hrqk", rpb.astype(F32), pick, precision=HI)
    cols = jnp.where(in_win, cols, NEG_INF)
    b = jnp.stack([cols[:, NA_KH - 1 - o:2 * NA_KH - 1 - o] for o in range(NA_KH)])
    h = rpb.shape[0]
    return jnp.transpose(b, (0, 1, 3, 2, 4)).reshape(NA_KH, h * GRID_W, NA_KH * GRID_W)


def _natten_kernel(q_ref, k_ref, v_ref, bias_ref, o_ref, *, rows, heads):
    width = q_ref.shape[1]
    nk = NA_KH * GRID_W
    head_of_lane = lax.broadcasted_iota(jnp.int32, (heads * GRID_W, width), 1) // HEAD_DIM
    head_of_row = lax.broadcasted_iota(jnp.int32, (heads * GRID_W, width), 0) // GRID_W
    own = head_of_lane == head_of_row
    def scores(j):
        r = pl.program_id(1) * NA_ROWS_PER_STEP + j
        rstart = jnp.clip(r - NA_KH // 2, 0, rows - NA_KH)
        start = pl.multiple_of(rstart * GRID_W, GRID_W)
        q = q_ref[j * GRID_W:(j + 1) * GRID_W, :]
        qs = jnp.where(own, jnp.concatenate([q] * heads, axis=0), jnp.zeros((), q.dtype))
        s = lax.dot_general(qs, k_ref[pl.ds(start, nk), :], (((1,), (1,)), ((), ())), preferred_element_type=F32)
        return s * (HEAD_DIM ** -0.5) + bias_ref[r - rstart], start

    ahead = [scores(j) for j in range(min(NA_SCORES_AHEAD, NA_ROWS_PER_STEP))]
    for j in range(NA_ROWS_PER_STEP):
        s, start = ahead.pop(0)
        if j + NA_SCORES_AHEAD < NA_ROWS_PER_STEP:
            ahead.append(scores(j + NA_SCORES_AHEAD))
        mx = jnp.max(s, axis=-1, keepdims=True)
        p = jnp.exp(s - mx)
        den = jnp.sum(p, axis=-1, keepdims=True)
        pv = jnp.dot(p.astype(BF16), v_ref[pl.ds(start, nk), :], preferred_element_type=F32)
        o = jnp.where(own, pv / den, 0.0)
        acc = o[0:GRID_W]
        for h in range(1, heads):
            acc = acc + o[h * GRID_W:(h + 1) * GRID_W]
        o_ref[j * GRID_W:(j + 1) * GRID_W, :] = acc.astype(o_ref.dtype)


def _natten(qkv, bias_tab, batch, seq, width):
    rows = seq // GRID_W
    assert rows >= NA_KH
    heads = width // HEAD_DIM
    steps = rows // NA_ROWS_PER_STEP
    assert steps * NA_ROWS_PER_STEP == rows
    tq = NA_ROWS_PER_STEP * GRID_W
    kern = functools.partial(_natten_kernel, rows=rows, heads=heads)
    return pl.pallas_call(
        kern,
        grid=(batch, steps),
        in_specs=[pl.BlockSpec((tq, width), lambda b, r: (b * steps + r, 0)),
                  pl.BlockSpec((seq, width), lambda b, r: (b, 1)),
                  pl.BlockSpec((seq, width), lambda b, r: (b, 2)),
                  pl.BlockSpec(bias_tab.shape, lambda b, r: (0, 0, 0))],
        out_specs=pl.BlockSpec((tq, width), lambda b, r: (b * steps + r, 0)),
        out_shape=jax.ShapeDtypeStruct((batch * seq, width), BF16),
        compiler_params=_cparams(2),
        name="natten",
    )(qkv, qkv, qkv, bias_tab)


def _rwkv_prep_tile(z, prev, nxt, lr, cw_ref, w0_ref, wup_ref, a0_ref, aup_ref, gup_ref, kk_ref, ka_ref, rk_ref, ones_ref,
                    r_o, v_o, nkk_o, lw_o, b_o, kd_o, bonus_o, g_o):
    tm = z.shape[0]
    width = z.shape[1] // 3
    row = lax.broadcasted_iota(jnp.int32, z.shape, 0)
    zm1 = jnp.where(row == 0, prev, pltpu.roll(z, 1, 0))
    zp1 = jnp.where(row == tm - 1, nxt, pltpu.roll(z, tm - 1, 0))
    rkv = zm1 * cw_ref[0:1, :] + z * cw_ref[1:2, :] + zp1 * cw_ref[2:3, :]
    r = rkv[:, :width]
    k = rkv[:, width:2 * width]
    v = rkv[:, 2 * width:]
    th = jnp.tanh(lr)
    sg = _sigmoid(lr)
    ones = ones_ref[...]

    def headsum(x):
        return _dot_split(x, ones)

    kk = k * kk_ref[...]
    kk = kk * lax.rsqrt(jnp.maximum(headsum(kk * kk), 1e-24))
    g_o[...] = jnp.dot(sg.astype(BF16), gup_ref[...], preferred_element_type=F32)
    r_o[...] = r
    v_o[...] = v
    nkk_o[...] = -kk
    kd_sum = jnp.zeros_like(r)
    th_b = th.astype(BF16)
    lr_b = lr.astype(BF16)
    for d in range(2):
        wl = jnp.dot(th_b, wup_ref[d], preferred_element_type=F32) + w0_ref[d:d + 1, :]
        lw_o[d] = -DECAY_SCALE * _sigmoid(wl)
        a = _sigmoid(jnp.dot(lr_b, aup_ref[d], preferred_element_type=F32) + a0_ref[d:d + 1, :])
        kd = k * (1.0 + (a - 1.0) * ka_ref[...])
        kd_o[d] = kd
        b_o[d] = kk * a
        kd_sum = kd_sum + kd
    bonus_o[...] = headsum(r * kd_sum * rk_ref[...]) * v


def _dot_nt(a, b):
    return lax.dot_general(a, b, (((1,), (1,)), ((), ())), preferred_element_type=F32)


def _dot_tn(a, b):
    return lax.dot_general(a, b, (((0,), (0,)), ((), ())), preferred_element_type=F32)


def _rwkv_scan_kernel(rf_ref, vf_ref, nf_ref, rb_ref, vb_ref, nb_ref, lwf_ref, bf_ref, kf_ref, lwb_ref, bb_ref, kb_ref,
                      yf_ref, yb_ref, s_ref, *, heads, batch):
    @pl.when(pl.program_id(0) == 0)
    def _():
        s_ref[...] = jnp.zeros_like(s_ref)

    n = SCAN_CHUNK
    pair_w = 2 * HEAD_DIM
    row = lax.broadcasted_iota(jnp.int32, (n, pair_w), 0)
    lane = lax.broadcasted_iota(jnp.int32, (n, pair_w), 1)
    col = lane & (HEAD_DIM - 1)
    even = lane < HEAD_DIM
    levels = n.bit_length()
    same = [(row >> k) == (col >> k) for k in range(levels)]
    eye = same[0].astype(F32)
    level_masks = [same[sh + 1] & jnp.logical_not(same[sh]) for sh in range(1, levels - 1)]

    def blockdiag(x2):
        xb = x2.astype(BF16)
        zero = jnp.zeros((), BF16)
        return jnp.concatenate([jnp.where(even, xb, zero), jnp.where(even, zero, xb)], axis=0)

    def mm(x2, y2):
        return jnp.dot(x2.astype(BF16), blockdiag(y2), preferred_element_type=F32)

    def mm_nt(x2, y2):
        return _dot_nt(x2.astype(BF16), blockdiag(y2))

    dirs = ((rf_ref, vf_ref, nf_ref, lwf_ref, bf_ref, kf_ref, yf_ref),
            (rb_ref, vb_ref, nb_ref, lwb_ref, bb_ref, kb_ref, yb_ref))
    def build(q):
        chains = []
        for d, (r_ref, v_ref, n_ref, lw_ref, b_ref, k_ref, y_ref) in enumerate(dirs):
            sub = q if d == 0 else SCAN_CHUNKS_PER_STEP - 1 - q
            rs = slice(sub * n, (sub + 1) * n)
            order = row - col if d == 0 else col - row
            strict = order > 0
            incl = order >= 0
            incl_b = jnp.where(incl[:, :n], 1.0, 0.0).astype(BF16)
            for bi in range(batch):
                lw = lw_ref[0, bi, rs]
                lw_hi, lw_mid = _split_bf16(lw)
                lw_lo = (lw - lw_hi.astype(F32) - lw_mid.astype(F32)).astype(BF16)
                g_inc = ((jnp.dot(incl_b, lw_lo, preferred_element_type=F32)
                          + jnp.dot(incl_b, lw_mid, preferred_element_type=F32))
                         + jnp.dot(incl_b, lw_hi, preferred_element_type=F32))
                g_tot = jnp.sum(lw, axis=0, keepdims=True)
                e_neg = jnp.exp(-g_inc)
                e_end = jnp.exp(g_tot - g_inc)
                decay = jnp.exp(g_tot)
                a_t = n_ref[bi, rs] * jnp.exp(g_inc - lw)
                r_t = r_ref[bi, rs] * jnp.exp(g_inc)
                bb = b_ref[0, bi, rs]
                kd = k_ref[0, bi, rs]
                b_t = bb * e_neg
                k_t = kd * e_neg
                ar_t = jnp.concatenate([a_t, r_t], axis=0).astype(BF16)
                bk_h = jnp.concatenate([bb * e_end, kd * e_end], axis=0).astype(BF16)
                v = v_ref[bi, rs]
                for p in range(heads // 2):
                    sl = slice(p * pair_w, (p + 1) * pair_w)
                    chains.append(dict(strict=strict, incl=incl, sl=sl, rs=rs, bi=bi, y_ref=y_ref,
                                       si=(d * batch + bi) * (heads // 2) + p, decay=decay[:, sl],
                                       ar=ar_t[:, sl], b=b_t[:, sl], k=k_t[:, sl], bk_h=bk_h[:, sl], v=v[:, sl]))
        return chains

    def state_free_stages(chains):
        def products():
            for ch in chains:
                pb = mm_nt(ch["ar"], ch["b"])
                pk = mm_nt(ch["ar"], ch["k"])
                ch["l_ab"] = jnp.where(ch["strict"], pb[:n], 0.0)
                ch["m_rb"] = jnp.where(ch["incl"], pb[n:], 0.0)
                ch["l_ak"] = jnp.where(ch["strict"], pk[:n], 0.0)
                ch["m_rk"] = jnp.where(ch["incl"], pk[n:], 0.0)
                ch["t"] = eye + jnp.where(same[1], ch["l_ab"], 0.0)

        def level_left(mask):
            for ch in chains:
                ch["tc"] = mm(ch["t"], jnp.where(mask, ch["l_ab"], 0.0))

        def level_right():
            for ch in chains:
                ch["t"] = ch["t"] + mm(ch["tc"], ch["t"])

        def values():
            for ch in chains:
                ch["kv"] = mm(jnp.concatenate([ch["l_ak"], ch["m_rk"]], axis=0), ch["v"])

        stages = [products]
        for mask in level_masks:
            stages += [functools.partial(level_left, mask), level_right]
        return stages + [values]

    def state_stages(chains):
        def read():
            for ch in chains:
                ch["s0"] = s_ref[ch["si"]]
                ch["x"] = mm_nt(ch["ar"], ch["s0"])

        def solve():
            for ch in chains:
                ch["u"] = mm(ch["t"], ch["x"][:n] + ch["kv"][:n])

        def emit():
            for ch in chains:
                y = ch["x"][n:] + mm(ch["m_rb"], ch["u"]) + ch["kv"][n:]
                ch["y_ref"][ch["bi"], ch["rs"], ch["sl"]] = y

        def write():
            for ch in chains:
                uv = jnp.concatenate([ch["u"], ch["v"]], axis=0).astype(BF16)
                full = _dot_tn(uv, ch["bk_h"])
                s_ref[ch["si"]] = ch["s0"] * ch["decay"] + jnp.where(even, full[:HEAD_DIM], full[HEAD_DIM:])

        return [read, solve, emit, write]

    pending = []
    for q in range(SCAN_CHUNKS_PER_STEP):
        chains = build(q)
        free = state_free_stages(chains)
        if pending:
            share = -(-len(free) // len(pending))
            for i, carried in enumerate(pending):
                carried()
                for stage in free[i * share:(i + 1) * share]:
                    stage()
        else:
            for stage in free:
                stage()
        pending = state_stages(chains)
    for carried in pending:
        carried()


def _rwkv_scan(r, v, nkk, lw, b, kd, batch, seq):
    t, width = r.shape
    heads = width // HEAD_DIM
    n = SCAN_CHUNK * SCAN_CHUNKS_PER_STEP
    nc = seq // n
    assert nc * n == seq
    r3, v3, n3 = (z.reshape(batch, seq, width) for z in (r, v, nkk))
    lw4, b4, k4 = (z.reshape(2, batch, seq, width) for z in (lw, b, kd))
    fwd = pl.BlockSpec((batch, n, width), lambda c: (0, c, 0))
    bwd = pl.BlockSpec((batch, n, width), lambda c: (0, nc - 1 - c, 0))
    fwd_d = pl.BlockSpec((1, batch, n, width), lambda c: (0, 0, c, 0))
    bwd_d = pl.BlockSpec((1, batch, n, width), lambda c: (1, 0, nc - 1 - c, 0))
    kern = functools.partial(_rwkv_scan_kernel, heads=heads, batch=batch)
    yf, yb = pl.pallas_call(
        kern,
        grid=(nc,),
        in_specs=[fwd, fwd, fwd, bwd, bwd, bwd, fwd_d, fwd_d, fwd_d, bwd_d, bwd_d, bwd_d],
        out_specs=[fwd, bwd],
        out_shape=[jax.ShapeDtypeStruct((batch, seq, width), F32)] * 2,
        scratch_shapes=[pltpu.VMEM((batch * heads, HEAD_DIM, 2 * HEAD_DIM), F32)],
        compiler_params=_cparams(1, "arbitrary"),
        name="rwkv_scan",
    )(r3, v3, n3, r3, v3, n3, lw4, b4, k4, lw4, b4, k4)
    return yf.reshape(t, width), yb.reshape(t, width)


def _pool_tile(p_ref, pp_ref, pn_ref, w_ref, sc_ref, ext_ref, tb, tiles_per_batch, seq):
    p = p_ref[...]
    tm, width = p.shape
    assert all(w == 2 << i for i, w in enumerate(POOL_WINDOWS)) and POOL_WINDOWS[-1] <= 2 * HALO
    n = tm + 2 * HALO
    pad = jnp.zeros((HALO, width), F32)
    for k in range(len(POOL_WINDOWS)):
        ext_ref[k, 0:HALO, :] = pad
        ext_ref[k, HALO + n:2 * HALO + n, :] = pad
    ext_ref[0, HALO:2 * HALO, :] = jnp.where(tb == 0, 0.0, pp_ref[...])
    ext_ref[0, 2 * HALO:2 * HALO + tm, :] = p
    ext_ref[0, 2 * HALO + tm:HALO + n, :] = jnp.where(tb == tiles_per_batch - 1, 0.0, pn_ref[...])

    def rows(k, first, count):
        return ext_ref[k, 2 * HALO + first:2 * HALO + first + count, :]

    ext_ref[1, HALO:HALO + n, :] = rows(0, -HALO - 1, n) + rows(0, -HALO, n)
    for k in range(1, len(POOL_WINDOWS) - 1):
        q = POOL_WINDOWS[k - 1] // 2
        ext_ref[k + 1, HALO:HALO + n, :] = rows(k, -HALO - q, n) + rows(k, -HALO + q, n)
    q = POOL_WINDOWS[-2] // 2
    sums = [rows(k + 1, 0, tm) for k in range(len(POOL_WINDOWS) - 1)]
    sums.append(rows(len(POOL_WINDOWS) - 1, -q, tm) + rows(len(POOL_WINDOWS) - 1, q, tm))

    t = tb * tm + lax.broadcasted_iota(jnp.int32, (tm, width), 0)
    grp = lax.broadcasted_iota(jnp.int32, (tm, width), 1) // (width // len(POOL_WINDOWS))
    pooled = jnp.zeros_like(p)
    for gi, win in enumerate(POOL_WINDOWS):
        half = win // 2
        lo = jnp.clip(t - half, 0, seq - 1)
        hi = jnp.clip(t + half - 1, 0, seq - 1)
        cnt = (hi - lo + 1).astype(F32)
        pooled = jnp.where(grp == gi, sums[gi] / cnt, pooled)
    pooled = pooled - p
    return jnp.dot(pooled, w_ref[...], preferred_element_type=F32) * sc_ref[...]


def _outproj_kernel(ya_ref, yf_ref, yb_ref, bonus_ref, g_ref, p_ref, pp_ref, pn_ref, x_ref, mod_ref,
                    wa_ref, wb_ref, wc_ref, pw_ref, psc_ref,
                    gng_ref, gnb_ref, ones_ref, l1g_ref, l1b_ref, wr_ref, br_ref,
                    x1_o, u2_o, ri_o, rw_o, cnt_o, cnt_ref, ext_ref, *, alpha, tiles_per_batch, seq):
    yc = _pool_tile(p_ref, pp_ref, pn_ref, pw_ref, psc_ref, ext_ref,
                    pl.program_id(0) % tiles_per_batch, tiles_per_batch, seq)
    m = mod_ref[0]
    ones = ones_ref[...]

    def headmean(x):
        return _dot_split(x, ones) * (1.0 / HEAD_DIM)

    ysum = yf_ref[...] + yb_ref[...]
    yc0 = ysum - headmean(ysum)
    yn = yc0 * lax.rsqrt(headmean(yc0 * yc0) + GN_EPS) * gng_ref[...] + gnb_ref[...]
    yb = (yn + bonus_ref[...]) * g_ref[...]
    mix = (jnp.dot(ya_ref[...].astype(BF16), wa_ref[...], preferred_element_type=F32)
           + jnp.dot(yb.astype(BF16), wb_ref[...], preferred_element_type=F32)
           + jnp.dot(yc.astype(BF16), wc_ref[...], preferred_element_type=F32))
    x1 = _ln(alpha * x_ref[...] + m[2:3] * mix) * l1g_ref[...] + l1b_ref[...]
    x1_o[...] = x1
    u2 = _ln(x1) * (1.0 + m[4:5]) + m[3:4]
    u2_o[...] = _pack_bf16_pairs(u2)

    u_hi, u_lo = _split_bf16(u2)
    hi_both = jnp.dot(u_hi, wr_ref[...], preferred_element_type=F32)
    lg = (hi_both[:, :LANES] + hi_both[:, LANES:]
          + jnp.dot(u_lo, wr_ref[:, :LANES], preferred_element_type=F32)) + br_ref[...]
    tm = lg.shape[0]
    lt = jnp.transpose(lg)[:ROUTER_ROWS]
    rowi = lax.broadcasted_iota(jnp.int32, lt.shape, 0)
    big = jnp.int32(1 << 20)
    gl = jnp.where(rowi < N_GROUPS, lt, -jnp.inf)
    gmax = jnp.max(gl, axis=0, keepdims=True)
    gidx = jnp.min(jnp.where(gl == gmax, rowi, big), axis=0, keepdims=True)
    pg_sel = 1.0 / jnp.sum(jnp.exp(gl - gmax), axis=0, keepdims=True)
    e_lo = N_GROUPS + gidx * EXPERTS_PER_GROUP
    el = jnp.where((rowi >= e_lo) & (rowi < e_lo + EXPERTS_PER_GROUP), lt, -jnp.inf)
    m1 = jnp.max(el, axis=0, keepdims=True)
    i1 = jnp.min(jnp.where(el == m1, rowi, big), axis=0, keepdims=True)
    el2 = jnp.where(rowi == i1, -jnp.inf, el)
    m2 = jnp.max(el2, axis=0, keepdims=True)
    i2 = jnp.min(jnp.where(el2 == m2, rowi, big), axis=0, keepdims=True)
    e21 = jnp.exp(m2 - m1)
    gate1 = pg_sel / (1.0 + e21)
    gate2 = pg_sel * e21 / (1.0 + e21)
    row128 = lax.broadcasted_iota(jnp.int32, (LANES, tm), 0)
    rw_o[...] = jnp.transpose(jnp.where(row128 == 0, gate1, jnp.where(row128 == 1, gate2, 0.0)))

    @pl.when(pl.program_id(0) == 0)
    def _():
        cnt_ref[...] = jnp.zeros_like(cnt_ref)

    earlier = (lax.broadcasted_iota(jnp.int32, (tm, tm), 0)
               < lax.broadcasted_iota(jnp.int32, (tm, tm), 1)).astype(BF16)
    oh1 = (rowi == i1).astype(F32)
    oh2 = (rowi == i2).astype(F32)
    run = cnt_ref[...]
    c1 = jnp.sum(oh1, axis=1, keepdims=True)
    before1 = run + jnp.dot(oh1.astype(BF16), earlier, preferred_element_type=F32)
    before2 = run + c1 + jnp.dot(oh2.astype(BF16), earlier, preferred_element_type=F32)
    rank1 = jnp.sum(oh1 * before1, axis=0, keepdims=True).astype(jnp.int32)
    rank2 = jnp.sum(oh2 * before2, axis=0, keepdims=True).astype(jnp.int32)
    total = run + c1 + jnp.sum(oh2, axis=1, keepdims=True)
    cnt_ref[...] = total
    cnt_o[...] = total
    row8 = lax.broadcasted_iota(jnp.int32, (SUBLANES, tm), 0)
    ri_o[...] = jnp.where(row8 == 0, i1 - N_GROUPS, jnp.where(row8 == 1, i2 - N_GROUPS,
                          jnp.where(row8 == 2, rank1, jnp.where(row8 == 3, rank2, 0))))


def _outproj(ya, yf, yb, bonus, g, praw, x2, modl, p, seq, tm, alpha):
    t, d = x2.shape
    tpb = seq // tm
    aw, bw, cw = ya.shape[1], bonus.shape[1], praw.shape[1]
    hb = tm // HALO
    nhb = t // HALO
    tok = lambda i: (i, 0)
    full2 = lambda i: (0, 0)
    kern = functools.partial(_outproj_kernel, alpha=alpha, tiles_per_batch=tpb, seq=seq)
    small = ["w_out_a", "w_out_b", "w_out_c", "pool_w", "pool_scale",
             "gn_gain", "gn_bias", "ones", "ln1_gain", "ln1_bias", "w_router", "b_router"]
    return pl.pallas_call(
        kern,
        grid=(t // tm,),
        in_specs=[pl.BlockSpec((tm, aw), tok),
                  pl.BlockSpec((tm, bw), tok), pl.BlockSpec((tm, bw), tok),
                  pl.BlockSpec((tm, bw), tok), pl.BlockSpec((tm, bw), tok),
                  pl.BlockSpec((tm, cw), tok),
                  pl.BlockSpec((HALO, cw), lambda i: (jnp.maximum(i * hb - 1, 0), 0)),
                  pl.BlockSpec((HALO, cw), lambda i: (jnp.minimum((i + 1) * hb, nhb - 1), 0)),
                  pl.BlockSpec((tm, d), tok),
                  pl.BlockSpec((1,) + modl.shape[1:], lambda i: (i // tpb, 0, 0))]
                 + [pl.BlockSpec(p[k].shape, functools.partial(lambda nd, i: (0,) * nd, p[k].ndim)) for k in small],
        out_specs=[pl.BlockSpec((tm, d), tok), pl.BlockSpec((tm, d // 2), tok),
                   pl.BlockSpec((SUBLANES, tm), lambda i: (0, i)), pl.BlockSpec((tm, LANES), tok),
                   pl.BlockSpec((ROUTER_ROWS, 1), full2)],
        out_shape=[jax.ShapeDtypeStruct((t, d), F32), jax.ShapeDtypeStruct((t, d // 2), jnp.uint32),
                   jax.ShapeDtypeStruct((SUBLANES, t), jnp.int32), jax.ShapeDtypeStruct((t, LANES), F32),
                   jax.ShapeDtypeStruct((ROUTER_ROWS, 1), F32)],
        scratch_shapes=[pltpu.VMEM((ROUTER_ROWS, 1), F32),pltpu.VMEM((len(POOL_WINDOWS), tm + 4 * HALO, cw), F32)],
        compiler_params=_cparams(1, "arbitrary"),
        name="outproj",
    )(ya, yf, yb, bonus, g, praw, praw, praw, x2, modl, *[p[k] for k in small])


def _dispatch(route_t, counts_rows, n_blocks, tm):
    counts = counts_rows[N_GROUPS:N_GROUPS + N_EXPERTS, 0].astype(jnp.int32)
    padded = ((counts + EXPERT_BLOCK - 1) // EXPERT_BLOCK) * EXPERT_BLOCK
    pends = jnp.cumsum(padded)
    pstarts = pends - padded
    e = route_t[:TOP_K]
    onehot = (e[..., None] == jnp.arange(N_EXPERTS, dtype=jnp.int32)).astype(F32)
    start_of = jnp.einsum("ktx,x->kt", onehot, pstarts.astype(F32), precision=HI)
    dest = route_t[TOP_K:2 * TOP_K] + start_of.astype(jnp.int32)
    t = route_t.shape[1]
    dest = dest.reshape(TOP_K, t // tm, tm).transpose(1, 0, 2).reshape(t // tm, 1, TOP_K * tm)
    block_start = jnp.arange(n_blocks, dtype=jnp.int32) * EXPERT_BLOCK
    block_e = jnp.minimum(jnp.sum((pends[None, :] <= block_start[:, None]).astype(jnp.int32), axis=1), N_EXPERTS - 1)
    meta = jnp.concatenate([block_e, (pends[-1] // EXPERT_BLOCK)[None]]).astype(jnp.int32)
    return dest, meta, pends.astype(jnp.int32)


def _scatter_rows_kernel(pends_ref, dest_ref, u_ref, xs_ref, zeros_ref, sem, zsem):
    tm = u_ref.shape[0] * SUBLANES

    @pl.when(pl.program_id(0) == 0)
    def _():
        zeros_ref[...] = jnp.zeros_like(zeros_ref)

        def tail_copy(e):
            tail = pl.ds(pl.multiple_of(pends_ref[e] - EXPERT_BLOCK, EXPERT_BLOCK), EXPERT_BLOCK)
            return pltpu.make_async_copy(zeros_ref, xs_ref.at[tail], zsem)

        def has_rows(e):
            return pends_ref[e] > (pends_ref[e - 1] if e > 0 else 0)

        def unused_copy(j):
            return pltpu.make_async_copy(zeros_ref, xs_ref.at[pl.ds(j * EXPERT_BLOCK, EXPERT_BLOCK)], zsem)

        def is_unused(j):
            return j * EXPERT_BLOCK >= pends_ref[N_EXPERTS - 1]

        n_blocks = xs_ref.shape[0] // EXPERT_BLOCK
        for e in range(N_EXPERTS):
            pl.when(has_rows(e))(lambda e=e: tail_copy(e).start())
        for j in range(n_blocks):
            pl.when(is_unused(j))(lambda j=j: unused_copy(j).start())
        for e in range(N_EXPERTS):
            pl.when(has_rows(e))(lambda e=e: tail_copy(e).wait())
        for j in range(n_blocks):
            pl.when(is_unused(j))(lambda j=j: unused_copy(j).wait())

    def issue(grp, carry):
        for j in range(SUBLANES):
            for k in range(TOP_K):
                dst = dest_ref[0, 0, k * tm + SUBLANES * grp + j]
                pltpu.make_async_copy(u_ref.at[grp, pl.ds(j, 1)], xs_ref.at[pl.ds(dst, 1)],
                                      sem).start(priority=(j * TOP_K + k) % 2)
        return carry

    lax.fori_loop(0, tm // SUBLANES, issue, 0)
    rows = pl.ds(0, TOP_K * tm)
    pltpu.make_async_copy(xs_ref.at[rows], xs_ref.at[rows], sem).wait()


def _scatter_rows(pends, u2, dest3, total, tm):
    t, d = u2.shape
    grid_spec = pltpu.PrefetchScalarGridSpec(
        num_scalar_prefetch=1,
        grid=(t // tm,),
        in_specs=[pl.BlockSpec((1, 1, TOP_K * tm), lambda i, p: (i, 0, 0), memory_space=pltpu.SMEM),
                  pl.BlockSpec((tm // SUBLANES, SUBLANES, d), lambda i, p: (i, 0, 0))],
        out_specs=pl.BlockSpec(memory_space=pl.ANY),
        scratch_shapes=[pltpu.VMEM((EXPERT_BLOCK, d), u2.dtype), pltpu.SemaphoreType.DMA(()),
                        pltpu.SemaphoreType.DMA(())],
    )
    return pl.pallas_call(
        _scatter_rows_kernel,
        grid_spec=grid_spec,
        out_shape=jax.ShapeDtypeStruct((total, d), u2.dtype),
        compiler_params=_cparams(1, "arbitrary"),
        name="scatter_rows",
    )(pends, dest3, u2.reshape(t // SUBLANES, SUBLANES, d))


def _experts_kernel(meta_ref, xs_ref, wg_ref, wu_ref, wd_ref, o_ref, wg_b, wu_b, wd_b):
    i = pl.program_id(0)
    n_used = meta_ref[pl.num_programs(0)]

    @pl.when((i == 0) | (meta_ref[i] != meta_ref[jnp.maximum(i - 1, 0)]))
    def _():
        wg_b[...] = wg_ref[0, 0].astype(BF16)
        wu_b[...] = wu_ref[0, 0].astype(BF16)
        wd_b[...] = wd_ref[0, 0].astype(BF16)

    @pl.when(i < n_used)
    def _():
        half = EXPERT_BLOCK // 2
        gate_up = []
        for r0 in (0, half):
            xb = _unpack_bf16_pairs(xs_ref[r0:r0 + half, :])
            gate_up.append((jnp.dot(xb, wg_b[...], preferred_element_type=F32),
                            jnp.dot(xb, wu_b[...], preferred_element_type=F32)))
        for r0, (gate, up) in zip((0, half), gate_up):
            hb = gate * _sigmoid(gate) * up
            y = jnp.dot(hb.astype(BF16), wd_b[...], preferred_element_type=F32)
            o_ref[r0:r0 + half, :] = _pack_bf16_pairs(y)

    @pl.when(i >= n_used)
    def _():
        o_ref[...] = jnp.zeros_like(o_ref)


def _experts(meta, xs, wg, wu, wd, layer):
    total, dp = xs.shape
    nb = total // EXPERT_BLOCK
    d, de = wg.shape[2:]
    grid_spec = pltpu.PrefetchScalarGridSpec(
        num_scalar_prefetch=1,
        grid=(nb,),
        in_specs=[pl.BlockSpec((EXPERT_BLOCK, dp), lambda i, m: (jnp.minimum(i, m[nb] - 1), 0)),
                  pl.BlockSpec((1, 1, d, de), lambda i, m: (layer, m[i], 0, 0)),
                  pl.BlockSpec((1, 1, d, de), lambda i, m: (layer, m[i], 0, 0)),
                  pl.BlockSpec((1, 1, de, d), lambda i, m: (layer, m[i], 0, 0))],
        out_specs=pl.BlockSpec((EXPERT_BLOCK, dp), lambda i, m: (i, 0)),
        scratch_shapes=[pltpu.VMEM((d, de), BF16), pltpu.VMEM((d, de), BF16), pltpu.VMEM((de, d), BF16)],
    )
    return pl.pallas_call(
        _experts_kernel,
        grid_spec=grid_spec,
        out_shape=jax.ShapeDtypeStruct((total, dp), xs.dtype),
        compiler_params=_cparams(1, "arbitrary"),
        name="experts",
    )(meta, xs, wg, wu, wd)


def _final_kernel(dcur_ref, dnext_ref, x1_ref, rw_ref, mod_ref, g_ref, b_ref, ys_ref, o_ref, ybuf, sem, *, alpha):
    i = pl.program_id(0)
    tm = x1_ref.shape[0]
    slot = i % 2

    def gather(d_ref, s):
        def issue(grp, carry):
            for j in range(SUBLANES):
                for k in range(TOP_K):
                    src = d_ref[0, 0, k * tm + SUBLANES * grp + j]
                    pltpu.make_async_copy(ys_ref.at[pl.ds(src, 1)], ybuf.at[s, k, grp, pl.ds(j, 1)],
                                          sem.at[s]).start(priority=(j * TOP_K + k) % 2)
            return carry

        lax.fori_loop(0, tm // SUBLANES, issue, 0)

    @pl.when(i == 0)
    def _():
        gather(dcur_ref, 0)

    @pl.when(i + 1 < pl.num_programs(0))
    def _():
        gather(dnext_ref, 1 - slot)

    pltpu.make_async_copy(ybuf.at[slot], ybuf.at[slot], sem.at[slot]).wait()
    m = mod_ref[0]
    rw = rw_ref[...]
    dp = ybuf.shape[-1]
    y1 = _unpack_bf16_pairs(ybuf[slot, 0].reshape(tm, dp)).astype(F32)
    y2 = _unpack_bf16_pairs(ybuf[slot, 1].reshape(tm, dp)).astype(F32)
    f = rw[:, 0:1] * y1 + rw[:, 1:2] * y2
    o_ref[...] = _ln(alpha * x1_ref[...] + m[5:6] * f) * g_ref[...] + b_ref[...]


def _final(x1, ysorted, dest3, rw, modl, gain, bias, seq, tm, alpha):
    t, d = x1.shape
    tpb = seq // tm
    n_tiles = t // tm
    tok = lambda i: (i, 0)
    kern = functools.partial(_final_kernel, alpha=alpha)
    dspec = lambda f: pl.BlockSpec((1, 1, TOP_K * tm), f, memory_space=pltpu.SMEM)
    return pl.pallas_call(
        kern,
        grid=(n_tiles,),
        in_specs=[dspec(lambda i: (i, 0, 0)), dspec(lambda i: (jnp.minimum(i + 1, n_tiles - 1), 0, 0)),
                  pl.BlockSpec((tm, d), tok), pl.BlockSpec((tm, LANES), tok),
                  pl.BlockSpec((1,) + modl.shape[1:], lambda i: (i // tpb, 0, 0)),
                  pl.BlockSpec(gain.shape, lambda i: (0, 0)), pl.BlockSpec(bias.shape, lambda i: (0, 0)),
                  pl.BlockSpec(memory_space=pl.ANY)],
        out_specs=pl.BlockSpec((tm, d), tok),
        out_shape=jax.ShapeDtypeStruct((t, d), F32),
        scratch_shapes=[pltpu.VMEM((2, TOP_K, tm // SUBLANES, SUBLANES, ysorted.shape[1]), ysorted.dtype),
                        pltpu.SemaphoreType.DMA((2,))],
        compiler_params=_cparams(1, "arbitrary"),
        name="final_ln",
    )(dest3, dest3, x1, rw, modl, gain, bias, ysorted)


def _block_diag(blocks):
    n, a, b = blocks.shape
    out = jnp.zeros((n * a, n * b), blocks.dtype)
    for i in range(n):
        out = out.at[i * a:(i + 1) * a, i * b:(i + 1) * b].set(blocks[i])
    return out


def _pad_rows(w, lo, total):
    return jnp.pad(w, ((lo, total - lo - w.shape[0]), (0, 0)))


def kernel(x, c, w_mod, b_mod, w_in, na_rpb, rw_conv, rw_w0, rw_w_up, rw_a0, rw_a_up, rw_g_up, rw_k_k, rw_k_a, rw_r_k, rw_gn_gain, rw_gn_bias, pool_w, pool_scale, w_out, ln1_gain, ln1_bias, ln2_gain, ln2_bias, moe_w_group, moe_b_group, moe_w_expert, moe_b_expert, moe_w_gate, moe_w_up, moe_w_down):
    batch, seq, d = x.shape
    depth = w_mod.shape[0]
    t = batch * seq
    a_w = na_rpb.shape[1] * HEAD_DIM
    b_w = rw_w0.shape[-1]
    c_w = pool_scale.shape[-1]
    lr_w = R_W + R_A + R_G
    alpha = (2 * depth) ** 0.25
    tm = tm_in = min(TOKEN_TILE, seq)
    assert seq % tm == 0 and seq % SCAN_CHUNK == 0 and seq % GRID_W == 0 and lr_w == LANES

    mod = _modulation(c, w_mod, b_mod)
    ones_blk = _block_diag(jnp.ones((b_w // HEAD_DIM, HEAD_DIM, HEAD_DIM), BF16))
    row = lambda v: v.reshape(1, -1)

    x2 = x.reshape(t, d)
    for l in range(depth):
        modl = mod[l]
        prep_params = {
            "conv": rw_conv[l], "w0": rw_w0[l], "a0": rw_a0[l],
            "w_up": jnp.stack([_pad_rows(rw_w_up[l, dd], 0, lr_w) for dd in range(2)]).astype(BF16),
            "a_up": jnp.stack([_pad_rows(rw_a_up[l, dd], R_W, lr_w) for dd in range(2)]).astype(BF16),
            "g_up": _pad_rows(rw_g_up[l], R_W + R_A, lr_w).astype(BF16),
            "k_k": row(rw_k_k[l]), "k_a": row(rw_k_a[l]), "r_k": row(rw_r_k[l]), "ones": ones_blk,
        }
        qkv, praw, r, v, nkk, lw, bb, kd, bonus, g = _inproj(x2, modl, w_in[l].astype(BF16), prep_params, seq, tm_in,
                                                             3 * a_w, 3 * b_w, lr_w, c_w)
        ya = _natten(qkv, _na_bias_table(na_rpb[l]), batch, seq, a_w)
        yf, yb = _rwkv_scan(r, v, nkk, lw, bb, kd, batch, seq)
        lane_pad = LANES - N_GROUPS - N_EXPERTS
        w_router = jnp.concatenate([moe_w_group[l], moe_w_expert[l], jnp.zeros((d, lane_pad), F32)], axis=1)
        b_router = jnp.concatenate([moe_b_group[l], moe_b_expert[l], jnp.zeros((lane_pad,), F32)]).reshape(1, LANES)
        wo = w_out[l].astype(BF16)
        out_params = {
            "w_out_a": wo[:a_w], "w_out_b": wo[a_w:a_w + b_w], "w_out_c": wo[a_w + b_w:],
            "pool_w": _block_diag(pool_w[l]), "pool_scale": row(pool_scale[l]),
            "gn_gain": row(rw_gn_gain[l]), "gn_bias": row(rw_gn_bias[l]), "ones": ones_blk,
            "ln1_gain": row(ln1_gain[l]), "ln1_bias": row(ln1_bias[l]),
            "w_router": jnp.concatenate(_split_bf16(w_router), axis=1), "b_router": b_router,
        }
        x1, u2, route_i, route_w, counts = _outproj(ya, yf, yb, bonus, g, praw, x2, modl, out_params, seq, tm, alpha)
        n_blocks = -(-(t * TOP_K) // EXPERT_BLOCK) + N_EXPERTS
        dest3, meta, pends = _dispatch(route_i, counts, n_blocks, tm)
        xs = _scatter_rows(pends, u2, dest3, n_blocks * EXPERT_BLOCK, tm)
        ysorted = _experts(meta, xs, moe_w_gate, moe_w_up, moe_w_down, l)
        x2 = _final(x1, ysorted, dest3, route_w, modl, row(ln2_gain[l]), row(ln2_bias[l]), seq, tm, alpha)
    return x2.reshape(batch, seq, d)
```
